```python
import math
import jax, jax.numpy as jnp
from jax import lax
import numpy as np

D_MODEL = 1024
BATCH = 8
SEQ = 4096
DEPTH = 2

CTX_LEN = 256
GRID_W = 64
HEAD_DIM = 64
HY_WIDTH = 256
GA_HEADS = 6
GA_KV_HEADS = 2
WA_HEADS = 6
WA_KV_HEADS = 2
MIX_WIDTH = HY_WIDTH + (GA_HEADS + WA_HEADS) * HEAD_DIM
Q_BLOCK = 128
WINDOW = 128
ROPE_THETA = 10000.0
ATTN_SCALE = HEAD_DIM ** -0.5
HY_SHORT = 3
HY_EMB = 33
HY_BANDS = (HY_EMB - 1) // 2
HY_FILTER_WIDTH = 64
HY_FAST_DECAY = 0.3
HY_SLOW_DECAY = 1.5
HY_TARGET = 1e-2
N_EXPERTS = 32
TOP_K = 4
D_EXPERT = D_MODEL
SWIGLU_ALPHA = 1.702
SWIGLU_LIMIT = 7.0
MOE_BLOCK = 256
EPS = 1e-6
NEG_INF = -1e30
IN_SEGMENTS = (('hy', 3 * HY_WIDTH), ('ga_q', GA_HEADS * HEAD_DIM), ('ga_k', GA_KV_HEADS * HEAD_DIM), ('ga_v', GA_KV_HEADS * HEAD_DIM), ('wa_q', WA_HEADS * HEAD_DIM), ('wa_k', WA_KV_HEADS * HEAD_DIM), ('wa_v', WA_KV_HEADS * HEAD_DIM))
IN_WIDTH = 3 * HY_WIDTH + (GA_HEADS + 2 * GA_KV_HEADS + WA_HEADS + 2 * WA_KV_HEADS) * HEAD_DIM

kernel_name = 'hybrid_hyena_gqa_swa_moe_dit'


def _col_range(name):
    lo = 0
    for seg, width in IN_SEGMENTS:
        if seg == name:
            return lo, lo + width
        lo += width
    raise KeyError(name)


def rms_norm(x, w):
    xf = x.astype(jnp.float32)
    y = xf * lax.rsqrt(jnp.mean(xf * xf, axis=-1, keepdims=True) + EPS)
    return (y * w.astype(jnp.float32)).astype(x.dtype)


def modulate(x, shift, scale):
    return x * (1 + scale) + shift


def _heads(t, n_heads):
    return t.reshape(t.shape[:-1] + (n_heads, HEAD_DIM))


def _groups(t, n_kv):
    return t.reshape(t.shape[:-2] + (n_kv, t.shape[-2] // n_kv, HEAD_DIM))


def axial_rope_tables(n):
    rows = n // GRID_W
    row = jnp.repeat(jnp.arange(rows, dtype=jnp.float32), GRID_W)
    col = jnp.tile(jnp.arange(GRID_W, dtype=jnp.float32), rows)
    pos = jnp.stack([row, col], axis=-1)
    n_freq = HEAD_DIM // 4
    inv_freq = ROPE_THETA ** (-jnp.arange(n_freq, dtype=jnp.float32) / n_freq)
    ang = pos[:, :, None] * inv_freq
    return jnp.cos(ang), jnp.sin(ang)


def apply_axial_rope(x, cos, sin):
    shp = x.shape
    xr = x.reshape(shp[:-1] + (2, 2, HEAD_DIM // 4))
    x1, x2 = xr[..., 0, :], xr[..., 1, :]
    c = cos[:, None].astype(x.dtype)
    s = sin[:, None].astype(x.dtype)
    out = jnp.stack([x1 * c - x2 * s, x2 * c + x1 * s], axis=-2)
    return out.reshape(shp)


def short_conv(u, w, b):
    ch = u.shape[-1]
    pad = HY_SHORT // 2
    y = lax.conv_general_dilated(u, w[:, None, :].astype(u.dtype), window_strides=(1,), padding=((pad, pad),), dimension_numbers=('NWC', 'WIO', 'NWC'), feature_group_count=ch)
    return y + b


def hyena_filters(n, filt):
    w1, b1, w2, b2, w3, b3, freq, w_out = filt
    f32 = jnp.float32
    t = jnp.linspace(0.0, 1.0, n, dtype=f32)[:, None]
    wpos = (2.0 * math.pi / n) * jnp.arange(n, dtype=f32)[:, None]
    bands = jnp.linspace(1e-4, HY_BANDS - 1, HY_BANDS, dtype=f32)
    z = jnp.concatenate([t, jnp.cos(wpos * bands), -jnp.sin(wpos * bands)], axis=-1)
    fr = freq.astype(f32)
    hdn = jnp.sin(fr * (z @ w1.astype(f32) + b1.astype(f32)))
    hdn = jnp.sin(fr * (hdn @ w2.astype(f32) + b2.astype(f32)))
    hdn = jnp.sin(fr * (hdn @ w3.astype(f32) + b3.astype(f32)))
    k = (hdn @ w_out.astype(f32)).reshape(n, 2, HY_WIDTH)
    max_decay = math.log(HY_TARGET) / HY_FAST_DECAY
    min_decay = math.log(HY_TARGET) / HY_SLOW_DECAY
    deltas = jnp.linspace(min_decay, max_decay, HY_WIDTH, dtype=f32)
    decay = jnp.exp(-t * jnp.abs(deltas))
    k = k * decay[:, None, :]
    return k[:, 0], k[:, 1]


def hyena_mix(u, conv_w, conv_b, filt, skip):
    n = u.shape[1]
    uc = short_conv(u, conv_w, conv_b)
    x0, x1, v = jnp.split(uc, 3, axis=-1)
    k_fwd, k_bwd = hyena_filters(n, filt)
    k_full = jnp.concatenate([k_fwd, k_bwd[::-1]], axis=0)
    z = (v * x1).astype(jnp.float32)
    y = jnp.fft.irfft(jnp.fft.rfft(z, n=2 * n, axis=1) * jnp.fft.rfft(k_full, axis=0)[None], n=2 * n, axis=1)[:, :n]
    y = y + z * skip.astype(jnp.float32)
    return x0 * y.astype(u.dtype)


def dense_gqa(q, k, v, sink=None):
    s = jnp.einsum('bqkgd,bskd->bkgqs', q, k).astype(jnp.float32) * ATTN_SCALE
    if sink is not None:
        sk = jnp.broadcast_to(sink.astype(jnp.float32)[None, :, :, None, None], s.shape[:-1] + (1,))
        s = jnp.concatenate([s, sk], axis=-1)
    p = jax.nn.softmax(s, axis=-1)
    if sink is not None:
        p = p[..., :-1]
    return jnp.einsum('bkgqs,bskd->bqkgd', p.astype(v.dtype), v)


def global_attention(q, k, v, k_ctx, v_ctx):
    bsz, n = q.shape[:2]
    k_all = jnp.concatenate([k, k_ctx], axis=1)
    v_all = jnp.concatenate([v, v_ctx], axis=1)
    qb = jnp.moveaxis(q.reshape((bsz, n // Q_BLOCK, Q_BLOCK) + q.shape[2:]), 1, 0)
    ob = lax.map(lambda qblk: dense_gqa(qblk, k_all, v_all), qb)
    return jnp.moveaxis(ob, 0, 1).reshape(bsz, n, -1)


def window_attention(q, k, v, k_ctx, v_ctx, sink):
    bsz, n, kv, g, dh = q.shape
    nb = n // Q_BLOCK
    band_len = 3 * Q_BLOCK

    def band(t):
        tp = jnp.pad(t, ((0, 0), (Q_BLOCK, Q_BLOCK), (0, 0), (0, 0))).reshape(bsz, nb + 2, Q_BLOCK, kv, dh)
        return jnp.concatenate([tp[:, :-2], tp[:, 1:-1], tp[:, 2:]], axis=2)

    kb, vb = band(k), band(v)
    qb = q.reshape(bsz, nb, Q_BLOCK, kv, g, dh)
    s_loc = jnp.einsum('bnqkgd,bnskd->bnkgqs', qb, kb).astype(jnp.float32) * ATTN_SCALE
    blk = jnp.arange(nb, dtype=jnp.int32)[:, None, None] * Q_BLOCK
    qpos = blk + jnp.arange(Q_BLOCK, dtype=jnp.int32)[None, :, None]
    kpos = blk - Q_BLOCK + jnp.arange(band_len, dtype=jnp.int32)[None, None, :]
    valid = (jnp.abs(kpos - qpos) <= WINDOW) & (kpos >= 0) & (kpos < n)
    s_loc = jnp.where(valid[None, :, None, None], s_loc, NEG_INF)
    s_ctx = jnp.einsum('bnqkgd,bckd->bnkgqc', qb, k_ctx).astype(jnp.float32) * ATTN_SCALE
    s_sink = jnp.broadcast_to(sink.astype(jnp.float32)[None, None, :, :, None, None], s_ctx.shape[:-1] + (1,))
    p = jax.nn.softmax(jnp.concatenate([s_loc, s_ctx, s_sink], axis=-1), axis=-1).astype(v.dtype)
    n_ctx = k_ctx.shape[1]
    o = jnp.einsum('bnkgqs,bnskd->bnqkgd', p[..., :band_len], vb) + jnp.einsum('bnkgqc,bckd->bnqkgd', p[..., band_len:band_len + n_ctx], v_ctx)
    return o.reshape(bsz, n, kv * g * dh)


def token_mixer(h, hc, rope_cos, rope_sin, w_in, b_in, conv_w, conv_b, filt, skip, q_norm, k_norm, sink, branch_norm, w_out, b_out, with_ctx_out):
    p = h @ w_in + b_in

    def lat(name):
        lo, hi = _col_range(name)
        return p[..., lo:hi]

    def cproj(name):
        lo, hi = _col_range(name)
        return hc @ w_in[:, lo:hi] + b_in[lo:hi]

    def rope(t):
        return apply_axial_rope(t, rope_cos, rope_sin)

    def merge(y_hy, y_ga, y_wa):
        g0 = HY_WIDTH
        g1 = HY_WIDTH + GA_HEADS * HEAD_DIM
        y = jnp.concatenate([rms_norm(y_hy, branch_norm[:g0]), rms_norm(y_ga, branch_norm[g0:g1]), rms_norm(y_wa, branch_norm[g1:])], axis=-1)
        return y @ w_out + b_out

    wa_sink = sink.reshape(WA_KV_HEADS, WA_HEADS // WA_KV_HEADS)
    y_hy = hyena_mix(lat('hy'), conv_w, conv_b, filt, skip)
    gq = rope(rms_norm(_heads(lat('ga_q'), GA_HEADS), q_norm))
    gk = rope(rms_norm(_heads(lat('ga_k'), GA_KV_HEADS), k_norm))
    gv = _heads(lat('ga_v'), GA_KV_HEADS)
    gkc = rms_norm(_heads(cproj('ga_k'), GA_KV_HEADS), k_norm)
    gvc = _heads(cproj('ga_v'), GA_KV_HEADS)
    y_ga = global_attention(_groups(gq, GA_KV_HEADS), gk, gv, gkc, gvc)
    wq = rope(_heads(lat('wa_q'), WA_HEADS))
    wk = rope(_heads(lat('wa_k'), WA_KV_HEADS))
    wv = _heads(lat('wa_v'), WA_KV_HEADS)
    wkc = _heads(cproj('wa_k'), WA_KV_HEADS)
    wvc = _heads(cproj('wa_v'), WA_KV_HEADS)
    y_wa = window_attention(_groups(wq, WA_KV_HEADS), wk, wv, wkc, wvc, wa_sink)
    y = merge(y_hy, y_ga, y_wa)
    if not with_ctx_out:
        return y, None
    yc_hy = hyena_mix(cproj('hy'), conv_w, conv_b, filt, skip)
    gqc = rms_norm(_heads(cproj('ga_q'), GA_HEADS), q_norm)
    yc_ga = dense_gqa(_groups(gqc, GA_KV_HEADS), gkc, gvc)
    yc_ga = yc_ga.reshape(yc_ga.shape[:2] + (-1,))
    wqc = _heads(cproj('wa_q'), WA_HEADS)
    yc_wa = dense_gqa(_groups(wqc, WA_KV_HEADS), wkc, wvc, wa_sink)
    yc_wa = yc_wa.reshape(yc_wa.shape[:2] + (-1,))
    return y, merge(yc_hy, yc_ga, yc_wa)


def moe_ffn(h, w_router, b_router, w1, b1, w2, b2):
    n_tok, d = h.shape
    f = w2.shape[1]
    logits = (h @ w_router + b_router).astype(jnp.float32)
    top_val, top_idx = lax.top_k(logits, TOP_K)
    gates = jax.nn.softmax(top_val, axis=-1)
    n_assign = n_tok * TOP_K
    n_blocks = -(-n_assign // MOE_BLOCK) + N_EXPERTS
    e_flat = top_idx.reshape(n_assign)
    order = jnp.argsort(e_flat)
    e_sorted = e_flat[order]
    tok_sorted = (order // TOP_K).astype(jnp.int32)
    g_sorted = gates.reshape(n_assign)[order]
    counts = jnp.bincount(e_flat, length=N_EXPERTS)
    starts = jnp.cumsum(counts) - counts
    padded = (counts + MOE_BLOCK - 1) // MOE_BLOCK * MOE_BLOCK
    ends = jnp.cumsum(padded)
    slot = (ends - padded)[e_sorted] + jnp.arange(n_assign, dtype=jnp.int32) - starts[e_sorted]
    slot_tok = jnp.full((n_blocks * MOE_BLOCK,), n_tok, jnp.int32).at[slot].set(tok_sorted)
    slot_gate = jnp.zeros((n_blocks * MOE_BLOCK,), jnp.float32).at[slot].set(g_sorted)
    block_expert = jnp.minimum(jnp.searchsorted(ends, jnp.arange(n_blocks, dtype=jnp.int32) * MOE_BLOCK, side='right'), N_EXPERTS - 1)
    h_pad = jnp.concatenate([h, jnp.zeros((1, d), h.dtype)], axis=0)

    def expert_block(acc, blk):
        tok, gate, e = blk
        a = h_pad[tok] @ w1[e] + b1[e]
        glu = jnp.minimum(a[:, :f], SWIGLU_LIMIT)
        lin = jnp.clip(a[:, f:], -SWIGLU_LIMIT, SWIGLU_LIMIT)
        out = (glu * jax.nn.sigmoid(SWIGLU_ALPHA * glu) * (lin + 1)) @ w2[e] + b2[e]
        return acc.at[tok].add(out.astype(jnp.float32) * gate[:, None]), None

    acc, _ = lax.scan(expert_block, jnp.zeros((n_tok + 1, d), jnp.float32), (slot_tok.reshape(n_blocks, MOE_BLOCK), slot_gate.reshape(n_blocks, MOE_BLOCK), block_expert))
    return acc[:n_tok].astype(h.dtype)


def setup_inputs(seed: int = 0) -> dict:
    key = jax.random.key(seed)
    keys = list(jax.random.split(key, 40))
    counter = [0]

    def nk():
        counter[0] += 1
        return keys[counter[0] - 1]

    def nrm(shape, std):
        return std * jax.random.normal(nk(), shape, jnp.float32)

    def gain(shape):
        return 1.0 + 0.05 * jax.random.normal(nk(), shape, jnp.float32)

    L, D, E, F = DEPTH, D_MODEL, N_EXPERTS, D_EXPERT
    return {
        'x': nrm((BATCH, SEQ, D), 1.0),
        'c': nrm((BATCH, D), 1.0),
        'ctx': nrm((BATCH, CTX_LEN, D), 1.0),
        'c_ctx': nrm((D,), 1.0),
        'w_ada': nrm((L, D, 6 * D), 0.02),
        'b_ada': nrm((L, 6 * D), 0.01),
        'norm_mix': gain((L, D)),
        'norm_ffn': gain((L, D)),
        'w_in': nrm((L, D, IN_WIDTH), D ** -0.5),
        'b_in': nrm((L, IN_WIDTH), 0.01),
        'hy_conv_w': nrm((L, HY_SHORT, 3 * HY_WIDTH), HY_SHORT ** -0.5),
        'hy_conv_b': nrm((L, 3 * HY_WIDTH), 0.01),
        'hy_filt_w1': nrm((L, HY_EMB, HY_FILTER_WIDTH), HY_EMB ** -0.5),
        'hy_filt_b1': nrm((L, HY_FILTER_WIDTH), 0.1),
        'hy_filt_w2': nrm((L, HY_FILTER_WIDTH, HY_FILTER_WIDTH), HY_FILTER_WIDTH ** -0.5),
        'hy_filt_b2': nrm((L, HY_FILTER_WIDTH), 0.1),
        'hy_filt_w3': nrm((L, HY_FILTER_WIDTH, HY_FILTER_WIDTH), HY_FILTER_WIDTH ** -0.5),
        'hy_filt_b3': nrm((L, HY_FILTER_WIDTH), 0.1),
        'hy_filt_freq': gain((L, HY_FILTER_WIDTH)),
        'hy_filt_out': nrm((L, HY_FILTER_WIDTH, 2 * HY_WIDTH), 0.005),
        'hy_skip': nrm((L, HY_WIDTH), 1.0),
        'ga_q_norm': gain((L, HEAD_DIM)),
        'ga_k_norm': gain((L, HEAD_DIM)),
        'wa_sink': nrm((L, WA_HEADS), 0.5),
        'branch_norm': gain((L, MIX_WIDTH)),
        'w_out': nrm((L, MIX_WIDTH, D), MIX_WIDTH ** -0.5),
        'b_out': nrm((L, D), 0.01),
        'w_router': nrm((L, D, E), D ** -0.5),
        'b_router': nrm((L, E), 0.01),
        'w_mlp1': nrm((L, E, D, 2 * F), D ** -0.5),
        'b_mlp1': nrm((L, E, 2 * F), 0.01),
        'w_mlp2': nrm((L, E, F, D), F ** -0.5),
        'b_mlp2': nrm((L, E, D), 0.01),
        'norm_final': gain((D,)),
    }


def reference(x, c, ctx, c_ctx, w_ada, b_ada, norm_mix, norm_ffn, w_in, b_in, hy_conv_w, hy_conv_b, hy_filt_w1, hy_filt_b1, hy_filt_w2, hy_filt_b2, hy_filt_w3, hy_filt_b3, hy_filt_freq, hy_filt_out, hy_skip, ga_q_norm, ga_k_norm, wa_sink, branch_norm, w_out, b_out, w_router, b_router, w_mlp1, b_mlp1, w_mlp2, b_mlp2, norm_final):
    bsz, n, d = x.shape
    rope_cos, rope_sin = axial_rope_tables(n)
    s_c = jax.nn.silu(c)
    s_cc = jax.nn.silu(c_ctx)
    xc = ctx
    for l in range(DEPTH):
        last = l == DEPTH - 1
        mod = (s_c @ w_ada[l] + b_ada[l])[:, None, :]
        mod_c = s_cc @ w_ada[l] + b_ada[l]
        sh1, sc1, g1, sh2, sc2, g2 = jnp.split(mod, 6, axis=-1)
        csh1, csc1, cg1, csh2, csc2, cg2 = jnp.split(mod_c, 6, axis=-1)
        h = modulate(rms_norm(x, norm_mix[l]), sh1, sc1)
        hc = modulate(rms_norm(xc, norm_mix[l]), csh1, csc1)
        filt = (hy_filt_w1[l], hy_filt_b1[l], hy_filt_w2[l], hy_filt_b2[l], hy_filt_w3[l], hy_filt_b3[l], hy_filt_freq[l], hy_filt_out[l])
        o, oc = token_mixer(h, hc, rope_cos, rope_sin, w_in[l], b_in[l], hy_conv_w[l], hy_conv_b[l], filt, hy_skip[l], ga_q_norm[l], ga_k_norm[l], wa_sink[l], branch_norm[l], w_out[l], b_out[l], not last)
        x = x + g1 * o
        h = modulate(rms_norm(x, norm_ffn[l]), sh2, sc2).reshape(bsz * n, d)
        moe_w = (w_router[l], b_router[l], w_mlp1[l], b_mlp1[l], w_mlp2[l], b_mlp2[l])
        if last:
            x = x + g2 * moe_ffn(h, *moe_w).reshape(bsz, n, d)
        else:
            xc = xc + cg1 * oc
            hc = modulate(rms_norm(xc, norm_ffn[l]), csh2, csc2).reshape(-1, d)
            y = moe_ffn(jnp.concatenate([h, hc], axis=0), *moe_w)
            x = x + g2 * y[:bsz * n].reshape(bsz, n, d)
            xc = xc + cg2 * y[bsz * n:].reshape(xc.shape)
    return rms_norm(x, norm_final)
```

```python
import functools
import math

import jax
import jax.numpy as jnp
from jax import lax
from jax.experimental import pallas as pl
from jax.experimental.pallas import tpu as pltpu

F32 = jnp.float32
BF16 = jnp.bfloat16
I32 = jnp.int32

D_MODEL = 1024
HEAD_DIM = 64
GRID_W = 64
HY_WIDTH = 256
GA_HEADS = 6
GA_KV_HEADS = 2
WA_HEADS = 6
WA_KV_HEADS = 2
GROUP = GA_HEADS // GA_KV_HEADS
Q_WIDTH = GA_HEADS * HEAD_DIM
KV_WIDTH = GA_KV_HEADS * HEAD_DIM
MIX_WIDTH = HY_WIDTH + 2 * Q_WIDTH
IN_WIDTH = 3 * HY_WIDTH + 2 * (Q_WIDTH + 2 * KV_WIDTH)
WINDOW = 128
ROPE_THETA = 10000.0
ATTN_SCALE = HEAD_DIM ** -0.5
HY_SHORT = 3
HY_EMB = 33
HY_BANDS = (HY_EMB - 1) // 2
HY_FAST_DECAY = 0.3
HY_SLOW_DECAY = 1.5
HY_TARGET = 1e-2
N_EXPERTS = 32
TOP_K = 4
SWIGLU_ALPHA = 1.702
SWIGLU_LIMIT = 7.0
EPS = 1e-6
NEG_INF = -1e30

LANES = 128
VMEM_LIMIT_V7X = 56 * 1024 * 1024

ROW_TILE = 256
KV_CHUNK = 512
MOE_TILE = 512
ROUTER_LANES = LANES


def _cp(sem, vmem=VMEM_LIMIT_V7X):
    return pltpu.CompilerParams(dimension_semantics=sem, vmem_limit_bytes=vmem)


def _rms(v, w):
    return v * lax.rsqrt(jnp.mean(v * v, axis=-1, keepdims=True) + EPS) * w


def _ada_kernel(c_ref, w_ref, b_ref, o_ref):
    c = c_ref[...]
    s = c / (1.0 + jnp.exp(-c))
    hi = s.astype(BF16)
    lo = (s - hi.astype(F32)).astype(BF16)
    w = w_ref[0]
    whi = w.astype(BF16)
    wlo = (w - whi.astype(F32)).astype(BF16)
    acc = jnp.dot(hi, whi, preferred_element_type=F32)
    acc += jnp.dot(lo, whi, preferred_element_type=F32)
    acc += jnp.dot(hi, wlo, preferred_element_type=F32)
    o_ref[0] = acc + b_ref[0]


def _ada_mod(c_all, w_ada, b_ada):
    n_layers, d, n6 = w_ada.shape
    tn = n6 // 4
    return pl.pallas_call(
        _ada_kernel,
        grid=(n_layers, n6 // tn),
        in_specs=[
            pl.BlockSpec((16, d), lambda l, j: (0, 0)),
            pl.BlockSpec((1, d, tn), lambda l, j: (l, 0, j)),
            pl.BlockSpec((1, 1, tn), lambda l, j: (l, 0, j)),
        ],
        out_specs=pl.BlockSpec((1, 16, tn), lambda l, j: (l, 0, j)),
        out_shape=jax.ShapeDtypeStruct((n_layers, 16, n6), F32),
        compiler_params=_cp(("arbitrary", "arbitrary")),
        name="ada_mod",
    )(c_all, w_ada, b_ada.reshape(n_layers, 1, n6))


def _inproj_kernel(x_ref, mod_ref, nw_ref, w_ref, b_ref, cos_ref, sin_ref, qn_ref, kn_ref, g_ref,
                   hy_ref, gq_ref, gk_ref, gv_ref, wq_ref, wk_ref, wv_ref):
    x = x_ref[0]
    tm = x.shape[0]
    shift = mod_ref[0, 0, 0:1, :]
    scale = mod_ref[0, 0, 1:2, :]
    h = _rms(x, nw_ref[...]) * (1.0 + scale) + shift
    p = jnp.dot(h.astype(BF16), w_ref[...], preferred_element_type=F32) + b_ref[...]
    hy_ref[0] = p[:, :3 * HY_WIDTH]

    cos = cos_ref[...]
    sin = sin_ref[...]
    lane = lax.broadcasted_iota(I32, (tm, LANES), 1)
    first = (lane % (HEAD_DIM // 2)) < (HEAD_DIM // 4)
    gmat = g_ref[...]

    def rope(v):
        sw = jnp.where(first, pltpu.roll(v, LANES - HEAD_DIM // 4, 1), pltpu.roll(v, HEAD_DIM // 4, 1))
        return v * cos + sw * sin

    def head_norm(v, w):
        t = v * v
        hi = t.astype(BF16)
        lo = (t - hi.astype(F32)).astype(BF16)
        ms = jnp.dot(hi, gmat, preferred_element_type=F32) + jnp.dot(lo, gmat, preferred_element_type=F32)
        return v * lax.rsqrt(ms + EPS) * w

    o = 3 * HY_WIDTH
    for u in range(Q_WIDTH // LANES):
        v = p[:, o + u * LANES:o + (u + 1) * LANES]
        v = rope(head_norm(v, qn_ref[...]))
        gq_ref[0, :, u * LANES:(u + 1) * LANES] = (v * ATTN_SCALE).astype(BF16)
    o += Q_WIDTH
    gk_ref[0] = rope(head_norm(p[:, o:o + KV_WIDTH], kn_ref[...])).astype(BF16)
    o += KV_WIDTH
    gv_ref[0] = p[:, o:o + KV_WIDTH].astype(BF16)
    o += KV_WIDTH
    for u in range(Q_WIDTH // LANES):
        v = rope(p[:, o + u * LANES:o + (u + 1) * LANES])
        wq_ref[0, :, u * LANES:(u + 1) * LANES] = (v * ATTN_SCALE).astype(BF16)
    o += Q_WIDTH
    wk_ref[0] = rope(p[:, o:o + KV_WIDTH]).astype(BF16)
    o += KV_WIDTH
    wv_ref[0] = p[:, o:o + KV_WIDTH].astype(BF16)


def _inproj(x_all, mod, norm_w, w_in, b_in, cos_t, sin_t, q_norm, k_norm, nt):
    bsz, rows, d = x_all.shape
    tm = ROW_TILE
    ntile = rows // tm
    gmat = jnp.kron(jnp.eye(LANES // HEAD_DIM, dtype=F32), jnp.full((HEAD_DIM, HEAD_DIM), 1.0 / HEAD_DIM, F32)).astype(BF16)
    qn = jnp.tile(q_norm, LANES // HEAD_DIM).reshape(1, LANES)
    kn = jnp.tile(k_norm, LANES // HEAD_DIM).reshape(1, LANES)
    tok = lambda w: pl.BlockSpec((1, tm, w), lambda b, i: (b, i, 0))
    const = lambda s: pl.BlockSpec(s, lambda b, i: (0,) * len(s))
    out_w = (3 * HY_WIDTH, Q_WIDTH, KV_WIDTH, KV_WIDTH, Q_WIDTH, KV_WIDTH, KV_WIDTH)
    out_dt = (F32, BF16, BF16, BF16, BF16, BF16, BF16)
    return pl.pallas_call(
        _inproj_kernel,
        grid=(bsz, ntile),
        in_specs=[
            tok(d),
            pl.BlockSpec((1, 1, 6, d), lambda b, i: (b, i // nt, 0, 0)),
            const((1, d)),
            const((d, IN_WIDTH)),
            const((1, IN_WIDTH)),
            pl.BlockSpec((tm, LANES), lambda b, i: (i, 0)),
            pl.BlockSpec((tm, LANES), lambda b, i: (i, 0)),
            const((1, LANES)),
            const((1, LANES)),
            const((LANES, LANES)),
        ],
        out_specs=[tok(w) for w in out_w],
        out_shape=[jax.ShapeDtypeStruct((bsz, rows, w), dt) for w, dt in zip(out_w, out_dt)],
        compiler_params=_cp(("arbitrary", "arbitrary")),
        name="inproj",
    )(x_all, mod, norm_w.reshape(1, d), w_in.astype(BF16), b_in.reshape(1, IN_WIDTH), cos_t, sin_t, qn, kn, gmat)


def _stack_q(q_ref, qs_ref):
    qf = q_ref[0].astype(F32)
    tq = qf.shape[0]
    for j in range(GA_KV_HEADS):
        for g in range(GROUP):
            h = GROUP * j + g
            blk = qf[:, (h // 2) * LANES:(h // 2 + 1) * LANES]
            if h % 2 != j:
                blk = pltpu.roll(blk, HEAD_DIM, 1)
            qs_ref[j, g * tq:(g + 1) * tq, :] = blk.astype(BF16)


def _unstack_o(norm_fn, o_ref, tq):
    lane = lax.broadcasted_iota(I32, (tq, LANES), 1)
    heads = []
    for j in range(GA_KV_HEADS):
        for g in range(GROUP):
            h = GROUP * j + g
            o = norm_fn(j, g)
            if h % 2 != j:
                o = pltpu.roll(o, HEAD_DIM, 1)
            heads.append(o)
    for u in range(Q_WIDTH // LANES):
        o_ref[0, :, u * LANES:(u + 1) * LANES] = jnp.where(lane < HEAD_DIM, heads[2 * u], heads[2 * u + 1])


def _half_masks(rows):
    lane = lax.broadcasted_iota(I32, (rows, LANES), 1)
    return [lane < HEAD_DIM, lane >= HEAD_DIM]


def _dot_t(a, b):
    return lax.dot_general(a, b, (((1,), (1,)), ((), ())), preferred_element_type=F32)


def _ga_kernel(q_ref, k_ref, v_ref, o_ref, qs_ref, m_ref, acc_ref, *, nt, n_lat):
    i = pl.program_id(1)
    tq = q_ref.shape[1]
    _stack_q(q_ref, qs_ref)
    m_ref[...] = jnp.full(m_ref.shape, NEG_INF, F32)
    acc_ref[...] = jnp.zeros(acc_ref.shape, F32)

    def chunk(kc, vc):
        masks = _half_masks(kc.shape[0])
        for j in range(GA_KV_HEADS):
            km = jnp.where(masks[j], kc, jnp.zeros_like(kc))
            vm = jnp.where(masks[j], vc, jnp.ones_like(vc))
            s = _dot_t(qs_ref[j], km)
            m_old = m_ref[j]
            m_new = jnp.maximum(m_old, jnp.max(s, axis=-1, keepdims=True))
            p = jnp.exp(s - m_new)
            acc_ref[j] = jnp.exp(m_old - m_new) * acc_ref[j] + jnp.dot(p.astype(BF16), vm, preferred_element_type=F32)
            m_ref[j] = m_new

    @pl.when(i < nt)
    def _():
        def body(c, carry):
            st = pl.multiple_of(c * KV_CHUNK, KV_CHUNK)
            chunk(k_ref[0, pl.ds(st, KV_CHUNK), :], v_ref[0, pl.ds(st, KV_CHUNK), :])
            return carry
        lax.fori_loop(0, n_lat // KV_CHUNK, body, 0)

    chunk(k_ref[0, n_lat:, :], v_ref[0, n_lat:, :])

    def norm_fn(j, g):
        a = acc_ref[j, g * tq:(g + 1) * tq, :]
        return a / pltpu.roll(a, HEAD_DIM, 1)

    _unstack_o(norm_fn, o_ref, tq)


def _global_attention(gq, gk, gv, nt, ntile):
    bsz, rows, _ = gq.shape
    tq = ROW_TILE
    n_lat = nt * tq
    return pl.pallas_call(
        functools.partial(_ga_kernel, nt=nt, n_lat=n_lat),
        grid=(bsz, ntile),
        in_specs=[
            pl.BlockSpec((1, tq, Q_WIDTH), lambda b, i: (b, i, 0)),
            pl.BlockSpec((1, rows, KV_WIDTH), lambda b, i: (b, 0, 0)),
            pl.BlockSpec((1, rows, KV_WIDTH), lambda b, i: (b, 0, 0)),
        ],
        out_specs=pl.BlockSpec((1, tq, Q_WIDTH), lambda b, i: (b, i, 0)),
        out_shape=jax.ShapeDtypeStruct((bsz, ntile * tq, Q_WIDTH), F32),
        scratch_shapes=[
            pltpu.VMEM((GA_KV_HEADS, GROUP * tq, LANES), BF16),
            pltpu.VMEM((GA_KV_HEADS, GROUP * tq, 1), F32),
            pltpu.VMEM((GA_KV_HEADS, GROUP * tq, LANES), F32),
        ],
        compiler_params=_cp(("arbitrary", "arbitrary")),
        name="global_attn",
    )(gq, gk, gv)


def _wa_kernel(q_ref, k_ref, v_ref, sink_ref, o_ref, qs_ref, acc_ref, l_ref, *, nt, n_lat):
    i = pl.program_id(1)
    tq = q_ref.shape[1]
    band = tq + 2 * WINDOW
    _stack_q(q_ref, qs_ref)
    st = pl.multiple_of(i * tq, tq)
    kb = k_ref[0, pl.ds(st, band), :]
    vb = v_ref[0, pl.ds(st, band), :]
    kc = k_ref[0, WINDOW + n_lat:WINDOW + n_lat + ROW_TILE, :]
    vc = v_ref[0, WINDOW + n_lat:WINDOW + n_lat + ROW_TILE, :]

    r = lax.broadcasted_iota(I32, (GROUP * tq, band), 0) % tq
    c = lax.broadcasted_iota(I32, (GROUP * tq, band), 1)
    rel = c - WINDOW - r
    kpos = c + (i * tq - WINDOW)
    valid = (jnp.abs(rel) <= WINDOW) & (kpos >= 0) & (kpos < n_lat) & (i < nt)

    mb = _half_masks(band)
    mc = _half_masks(ROW_TILE)
    for j in range(WA_KV_HEADS):
        q = qs_ref[j]
        s_loc = jnp.where(valid, _dot_t(q, jnp.where(mb[j], kb, jnp.zeros_like(kb))), NEG_INF)
        s_ctx = _dot_t(q, jnp.where(mc[j], kc, jnp.zeros_like(kc)))
        sink = jnp.concatenate(
            [jnp.broadcast_to(sink_ref[GROUP * j + g:GROUP * j + g + 1, 0:1], (tq, 1)) for g in range(GROUP)], axis=0)
        m = jnp.maximum(jnp.maximum(jnp.max(s_loc, axis=-1, keepdims=True), jnp.max(s_ctx, axis=-1, keepdims=True)), sink)
        p_loc = jnp.exp(s_loc - m).astype(BF16)
        p_ctx = jnp.exp(s_ctx - m).astype(BF16)
        acc_ref[j] = (jnp.dot(p_loc, jnp.where(mb[j], vb, jnp.ones_like(vb)), preferred_element_type=F32)
                      + jnp.dot(p_ctx, jnp.where(mc[j], vc, jnp.ones_like(vc)), preferred_element_type=F32))
        l_ref[j] = jnp.exp(sink - m)

    def norm_fn(j, g):
        a = acc_ref[j, g * tq:(g + 1) * tq, :]
        return a / (pltpu.roll(a, HEAD_DIM, 1) + l_ref[j, g * tq:(g + 1) * tq, :])

    _unstack_o(norm_fn, o_ref, tq)


def _window_attention(wq, wk, wv, sink, nt, ntile):
    bsz, rows, _ = wq.shape
    tq = ROW_TILE
    n_lat = nt * tq
    pad = ((0, 0), (WINDOW, WINDOW), (0, 0))
    kp = jnp.pad(wk, pad)
    vp = jnp.pad(wv, pad)
    sink_b = jnp.zeros((8, LANES), F32).at[:WA_HEADS].set(jnp.broadcast_to(sink[:, None], (WA_HEADS, LANES)))
    return pl.pallas_call(
        functools.partial(_wa_kernel, nt=nt, n_lat=n_lat),
        grid=(bsz, ntile),
        in_specs=[
            pl.BlockSpec((1, tq, Q_WIDTH), lambda b, i: (b, i, 0)),
            pl.BlockSpec((1, rows + 2 * WINDOW, KV_WIDTH), lambda b, i: (b, 0, 0)),
            pl.BlockSpec((1, rows + 2 * WINDOW, KV_WIDTH), lambda b, i: (b, 0, 0)),
            pl.BlockSpec((8, LANES), lambda b, i: (0, 0)),
        ],
        out_specs=pl.BlockSpec((1, tq, Q_WIDTH), lambda b, i: (b, i, 0)),
        out_shape=jax.ShapeDtypeStruct((bsz, ntile * tq, Q_WIDTH), F32),
        scratch_shapes=[
            pltpu.VMEM((WA_KV_HEADS, GROUP * tq, LANES), BF16),
            pltpu.VMEM((WA_KV_HEADS, GROUP * tq, LANES), F32),
            pltpu.VMEM((WA_KV_HEADS, GROUP * tq, 1), F32),
        ],
        compiler_params=_cp(("arbitrary", "arbitrary")),
        name="window_attn",
    )(wq, kp, vp, sink_b)


def _merge_kernel(x_ref, hy_ref, ga_ref, wa_ref, mod_ref, bn_ref, wo_ref, bo_ref, nf_ref, wrh_ref, wrl_ref, br_ref,
                  xo_ref, h2_ref, ti_ref, tg_ref):
    bn = bn_ref[...]
    g0 = HY_WIDTH
    g1 = HY_WIDTH + Q_WIDTH
    y = jnp.concatenate([_rms(hy_ref[0], bn[:, :g0]), _rms(ga_ref[0], bn[:, g0:g1]), _rms(wa_ref[0], bn[:, g1:])], axis=-1)
    o = jnp.dot(y.astype(BF16), wo_ref[...], preferred_element_type=F32) + bo_ref[...]
    x1 = x_ref[0] + mod_ref[0, 0, 2:3, :] * o
    xo_ref[0] = x1
    h2 = _rms(x1, nf_ref[...]) * (1.0 + mod_ref[0, 0, 4:5, :]) + mod_ref[0, 0, 3:4, :]
    h2_ref[0] = h2

    hi = h2.astype(BF16)
    lo = (h2 - hi.astype(F32)).astype(BF16)
    logits = (jnp.dot(hi, wrh_ref[...], preferred_element_type=F32) + jnp.dot(lo, wrh_ref[...], preferred_element_type=F32)
              + jnp.dot(hi, wrl_ref[...], preferred_element_type=F32)) + br_ref[...]
    tm = logits.shape[0]
    lane = lax.broadcasted_iota(I32, (tm, ROUTER_LANES), 1)
    ti = jnp.zeros((tm, ROUTER_LANES), I32)
    tv = jnp.zeros((tm, ROUTER_LANES), F32)
    v0 = None
    for k in range(TOP_K):
        mx = jnp.max(logits, axis=-1, keepdims=True)
        idx = jnp.min(jnp.where(logits == mx, lane, ROUTER_LANES), axis=-1, keepdims=True)
        if k == 0:
            v0 = mx
        ti = jnp.where(lane == k, idx, ti)
        tv = jnp.where(lane == k, jnp.exp(mx - v0), tv)
        logits = jnp.where(lane == idx, NEG_INF, logits)
    ti_ref[0] = ti
    tg_ref[0] = tv / jnp.sum(tv, axis=-1, keepdims=True)


def _merge(x_all, y_hy, y_ga, y_wa, mod, branch_norm, w_out, b_out, norm_ffn, w_router, b_router, nt, ntile):
    bsz, _, d = x_all.shape
    tm = ROW_TILE
    rows = ntile * tm
    wr = jnp.zeros((d, ROUTER_LANES), F32).at[:, :N_EXPERTS].set(w_router)
    wrh = wr.astype(BF16)
    wrl = (wr - wrh.astype(F32)).astype(BF16)
    br = jnp.full((1, ROUTER_LANES), NEG_INF, F32).at[0, :N_EXPERTS].set(b_router)
    tok = lambda w: pl.BlockSpec((1, tm, w), lambda b, i: (b, i, 0))
    const = lambda s: pl.BlockSpec(s, lambda b, i: (0,) * len(s))
    return pl.pallas_call(
        _merge_kernel,
        grid=(bsz, ntile),
        in_specs=[
            tok(d), tok(HY_WIDTH), tok(Q_WIDTH), tok(Q_WIDTH),
            pl.BlockSpec((1, 1, 6, d), lambda b, i: (b, i // nt, 0, 0)),
            const((1, MIX_WIDTH)), const((MIX_WIDTH, d)), const((1, d)), const((1, d)),
            const((d, ROUTER_LANES)), const((d, ROUTER_LANES)), const((1, ROUTER_LANES)),
        ],
        out_specs=[tok(d), tok(d), tok(ROUTER_LANES), tok(ROUTER_LANES)],
        out_shape=[
            jax.ShapeDtypeStruct((bsz, rows, d), F32),
            jax.ShapeDtypeStruct((bsz, rows, d), F32),
            jax.ShapeDtypeStruct((bsz, rows, ROUTER_LANES), I32),
            jax.ShapeDtypeStruct((bsz, rows, ROUTER_LANES), F32),
        ],
        compiler_params=_cp(("arbitrary", "arbitrary")),
        name="merge_router",
    )(x_all, y_hy, y_ga, y_wa, mod, branch_norm.reshape(1, MIX_WIDTH), w_out.astype(BF16), b_out.reshape(1, d),
      norm_ffn.reshape(1, d), wrh, wrl, br)


def _moe_plan(top_idx, tm):
    n_tok = top_idx.shape[0]
    n_assign = n_tok * TOP_K
    n_blocks = -(-n_assign // tm) + N_EXPERTS
    e_flat = top_idx.reshape(n_assign)
    onehot = (e_flat[:, None] == jnp.arange(N_EXPERTS, dtype=I32)[None, :]).astype(I32)
    csum = jnp.cumsum(onehot, axis=0)
    rank = jnp.take_along_axis(csum, e_flat[:, None], axis=1)[:, 0] - 1
    counts = csum[-1]
    padded = (counts + tm - 1) // tm * tm
    ends = jnp.cumsum(padded)
    slot = (ends - padded)[e_flat] + rank
    tok = jnp.arange(n_assign, dtype=I32) // TOP_K
    slot_tok = jnp.zeros((n_blocks * tm,), I32).at[slot].set(tok)
    block_expert = jnp.minimum(jnp.searchsorted(ends, jnp.arange(n_blocks, dtype=I32) * tm, side='right'), N_EXPERTS - 1).astype(I32)
    n_used = (ends[-1] // tm).astype(I32).reshape(1)
    return slot.astype(I32), slot_tok.reshape(n_blocks, 1, tm), block_expert, n_used


def _ffn_kernel(be_ref, nu_ref, tok_ref, tokn_ref, h_hbm, w1_ref, b1_ref, w2_ref, b2_ref, y_ref,
                xbuf, w1b, w2b, sem):
    i = pl.program_id(0)
    tm = xbuf.shape[1]
    n_used = nu_ref[0]
    f = w2_ref.shape[1]

    def row_copy(t, slot, r):
        return pltpu.make_async_copy(h_hbm.at[pl.ds(t, 1), :], xbuf.at[slot, pl.ds(r, 1), :], sem.at[slot])

    def issue(idx_ref, slot):
        def body(r, carry):
            row_copy(idx_ref[0, 0, r], slot, r).start()
            return carry
        lax.fori_loop(0, tm, body, 0, unroll=8)

    @pl.when(i == 0)
    def _():
        issue(tok_ref, 0)

    @pl.when(i + 1 < n_used)
    def _():
        issue(tokn_ref, (i + 1) % 2)

    @pl.when(i < n_used)
    def _():
        slot = i % 2
        pltpu.make_async_copy(h_hbm.at[pl.ds(0, tm), :], xbuf.at[slot], sem.at[slot]).wait()

        @pl.when((i == 0) | (be_ref[i] != be_ref[jnp.maximum(i - 1, 0)]))
        def _():
            w1b[...] = w1_ref[0].astype(BF16)
            w2b[...] = w2_ref[0].astype(BF16)

        a = jnp.dot(xbuf[slot].astype(BF16), w1b[...], preferred_element_type=F32) + b1_ref[0]
        glu = jnp.minimum(a[:, :f], SWIGLU_LIMIT)
        lin = jnp.clip(a[:, f:], -SWIGLU_LIMIT, SWIGLU_LIMIT)
        act = glu / (1.0 + jnp.exp(-SWIGLU_ALPHA * glu)) * (lin + 1.0)
        y_ref[...] = jnp.dot(act.astype(BF16), w2b[...], preferred_element_type=F32) + b2_ref[0]

    @pl.when(i >= n_used)
    def _():
        y_ref[...] = jnp.zeros(y_ref.shape, F32)


def _moe_ffn(h_rows, slot_tok, block_expert, n_used, w1, b1, w2, b2):
    n_blocks, _, tm = slot_tok.shape
    n_exp, d, f2 = w1.shape
    f = w2.shape[1]
    last = n_blocks - 1
    grid_spec = pltpu.PrefetchScalarGridSpec(
        num_scalar_prefetch=2,
        grid=(n_blocks,),
        in_specs=[
            pl.BlockSpec((1, 1, tm), lambda i, be, nu: (i, 0, 0), memory_space=pltpu.SMEM),
            pl.BlockSpec((1, 1, tm), lambda i, be, nu: (jnp.minimum(i + 1, last), 0, 0), memory_space=pltpu.SMEM),
            pl.BlockSpec(memory_space=pl.ANY),
            pl.BlockSpec((1, d, f2), lambda i, be, nu: (be[i], 0, 0)),
            pl.BlockSpec((1, 1, f2), lambda i, be, nu: (be[i], 0, 0)),
            pl.BlockSpec((1, f, d), lambda i, be, nu: (be[i], 0, 0)),
            pl.BlockSpec((1, 1, d), lambda i, be, nu: (be[i], 0, 0)),
        ],
        out_specs=pl.BlockSpec((tm, d), lambda i, be, nu: (i, 0)),
        scratch_shapes=[
            pltpu.VMEM((2, tm, d), F32),
            pltpu.VMEM((d, f2), BF16),
            pltpu.VMEM((f, d), BF16),
            pltpu.SemaphoreType.DMA((2,)),
        ],
    )
    return pl.pallas_call(
        _ffn_kernel,
        grid_spec=grid_spec,
        out_shape=jax.ShapeDtypeStruct((n_blocks * tm, d), F32),
        compiler_params=_cp(("arbitrary",)),
        name="moe_ffn",
    )(block_expert, n_used, slot_tok, slot_tok, h_rows, w1, b1.reshape(n_exp, 1, f2), w2, b2.reshape(n_exp, 1, d))


def _combine_kernel(slot_ref, slotn_ref, y_hbm, x_ref, mod_ref, gate_ref, nf_ref, o_ref, ybuf, sem, *, final):
    b = pl.program_id(0)
    i = pl.program_id(1)
    step = b * pl.num_programs(1) + i
    n_steps = pl.num_programs(0) * pl.num_programs(1)
    tc = x_ref.shape[1]

    def row_copy(s, slot, r):
        return pltpu.make_async_copy(y_hbm.at[pl.ds(s, 1), :], ybuf.at[slot, pl.ds(r, 1), :], sem.at[slot])

    def issue(idx_ref, slot):
        def body(r, carry):
            for k in range(TOP_K):
                row_copy(idx_ref[0, 0, r * TOP_K + k], slot, k * tc + r).start()
            return carry
        lax.fori_loop(0, tc, body, 0, unroll=2)

    @pl.when(step == 0)
    def _():
        issue(slot_ref, 0)

    @pl.when(step + 1 < n_steps)
    def _():
        issue(slotn_ref, (step + 1) % 2)

    slot = step % 2
    pltpu.make_async_copy(y_hbm.at[pl.ds(0, TOP_K * tc), :], ybuf.at[slot], sem.at[slot]).wait()
    gates = gate_ref[0]
    acc = gates[:, 0:1] * ybuf[slot, 0:tc, :]
    for k in range(1, TOP_K):
        acc += gates[:, k:k + 1] * ybuf[slot, k * tc:(k + 1) * tc, :]
    out = x_ref[0] + mod_ref[0, 0, 5:6, :] * acc
    if final:
        out = _rms(out, nf_ref[...])
    o_ref[0] = out


def _combine(y_sorted, slot, x_mid, mod, gates, norm_final, nt, ntile, final):
    bsz, _, d = x_mid.shape
    tc = ROW_TILE
    n_steps = bsz * ntile
    slots = slot.reshape(n_steps, 1, tc * TOP_K)
    tok = lambda w: pl.BlockSpec((1, tc, w), lambda b, i: (b, i, 0))
    return pl.pallas_call(
        functools.partial(_combine_kernel, final=final),
        grid=(bsz, ntile),
        in_specs=[
            pl.BlockSpec((1, 1, tc * TOP_K), lambda b, i: (b * ntile + i, 0, 0), memory_space=pltpu.SMEM),
            pl.BlockSpec((1, 1, tc * TOP_K), lambda b, i: (jnp.minimum(b * ntile + i + 1, n_steps - 1), 0, 0),
                         memory_space=pltpu.SMEM),
            pl.BlockSpec(memory_space=pl.ANY),
            tok(d),
            pl.BlockSpec((1, 1, 6, d), lambda b, i: (b, i // nt, 0, 0)),
            tok(ROUTER_LANES),
            pl.BlockSpec((1, d), lambda b, i: (0, 0)),
        ],
        out_specs=tok(d),
        out_shape=jax.ShapeDtypeStruct((bsz, ntile * tc, d), F32),
        scratch_shapes=[pltpu.VMEM((2, TOP_K * tc, d), F32), pltpu.SemaphoreType.DMA((2,))],
        compiler_params=_cp(("arbitrary", "arbitrary")),
        name="moe_combine",
    )(slots, slots, y_sorted, x_mid, mod, gates, norm_final.reshape(1, d))


def _hyena_filters(n, filt):
    w1, b1, w2, b2, w3, b3, freq, w_out = filt
    t = jnp.linspace(0.0, 1.0, n, dtype=F32)[:, None]
    wpos = (2.0 * math.pi / n) * jnp.arange(n, dtype=F32)[:, None]
    bands = jnp.linspace(1e-4, HY_BANDS - 1, HY_BANDS, dtype=F32)
    z = jnp.concatenate([t, jnp.cos(wpos * bands), -jnp.sin(wpos * bands)], axis=-1)
    hdn = jnp.sin(freq * (z @ w1 + b1))
    hdn = jnp.sin(freq * (hdn @ w2 + b2))
    hdn = jnp.sin(freq * (hdn @ w3 + b3))
    k = (hdn @ w_out).reshape(n, 2, HY_WIDTH)
    max_decay = math.log(HY_TARGET) / HY_FAST_DECAY
    min_decay = math.log(HY_TARGET) / HY_SLOW_DECAY
    deltas = jnp.linspace(min_decay, max_decay, HY_WIDTH, dtype=F32)
    k = k * jnp.exp(-t * jnp.abs(deltas))[:, None, :]
    return k[:, 0], k[:, 1]


def _hyena_mix(u, conv_w, conv_b, filt, skip):
    n = u.shape[1]
    up = jnp.pad(u, ((0, 0), (1, 1), (0, 0)))
    uc = up[:, :-2] * conv_w[0] + up[:, 1:-1] * conv_w[1] + up[:, 2:] * conv_w[2] + conv_b
    x0, x1, v = jnp.split(uc, 3, axis=-1)
    k_fwd, k_bwd = _hyena_filters(n, filt)
    k_full = jnp.concatenate([k_fwd, k_bwd[::-1]], axis=0)
    z = v * x1
    y = jnp.fft.irfft(jnp.fft.rfft(z, n=2 * n, axis=1) * jnp.fft.rfft(k_full, axis=0)[None], n=2 * n, axis=1)[:, :n]
    return x0 * (y + z * skip)


def _rope_tables(n_lat, rows):
    nrow = n_lat // GRID_W
    row = jnp.repeat(jnp.arange(nrow, dtype=F32), GRID_W)
    col = jnp.tile(jnp.arange(GRID_W, dtype=F32), nrow)
    n_freq = HEAD_DIM // 4
    inv_freq = ROPE_THETA ** (-jnp.arange(n_freq, dtype=F32) / n_freq)
    ar = row[:, None] * inv_freq
    ac = col[:, None] * inv_freq
    cos64 = jnp.concatenate([jnp.cos(ar), jnp.cos(ar), jnp.cos(ac), jnp.cos(ac)], axis=-1)
    sin64 = jnp.concatenate([-jnp.sin(ar), jnp.sin(ar), -jnp.sin(ac), jnp.sin(ac)], axis=-1)
    pad = rows - n_lat
    cos_t = jnp.concatenate([jnp.tile(cos64, (1, LANES // HEAD_DIM)), jnp.ones((pad, LANES), F32)], axis=0)
    sin_t = jnp.concatenate([jnp.tile(sin64, (1, LANES // HEAD_DIM)), jnp.zeros((pad, LANES), F32)], axis=0)
    return cos_t, sin_t


def kernel(x, c, ctx, c_ctx, w_ada, b_ada, norm_mix, norm_ffn, w_in, b_in, hy_conv_w, hy_conv_b, hy_filt_w1, hy_filt_b1, hy_filt_w2, hy_filt_b2, hy_filt_w3, hy_filt_b3, hy_filt_freq, hy_filt_out, hy_skip, ga_q_norm, ga_k_norm, wa_sink, branch_norm, w_out, b_out, w_router, b_router, w_mlp1, b_mlp1, w_mlp2, b_mlp2, norm_final):
    bsz, n_lat, d = x.shape
    n_ctx = ctx.shape[1]
    depth = w_ada.shape[0]
    assert n_ctx == ROW_TILE and n_lat % KV_CHUNK == 0 and bsz < 16 and d == D_MODEL
    nt = n_lat // ROW_TILE
    rows = n_lat + n_ctx
    cos_t, sin_t = _rope_tables(n_lat, rows)

    c_all = jnp.zeros((16, d), F32).at[:bsz].set(c).at[bsz].set(c_ctx)
    mod_all = _ada_mod(c_all, w_ada, b_ada)
    x_all = jnp.concatenate([x, ctx], axis=1)

    for l in range(depth):
        last = l == depth - 1
        ntile = nt if last else nt + 1
        m = mod_all[l].reshape(16, 6, d)
        mod = jnp.stack([m[:bsz], jnp.broadcast_to(m[bsz], (bsz, 6, d))], axis=1)

        hy, gq, gk, gv, wq, wk, wv = _inproj(x_all, mod, norm_mix[l], w_in[l], b_in[l], cos_t, sin_t,
                                             ga_q_norm[l], ga_k_norm[l], nt)
        filt = (hy_filt_w1[l], hy_filt_b1[l], hy_filt_w2[l], hy_filt_b2[l], hy_filt_w3[l], hy_filt_b3[l],
                hy_filt_freq[l], hy_filt_out[l])
        y_hy = _hyena_mix(hy[:, :n_lat], hy_conv_w[l], hy_conv_b[l], filt, hy_skip[l])
        if not last:
            yc_hy = _hyena_mix(hy[:, n_lat:], hy_conv_w[l], hy_conv_b[l], filt, hy_skip[l])
            y_hy = jnp.concatenate([y_hy, yc_hy], axis=1)
        y_ga = _global_attention(gq, gk, gv, nt, ntile)
        y_wa = _window_attention(wq, wk, wv, wa_sink[l], nt, ntile)
        x_mid, h2, top_i, top_g = _merge(x_all, y_hy, y_ga, y_wa, mod, branch_norm[l], w_out[l], b_out[l],
                                         norm_ffn[l], w_router[l], b_router[l], nt, ntile)
        n_tok = bsz * ntile * ROW_TILE
        slot, slot_tok, block_expert, n_used = _moe_plan(top_i[..., :TOP_K].reshape(n_tok, TOP_K), MOE_TILE)
        y_sorted = _moe_ffn(h2.reshape(n_tok, d), slot_tok, block_expert, n_used, w_mlp1[l], b_mlp1[l], w_mlp2[l], b_mlp2[l])
        x_all = _combine(y_sorted, slot, x_mid, mod, top_g, norm_final, nt, ntile, last)
    return x_all
```

```python
import functools
import math

import jax
import jax.numpy as jnp
from jax import lax
from jax.experimental import pallas as pl
from jax.experimental.pallas import tpu as pltpu

F32 = jnp.float32
BF16 = jnp.bfloat16
I32 = jnp.int32

D_MODEL = 1024
HEAD_DIM = 64
GRID_W = 64
HY_WIDTH = 256
GA_HEADS = 6
GA_KV_HEADS = 2
WA_HEADS = 6
WA_KV_HEADS = 2
GROUP = GA_HEADS // GA_KV_HEADS
Q_WIDTH = GA_HEADS * HEAD_DIM
KV_WIDTH = GA_KV_HEADS * HEAD_DIM
MIX_WIDTH = HY_WIDTH + 2 * Q_WIDTH
IN_WIDTH = 3 * HY_WIDTH + 2 * (Q_WIDTH + 2 * KV_WIDTH)
WINDOW = 128
ROPE_THETA = 10000.0
ATTN_SCALE = HEAD_DIM ** -0.5
HY_SHORT = 3
HY_EMB = 33
HY_BANDS = (HY_EMB - 1) // 2
HY_FAST_DECAY = 0.3
HY_SLOW_DECAY = 1.5
HY_TARGET = 1e-2
N_EXPERTS = 32
TOP_K = 4
SWIGLU_ALPHA = 1.702
SWIGLU_LIMIT = 7.0
EPS = 1e-6
NEG_INF = -1e30

LANES = 128
VMEM_LIMIT_V7X = 56 * 1024 * 1024

ROW_TILE = 256
KV_CHUNK = 512
MOE_TILE = 512
ROUTER_LANES = LANES


def _cp(sem, vmem=VMEM_LIMIT_V7X):
    return pltpu.CompilerParams(dimension_semantics=sem, vmem_limit_bytes=vmem)


def _rms(v, w):
    return v * lax.rsqrt(jnp.mean(v * v, axis=-1, keepdims=True) + EPS) * w


def _ada_kernel(c_ref, w_ref, b_ref, o_ref):
    c = c_ref[...]
    s = c / (1.0 + jnp.exp(-c))
    hi = s.astype(BF16)
    lo = (s - hi.astype(F32)).astype(BF16)
    w = w_ref[0]
    whi = w.astype(BF16)
    wlo = (w - whi.astype(F32)).astype(BF16)
    acc = jnp.dot(hi, whi, preferred_element_type=F32)
    acc += jnp.dot(lo, whi, preferred_element_type=F32)
    acc += jnp.dot(hi, wlo, preferred_element_type=F32)
    o_ref[0] = acc + b_ref[0]


def _ada_mod(c_all, w_ada, b_ada):
    n_layers, d, n6 = w_ada.shape
    tn = n6 // 4
    return pl.pallas_call(
        _ada_kernel,
        grid=(n_layers, n6 // tn),
        in_specs=[
            pl.BlockSpec((16, d), lambda l, j: (0, 0)),
            pl.BlockSpec((1, d, tn), lambda l, j: (l, 0, j)),
            pl.BlockSpec((1, 1, tn), lambda l, j: (l, 0, j)),
        ],
        out_specs=pl.BlockSpec((1, 16, tn), lambda l, j: (l, 0, j)),
        out_shape=jax.ShapeDtypeStruct((n_layers, 16, n6), F32),
        compiler_params=_cp(("arbitrary", "arbitrary")),
        name="ada_mod",
    )(c_all, w_ada, b_ada.reshape(n_layers, 1, n6))


def _inproj_kernel(x_ref, mod_ref, nw_ref, w_ref, b_ref, cos_ref, sin_ref, qn_ref, kn_ref, g_ref,
                   hy_ref, gq_ref, gk_ref, gv_ref, wq_ref, wk_ref, wv_ref):
    x = x_ref[0]
    tm = x.shape[0]
    shift = mod_ref[0, 0, 0:1, :]
    scale = mod_ref[0, 0, 1:2, :]
    h = _rms(x, nw_ref[...]) * (1.0 + scale) + shift
    p = jnp.dot(h.astype(BF16), w_ref[...], preferred_element_type=F32) + b_ref[...]
    hy_ref[0] = p[:, :3 * HY_WIDTH]

    cos = cos_ref[...]
    sin = sin_ref[...]
    lane = lax.broadcasted_iota(I32, (tm, LANES), 1)
    first = (lane % (HEAD_DIM // 2)) < (HEAD_DIM // 4)
    gmat = g_ref[...]

    def rope(v):
        sw = jnp.where(first, pltpu.roll(v, LANES - HEAD_DIM // 4, 1), pltpu.roll(v, HEAD_DIM // 4, 1))
        return v * cos + sw * sin

    def head_norm(v, w):
        t = v * v
        hi = t.astype(BF16)
        lo = (t - hi.astype(F32)).astype(BF16)
        ms = jnp.dot(hi, gmat, preferred_element_type=F32) + jnp.dot(lo, gmat, preferred_element_type=F32)
        return v * lax.rsqrt(ms + EPS) * w

    o = 3 * HY_WIDTH
    for u in range(Q_WIDTH // LANES):
        v = p[:, o + u * LANES:o + (u + 1) * LANES]
        v = rope(head_norm(v, qn_ref[...]))
        gq_ref[0, :, u * LANES:(u + 1) * LANES] = (v * ATTN_SCALE).astype(BF16)
    o += Q_WIDTH
    gk_ref[0] = rope(head_norm(p[:, o:o + KV_WIDTH], kn_ref[...])).astype(BF16)
    o += KV_WIDTH
    gv_ref[0] = p[:, o:o + KV_WIDTH].astype(BF16)
    o += KV_WIDTH
    for u in range(Q_WIDTH // LANES):
        v = rope(p[:, o + u * LANES:o + (u + 1) * LANES])
        wq_ref[0, :, u * LANES:(u + 1) * LANES] = (v * ATTN_SCALE).astype(BF16)
    o += Q_WIDTH
    wk_ref[0] = rope(p[:, o:o + KV_WIDTH]).astype(BF16)
    o += KV_WIDTH
    wv_ref[0] = p[:, o:o + KV_WIDTH].astype(BF16)


def _inproj(x_all, mod, norm_w, w_in, b_in, cos_t, sin_t, q_norm, k_norm, nt):
    bsz, rows, d = x_all.shape
    tm = ROW_TILE
    ntile = rows // tm
    gmat = jnp.kron(jnp.eye(LANES // HEAD_DIM, dtype=F32), jnp.full((HEAD_DIM, HEAD_DIM), 1.0 / HEAD_DIM, F32)).astype(BF16)
    qn = jnp.tile(q_norm, LANES // HEAD_DIM).reshape(1, LANES)
    kn = jnp.tile(k_norm, LANES // HEAD_DIM).reshape(1, LANES)
    tok = lambda w: pl.BlockSpec((1, tm, w), lambda b, i: (b, i, 0))
    const = lambda s: pl.BlockSpec(s, lambda b, i: (0,) * len(s))
    out_w = (3 * HY_WIDTH, Q_WIDTH, KV_WIDTH, KV_WIDTH, Q_WIDTH, KV_WIDTH, KV_WIDTH)
    out_dt = (F32, BF16, BF16, BF16, BF16, BF16, BF16)
    return pl.pallas_call(
        _inproj_kernel,
        grid=(bsz, ntile),
        in_specs=[
            tok(d),
            pl.BlockSpec((1, 1, 6, d), lambda b, i: (b, i // nt, 0, 0)),
            const((1, d)),
            const((d, IN_WIDTH)),
            const((1, IN_WIDTH)),
            pl.BlockSpec((tm, LANES), lambda b, i: (i, 0)),
            pl.BlockSpec((tm, LANES), lambda b, i: (i, 0)),
            const((1, LANES)),
            const((1, LANES)),
            const((LANES, LANES)),
        ],
        out_specs=[tok(w) for w in out_w],
        out_shape=[jax.ShapeDtypeStruct((bsz, rows, w), dt) for w, dt in zip(out_w, out_dt)],
        compiler_params=_cp(("arbitrary", "arbitrary")),
        name="inproj",
    )(x_all, mod, norm_w.reshape(1, d), w_in.astype(BF16), b_in.reshape(1, IN_WIDTH), cos_t, sin_t, qn, kn, gmat)


def _stack_q(q_ref, qs_ref):
    qf = q_ref[0].astype(F32)
    tq = qf.shape[0]
    for j in range(GA_KV_HEADS):
        for g in range(GROUP):
            h = GROUP * j + g
            blk = qf[:, (h // 2) * LANES:(h // 2 + 1) * LANES]
            if h % 2 != j:
                blk = pltpu.roll(blk, HEAD_DIM, 1)
            qs_ref[j, g * tq:(g + 1) * tq, :] = blk.astype(BF16)


def _unstack_o(norm_fn, o_ref, tq):
    lane = lax.broadcasted_iota(I32, (tq, LANES), 1)
    heads = []
    for j in range(GA_KV_HEADS):
        for g in range(GROUP):
            h = GROUP * j + g
            o = norm_fn(j, g)
            if h % 2 != j:
                o = pltpu.roll(o, HEAD_DIM, 1)
            heads.append(o)
    for u in range(Q_WIDTH // LANES):
        o_ref[0, :, u * LANES:(u + 1) * LANES] = jnp.where(lane < HEAD_DIM, heads[2 * u], heads[2 * u + 1])


def _half_masks(rows):
    lane = lax.broadcasted_iota(I32, (rows, LANES), 1)
    return [lane < HEAD_DIM, lane >= HEAD_DIM]


def _dot_t(a, b):
    return lax.dot_general(a, b, (((1,), (1,)), ((), ())), preferred_element_type=F32)


def _ga_kernel(q_ref, k_ref, v_ref, o_ref, qs_ref, m_ref, acc_ref, *, nt, n_lat):
    i = pl.program_id(1)
    tq = q_ref.shape[1]
    _stack_q(q_ref, qs_ref)
    m_ref[...] = jnp.full(m_ref.shape, NEG_INF, F32)
    acc_ref[...] = jnp.zeros(acc_ref.shape, F32)

    def chunk(kc, vc):
        masks = _half_masks(kc.shape[0])
        for j in range(GA_KV_HEADS):
            km = jnp.where(masks[j], kc, jnp.zeros_like(kc))
            vm = jnp.where(masks[j], vc, jnp.ones_like(vc))
            s = _dot_t(qs_ref[j], km)
            m_old = m_ref[j]
            m_new = jnp.maximum(m_old, jnp.max(s, axis=-1, keepdims=True))
            p = jnp.exp(s - jnp.tile(m_new, (1, s.shape[1] // LANES)))
            acc_ref[j] = jnp.exp(m_old - m_new) * acc_ref[j] + jnp.dot(p.astype(BF16), vm, preferred_element_type=F32)
            m_ref[j] = m_new

    @pl.when(i < nt)
    def _():
        def body(c, carry):
            st = pl.multiple_of(c * KV_CHUNK, KV_CHUNK)
            chunk(k_ref[0, pl.ds(st, KV_CHUNK), :], v_ref[0, pl.ds(st, KV_CHUNK), :])
            return carry
        lax.fori_loop(0, n_lat // KV_CHUNK, body, 0)

    chunk(k_ref[0, n_lat:, :], v_ref[0, n_lat:, :])

    def norm_fn(j, g):
        a = acc_ref[j, g * tq:(g + 1) * tq, :]
        return a / pltpu.roll(a, HEAD_DIM, 1)

    _unstack_o(norm_fn, o_ref, tq)


def _global_attention(gq, gk, gv, nt, ntile):
    bsz, rows, _ = gq.shape
    tq = ROW_TILE
    n_lat = nt * tq
    return pl.pallas_call(
        functools.partial(_ga_kernel, nt=nt, n_lat=n_lat),
        grid=(bsz, ntile),
        in_specs=[
            pl.BlockSpec((1, tq, Q_WIDTH), lambda b, i: (b, i, 0)),
            pl.BlockSpec((1, rows, KV_WIDTH), lambda b, i: (b, 0, 0)),
            pl.BlockSpec((1, rows, KV_WIDTH), lambda b, i: (b, 0, 0)),
        ],
        out_specs=pl.BlockSpec((1, tq, Q_WIDTH), lambda b, i: (b, i, 0)),
        out_shape=jax.ShapeDtypeStruct((bsz, ntile * tq, Q_WIDTH), F32),
        scratch_shapes=[
            pltpu.VMEM((GA_KV_HEADS, GROUP * tq, LANES), BF16),
            pltpu.VMEM((GA_KV_HEADS, GROUP * tq, LANES), F32),
            pltpu.VMEM((GA_KV_HEADS, GROUP * tq, LANES), F32),
        ],
        compiler_params=_cp(("arbitrary", "arbitrary")),
        name="global_attn",
    )(gq, gk, gv)


def _wa_kernel(q_ref, k_ref, v_ref, sink_ref, o_ref, qs_ref, acc_ref, l_ref, *, nt, n_lat):
    i = pl.program_id(1)
    tq = q_ref.shape[1]
    band = tq + 2 * WINDOW
    _stack_q(q_ref, qs_ref)
    st = pl.multiple_of(i * tq, tq)
    kb = k_ref[0, pl.ds(st, band), :]
    vb = v_ref[0, pl.ds(st, band), :]
    kc = k_ref[0, WINDOW + n_lat:WINDOW + n_lat + ROW_TILE, :]
    vc = v_ref[0, WINDOW + n_lat:WINDOW + n_lat + ROW_TILE, :]

    r = lax.broadcasted_iota(I32, (GROUP * tq, band), 0) % tq
    c = lax.broadcasted_iota(I32, (GROUP * tq, band), 1)
    rel = c - WINDOW - r
    kpos = c + (i * tq - WINDOW)
    valid = (jnp.abs(rel) <= WINDOW) & (kpos >= 0) & (kpos < n_lat) & (i < nt)

    mb = _half_masks(band)
    mc = _half_masks(ROW_TILE)
    for j in range(WA_KV_HEADS):
        q = qs_ref[j]
        s_loc = jnp.where(valid, _dot_t(q, jnp.where(mb[j], kb, jnp.zeros_like(kb))), NEG_INF)
        s_ctx = _dot_t(q, jnp.where(mc[j], kc, jnp.zeros_like(kc)))
        sink = jnp.concatenate(
            [jnp.broadcast_to(sink_ref[GROUP * j + g:GROUP * j + g + 1, :], (tq, LANES)) for g in range(GROUP)], axis=0)
        m = jnp.maximum(jnp.maximum(jnp.max(s_loc, axis=-1, keepdims=True), jnp.max(s_ctx, axis=-1, keepdims=True)), sink)
        p_loc = jnp.exp(s_loc - jnp.tile(m, (1, band // LANES))).astype(BF16)
        p_ctx = jnp.exp(s_ctx - jnp.tile(m, (1, ROW_TILE // LANES))).astype(BF16)
        acc_ref[j] = (jnp.dot(p_loc, jnp.where(mb[j], vb, jnp.ones_like(vb)), preferred_element_type=F32)
                      + jnp.dot(p_ctx, jnp.where(mc[j], vc, jnp.ones_like(vc)), preferred_element_type=F32))
        l_ref[j] = jnp.exp(sink - m)

    def norm_fn(j, g):
        a = acc_ref[j, g * tq:(g + 1) * tq, :]
        return a / (pltpu.roll(a, HEAD_DIM, 1) + l_ref[j, g * tq:(g + 1) * tq, :])

    _unstack_o(norm_fn, o_ref, tq)


def _window_attention(wq, wk, wv, sink, nt, ntile):
    bsz, rows, _ = wq.shape
    tq = ROW_TILE
    n_lat = nt * tq
    pad = ((0, 0), (WINDOW, WINDOW), (0, 0))
    kp = jnp.pad(wk, pad)
    vp = jnp.pad(wv, pad)
    sink_b = jnp.zeros((8, LANES), F32).at[:WA_HEADS].set(jnp.broadcast_to(sink[:, None], (WA_HEADS, LANES)))
    return pl.pallas_call(
        functools.partial(_wa_kernel, nt=nt, n_lat=n_lat),
        grid=(bsz, ntile),
        in_specs=[
            pl.BlockSpec((1, tq, Q_WIDTH), lambda b, i: (b, i, 0)),
            pl.BlockSpec((1, rows + 2 * WINDOW, KV_WIDTH), lambda b, i: (b, 0, 0)),
            pl.BlockSpec((1, rows + 2 * WINDOW, KV_WIDTH), lambda b, i: (b, 0, 0)),
            pl.BlockSpec((8, LANES), lambda b, i: (0, 0)),
        ],
        out_specs=pl.BlockSpec((1, tq, Q_WIDTH), lambda b, i: (b, i, 0)),
        out_shape=jax.ShapeDtypeStruct((bsz, ntile * tq, Q_WIDTH), F32),
        scratch_shapes=[
            pltpu.VMEM((WA_KV_HEADS, GROUP * tq, LANES), BF16),
            pltpu.VMEM((WA_KV_HEADS, GROUP * tq, LANES), F32),
            pltpu.VMEM((WA_KV_HEADS, GROUP * tq, LANES), F32),
        ],
        compiler_params=_cp(("arbitrary", "arbitrary")),
        name="window_attn",
    )(wq, kp, vp, sink_b)


def _merge_kernel(x_ref, hy_ref, ga_ref, wa_ref, mod_ref, bn_ref, wo_ref, bo_ref, nf_ref, wrh_ref, wrl_ref, br_ref,
                  xo_ref, h2_ref, ti_ref, tg_ref):
    bn = bn_ref[...]
    g0 = HY_WIDTH
    g1 = HY_WIDTH + Q_WIDTH
    y = jnp.concatenate([_rms(hy_ref[0], bn[:, :g0]), _rms(ga_ref[0], bn[:, g0:g1]), _rms(wa_ref[0], bn[:, g1:])], axis=-1)
    o = jnp.dot(y.astype(BF16), wo_ref[...], preferred_element_type=F32) + bo_ref[...]
    x1 = x_ref[0] + mod_ref[0, 0, 2:3, :] * o
    xo_ref[0] = x1
    h2 = _rms(x1, nf_ref[...]) * (1.0 + mod_ref[0, 0, 4:5, :]) + mod_ref[0, 0, 3:4, :]
    h2_ref[0] = h2

    hi = h2.astype(BF16)
    lo = (h2 - hi.astype(F32)).astype(BF16)
    logits = (jnp.dot(hi, wrh_ref[...], preferred_element_type=F32) + jnp.dot(lo, wrh_ref[...], preferred_element_type=F32)
              + jnp.dot(hi, wrl_ref[...], preferred_element_type=F32)) + br_ref[...]
    tm = logits.shape[0]
    lane = lax.broadcasted_iota(I32, (tm, ROUTER_LANES), 1)
    ti = jnp.zeros((tm, ROUTER_LANES), I32)
    tv = jnp.zeros((tm, ROUTER_LANES), F32)
    v0 = None
    for k in range(TOP_K):
        mx = jnp.max(logits, axis=-1, keepdims=True)
        idx = jnp.min(jnp.where(logits == mx, lane, ROUTER_LANES), axis=-1, keepdims=True)
        if k == 0:
            v0 = mx
        ti = jnp.where(lane == k, idx, ti)
        tv = jnp.where(lane == k, jnp.exp(mx - v0), tv)
        logits = jnp.where(lane == idx, NEG_INF, logits)
    ti_ref[0] = ti
    tg_ref[0] = tv / jnp.sum(tv, axis=-1, keepdims=True)


def _merge(x_all, y_hy, y_ga, y_wa, mod, branch_norm, w_out, b_out, norm_ffn, w_router, b_router, nt, ntile):
    bsz, _, d = x_all.shape
    tm = ROW_TILE
    rows = ntile * tm
    wr = jnp.zeros((d, ROUTER_LANES), F32).at[:, :N_EXPERTS].set(w_router)
    wrh = wr.astype(BF16)
    wrl = (wr - wrh.astype(F32)).astype(BF16)
    br = jnp.full((1, ROUTER_LANES), NEG_INF, F32).at[0, :N_EXPERTS].set(b_router)
    tok = lambda w: pl.BlockSpec((1, tm, w), lambda b, i: (b, i, 0))
    const = lambda s: pl.BlockSpec(s, lambda b, i: (0,) * len(s))
    return pl.pallas_call(
        _merge_kernel,
        grid=(bsz, ntile),
        in_specs=[
            tok(d), tok(HY_WIDTH), tok(Q_WIDTH), tok(Q_WIDTH),
            pl.BlockSpec((1, 1, 6, d), lambda b, i: (b, i // nt, 0, 0)),
            const((1, MIX_WIDTH)), const((MIX_WIDTH, d)), const((1, d)), const((1, d)),
            const((d, ROUTER_LANES)), const((d, ROUTER_LANES)), const((1, ROUTER_LANES)),
        ],
        out_specs=[tok(d), tok(d), tok(ROUTER_LANES), tok(ROUTER_LANES)],
        out_shape=[
            jax.ShapeDtypeStruct((bsz, rows, d), F32),
            jax.ShapeDtypeStruct((bsz, rows, d), F32),
            jax.ShapeDtypeStruct((bsz, rows, ROUTER_LANES), I32),
            jax.ShapeDtypeStruct((bsz, rows, ROUTER_LANES), F32),
        ],
        compiler_params=_cp(("arbitrary", "arbitrary")),
        name="merge_router",
    )(x_all, y_hy, y_ga, y_wa, mod, branch_norm.reshape(1, MIX_WIDTH), w_out.astype(BF16), b_out.reshape(1, d),
      norm_ffn.reshape(1, d), wrh, wrl, br)


def _moe_plan(top_idx, tm):
    n_tok = top_idx.shape[0]
    n_assign = n_tok * TOP_K
    n_blocks = -(-n_assign // tm) + N_EXPERTS
    e_flat = top_idx.reshape(n_assign)
    onehot = (e_flat[:, None] == jnp.arange(N_EXPERTS, dtype=I32)[None, :]).astype(I32)
    csum = jnp.cumsum(onehot, axis=0)
    rank = jnp.take_along_axis(csum, e_flat[:, None], axis=1)[:, 0] - 1
    counts = csum[-1]
    padded = (counts + tm - 1) // tm * tm
    ends = jnp.cumsum(padded)
    slot = (ends - padded)[e_flat] + rank
    tok = jnp.arange(n_assign, dtype=I32) // TOP_K
    slot_tok = jnp.zeros((n_blocks * tm,), I32).at[slot].set(tok)
    block_expert = jnp.minimum(jnp.searchsorted(ends, jnp.arange(n_blocks, dtype=I32) * tm, side='right'), N_EXPERTS - 1).astype(I32)
    n_used = (ends[-1] // tm).astype(I32).reshape(1)
    return slot.astype(I32), slot_tok.reshape(n_blocks, 1, tm), block_expert, n_used


def _ffn_kernel(be_ref, nu_ref, tok_ref, tokn_ref, h_hbm, w1_ref, b1_ref, w2_ref, b2_ref, y_ref,
                xbuf, w1b, w2b, sem):
    i = pl.program_id(0)
    tm = xbuf.shape[1]
    n_used = nu_ref[0]
    f = w2_ref.shape[1]

    def row_copy(t, slot, r):
        return pltpu.make_async_copy(h_hbm.at[pl.ds(t, 1), :], xbuf.at[slot, pl.ds(r, 1), :], sem.at[slot])

    def issue(idx_ref, slot):
        def body(r, carry):
            row_copy(idx_ref[0, 0, r], slot, r).start()
            return carry
        lax.fori_loop(0, tm, body, 0, unroll=8)

    @pl.when(i == 0)
    def _():
        issue(tok_ref, 0)

    @pl.when(i + 1 < n_used)
    def _():
        issue(tokn_ref, (i + 1) % 2)

    @pl.when(i < n_used)
    def _():
        slot = i % 2
        pltpu.make_async_copy(h_hbm.at[pl.ds(0, tm), :], xbuf.at[slot], sem.at[slot]).wait()

        @pl.when((i == 0) | (be_ref[i] != be_ref[jnp.maximum(i - 1, 0)]))
        def _():
            w1b[...] = w1_ref[0].astype(BF16)
            w2b[...] = w2_ref[0].astype(BF16)

        a = jnp.dot(xbuf[slot].astype(BF16), w1b[...], preferred_element_type=F32) + b1_ref[0]
        glu = jnp.minimum(a[:, :f], SWIGLU_LIMIT)
        lin = jnp.clip(a[:, f:], -SWIGLU_LIMIT, SWIGLU_LIMIT)
        act = glu / (1.0 + jnp.exp(-SWIGLU_ALPHA * glu)) * (lin + 1.0)
        y_ref[...] = jnp.dot(act.astype(BF16), w2b[...], preferred_element_type=F32) + b2_ref[0]

    @pl.when(i >= n_used)
    def _():
        y_ref[...] = jnp.zeros(y_ref.shape, F32)


def _moe_ffn(h_rows, slot_tok, block_expert, n_used, w1, b1, w2, b2):
    n_blocks, _, tm = slot_tok.shape
    n_exp, d, f2 = w1.shape
    f = w2.shape[1]
    last = n_blocks - 1
    grid_spec = pltpu.PrefetchScalarGridSpec(
        num_scalar_prefetch=2,
        grid=(n_blocks,),
        in_specs=[
            pl.BlockSpec((1, 1, tm), lambda i, be, nu: (i, 0, 0), memory_space=pltpu.SMEM),
            pl.BlockSpec((1, 1, tm), lambda i, be, nu: (jnp.minimum(i + 1, last), 0, 0), memory_space=pltpu.SMEM),
            pl.BlockSpec(memory_space=pl.ANY),
            pl.BlockSpec((1, d, f2), lambda i, be, nu: (be[i], 0, 0)),
            pl.BlockSpec((1, 1, f2), lambda i, be, nu: (be[i], 0, 0)),
            pl.BlockSpec((1, f, d), lambda i, be, nu: (be[i], 0, 0)),
            pl.BlockSpec((1, 1, d), lambda i, be, nu: (be[i], 0, 0)),
        ],
        out_specs=pl.BlockSpec((tm, d), lambda i, be, nu: (i, 0)),
        scratch_shapes=[
            pltpu.VMEM((2, tm, d), F32),
            pltpu.VMEM((d, f2), BF16),
            pltpu.VMEM((f, d), BF16),
            pltpu.SemaphoreType.DMA((2,)),
        ],
    )
    return pl.pallas_call(
        _ffn_kernel,
        grid_spec=grid_spec,
        out_shape=jax.ShapeDtypeStruct((n_blocks * tm, d), F32),
        compiler_params=_cp(("arbitrary",)),
        name="moe_ffn",
    )(block_expert, n_used, slot_tok, slot_tok, h_rows, w1, b1.reshape(n_exp, 1, f2), w2, b2.reshape(n_exp, 1, d))


def _combine_kernel(slot_ref, slotn_ref, y_hbm, x_ref, mod_ref, gate_ref, nf_ref, o_ref, ybuf, sem, *, final):
    b = pl.program_id(0)
    i = pl.program_id(1)
    step = b * pl.num_programs(1) + i
    n_steps = pl.num_programs(0) * pl.num_programs(1)
    tc = x_ref.shape[1]

    def row_copy(s, slot, r):
        return pltpu.make_async_copy(y_hbm.at[pl.ds(s, 1), :], ybuf.at[slot, pl.ds(r, 1), :], sem.at[slot])

    def issue(idx_ref, slot):
        def body(r, carry):
            for k in range(TOP_K):
                row_copy(idx_ref[0, 0, r * TOP_K + k], slot, k * tc + r).start()
            return carry
        lax.fori_loop(0, tc, body, 0, unroll=2)

    @pl.when(step == 0)
    def _():
        issue(slot_ref, 0)

    @pl.when(step + 1 < n_steps)
    def _():
        issue(slotn_ref, (step + 1) % 2)

    slot = step % 2
    pltpu.make_async_copy(y_hbm.at[pl.ds(0, TOP_K * tc), :], ybuf.at[slot], sem.at[slot]).wait()
    gates = gate_ref[0]
    acc = gates[:, 0:1] * ybuf[slot, 0:tc, :]
    for k in range(1, TOP_K):
        acc += gates[:, k:k + 1] * ybuf[slot, k * tc:(k + 1) * tc, :]
    out = x_ref[0] + mod_ref[0, 0, 5:6, :] * acc
    if final:
        out = _rms(out, nf_ref[...])
    o_ref[0] = out


def _combine(y_sorted, slot, x_mid, mod, gates, norm_final, nt, ntile, final):
    bsz, _, d = x_mid.shape
    tc = ROW_TILE
    n_steps = bsz * ntile
    slots = slot.reshape(n_steps, 1, tc * TOP_K)
    tok = lambda w: pl.BlockSpec((1, tc, w), lambda b, i: (b, i, 0))
    return pl.pallas_call(
        functools.partial(_combine_kernel, final=final),
        grid=(bsz, ntile),
        in_specs=[
            pl.BlockSpec((1, 1, tc * TOP_K), lambda b, i: (b * ntile + i, 0, 0), memory_space=pltpu.SMEM),
            pl.BlockSpec((1, 1, tc * TOP_K), lambda b, i: (jnp.minimum(b * ntile + i + 1, n_steps - 1), 0, 0),
                         memory_space=pltpu.SMEM),
            pl.BlockSpec(memory_space=pl.ANY),
            tok(d),
            pl.BlockSpec((1, 1, 6, d), lambda b, i: (b, i // nt, 0, 0)),
            tok(ROUTER_LANES),
            pl.BlockSpec((1, d), lambda b, i: (0, 0)),
        ],
        out_specs=tok(d),
        out_shape=jax.ShapeDtypeStruct((bsz, ntile * tc, d), F32),
        scratch_shapes=[pltpu.VMEM((2, TOP_K * tc, d), F32), pltpu.SemaphoreType.DMA((2,))],
        compiler_params=_cp(("arbitrary", "arbitrary")),
        name="moe_combine",
    )(slots, slots, y_sorted, x_mid, mod, gates, norm_final.reshape(1, d))


def _split_dot(a, w):
    ah = a.astype(BF16)
    al = (a - ah.astype(F32)).astype(BF16)
    wh = w.astype(BF16)
    wl = (w - wh.astype(F32)).astype(BF16)
    return (jnp.dot(ah, wh, preferred_element_type=F32) + jnp.dot(al, wh, preferred_element_type=F32)
            + jnp.dot(ah, wl, preferred_element_type=F32))


def _filt_kernel(z_ref, w1_ref, b1_ref, w2_ref, b2_ref, w3_ref, b3_ref, fr_ref, wo_ref, dec_ref, o_ref):
    fr = fr_ref[...]
    h = jnp.sin(fr * (_split_dot(z_ref[...], w1_ref[...]) + b1_ref[...]))
    h = jnp.sin(fr * (_split_dot(h, w2_ref[...]) + b2_ref[...]))
    h = jnp.sin(fr * (_split_dot(h, w3_ref[...]) + b3_ref[...]))
    o_ref[...] = _split_dot(h, wo_ref[...]) * dec_ref[...]


def _hyena_filters_pl(n, filt):
    w1, b1, w2, b2, w3, b3, freq, w_out = filt
    fw = w2.shape[0]
    t = jnp.linspace(0.0, 1.0, n, dtype=F32)[:, None]
    wpos = (2.0 * math.pi / n) * jnp.arange(n, dtype=F32)[:, None]
    bands = jnp.linspace(1e-4, HY_BANDS - 1, HY_BANDS, dtype=F32)
    z = jnp.concatenate([t, jnp.cos(wpos * bands), -jnp.sin(wpos * bands), jnp.zeros((n, fw - HY_EMB), F32)], axis=-1)
    w1p = jnp.concatenate([w1, jnp.zeros((fw - HY_EMB, fw), F32)], axis=0)
    max_decay = math.log(HY_TARGET) / HY_FAST_DECAY
    min_decay = math.log(HY_TARGET) / HY_SLOW_DECAY
    deltas = jnp.linspace(min_decay, max_decay, HY_WIDTH, dtype=F32)
    decay = jnp.tile(jnp.exp(-t * jnp.abs(deltas)), (1, 2))
    tm = min(n, 512)
    row = lambda w: pl.BlockSpec((tm, w), lambda i: (i, 0))
    const = lambda s: pl.BlockSpec(s, lambda i: (0, 0))
    return pl.pallas_call(
        _filt_kernel,
        grid=(n // tm,),
        in_specs=[row(fw), const((fw, fw)), const((1, fw)), const((fw, fw)), const((1, fw)), const((fw, fw)), const((1, fw)),
                  const((1, fw)), const((fw, 2 * HY_WIDTH)), row(2 * HY_WIDTH)],
        out_specs=row(2 * HY_WIDTH),
        out_shape=jax.ShapeDtypeStruct((n, 2 * HY_WIDTH), F32),
        compiler_params=_cp(("arbitrary",)),
        name="hyena_filters",
    )(z, w1p, b1.reshape(1, fw), w2, b2.reshape(1, fw), w3, b3.reshape(1, fw), freq.reshape(1, fw), w_out, decay)


def _gate_kernel(u_ref, prev_ref, next_ref, w_ref, b_ref, z_ref, x0_ref, *, nt):
    i = pl.program_id(1)
    u = u_ref[0]
    tm = u.shape[0]
    w = w_ref[...]
    has_prev = jnp.where((i != 0) & (i != nt), 1.0, 0.0)
    has_next = jnp.where((i != nt - 1) & (i != nt), 1.0, 0.0)
    row = lax.broadcasted_iota(I32, u.shape, 0)
    up = jnp.where(row == 0, prev_ref[0, 7:8, :] * has_prev, pltpu.roll(u, 1, 0))
    un = jnp.where(row == tm - 1, next_ref[0, 0:1, :] * has_next, pltpu.roll(u, tm - 1, 0))
    uc = up * w[0:1, :] + u * w[1:2, :] + un * w[2:3, :] + b_ref[...]
    x0_ref[0] = uc[:, :HY_WIDTH]
    z_ref[0] = uc[:, 2 * HY_WIDTH:] * uc[:, HY_WIDTH:2 * HY_WIDTH]


def _hyena_gate(hy, conv_w, conv_b, nt, ntile):
    bsz, rows, w3 = hy.shape
    tm = ROW_TILE
    sub = tm // 8
    last8 = rows // 8 - 1
    out = pl.BlockSpec((1, tm, HY_WIDTH), lambda b, i: (b, i, 0))
    return pl.pallas_call(
        functools.partial(_gate_kernel, nt=nt),
        grid=(bsz, ntile),
        in_specs=[
            pl.BlockSpec((1, tm, w3), lambda b, i: (b, i, 0)),
            pl.BlockSpec((1, 8, w3), lambda b, i: (b, jnp.maximum(i * sub - 1, 0), 0)),
            pl.BlockSpec((1, 8, w3), lambda b, i: (b, jnp.minimum((i + 1) * sub, last8), 0)),
            pl.BlockSpec((HY_SHORT, w3), lambda b, i: (0, 0)),
            pl.BlockSpec((1, w3), lambda b, i: (0, 0)),
        ],
        out_specs=[out, out],
        out_shape=[jax.ShapeDtypeStruct((bsz, ntile * tm, HY_WIDTH), F32)] * 2,
        compiler_params=_cp(("arbitrary", "arbitrary")),
        name="hyena_gate",
    )(hy, hy, hy, conv_w, conv_b.reshape(1, w3))


FFT_N2 = LANES


def _dft_tables(n_fft):
    n1 = n_fft // FFT_N2
    k1 = jnp.arange(n1, dtype=I32)
    a = (2.0 * math.pi / n1) * ((k1[:, None] * k1[None, :]) % n1).astype(F32)
    f1 = jnp.stack([jnp.cos(a), -jnp.sin(a)], axis=1).reshape(2 * n1, n1)
    g = (jnp.stack([jnp.cos(a), -jnp.sin(a)], axis=2).reshape(n1, 2 * n1) / n_fft)
    k2 = jnp.arange(FFT_N2, dtype=I32)
    m = (k2[None, None, :] * (k1[:, None, None] + n1 * k2[None, :, None])) % n_fft
    th = (2.0 * math.pi / n_fft) * m.astype(F32)
    mre, mim = jnp.cos(th), -jnp.sin(th)
    mr = jnp.concatenate([jnp.concatenate([mre, -mim], axis=2), jnp.concatenate([mim, mre], axis=2)], axis=1)
    return f1.astype(BF16), g.astype(BF16), mr.astype(BF16), jnp.swapaxes(mr, 1, 2).astype(BF16)


def _fft1_kernel(x_ref, f_ref, o_ref, *, n1_in):
    f = f_ref[...]
    n_rows = f.shape[0]

    def body(n2, carry):
        xs = x_ref[pl.ds(n2, n1_in, stride=FFT_N2), :].astype(BF16)
        o_ref[pl.ds(n2, n_rows, stride=FFT_N2), :] = jnp.dot(f, xs, preferred_element_type=F32)
        return carry
    lax.fori_loop(0, FFT_N2, body, 0)


def _fft1(x, f1, n1_in):
    bsz, _, ch = x.shape
    n_rows = f1.shape[0]
    return pl.pallas_call(
        functools.partial(_fft1_kernel, n1_in=n1_in),
        grid=(bsz, ch // LANES),
        in_specs=[pl.BlockSpec((None, n1_in * FFT_N2, LANES), lambda b, h: (b, 0, h)),
                  pl.BlockSpec((n_rows, n1_in), lambda b, h: (0, 0))],
        out_specs=pl.BlockSpec((None, n_rows * FFT_N2, LANES), lambda b, h: (b, 0, h)),
        out_shape=jax.ShapeDtypeStruct((bsz, n_rows * FFT_N2, ch), F32),
        compiler_params=_cp(("arbitrary", "arbitrary")),
        name="hyena_fft1",
    )(x, f1[:, :n1_in])


def _spec_kernel(a_ref, mr_ref, o_ref):
    o_ref[0] = jnp.dot(mr_ref[0], a_ref[0].astype(BF16), preferred_element_type=F32)


def _fft_spectrum(a, mr):
    _, rows, ch = a.shape
    blk = 2 * FFT_N2
    return pl.pallas_call(
        _spec_kernel,
        grid=(rows // blk,),
        in_specs=[pl.BlockSpec((1, blk, ch), lambda k: (0, k, 0)), pl.BlockSpec((1, blk, blk), lambda k: (k, 0, 0))],
        out_specs=pl.BlockSpec((1, blk, ch), lambda k: (0, k, 0)),
        out_shape=jax.ShapeDtypeStruct((1, rows, ch), F32),
        compiler_params=_cp(("arbitrary",)),
        name="hyena_filter_spectrum",
    )(a, mr)


def _mid_kernel(a_ref, mr_ref, mrt_ref, ks_ref, o_ref):
    x = jnp.dot(mr_ref[0], a_ref[0].astype(BF16), preferred_element_type=F32)
    xre, xim = x[:FFT_N2], x[FFT_N2:]
    kre, kim = ks_ref[0, :FFT_N2, :], ks_ref[0, FFT_N2:, :]
    y = jnp.concatenate([xre * kre - xim * kim, xre * kim + xim * kre], axis=0)
    o_ref[0] = jnp.dot(mrt_ref[0], y.astype(BF16), preferred_element_type=F32)


def _fft_mid(a, mr, mrt, kspec):
    bsz, rows, ch = a.shape
    blk = 2 * FFT_N2
    return pl.pallas_call(
        _mid_kernel,
        grid=(rows // blk, bsz),
        in_specs=[pl.BlockSpec((1, blk, ch), lambda k, b: (b, k, 0)),
                  pl.BlockSpec((1, blk, blk), lambda k, b: (k, 0, 0)),
                  pl.BlockSpec((1, blk, blk), lambda k, b: (k, 0, 0)),
                  pl.BlockSpec((1, blk, ch), lambda k, b: (0, k, 0))],
        out_specs=pl.BlockSpec((1, blk, ch), lambda k, b: (b, k, 0)),
        out_shape=jax.ShapeDtypeStruct((bsz, rows, ch), F32),
        compiler_params=_cp(("arbitrary", "arbitrary")),
        name="hyena_fft_mid",
    )(a, mr, mrt, kspec)


def _ifft1_kernel(b_ref, g_ref, z_ref, x0_ref, skip_ref, o_ref, *, n1_out):
    g = g_ref[...]
    n_rows = g.shape[1]
    skip = skip_ref[...]

    def body(n2, carry):
        bs = b_ref[pl.ds(n2, n_rows, stride=FFT_N2), :].astype(BF16)
        y = jnp.dot(g, bs, preferred_element_type=F32)
        rows = pl.ds(n2, n1_out, stride=FFT_N2)
        zs = z_ref[rows, :]
        o_ref[rows, :] = x0_ref[rows, :] * (y + zs * skip)
        return carry
    lax.fori_loop(0, FFT_N2, body, 0)


def _ifft1(bmat, g, z, x0, skip, n1_out):
    bsz, rows, ch = bmat.shape
    n = n1_out * FFT_N2
    seq = pl.BlockSpec((None, n, LANES), lambda b, h: (b, 0, h))
    return pl.pallas_call(
        functools.partial(_ifft1_kernel, n1_out=n1_out),
        grid=(bsz, ch // LANES),
        in_specs=[pl.BlockSpec((None, rows, LANES), lambda b, h: (b, 0, h)),
                  pl.BlockSpec((n1_out, g.shape[1]), lambda b, h: (0, 0)),
                  seq, seq, pl.BlockSpec((1, LANES), lambda b, h: (0, h))],
        out_specs=seq,
        out_shape=jax.ShapeDtypeStruct((bsz, n, ch), F32),
        compiler_params=_cp(("arbitrary", "arbitrary")),
        name="hyena_ifft1",
    )(bmat, g[:n1_out], z, x0, skip.reshape(1, ch))


def _hyena_fftconv(z, x0, k2, skip):
    n = k2.shape[0]
    n_fft = 2 * n
    f1, g, mr, mrt = _dft_tables(n_fft)
    k_full = jnp.concatenate([k2[:, :HY_WIDTH], k2[::-1, HY_WIDTH:]], axis=0)[None]
    kspec = _fft_spectrum(_fft1(k_full, f1, n_fft // FFT_N2), mr)
    a = _fft1(z, f1, n // FFT_N2)
    bmat = _fft_mid(a, mr, mrt, kspec)
    return _ifft1(bmat, g, z, x0, skip, n // FFT_N2)


def _ctxconv_kernel(z_ref, x0_ref, kf_ref, fd_ref, gd_ref, skip_ref, o_ref, ks_ref):
    n = z_ref.shape[1]
    half = fd_ref.shape[0] // 2

    @pl.when(pl.program_id(0) == 0)
    def _():
        ks_ref[...] = jnp.dot(fd_ref[...], kf_ref[...].astype(BF16), preferred_element_type=F32)

    z = z_ref[0]
    x = jnp.dot(fd_ref[:, :n], z.astype(BF16), preferred_element_type=F32)
    xre, xim = x[:half], x[half:]
    kre, kim = ks_ref[:half, :], ks_ref[half:, :]
    y = jnp.concatenate([xre * kre - xim * kim, xre * kim + xim * kre], axis=0)
    conv = jnp.dot(gd_ref[...], y.astype(BF16), preferred_element_type=F32)
    o_ref[0] = x0_ref[0] * (conv + z * skip_ref[...])


def _hyena_ctxconv(z, x0, k2, skip, row0):
    n = k2.shape[0]
    bsz, _, ch = z.shape
    n_fft = 2 * n
    kk = jnp.arange(n_fft, dtype=I32)
    th = (2.0 * math.pi / n_fft) * ((kk[:, None] * kk[None, :]) % n_fft).astype(F32)
    fd = jnp.concatenate([jnp.cos(th), -jnp.sin(th)], axis=0).astype(BF16)
    gd = (jnp.concatenate([jnp.cos(th[:n]), -jnp.sin(th[:n])], axis=1) / n_fft).astype(BF16)
    k_full = jnp.concatenate([k2[:, :HY_WIDTH], k2[::-1, HY_WIDTH:]], axis=0)
    blk = row0 // n
    seq = pl.BlockSpec((1, n, ch), lambda b: (b, blk, 0))
    const = lambda s: pl.BlockSpec(s, lambda b: (0, 0))
    return pl.pallas_call(
        _ctxconv_kernel,
        grid=(bsz,),
        in_specs=[seq, seq, const((n_fft, ch)), const((2 * n_fft, n_fft)), const((n, 2 * n_fft)), const((1, ch))],
        out_specs=pl.BlockSpec((1, n, ch), lambda b: (b, 0, 0)),
        out_shape=jax.ShapeDtypeStruct((bsz, n, ch), F32),
        scratch_shapes=[pltpu.VMEM((2 * n_fft, ch), F32)],
        compiler_params=_cp(("arbitrary",)),
        name="hyena_ctxconv",
    )(z, x0, k_full, fd, gd, skip.reshape(1, ch))


def _rope_tables(n_lat, rows):
    nrow = n_lat // GRID_W
    row = jnp.repeat(jnp.arange(nrow, dtype=F32), GRID_W)
    col = jnp.tile(jnp.arange(GRID_W, dtype=F32), nrow)
    n_freq = HEAD_DIM // 4
    inv_freq = ROPE_THETA ** (-jnp.arange(n_freq, dtype=F32) / n_freq)
    ar = row[:, None] * inv_freq
    ac = col[:, None] * inv_freq
    cos64 = jnp.concatenate([jnp.cos(ar), jnp.cos(ar), jnp.cos(ac), jnp.cos(ac)], axis=-1)
    sin64 = jnp.concatenate([-jnp.sin(ar), jnp.sin(ar), -jnp.sin(ac), jnp.sin(ac)], axis=-1)
    pad = rows - n_lat
    cos_t = jnp.concatenate([jnp.tile(cos64, (1, LANES // HEAD_DIM)), jnp.ones((pad, LANES), F32)], axis=0)
    sin_t = jnp.concatenate([jnp.tile(sin64, (1, LANES // HEAD_DIM)), jnp.zeros((pad, LANES), F32)], axis=0)
    return cos_t, sin_t


def kernel(x, c, ctx, c_ctx, w_ada, b_ada, norm_mix, norm_ffn, w_in, b_in, hy_conv_w, hy_conv_b, hy_filt_w1, hy_filt_b1, hy_filt_w2, hy_filt_b2, hy_filt_w3, hy_filt_b3, hy_filt_freq, hy_filt_out, hy_skip, ga_q_norm, ga_k_norm, wa_sink, branch_norm, w_out, b_out, w_router, b_router, w_mlp1, b_mlp1, w_mlp2, b_mlp2, norm_final):
    bsz, n_lat, d = x.shape
    n_ctx = ctx.shape[1]
    depth = w_ada.shape[0]
    assert n_ctx == ROW_TILE and n_lat % KV_CHUNK == 0 and bsz < 16 and d == D_MODEL
    nt = n_lat // ROW_TILE
    rows = n_lat + n_ctx
    cos_t, sin_t = _rope_tables(n_lat, rows)

    c_all = jnp.zeros((16, d), F32).at[:bsz].set(c).at[bsz].set(c_ctx)
    mod_all = _ada_mod(c_all, w_ada, b_ada)
    x_all = jnp.concatenate([x, ctx], axis=1)

    for l in range(depth):
        last = l == depth - 1
        ntile = nt if last else nt + 1
        m = mod_all[l].reshape(16, 6, d)
        mod = jnp.stack([m[:bsz], jnp.broadcast_to(m[bsz], (bsz, 6, d))], axis=1)

        hy, gq, gk, gv, wq, wk, wv = _inproj(x_all, mod, norm_mix[l], w_in[l], b_in[l], cos_t, sin_t,
                                             ga_q_norm[l], ga_k_norm[l], nt)
        filt = (hy_filt_w1[l], hy_filt_b1[l], hy_filt_w2[l], hy_filt_b2[l], hy_filt_w3[l], hy_filt_b3[l],
                hy_filt_freq[l], hy_filt_out[l])
        z, x0 = _hyena_gate(hy, hy_conv_w[l], hy_conv_b[l], nt, ntile)
        y_hy = _hyena_fftconv(z, x0, _hyena_filters_pl(n_lat, filt), hy_skip[l])
        if not last:
            yc_hy = _hyena_ctxconv(z, x0, _hyena_filters_pl(n_ctx, filt), hy_skip[l], n_lat)
            y_hy = jnp.concatenate([y_hy, yc_hy], axis=1)
        y_ga = _global_attention(gq, gk, gv, nt, ntile)
        y_wa = _window_attention(wq, wk, wv, wa_sink[l], nt, ntile)
        x_mid, h2, top_i, top_g = _merge(x_all, y_hy, y_ga, y_wa, mod, branch_norm[l], w_out[l], b_out[l],
                                         norm_ffn[l], w_router[l], b_router[l], nt, ntile)
        n_tok = bsz * ntile * ROW_TILE
        slot, slot_tok, block_expert, n_used = _moe_plan(top_i[..., :TOP_K].reshape(n_tok, TOP_K), MOE_TILE)
        y_sorted = _moe_ffn(h2.reshape(n_tok, d), slot_tok, block_expert, n_used, w_mlp1[l], b_mlp1[l], w_mlp2[l], b_mlp2[l])
        x_all = _combine(y_sorted, slot, x_mid, mod, top_g, norm_final, nt, ntile, last)
    return x_all
```

```python
import functools
import math

import jax
import jax.numpy as jnp
from jax import lax
from jax.experimental import pallas as pl
from jax.experimental.pallas import tpu as pltpu

F32 = jnp.float32
BF16 = jnp.bfloat16
I32 = jnp.int32

D_MODEL = 1024
HEAD_DIM = 64
GRID_W = 64
HY_WIDTH = 256
GA_HEADS = 6
GA_KV_HEADS = 2
WA_HEADS = 6
WA_KV_HEADS = 2
GROUP = GA_HEADS // GA_KV_HEADS
Q_WIDTH = GA_HEADS * HEAD_DIM
KV_WIDTH = GA_KV_HEADS * HEAD_DIM
MIX_WIDTH = HY_WIDTH + 2 * Q_WIDTH
IN_WIDTH = 3 * HY_WIDTH + 2 * (Q_WIDTH + 2 * KV_WIDTH)
WINDOW = 128
ROPE_THETA = 10000.0
ATTN_SCALE = HEAD_DIM ** -0.5
HY_SHORT = 3
HY_EMB = 33
HY_BANDS = (HY_EMB - 1) // 2
HY_FAST_DECAY = 0.3
HY_SLOW_DECAY = 1.5
HY_TARGET = 1e-2
N_EXPERTS = 32
TOP_K = 4
SWIGLU_ALPHA = 1.702
SWIGLU_LIMIT = 7.0
EPS = 1e-6
NEG_INF = -1e30

LANES = 128
VMEM_LIMIT_V7X = 56 * 1024 * 1024

ROW_TILE = 256
KV_CHUNK = 512
MOE_TILE = 512
ROUTER_LANES = LANES


def _cp(sem, vmem=VMEM_LIMIT_V7X):
    return pltpu.CompilerParams(dimension_semantics=sem, vmem_limit_bytes=vmem)


def _rms(v, w):
    return v * lax.rsqrt(jnp.mean(v * v, axis=-1, keepdims=True) + EPS) * w


def _ada_kernel(c_ref, w_ref, b_ref, o_ref):
    c = c_ref[...]
    s = c / (1.0 + jnp.exp(-c))
    hi = s.astype(BF16)
    lo = (s - hi.astype(F32)).astype(BF16)
    w = w_ref[0]
    whi = w.astype(BF16)
    wlo = (w - whi.astype(F32)).astype(BF16)
    acc = jnp.dot(hi, whi, preferred_element_type=F32)
    acc += jnp.dot(lo, whi, preferred_element_type=F32)
    acc += jnp.dot(hi, wlo, preferred_element_type=F32)
    o_ref[0] = acc + b_ref[0]


def _ada_mod(c_all, w_ada, b_ada):
    n_layers, d, n6 = w_ada.shape
    tn = n6 // 4
    return pl.pallas_call(
        _ada_kernel,
        grid=(n_layers, n6 // tn),
        in_specs=[
            pl.BlockSpec((16, d), lambda l, j: (0, 0)),
            pl.BlockSpec((1, d, tn), lambda l, j: (l, 0, j)),
            pl.BlockSpec((1, 1, tn), lambda l, j: (l, 0, j)),
        ],
        out_specs=pl.BlockSpec((1, 16, tn), lambda l, j: (l, 0, j)),
        out_shape=jax.ShapeDtypeStruct((n_layers, 16, n6), F32),
        compiler_params=_cp(("arbitrary", "arbitrary")),
        name="ada_mod",
    )(c_all, w_ada, b_ada.reshape(n_layers, 1, n6))


def _inproj_kernel(x_ref, mod_ref, nw_ref, w_ref, b_ref, cos_ref, sin_ref, qn_ref, kn_ref, g_ref,
                   hy_ref, gq_ref, gk_ref, gv_ref, wq_ref, wk_ref, wv_ref):
    x = x_ref[0]
    tm = x.shape[0]
    shift = mod_ref[0, 0, 0:1, :]
    scale = mod_ref[0, 0, 1:2, :]
    h = _rms(x, nw_ref[...]) * (1.0 + scale) + shift
    p = jnp.dot(h.astype(BF16), w_ref[...], preferred_element_type=F32) + b_ref[...]
    hy_ref[0] = p[:, :3 * HY_WIDTH]

    cos = cos_ref[...]
    sin = sin_ref[...]
    lane = lax.broadcasted_iota(I32, (tm, LANES), 1)
    first = (lane % (HEAD_DIM // 2)) < (HEAD_DIM // 4)
    gmat = g_ref[...]

    def rope(v):
        sw = jnp.where(first, pltpu.roll(v, LANES - HEAD_DIM // 4, 1), pltpu.roll(v, HEAD_DIM // 4, 1))
        return v * cos + sw * sin

    def head_norm(v, w):
        t = v * v
        hi = t.astype(BF16)
        lo = (t - hi.astype(F32)).astype(BF16)
        ms = jnp.dot(hi, gmat, preferred_element_type=F32) + jnp.dot(lo, gmat, preferred_element_type=F32)
        return v * lax.rsqrt(ms + EPS) * w

    o = 3 * HY_WIDTH
    for u in range(Q_WIDTH // LANES):
        v = p[:, o + u * LANES:o + (u + 1) * LANES]
        v = rope(head_norm(v, qn_ref[...]))
        gq_ref[0, :, u * LANES:(u + 1) * LANES] = (v * ATTN_SCALE).astype(BF16)
    o += Q_WIDTH
    gk_ref[0] = rope(head_norm(p[:, o:o + KV_WIDTH], kn_ref[...])).astype(BF16)
    o += KV_WIDTH
    gv_ref[0] = p[:, o:o + KV_WIDTH].astype(BF16)
    o += KV_WIDTH
    for u in range(Q_WIDTH // LANES):
        v = rope(p[:, o + u * LANES:o + (u + 1) * LANES])
        wq_ref[0, :, u * LANES:(u + 1) * LANES] = (v * ATTN_SCALE).astype(BF16)
    o += Q_WIDTH
    wk_ref[0] = rope(p[:, o:o + KV_WIDTH]).astype(BF16)
    o += KV_WIDTH
    wv_ref[0] = p[:, o:o + KV_WIDTH].astype(BF16)


def _inproj(x_all, mod, norm_w, w_in, b_in, cos_t, sin_t, q_norm, k_norm, nt):
    bsz, rows, d = x_all.shape
    tm = ROW_TILE
    ntile = rows // tm
    gmat = jnp.kron(jnp.eye(LANES // HEAD_DIM, dtype=F32), jnp.full((HEAD_DIM, HEAD_DIM), 1.0 / HEAD_DIM, F32)).astype(BF16)
    qn = jnp.tile(q_norm, LANES // HEAD_DIM).reshape(1, LANES)
    kn = jnp.tile(k_norm, LANES // HEAD_DIM).reshape(1, LANES)
    tok = lambda w: pl.BlockSpec((1, tm, w), lambda b, i: (b, i, 0))
    const = lambda s: pl.BlockSpec(s, lambda b, i: (0,) * len(s))
    out_w = (3 * HY_WIDTH, Q_WIDTH, KV_WIDTH, KV_WIDTH, Q_WIDTH, KV_WIDTH, KV_WIDTH)
    out_dt = (F32, BF16, BF16, BF16, BF16, BF16, BF16)
    return pl.pallas_call(
        _inproj_kernel,
        grid=(bsz, ntile),
        in_specs=[
            tok(d),
            pl.BlockSpec((1, 1, 6, d), lambda b, i: (b, i // nt, 0, 0)),
            const((1, d)),
            const((d, IN_WIDTH)),
            const((1, IN_WIDTH)),
            pl.BlockSpec((tm, LANES), lambda b, i: (i, 0)),
            pl.BlockSpec((tm, LANES), lambda b, i: (i, 0)),
            const((1, LANES)),
            const((1, LANES)),
            const((LANES, LANES)),
        ],
        out_specs=[tok(w) for w in out_w],
        out_shape=[jax.ShapeDtypeStruct((bsz, rows, w), dt) for w, dt in zip(out_w, out_dt)],
        compiler_params=_cp(("arbitrary", "arbitrary")),
        name="inproj",
    )(x_all, mod, norm_w.reshape(1, d), w_in.astype(BF16), b_in.reshape(1, IN_WIDTH), cos_t, sin_t, qn, kn, gmat)


def _stack_q(q_ref, qs_ref):
    qf = q_ref[0].astype(F32)
    tq = qf.shape[0]
    for j in range(GA_KV_HEADS):
        for g in range(GROUP):
            h = GROUP * j + g
            blk = qf[:, (h // 2) * LANES:(h // 2 + 1) * LANES]
            if h % 2 != j:
                blk = pltpu.roll(blk, HEAD_DIM, 1)
            qs_ref[j, g * tq:(g + 1) * tq, :] = blk.astype(BF16)


def _unstack_o(norm_fn, o_ref, tq):
    lane = lax.broadcasted_iota(I32, (tq, LANES), 1)
    heads = []
    for j in range(GA_KV_HEADS):
        for g in range(GROUP):
            h = GROUP * j + g
            o = norm_fn(j, g)
            if h % 2 != j:
                o = pltpu.roll(o, HEAD_DIM, 1)
            heads.append(o)
    for u in range(Q_WIDTH // LANES):
        o_ref[0, :, u * LANES:(u + 1) * LANES] = jnp.where(lane < HEAD_DIM, heads[2 * u], heads[2 * u + 1])


def _half_masks(rows):
    lane = lax.broadcasted_iota(I32, (rows, LANES), 1)
    return [lane < HEAD_DIM, lane >= HEAD_DIM]


def _dot_t(a, b):
    return lax.dot_general(a, b, (((1,), (1,)), ((), ())), preferred_element_type=F32)


def _ga_kernel(q_ref, k_ref, v_ref, o_ref, qs_ref, m_ref, acc_ref, *, nt, n_lat):
    i = pl.program_id(1)
    tq = q_ref.shape[1]
    _stack_q(q_ref, qs_ref)
    m_ref[...] = jnp.full(m_ref.shape, NEG_INF, F32)
    acc_ref[...] = jnp.zeros(acc_ref.shape, F32)

    def chunk(kc, vc):
        masks = _half_masks(kc.shape[0])
        for j in range(GA_KV_HEADS):
            km = jnp.where(masks[j], kc, jnp.zeros_like(kc))
            vm = jnp.where(masks[j], vc, jnp.ones_like(vc))
            s = _dot_t(qs_ref[j], km)
            m_old = m_ref[j]
            m_new = jnp.maximum(m_old, jnp.max(s, axis=-1, keepdims=True))
            p = jnp.exp(s - jnp.tile(m_new, (1, s.shape[1] // LANES)))
            acc_ref[j] = jnp.exp(m_old - m_new) * acc_ref[j] + jnp.dot(p.astype(BF16), vm, preferred_element_type=F32)
            m_ref[j] = m_new

    @pl.when(i < nt)
    def _():
        def body(c, carry):
            st = pl.multiple_of(c * KV_CHUNK, KV_CHUNK)
            chunk(k_ref[0, pl.ds(st, KV_CHUNK), :], v_ref[0, pl.ds(st, KV_CHUNK), :])
            return carry
        lax.fori_loop(0, n_lat // KV_CHUNK, body, 0)

    chunk(k_ref[0, n_lat:, :], v_ref[0, n_lat:, :])

    def norm_fn(j, g):
        a = acc_ref[j, g * tq:(g + 1) * tq, :]
        return a / pltpu.roll(a, HEAD_DIM, 1)

    _unstack_o(norm_fn, o_ref, tq)


def _global_attention(gq, gk, gv, nt, ntile):
    bsz, rows, _ = gq.shape
    tq = ROW_TILE
    n_lat = nt * tq
    return pl.pallas_call(
        functools.partial(_ga_kernel, nt=nt, n_lat=n_lat),
        grid=(bsz, ntile),
        in_specs=[
            pl.BlockSpec((1, tq, Q_WIDTH), lambda b, i: (b, i, 0)),
            pl.BlockSpec((1, rows, KV_WIDTH), lambda b, i: (b, 0, 0)),
            pl.BlockSpec((1, rows, KV_WIDTH), lambda b, i: (b, 0, 0)),
        ],
        out_specs=pl.BlockSpec((1, tq, Q_WIDTH), lambda b, i: (b, i, 0)),
        out_shape=jax.ShapeDtypeStruct((bsz, ntile * tq, Q_WIDTH), F32),
        scratch_shapes=[
            pltpu.VMEM((GA_KV_HEADS, GROUP * tq, LANES), BF16),
            pltpu.VMEM((GA_KV_HEADS, GROUP * tq, LANES), F32),
            pltpu.VMEM((GA_KV_HEADS, GROUP * tq, LANES), F32),
        ],
        compiler_params=_cp(("arbitrary", "arbitrary")),
        name="global_attn",
    )(gq, gk, gv)


def _wa_kernel(q_ref, k_ref, v_ref, sink_ref, o_ref, qs_ref, acc_ref, l_ref, *, nt, n_lat):
    i = pl.program_id(1)
    tq = q_ref.shape[1]
    band = tq + 2 * WINDOW
    _stack_q(q_ref, qs_ref)
    st = pl.multiple_of(i * tq, tq)
    kb = k_ref[0, pl.ds(st, band), :]
    vb = v_ref[0, pl.ds(st, band), :]
    kc = k_ref[0, WINDOW + n_lat:WINDOW + n_lat + ROW_TILE, :]
    vc = v_ref[0, WINDOW + n_lat:WINDOW + n_lat + ROW_TILE, :]

    r = lax.broadcasted_iota(I32, (GROUP * tq, band), 0) % tq
    c = lax.broadcasted_iota(I32, (GROUP * tq, band), 1)
    rel = c - WINDOW - r
    kpos = c + (i * tq - WINDOW)
    valid = (jnp.abs(rel) <= WINDOW) & (kpos >= 0) & (kpos < n_lat) & (i < nt)

    mb = _half_masks(band)
    mc = _half_masks(ROW_TILE)
    for j in range(WA_KV_HEADS):
        q = qs_ref[j]
        s_loc = jnp.where(valid, _dot_t(q, jnp.where(mb[j], kb, jnp.zeros_like(kb))), NEG_INF)
        s_ctx = _dot_t(q, jnp.where(mc[j], kc, jnp.zeros_like(kc)))
        sink = jnp.concatenate(
            [jnp.broadcast_to(sink_ref[GROUP * j + g:GROUP * j + g + 1, :], (tq, LANES)) for g in range(GROUP)], axis=0)
        m = jnp.maximum(jnp.maximum(jnp.max(s_loc, axis=-1, keepdims=True), jnp.max(s_ctx, axis=-1, keepdims=True)), sink)
        p_loc = jnp.exp(s_loc - jnp.tile(m, (1, band // LANES))).astype(BF16)
        p_ctx = jnp.exp(s_ctx - jnp.tile(m, (1, ROW_TILE // LANES))).astype(BF16)
        acc_ref[j] = (jnp.dot(p_loc, jnp.where(mb[j], vb, jnp.ones_like(vb)), preferred_element_type=F32)
                      + jnp.dot(p_ctx, jnp.where(mc[j], vc, jnp.ones_like(vc)), preferred_element_type=F32))
        l_ref[j] = jnp.exp(sink - m)

    def norm_fn(j, g):
        a = acc_ref[j, g * tq:(g + 1) * tq, :]
        return a / (pltpu.roll(a, HEAD_DIM, 1) + l_ref[j, g * tq:(g + 1) * tq, :])

    _unstack_o(norm_fn, o_ref, tq)


def _window_attention(wq, wk, wv, sink, nt, ntile):
    bsz, rows, _ = wq.shape
    tq = ROW_TILE
    n_lat = nt * tq
    pad = ((0, 0), (WINDOW, WINDOW), (0, 0))
    kp = jnp.pad(wk, pad)
    vp = jnp.pad(wv, pad)
    sink_b = jnp.zeros((8, LANES), F32).at[:WA_HEADS].set(jnp.broadcast_to(sink[:, None], (WA_HEADS, LANES)))
    return pl.pallas_call(
        functools.partial(_wa_kernel, nt=nt, n_lat=n_lat),
        grid=(bsz, ntile),
        in_specs=[
            pl.BlockSpec((1, tq, Q_WIDTH), lambda b, i: (b, i, 0)),
            pl.BlockSpec((1, rows + 2 * WINDOW, KV_WIDTH), lambda b, i: (b, 0, 0)),
            pl.BlockSpec((1, rows + 2 * WINDOW, KV_WIDTH), lambda b, i: (b, 0, 0)),
            pl.BlockSpec((8, LANES), lambda b, i: (0, 0)),
        ],
        out_specs=pl.BlockSpec((1, tq, Q_WIDTH), lambda b, i: (b, i, 0)),
        out_shape=jax.ShapeDtypeStruct((bsz, ntile * tq, Q_WIDTH), F32),
        scratch_shapes=[
            pltpu.VMEM((WA_KV_HEADS, GROUP * tq, LANES), BF16),
            pltpu.VMEM((WA_KV_HEADS, GROUP * tq, LANES), F32),
            pltpu.VMEM((WA_KV_HEADS, GROUP * tq, LANES), F32),
        ],
        compiler_params=_cp(("arbitrary", "arbitrary")),
        name="window_attn",
    )(wq, kp, vp, sink_b)


def _merge_kernel(x_ref, hy_ref, ga_ref, wa_ref, mod_ref, bn_ref, wo_ref, bo_ref, nf_ref, wrh_ref, wrl_ref, br_ref,
                  xo_ref, h2_ref, ti_ref, tg_ref):
    bn = bn_ref[...]
    g0 = HY_WIDTH
    g1 = HY_WIDTH + Q_WIDTH
    y = jnp.concatenate([_rms(hy_ref[0], bn[:, :g0]), _rms(ga_ref[0], bn[:, g0:g1]), _rms(wa_ref[0], bn[:, g1:])], axis=-1)
    o = jnp.dot(y.astype(BF16), wo_ref[...], preferred_element_type=F32) + bo_ref[...]
    x1 = x_ref[0] + mod_ref[0, 0, 2:3, :] * o
    xo_ref[0] = x1
    h2 = _rms(x1, nf_ref[...]) * (1.0 + mod_ref[0, 0, 4:5, :]) + mod_ref[0, 0, 3:4, :]
    h2_ref[0] = h2

    hi = h2.astype(BF16)
    lo = (h2 - hi.astype(F32)).astype(BF16)
    logits = (jnp.dot(hi, wrh_ref[...], preferred_element_type=F32) + jnp.dot(lo, wrh_ref[...], preferred_element_type=F32)
              + jnp.dot(hi, wrl_ref[...], preferred_element_type=F32)) + br_ref[...]
    tm = logits.shape[0]
    lane = lax.broadcasted_iota(I32, (tm, ROUTER_LANES), 1)
    ti = jnp.zeros((tm, ROUTER_LANES), I32)
    tv = jnp.zeros((tm, ROUTER_LANES), F32)
    v0 = None
    for k in range(TOP_K):
        mx = jnp.max(logits, axis=-1, keepdims=True)
        idx = jnp.min(jnp.where(logits == mx, lane, ROUTER_LANES), axis=-1, keepdims=True)
        if k == 0:
            v0 = mx
        ti = jnp.where(lane == k, idx, ti)
        tv = jnp.where(lane == k, jnp.exp(mx - v0), tv)
        logits = jnp.where(lane == idx, NEG_INF, logits)
    ti_ref[0] = ti
    tg_ref[0] = tv / jnp.sum(tv, axis=-1, keepdims=True)


def _merge(x_all, y_hy, y_ga, y_wa, mod, branch_norm, w_out, b_out, norm_ffn, w_router, b_router, nt, ntile):
    bsz, _, d = x_all.shape
    tm = ROW_TILE
    rows = ntile * tm
    wr = jnp.zeros((d, ROUTER_LANES), F32).at[:, :N_EXPERTS].set(w_router)
    wrh = wr.astype(BF16)
    wrl = (wr - wrh.astype(F32)).astype(BF16)
    br = jnp.full((1, ROUTER_LANES), NEG_INF, F32).at[0, :N_EXPERTS].set(b_router)
    tok = lambda w: pl.BlockSpec((1, tm, w), lambda b, i: (b, i, 0))
    const = lambda s: pl.BlockSpec(s, lambda b, i: (0,) * len(s))
    return pl.pallas_call(
        _merge_kernel,
        grid=(bsz, ntile),
        in_specs=[
            tok(d), tok(HY_WIDTH), tok(Q_WIDTH), tok(Q_WIDTH),
            pl.BlockSpec((1, 1, 6, d), lambda b, i: (b, i // nt, 0, 0)),
            const((1, MIX_WIDTH)), const((MIX_WIDTH, d)), const((1, d)), const((1, d)),
            const((d, ROUTER_LANES)), const((d, ROUTER_LANES)), const((1, ROUTER_LANES)),
        ],
        out_specs=[tok(d), tok(d), tok(ROUTER_LANES), tok(ROUTER_LANES)],
        out_shape=[
            jax.ShapeDtypeStruct((bsz, rows, d), F32),
            jax.ShapeDtypeStruct((bsz, rows, d), F32),
            jax.ShapeDtypeStruct((bsz, rows, ROUTER_LANES), I32),
            jax.ShapeDtypeStruct((bsz, rows, ROUTER_LANES), F32),
        ],
        compiler_params=_cp(("arbitrary", "arbitrary")),
        name="merge_router",
    )(x_all, y_hy, y_ga, y_wa, mod, branch_norm.reshape(1, MIX_WIDTH), w_out.astype(BF16), b_out.reshape(1, d),
      norm_ffn.reshape(1, d), wrh, wrl, br)


def _moe_plan(top_idx, tm):
    n_tok = top_idx.shape[0]
    n_assign = n_tok * TOP_K
    n_blocks = -(-n_assign // tm) + N_EXPERTS
    e_flat = top_idx.reshape(n_assign)
    onehot = (e_flat[:, None] == jnp.arange(N_EXPERTS, dtype=I32)[None, :]).astype(I32)
    csum = jnp.cumsum(onehot, axis=0)
    rank = jnp.take_along_axis(csum, e_flat[:, None], axis=1)[:, 0] - 1
    counts = csum[-1]
    padded = (counts + tm - 1) // tm * tm
    ends = jnp.cumsum(padded)
    slot = (ends - padded)[e_flat] + rank
    tok = jnp.arange(n_assign, dtype=I32) // TOP_K
    slot_tok = jnp.zeros((n_blocks * tm,), I32).at[slot].set(tok, unique_indices=True)
    block_expert = jnp.minimum(jnp.searchsorted(ends, jnp.arange(n_blocks, dtype=I32) * tm, side='right'), N_EXPERTS - 1).astype(I32)
    n_used = (ends[-1] // tm).astype(I32).reshape(1)
    return slot.astype(I32), slot_tok.reshape(n_blocks, 1, tm), block_expert, n_used


def _ffn_kernel(be_ref, nu_ref, tok_ref, tokn_ref, h_hbm, w1_ref, b1_ref, w2_ref, b2_ref, y_ref,
                xbuf0, xbuf1, w1b, w2b, sem):
    i = pl.program_id(0)
    tm = xbuf0.shape[0]
    n_used = nu_ref[0]
    f = w2_ref.shape[1]

    def issue(idx_ref, buf, s):
        for r in range(tm):
            pltpu.make_async_copy(h_hbm.at[pl.ds(idx_ref[0, 0, r], 1), :], buf.at[pl.ds(r, 1), :], sem.at[s]).start()

    def wait(buf, s):
        pltpu.make_async_copy(h_hbm.at[pl.ds(0, tm), :], buf, sem.at[s]).wait()

    @pl.when(i == 0)
    def _():
        issue(tok_ref, xbuf0, 0)

    def step(cur, nxt, sc, sn):
        wait(cur, sc)

        @pl.when(i < n_used)
        def _():
            @pl.when((i == 0) | (be_ref[i] != be_ref[jnp.maximum(i - 1, 0)]))
            def _():
                w1b[...] = w1_ref[0].astype(BF16)
                w2b[...] = w2_ref[0].astype(BF16)

            issue(tokn_ref, nxt, sn)
            a = jnp.dot(cur[...].astype(BF16), w1b[...], preferred_element_type=F32) + b1_ref[0]
            glu = jnp.minimum(a[:, :f], SWIGLU_LIMIT)
            lin = jnp.clip(a[:, f:], -SWIGLU_LIMIT, SWIGLU_LIMIT)
            act = glu / (1.0 + jnp.exp(-SWIGLU_ALPHA * glu)) * (lin + 1.0)
            y_ref[...] = jnp.dot(act.astype(BF16), w2b[...], preferred_element_type=F32) + b2_ref[0]

        @pl.when(i >= n_used)
        def _():
            issue(tokn_ref, nxt, sn)
            y_ref[...] = jnp.zeros(y_ref.shape, F32)

        @pl.when(i == pl.num_programs(0) - 1)
        def _():
            wait(nxt, sn)

    @pl.when(i % 2 == 0)
    def _():
        step(xbuf0, xbuf1, 0, 1)

    @pl.when(i % 2 == 1)
    def _():
        step(xbuf1, xbuf0, 1, 0)


def _moe_ffn(h_rows, slot_tok, block_expert, n_used, w1, b1, w2, b2):
    n_blocks, _, tm = slot_tok.shape
    n_exp, d, f2 = w1.shape
    f = w2.shape[1]
    last = n_blocks - 1
    grid_spec = pltpu.PrefetchScalarGridSpec(
        num_scalar_prefetch=2,
        grid=(n_blocks,),
        in_specs=[
            pl.BlockSpec((1, 1, tm), lambda i, be, nu: (i, 0, 0), memory_space=pltpu.SMEM),
            pl.BlockSpec((1, 1, tm), lambda i, be, nu: (jnp.minimum(i + 1, last), 0, 0), memory_space=pltpu.SMEM),
            pl.BlockSpec(memory_space=pl.ANY),
            pl.BlockSpec((1, d, f2), lambda i, be, nu: (be[i], 0, 0)),
            pl.BlockSpec((1, 1, f2), lambda i, be, nu: (be[i], 0, 0)),
            pl.BlockSpec((1, f, d), lambda i, be, nu: (be[i], 0, 0)),
            pl.BlockSpec((1, 1, d), lambda i, be, nu: (be[i], 0, 0)),
        ],
        out_specs=pl.BlockSpec((tm, d), lambda i, be, nu: (i, 0)),
        scratch_shapes=[
            pltpu.VMEM((tm, d), F32),
            pltpu.VMEM((tm, d), F32),
            pltpu.VMEM((d, f2), BF16),
            pltpu.VMEM((f, d), BF16),
            pltpu.SemaphoreType.DMA((2,)),
        ],
    )
    return pl.pallas_call(
        _ffn_kernel,
        grid_spec=grid_spec,
        out_shape=jax.ShapeDtypeStruct((n_blocks * tm, d), F32),
        compiler_params=_cp(("arbitrary",)),
        name="moe_ffn",
    )(block_expert, n_used, slot_tok, slot_tok, h_rows, w1, b1.reshape(n_exp, 1, f2), w2, b2.reshape(n_exp, 1, d))


def _combine_kernel(slot_ref, slotn_ref, y_hbm, x_ref, mod_ref, gate_ref, nf_ref, o_ref, ybuf, sem, *, final):
    b = pl.program_id(0)
    i = pl.program_id(1)
    step = b * pl.num_programs(1) + i
    n_steps = pl.num_programs(0) * pl.num_programs(1)
    tc = x_ref.shape[1]

    def row_copy(s, slot, r):
        return pltpu.make_async_copy(y_hbm.at[pl.ds(s, 1), :], ybuf.at[slot, pl.ds(r, 1), :], sem.at[slot])

    def issue(idx_ref, slot):
        def body(r, carry):
            for k in range(TOP_K):
                row_copy(idx_ref[0, 0, r * TOP_K + k], slot, k * tc + r).start()
            return carry
        lax.fori_loop(0, tc, body, 0, unroll=2)

    @pl.when(step == 0)
    def _():
        issue(slot_ref, 0)

    @pl.when(step + 1 < n_steps)
    def _():
        issue(slotn_ref, (step + 1) % 2)

    slot = step % 2
    pltpu.make_async_copy(y_hbm.at[pl.ds(0, TOP_K * tc), :], ybuf.at[slot], sem.at[slot]).wait()
    gates = gate_ref[0]
    acc = gates[:, 0:1] * ybuf[slot, 0:tc, :]
    for k in range(1, TOP_K):
        acc += gates[:, k:k + 1] * ybuf[slot, k * tc:(k + 1) * tc, :]
    out = x_ref[0] + mod_ref[0, 0, 5:6, :] * acc
    if final:
        out = _rms(out, nf_ref[...])
    o_ref[0] = out


def _combine(y_sorted, slot, x_mid, mod, gates, norm_final, nt, ntile, final):
    bsz, _, d = x_mid.shape
    tc = ROW_TILE
    n_steps = bsz * ntile
    slots = slot.reshape(n_steps, 1, tc * TOP_K)
    tok = lambda w: pl.BlockSpec((1, tc, w), lambda b, i: (b, i, 0))
    return pl.pallas_call(
        functools.partial(_combine_kernel, final=final),
        grid=(bsz, ntile),
        in_specs=[
            pl.BlockSpec((1, 1, tc * TOP_K), lambda b, i: (b * ntile + i, 0, 0), memory_space=pltpu.SMEM),
            pl.BlockSpec((1, 1, tc * TOP_K), lambda b, i: (jnp.minimum(b * ntile + i + 1, n_steps - 1), 0, 0),
                         memory_space=pltpu.SMEM),
            pl.BlockSpec(memory_space=pl.ANY),
            tok(d),
            pl.BlockSpec((1, 1, 6, d), lambda b, i: (b, i // nt, 0, 0)),
            tok(ROUTER_LANES),
            pl.BlockSpec((1, d), lambda b, i: (0, 0)),
        ],
        out_specs=tok(d),
        out_shape=jax.ShapeDtypeStruct((bsz, ntile * tc, d), F32),
        scratch_shapes=[pltpu.VMEM((2, TOP_K * tc, d), F32), pltpu.SemaphoreType.DMA((2,))],
        compiler_params=_cp(("arbitrary", "arbitrary")),
        name="moe_combine",
    )(slots, slots, y_sorted, x_mid, mod, gates, norm_final.reshape(1, d))


def _split_dot(a, w):
    ah = a.astype(BF16)
    al = (a - ah.astype(F32)).astype(BF16)
    wh = w.astype(BF16)
    wl = (w - wh.astype(F32)).astype(BF16)
    return (jnp.dot(ah, wh, preferred_element_type=F32) + jnp.dot(al, wh, preferred_element_type=F32)
            + jnp.dot(ah, wl, preferred_element_type=F32))


def _filt_kernel(z_ref, w1_ref, b1_ref, w2_ref, b2_ref, w3_ref, b3_ref, fr_ref, wo_ref, dec_ref, o_ref):
    fr = fr_ref[...]
    h = jnp.sin(fr * (_split_dot(z_ref[...], w1_ref[...]) + b1_ref[...]))
    h = jnp.sin(fr * (_split_dot(h, w2_ref[...]) + b2_ref[...]))
    h = jnp.sin(fr * (_split_dot(h, w3_ref[...]) + b3_ref[...]))
    o_ref[...] = _split_dot(h, wo_ref[...]) * dec_ref[...]


def _hyena_filters_pl(n, filt):
    w1, b1, w2, b2, w3, b3, freq, w_out = filt
    fw = w2.shape[0]
    t = jnp.linspace(0.0, 1.0, n, dtype=F32)[:, None]
    wpos = (2.0 * math.pi / n) * jnp.arange(n, dtype=F32)[:, None]
    bands = jnp.linspace(1e-4, HY_BANDS - 1, HY_BANDS, dtype=F32)
    z = jnp.concatenate([t, jnp.cos(wpos * bands), -jnp.sin(wpos * bands), jnp.zeros((n, fw - HY_EMB), F32)], axis=-1)
    w1p = jnp.concatenate([w1, jnp.zeros((fw - HY_EMB, fw), F32)], axis=0)
    max_decay = math.log(HY_TARGET) / HY_FAST_DECAY
    min_decay = math.log(HY_TARGET) / HY_SLOW_DECAY
    deltas = jnp.linspace(min_decay, max_decay, HY_WIDTH, dtype=F32)
    decay = jnp.tile(jnp.exp(-t * jnp.abs(deltas)), (1, 2))
    tm = min(n, 512)
    row = lambda w: pl.BlockSpec((tm, w), lambda i: (i, 0))
    const = lambda s: pl.BlockSpec(s, lambda i: (0, 0))
    return pl.pallas_call(
        _filt_kernel,
        grid=(n // tm,),
        in_specs=[row(fw), const((fw, fw)), const((1, fw)), const((fw, fw)), const((1, fw)), const((fw, fw)), const((1, fw)),
                  const((1, fw)), const((fw, 2 * HY_WIDTH)), row(2 * HY_WIDTH)],
        out_specs=row(2 * HY_WIDTH),
        out_shape=jax.ShapeDtypeStruct((n, 2 * HY_WIDTH), F32),
        compiler_params=_cp(("arbitrary",)),
        name="hyena_filters",
    )(z, w1p, b1.reshape(1, fw), w2, b2.reshape(1, fw), w3, b3.reshape(1, fw), freq.reshape(1, fw), w_out, decay)


def _gate_kernel(u_ref, prev_ref, next_ref, w_ref, b_ref, z_ref, x0_ref, *, nt):
    i = pl.program_id(1)
    u = u_ref[0]
    tm = u.shape[0]
    w = w_ref[...]
    has_prev = jnp.where((i != 0) & (i != nt), 1.0, 0.0)
    has_next = jnp.where((i != nt - 1) & (i != nt), 1.0, 0.0)
    row = lax.broadcasted_iota(I32, u.shape, 0)
    up = jnp.where(row == 0, prev_ref[0, 7:8, :] * has_prev, pltpu.roll(u, 1, 0))
    un = jnp.where(row == tm - 1, next_ref[0, 0:1, :] * has_next, pltpu.roll(u, tm - 1, 0))
    uc = up * w[0:1, :] + u * w[1:2, :] + un * w[2:3, :] + b_ref[...]
    x0_ref[0] = uc[:, :HY_WIDTH]
    z_ref[0] = uc[:, 2 * HY_WIDTH:] * uc[:, HY_WIDTH:2 * HY_WIDTH]


def _hyena_gate(hy, conv_w, conv_b, nt, ntile):
    bsz, rows, w3 = hy.shape
    tm = ROW_TILE
    sub = tm // 8
    last8 = rows // 8 - 1
    out = pl.BlockSpec((1, tm, HY_WIDTH), lambda b, i: (b, i, 0))
    return pl.pallas_call(
        functools.partial(_gate_kernel, nt=nt),
        grid=(bsz, ntile),
        in_specs=[
            pl.BlockSpec((1, tm, w3), lambda b, i: (b, i, 0)),
            pl.BlockSpec((1, 8, w3), lambda b, i: (b, jnp.maximum(i * sub - 1, 0), 0)),
            pl.BlockSpec((1, 8, w3), lambda b, i: (b, jnp.minimum((i + 1) * sub, last8), 0)),
            pl.BlockSpec((HY_SHORT, w3), lambda b, i: (0, 0)),
            pl.BlockSpec((1, w3), lambda b, i: (0, 0)),
        ],
        out_specs=[out, out],
        out_shape=[jax.ShapeDtypeStruct((bsz, ntile * tm, HY_WIDTH), F32)] * 2,
        compiler_params=_cp(("arbitrary", "arbitrary")),
        name="hyena_gate",
    )(hy, hy, hy, conv_w, conv_b.reshape(1, w3))


FFT_N2 = LANES


def _dft_tables(n_fft):
    n1 = n_fft // FFT_N2
    k1 = jnp.arange(n1, dtype=I32)
    a = (2.0 * math.pi / n1) * ((k1[:, None] * k1[None, :]) % n1).astype(F32)
    f1 = jnp.stack([jnp.cos(a), -jnp.sin(a)], axis=1).reshape(2 * n1, n1)
    g = (jnp.stack([jnp.cos(a), -jnp.sin(a)], axis=2).reshape(n1, 2 * n1) / n_fft)
    k2 = jnp.arange(FFT_N2, dtype=I32)
    m = (k2[None, None, :] * (k1[:, None, None] + n1 * k2[None, :, None])) % n_fft
    th = (2.0 * math.pi / n_fft) * m.astype(F32)
    mre, mim = jnp.cos(th), -jnp.sin(th)
    mr = jnp.concatenate([jnp.concatenate([mre, -mim], axis=2), jnp.concatenate([mim, mre], axis=2)], axis=1)
    return f1.astype(BF16), g.astype(BF16), mr.astype(BF16), jnp.swapaxes(mr, 1, 2).astype(BF16)


def _fft1_kernel(x_ref, f_ref, o_ref, *, n1_in):
    f = f_ref[...]
    n_rows = f.shape[0]

    def body(n2, carry):
        xs = x_ref[pl.ds(n2, n1_in, stride=FFT_N2), :].astype(BF16)
        o_ref[pl.ds(n2, n_rows, stride=FFT_N2), :] = jnp.dot(f, xs, preferred_element_type=F32)
        return carry
    lax.fori_loop(0, FFT_N2, body, 0, unroll=8)


def _fft1(x, f1, n1_in):
    bsz, _, ch = x.shape
    n_rows = f1.shape[0]
    return pl.pallas_call(
        functools.partial(_fft1_kernel, n1_in=n1_in),
        grid=(bsz, ch // LANES),
        in_specs=[pl.BlockSpec((None, n1_in * FFT_N2, LANES), lambda b, h: (b, 0, h)),
                  pl.BlockSpec((n_rows, n1_in), lambda b, h: (0, 0))],
        out_specs=pl.BlockSpec((None, n_rows * FFT_N2, LANES), lambda b, h: (b, 0, h)),
        out_shape=jax.ShapeDtypeStruct((bsz, n_rows * FFT_N2, ch), F32),
        compiler_params=_cp(("arbitrary", "arbitrary")),
        name="hyena_fft1",
    )(x, f1[:, :n1_in])


def _spec_kernel(a_ref, mr_ref, o_ref):
    o_ref[0] = jnp.dot(mr_ref[0], a_ref[0].astype(BF16), preferred_element_type=F32)


def _fft_spectrum(a, mr):
    _, rows, ch = a.shape
    blk = 2 * FFT_N2
    return pl.pallas_call(
        _spec_kernel,
        grid=(rows // blk,),
        in_specs=[pl.BlockSpec((1, blk, ch), lambda k: (0, k, 0)), pl.BlockSpec((1, blk, blk), lambda k: (k, 0, 0))],
        out_specs=pl.BlockSpec((1, blk, ch), lambda k: (0, k, 0)),
        out_shape=jax.ShapeDtypeStruct((1, rows, ch), F32),
        compiler_params=_cp(("arbitrary",)),
        name="hyena_filter_spectrum",
    )(a, mr)


def _mid_kernel(a_ref, mr_ref, mrt_ref, ks_ref, o_ref):
    kre, kim = ks_ref[0, :FFT_N2, :], ks_ref[0, FFT_N2:, :]
    for b in range(a_ref.shape[0]):
        x = jnp.dot(mr_ref[0], a_ref[b].astype(BF16), preferred_element_type=F32)
        xre, xim = x[:FFT_N2], x[FFT_N2:]
        y = jnp.concatenate([xre * kre - xim * kim, xre * kim + xim * kre], axis=0)
        o_ref[b] = jnp.dot(mrt_ref[0], y.astype(BF16), preferred_element_type=F32)


def _fft_mid(a, mr, mrt, kspec):
    bsz, rows, ch = a.shape
    blk = 2 * FFT_N2
    return pl.pallas_call(
        _mid_kernel,
        grid=(rows // blk,),
        in_specs=[pl.BlockSpec((bsz, blk, ch), lambda k: (0, k, 0)),
                  pl.BlockSpec((1, blk, blk), lambda k: (k, 0, 0)),
                  pl.BlockSpec((1, blk, blk), lambda k: (k, 0, 0)),
                  pl.BlockSpec((1, blk, ch), lambda k: (0, k, 0))],
        out_specs=pl.BlockSpec((bsz, blk, ch), lambda k: (0, k, 0)),
        out_shape=jax.ShapeDtypeStruct((bsz, rows, ch), F32),
        compiler_params=_cp(("arbitrary",)),
        name="hyena_fft_mid",
    )(a, mr, mrt, kspec)


def _ifft1_kernel(b_ref, g_ref, z_ref, x0_ref, skip_ref, o_ref, *, n1_out):
    g = g_ref[...]
    n_rows = g.shape[1]
    skip = skip_ref[...]

    def body(n2, carry):
        bs = b_ref[pl.ds(n2, n_rows, stride=FFT_N2), :].astype(BF16)
        y = jnp.dot(g, bs, preferred_element_type=F32)
        rows = pl.ds(n2, n1_out, stride=FFT_N2)
        zs = z_ref[rows, :]
        o_ref[rows, :] = x0_ref[rows, :] * (y + zs * skip)
        return carry
    lax.fori_loop(0, FFT_N2, body, 0, unroll=8)


def _ifft1(bmat, g, z, x0, skip, n1_out):
    bsz, rows, ch = bmat.shape
    n = n1_out * FFT_N2
    seq = pl.BlockSpec((None, n, LANES), lambda b, h: (b, 0, h))
    return pl.pallas_call(
        functools.partial(_ifft1_kernel, n1_out=n1_out),
        grid=(bsz, ch // LANES),
        in_specs=[pl.BlockSpec((None, rows, LANES), lambda b, h: (b, 0, h)),
                  pl.BlockSpec((n1_out, g.shape[1]), lambda b, h: (0, 0)),
                  seq, seq, pl.BlockSpec((1, LANES), lambda b, h: (0, h))],
        out_specs=seq,
        out_shape=jax.ShapeDtypeStruct((bsz, n, ch), F32),
        compiler_params=_cp(("arbitrary", "arbitrary")),
        name="hyena_ifft1",
    )(bmat, g[:n1_out], z, x0, skip.reshape(1, ch))


def _hyena_fftconv(z, x0, k2, skip):
    n = k2.shape[0]
    n_fft = 2 * n
    f1, g, mr, mrt = _dft_tables(n_fft)
    k_full = jnp.concatenate([k2[:, :HY_WIDTH], k2[::-1, HY_WIDTH:]], axis=0)[None]
    kspec = _fft_spectrum(_fft1(k_full, f1, n_fft // FFT_N2), mr)
    a = _fft1(z, f1, n // FFT_N2)
    bmat = _fft_mid(a, mr, mrt, kspec)
    return _ifft1(bmat, g, z, x0, skip, n // FFT_N2)


def _ctxconv_kernel(z_ref, x0_ref, kf_ref, fd_ref, gd_ref, skip_ref, o_ref, ks_ref):
    n = z_ref.shape[1]
    half = fd_ref.shape[0] // 2

    @pl.when(pl.program_id(0) == 0)
    def _():
        ks_ref[...] = jnp.dot(fd_ref[...], kf_ref[...].astype(BF16), preferred_element_type=F32)

    z = z_ref[0]
    x = jnp.dot(fd_ref[:, :n], z.astype(BF16), preferred_element_type=F32)
    xre, xim = x[:half], x[half:]
    kre, kim = ks_ref[:half, :], ks_ref[half:, :]
    y = jnp.concatenate([xre * kre - xim * kim, xre * kim + xim * kre], axis=0)
    conv = jnp.dot(gd_ref[...], y.astype(BF16), preferred_element_type=F32)
    o_ref[0] = x0_ref[0] * (conv + z * skip_ref[...])


def _hyena_ctxconv(z, x0, k2, skip, row0):
    n = k2.shape[0]
    bsz, _, ch = z.shape
    n_fft = 2 * n
    kk = jnp.arange(n_fft, dtype=I32)
    th = (2.0 * math.pi / n_fft) * ((kk[:, None] * kk[None, :]) % n_fft).astype(F32)
    fd = jnp.concatenate([jnp.cos(th), -jnp.sin(th)], axis=0).astype(BF16)
    gd = (jnp.concatenate([jnp.cos(th[:n]), -jnp.sin(th[:n])], axis=1) / n_fft).astype(BF16)
    k_full = jnp.concatenate([k2[:, :HY_WIDTH], k2[::-1, HY_WIDTH:]], axis=0)
    blk = row0 // n
    seq = pl.BlockSpec((1, n, ch), lambda b: (b, blk, 0))
    const = lambda s: pl.BlockSpec(s, lambda b: (0, 0))
    return pl.pallas_call(
        _ctxconv_kernel,
        grid=(bsz,),
        in_specs=[seq, seq, const((n_fft, ch)), const((2 * n_fft, n_fft)), const((n, 2 * n_fft)), const((1, ch))],
        out_specs=pl.BlockSpec((1, n, ch), lambda b: (b, 0, 0)),
        out_shape=jax.ShapeDtypeStruct((bsz, n, ch), F32),
        scratch_shapes=[pltpu.VMEM((2 * n_fft, ch), F32)],
        compiler_params=_cp(("arbitrary",)),
        name="hyena_ctxconv",
    )(z, x0, k_full, fd, gd, skip.reshape(1, ch))


def _rope_tables(n_lat, rows):
    nrow = n_lat // GRID_W
    row = jnp.repeat(jnp.arange(nrow, dtype=F32), GRID_W)
    col = jnp.tile(jnp.arange(GRID_W, dtype=F32), nrow)
    n_freq = HEAD_DIM // 4
    inv_freq = ROPE_THETA ** (-jnp.arange(n_freq, dtype=F32) / n_freq)
    ar = row[:, None] * inv_freq
    ac = col[:, None] * inv_freq
    cos64 = jnp.concatenate([jnp.cos(ar), jnp.cos(ar), jnp.cos(ac), jnp.cos(ac)], axis=-1)
    sin64 = jnp.concatenate([-jnp.sin(ar), jnp.sin(ar), -jnp.sin(ac), jnp.sin(ac)], axis=-1)
    pad = rows - n_lat
    cos_t = jnp.concatenate([jnp.tile(cos64, (1, LANES // HEAD_DIM)), jnp.ones((pad, LANES), F32)], axis=0)
    sin_t = jnp.concatenate([jnp.tile(sin64, (1, LANES // HEAD_DIM)), jnp.zeros((pad, LANES), F32)], axis=0)
    return cos_t, sin_t


def kernel(x, c, ctx, c_ctx, w_ada, b_ada, norm_mix, norm_ffn, w_in, b_in, hy_conv_w, hy_conv_b, hy_filt_w1, hy_filt_b1, hy_filt_w2, hy_filt_b2, hy_filt_w3, hy_filt_b3, hy_filt_freq, hy_filt_out, hy_skip, ga_q_norm, ga_k_norm, wa_sink, branch_norm, w_out, b_out, w_router, b_router, w_mlp1, b_mlp1, w_mlp2, b_mlp2, norm_final):
    bsz, n_lat, d = x.shape
    n_ctx = ctx.shape[1]
    depth = w_ada.shape[0]
    assert n_ctx == ROW_TILE and n_lat % KV_CHUNK == 0 and bsz < 16 and d == D_MODEL
    nt = n_lat // ROW_TILE
    rows = n_lat + n_ctx
    cos_t, sin_t = _rope_tables(n_lat, rows)

    c_all = jnp.zeros((16, d), F32).at[:bsz].set(c).at[bsz].set(c_ctx)
    mod_all = _ada_mod(c_all, w_ada, b_ada)
    x_all = jnp.concatenate([x, ctx], axis=1)

    for l in range(depth):
        last = l == depth - 1
        ntile = nt if last else nt + 1
        m = mod_all[l].reshape(16, 6, d)
        mod = jnp.stack([m[:bsz], jnp.broadcast_to(m[bsz], (bsz, 6, d))], axis=1)

        hy, gq, gk, gv, wq, wk, wv = _inproj(x_all, mod, norm_mix[l], w_in[l], b_in[l], cos_t, sin_t,
                                             ga_q_norm[l], ga_k_norm[l], nt)
        filt = (hy_filt_w1[l], hy_filt_b1[l], hy_filt_w2[l], hy_filt_b2[l], hy_filt_w3[l], hy_filt_b3[l],
                hy_filt_freq[l], hy_filt_out[l])
        z, x0 = _hyena_gate(hy, hy_conv_w[l], hy_conv_b[l], nt, ntile)
        y_hy = _hyena_fftconv(z, x0, _hyena_filters_pl(n_lat, filt), hy_skip[l])
        if not last:
            yc_hy = _hyena_ctxconv(z, x0, _hyena_filters_pl(n_ctx, filt), hy_skip[l], n_lat)
            y_hy = jnp.concatenate([y_hy, yc_hy], axis=1)
        y_ga = _global_attention(gq, gk, gv, nt, ntile)
        y_wa = _window_attention(wq, wk, wv, wa_sink[l], nt, ntile)
        x_mid, h2, top_i, top_g = _merge(x_all, y_hy, y_ga, y_wa, mod, branch_norm[l], w_out[l], b_out[l],
                                         norm_ffn[l], w_router[l], b_router[l], nt, ntile)
        n_tok = bsz * ntile * ROW_TILE
        slot, slot_tok, block_expert, n_used = _moe_plan(top_i[..., :TOP_K].reshape(n_tok, TOP_K), MOE_TILE)
        y_sorted = _moe_ffn(h2.reshape(n_tok, d), slot_tok, block_expert, n_used, w_mlp1[l], b_mlp1[l], w_mlp2[l], b_mlp2[l])
        x_all = _combine(y_sorted, slot, x_mid, mod, top_g, norm_final, nt, ntile, last)
    return x_all
```

```python
import functools
import math

import jax
import jax.numpy as jnp
from jax import lax
from jax.experimental import pallas as pl
from jax.experimental.pallas import tpu as pltpu

F32 = jnp.float32
BF16 = jnp.bfloat16
I32 = jnp.int32

D_MODEL = 1024
HEAD_DIM = 64
GRID_W = 64
HY_WIDTH = 256
GA_HEADS = 6
GA_KV_HEADS = 2
WA_HEADS = 6
WA_KV_HEADS = 2
GROUP = GA_HEADS // GA_KV_HEADS
Q_WIDTH = GA_HEADS * HEAD_DIM
KV_WIDTH = GA_KV_HEADS * HEAD_DIM
MIX_WIDTH = HY_WIDTH + 2 * Q_WIDTH
IN_WIDTH = 3 * HY_WIDTH + 2 * (Q_WIDTH + 2 * KV_WIDTH)
WINDOW = 128
ROPE_THETA = 10000.0
ATTN_SCALE = HEAD_DIM ** -0.5
HY_SHORT = 3
HY_EMB = 33
HY_BANDS = (HY_EMB - 1) // 2
HY_FAST_DECAY = 0.3
HY_SLOW_DECAY = 1.5
HY_TARGET = 1e-2
N_EXPERTS = 32
TOP_K = 4
SWIGLU_ALPHA = 1.702
SWIGLU_LIMIT = 7.0
EPS = 1e-6
NEG_INF = -1e30

LANES = 128
VMEM_LIMIT_V7X = 56 * 1024 * 1024

ROW_TILE = 256
KV_CHUNK = 512
MOE_TILE = 512
ROUTER_LANES = LANES


def _cp(sem, vmem=VMEM_LIMIT_V7X):
    return pltpu.CompilerParams(dimension_semantics=sem, vmem_limit_bytes=vmem)


def _rms(v, w):
    return v * lax.rsqrt(jnp.mean(v * v, axis=-1, keepdims=True) + EPS) * w


def _ada_kernel(c_ref, w_ref, b_ref, o_ref):
    c = c_ref[...]
    s = c / (1.0 + jnp.exp(-c))
    hi = s.astype(BF16)
    lo = (s - hi.astype(F32)).astype(BF16)
    w = w_ref[0]
    whi = w.astype(BF16)
    wlo = (w - whi.astype(F32)).astype(BF16)
    acc = jnp.dot(hi, whi, preferred_element_type=F32)
    acc += jnp.dot(lo, whi, preferred_element_type=F32)
    acc += jnp.dot(hi, wlo, preferred_element_type=F32)
    o_ref[0] = acc + b_ref[0]


def _ada_mod(c_all, w_ada, b_ada):
    n_layers, d, n6 = w_ada.shape
    tn = n6 // 4
    return pl.pallas_call(
        _ada_kernel,
        grid=(n_layers, n6 // tn),
        in_specs=[
            pl.BlockSpec((16, d), lambda l, j: (0, 0)),
            pl.BlockSpec((1, d, tn), lambda l, j: (l, 0, j)),
            pl.BlockSpec((1, 1, tn), lambda l, j: (l, 0, j)),
        ],
        out_specs=pl.BlockSpec((1, 16, tn), lambda l, j: (l, 0, j)),
        out_shape=jax.ShapeDtypeStruct((n_layers, 16, n6), F32),
        compiler_params=_cp(("arbitrary", "arbitrary")),
        name="ada_mod",
    )(c_all, w_ada, b_ada.reshape(n_layers, 1, n6))


def _inproj_kernel(x_ref, mod_ref, nw_ref, w_ref, b_ref, cos_ref, sin_ref, qn_ref, kn_ref, g_ref,
                   hy_ref, gq_ref, gk_ref, gv_ref, wq_ref, wk_ref, wv_ref):
    x = x_ref[0]
    tm = x.shape[0]
    shift = mod_ref[0, 0, 0:1, :]
    scale = mod_ref[0, 0, 1:2, :]
    h = _rms(x, nw_ref[...]) * (1.0 + scale) + shift
    p = jnp.dot(h.astype(BF16), w_ref[...], preferred_element_type=F32) + b_ref[...]
    hy_ref[0] = p[:, :3 * HY_WIDTH]

    cos = cos_ref[...]
    sin = sin_ref[...]
    lane = lax.broadcasted_iota(I32, (tm, LANES), 1)
    first = (lane % (HEAD_DIM // 2)) < (HEAD_DIM // 4)
    gmat = g_ref[...]

    def rope(v):
        sw = jnp.where(first, pltpu.roll(v, LANES - HEAD_DIM // 4, 1), pltpu.roll(v, HEAD_DIM // 4, 1))
        return v * cos + sw * sin

    def head_norm(v, w):
        t = v * v
        hi = t.astype(BF16)
        lo = (t - hi.astype(F32)).astype(BF16)
        ms = jnp.dot(hi, gmat, preferred_element_type=F32) + jnp.dot(lo, gmat, preferred_element_type=F32)
        return v * lax.rsqrt(ms + EPS) * w

    o = 3 * HY_WIDTH
    for u in range(Q_WIDTH // LANES):
        v = p[:, o + u * LANES:o + (u + 1) * LANES]
        v = rope(head_norm(v, qn_ref[...]))
        gq_ref[0, :, u * LANES:(u + 1) * LANES] = (v * ATTN_SCALE).astype(BF16)
    o += Q_WIDTH
    gk_ref[0] = rope(head_norm(p[:, o:o + KV_WIDTH], kn_ref[...])).astype(BF16)
    o += KV_WIDTH
    gv_ref[0] = p[:, o:o + KV_WIDTH].astype(BF16)
    o += KV_WIDTH
    for u in range(Q_WIDTH // LANES):
        v = rope(p[:, o + u * LANES:o + (u + 1) * LANES])
        wq_ref[0, :, u * LANES:(u + 1) * LANES] = (v * ATTN_SCALE).astype(BF16)
    o += Q_WIDTH
    wk_ref[0] = rope(p[:, o:o + KV_WIDTH]).astype(BF16)
    o += KV_WIDTH
    wv_ref[0] = p[:, o:o + KV_WIDTH].astype(BF16)


def _inproj(x_all, mod, norm_w, w_in, b_in, cos_t, sin_t, q_norm, k_norm, nt):
    bsz, rows, d = x_all.shape
    tm = ROW_TILE
    ntile = rows // tm
    gmat = jnp.kron(jnp.eye(LANES // HEAD_DIM, dtype=F32), jnp.full((HEAD_DIM, HEAD_DIM), 1.0 / HEAD_DIM, F32)).astype(BF16)
    qn = jnp.tile(q_norm, LANES // HEAD_DIM).reshape(1, LANES)
    kn = jnp.tile(k_norm, LANES // HEAD_DIM).reshape(1, LANES)
    tok = lambda w: pl.BlockSpec((1, tm, w), lambda b, i: (b, i, 0))
    const = lambda s: pl.BlockSpec(s, lambda b, i: (0,) * len(s))
    out_w = (3 * HY_WIDTH, Q_WIDTH, KV_WIDTH, KV_WIDTH, Q_WIDTH, KV_WIDTH, KV_WIDTH)
    out_dt = (F32, BF16, BF16, BF16, BF16, BF16, BF16)
    return pl.pallas_call(
        _inproj_kernel,
        grid=(bsz, ntile),
        in_specs=[
            tok(d),
            pl.BlockSpec((1, 1, 6, d), lambda b, i: (b, i // nt, 0, 0)),
            const((1, d)),
            const((d, IN_WIDTH)),
            const((1, IN_WIDTH)),
            pl.BlockSpec((tm, LANES), lambda b, i: (i, 0)),
            pl.BlockSpec((tm, LANES), lambda b, i: (i, 0)),
            const((1, LANES)),
            const((1, LANES)),
            const((LANES, LANES)),
        ],
        out_specs=[tok(w) for w in out_w],
        out_shape=[jax.ShapeDtypeStruct((bsz, rows, w), dt) for w, dt in zip(out_w, out_dt)],
        compiler_params=_cp(("arbitrary", "arbitrary")),
        name="inproj",
    )(x_all, mod, norm_w.reshape(1, d), w_in.astype(BF16), b_in.reshape(1, IN_WIDTH), cos_t, sin_t, qn, kn, gmat)


def _stack_q(q_ref, qs_ref):
    qf = q_ref[0].astype(F32)
    tq = qf.shape[0]
    for j in range(GA_KV_HEADS):
        for g in range(GROUP):
            h = GROUP * j + g
            blk = qf[:, (h // 2) * LANES:(h // 2 + 1) * LANES]
            if h % 2 != j:
                blk = pltpu.roll(blk, HEAD_DIM, 1)
            qs_ref[j, g * tq:(g + 1) * tq, :] = blk.astype(BF16)


def _unstack_o(norm_fn, o_ref, tq):
    lane = lax.broadcasted_iota(I32, (tq, LANES), 1)
    heads = []
    for j in range(GA_KV_HEADS):
        for g in range(GROUP):
            h = GROUP * j + g
            o = norm_fn(j, g)
            if h % 2 != j:
                o = pltpu.roll(o, HEAD_DIM, 1)
            heads.append(o)
    for u in range(Q_WIDTH // LANES):
        o_ref[0, :, u * LANES:(u + 1) * LANES] = jnp.where(lane < HEAD_DIM, heads[2 * u], heads[2 * u + 1])


def _half_masks(rows):
    lane = lax.broadcasted_iota(I32, (rows, LANES), 1)
    return [lane < HEAD_DIM, lane >= HEAD_DIM]


def _dot_t(a, b):
    return lax.dot_general(a, b, (((1,), (1,)), ((), ())), preferred_element_type=F32)


def _ga_kernel(q_ref, k_ref, v_ref, o_ref, qs_ref, m_ref, acc_ref, *, nt, n_lat):
    i = pl.program_id(1)
    tq = q_ref.shape[1]
    _stack_q(q_ref, qs_ref)
    m_ref[...] = jnp.full(m_ref.shape, NEG_INF, F32)
    acc_ref[...] = jnp.zeros(acc_ref.shape, F32)

    def chunk(kc, vc):
        masks = _half_masks(kc.shape[0])
        for j in range(GA_KV_HEADS):
            km = jnp.where(masks[j], kc, jnp.zeros_like(kc))
            vm = jnp.where(masks[j], vc, jnp.ones_like(vc))
            s = _dot_t(qs_ref[j], km)
            m_old = m_ref[j]
            m_new = jnp.maximum(m_old, jnp.max(s, axis=-1, keepdims=True))
            p = jnp.exp(s - jnp.tile(m_new, (1, s.shape[1] // LANES)))
            acc_ref[j] = jnp.exp(m_old - m_new) * acc_ref[j] + jnp.dot(p.astype(BF16), vm, preferred_element_type=F32)
            m_ref[j] = m_new

    @pl.when(i < nt)
    def _():
        def body(c, carry):
            st = pl.multiple_of(c * KV_CHUNK, KV_CHUNK)
            chunk(k_ref[0, pl.ds(st, KV_CHUNK), :], v_ref[0, pl.ds(st, KV_CHUNK), :])
            return carry
        lax.fori_loop(0, n_lat // KV_CHUNK, body, 0)

    chunk(k_ref[0, n_lat:, :], v_ref[0, n_lat:, :])

    def norm_fn(j, g):
        a = acc_ref[j, g * tq:(g + 1) * tq, :]
        return a / pltpu.roll(a, HEAD_DIM, 1)

    _unstack_o(norm_fn, o_ref, tq)


def _global_attention(gq, gk, gv, nt, ntile):
    bsz, rows, _ = gq.shape
    tq = ROW_TILE
    n_lat = nt * tq
    return pl.pallas_call(
        functools.partial(_ga_kernel, nt=nt, n_lat=n_lat),
        grid=(bsz, ntile),
        in_specs=[
            pl.BlockSpec((1, tq, Q_WIDTH), lambda b, i: (b, i, 0)),
            pl.BlockSpec((1, rows, KV_WIDTH), lambda b, i: (b, 0, 0)),
            pl.BlockSpec((1, rows, KV_WIDTH), lambda b, i: (b, 0, 0)),
        ],
        out_specs=pl.BlockSpec((1, tq, Q_WIDTH), lambda b, i: (b, i, 0)),
        out_shape=jax.ShapeDtypeStruct((bsz, ntile * tq, Q_WIDTH), F32),
        scratch_shapes=[
            pltpu.VMEM((GA_KV_HEADS, GROUP * tq, LANES), BF16),
            pltpu.VMEM((GA_KV_HEADS, GROUP * tq, LANES), F32),
            pltpu.VMEM((GA_KV_HEADS, GROUP * tq, LANES), F32),
        ],
        compiler_params=_cp(("arbitrary", "arbitrary")),
        name="global_attn",
    )(gq, gk, gv)


def _wa_kernel(q_ref, k_ref, v_ref, sink_ref, o_ref, qs_ref, acc_ref, l_ref, *, nt, n_lat):
    i = pl.program_id(1)
    tq = q_ref.shape[1]
    band = tq + 2 * WINDOW
    _stack_q(q_ref, qs_ref)
    st = pl.multiple_of(i * tq, tq)
    kb = k_ref[0, pl.ds(st, band), :]
    vb = v_ref[0, pl.ds(st, band), :]
    kc = k_ref[0, WINDOW + n_lat:WINDOW + n_lat + ROW_TILE, :]
    vc = v_ref[0, WINDOW + n_lat:WINDOW + n_lat + ROW_TILE, :]

    r = lax.broadcasted_iota(I32, (GROUP * tq, band), 0) % tq
    c = lax.broadcasted_iota(I32, (GROUP * tq, band), 1)
    rel = c - WINDOW - r
    kpos = c + (i * tq - WINDOW)
    valid = (jnp.abs(rel) <= WINDOW) & (kpos >= 0) & (kpos < n_lat) & (i < nt)

    mb = _half_masks(band)
    mc = _half_masks(ROW_TILE)
    for j in range(WA_KV_HEADS):
        q = qs_ref[j]
        s_loc = jnp.where(valid, _dot_t(q, jnp.where(mb[j], kb, jnp.zeros_like(kb))), NEG_INF)
        s_ctx = _dot_t(q, jnp.where(mc[j], kc, jnp.zeros_like(kc)))
        sink = jnp.concatenate(
            [jnp.broadcast_to(sink_ref[GROUP * j + g:GROUP * j + g + 1, :], (tq, LANES)) for g in range(GROUP)], axis=0)
        m = jnp.maximum(jnp.maximum(jnp.max(s_loc, axis=-1, keepdims=True), jnp.max(s_ctx, axis=-1, keepdims=True)), sink)
        p_loc = jnp.exp(s_loc - jnp.tile(m, (1, band // LANES))).astype(BF16)
        p_ctx = jnp.exp(s_ctx - jnp.tile(m, (1, ROW_TILE // LANES))).astype(BF16)
        acc_ref[j] = (jnp.dot(p_loc, jnp.where(mb[j], vb, jnp.ones_like(vb)), preferred_element_type=F32)
                      + jnp.dot(p_ctx, jnp.where(mc[j], vc, jnp.ones_like(vc)), preferred_element_type=F32))
        l_ref[j] = jnp.exp(sink - m)

    def norm_fn(j, g):
        a = acc_ref[j, g * tq:(g + 1) * tq, :]
        return a / (pltpu.roll(a, HEAD_DIM, 1) + l_ref[j, g * tq:(g + 1) * tq, :])

    _unstack_o(norm_fn, o_ref, tq)


def _window_attention(wq, wk, wv, sink, nt, ntile):
    bsz, rows, _ = wq.shape
    tq = ROW_TILE
    n_lat = nt * tq
    pad = ((0, 0), (WINDOW, WINDOW), (0, 0))
    kp = jnp.pad(wk, pad)
    vp = jnp.pad(wv, pad)
    sink_b = jnp.zeros((8, LANES), F32).at[:WA_HEADS].set(jnp.broadcast_to(sink[:, None], (WA_HEADS, LANES)))
    return pl.pallas_call(
        functools.partial(_wa_kernel, nt=nt, n_lat=n_lat),
        grid=(bsz, ntile),
        in_specs=[
            pl.BlockSpec((1, tq, Q_WIDTH), lambda b, i: (b, i, 0)),
            pl.BlockSpec((1, rows + 2 * WINDOW, KV_WIDTH), lambda b, i: (b, 0, 0)),
            pl.BlockSpec((1, rows + 2 * WINDOW, KV_WIDTH), lambda b, i: (b, 0, 0)),
            pl.BlockSpec((8, LANES), lambda b, i: (0, 0)),
        ],
        out_specs=pl.BlockSpec((1, tq, Q_WIDTH), lambda b, i: (b, i, 0)),
        out_shape=jax.ShapeDtypeStruct((bsz, ntile * tq, Q_WIDTH), F32),
        scratch_shapes=[
            pltpu.VMEM((WA_KV_HEADS, GROUP * tq, LANES), BF16),
            pltpu.VMEM((WA_KV_HEADS, GROUP * tq, LANES), F32),
            pltpu.VMEM((WA_KV_HEADS, GROUP * tq, LANES), F32),
        ],
        compiler_params=_cp(("arbitrary", "arbitrary")),
        name="window_attn",
    )(wq, kp, vp, sink_b)


def _store_row_tiles(ref, v):
    rows = v.shape[0]
    for s in range(v.shape[1] // LANES):
        ref[pl.ds(s, rows, stride=8), :] = v[:, s * LANES:(s + 1) * LANES]


def _load_row_tiles(ref, row0, rows):
    return jnp.concatenate([ref[pl.ds(row0 * 8 + s, rows, stride=8), :] for s in range(8)], axis=1)


def _merge_kernel(x_ref, hy_ref, ga_ref, wa_ref, mod_ref, bn_ref, wo_ref, bo_ref, nf_ref, wrh_ref, wrl_ref, br_ref, tri_ref,
                  xo_ref, h2_ref, tg_ref, te_ref, tr_ref, cnt_ref, base_ref):
    @pl.when((pl.program_id(0) == 0) & (pl.program_id(1) == 0))
    def _():
        base_ref[...] = jnp.zeros(base_ref.shape, F32)

    bn = bn_ref[...]
    g0 = HY_WIDTH
    g1 = HY_WIDTH + Q_WIDTH
    y = jnp.concatenate([_rms(hy_ref[0], bn[:, :g0]), _rms(ga_ref[0], bn[:, g0:g1]), _rms(wa_ref[0], bn[:, g1:])], axis=-1)
    o = jnp.dot(y.astype(BF16), wo_ref[...], preferred_element_type=F32) + bo_ref[...]
    x1 = x_ref[0] + mod_ref[0, 0, 2:3, :] * o
    xo_ref[0] = x1
    h2 = _rms(x1, nf_ref[...]) * (1.0 + mod_ref[0, 0, 4:5, :]) + mod_ref[0, 0, 3:4, :]
    _store_row_tiles(h2_ref, h2)

    hi = h2.astype(BF16)
    lo = (h2 - hi.astype(F32)).astype(BF16)
    logits = (jnp.dot(hi, wrh_ref[...], preferred_element_type=F32) + jnp.dot(lo, wrh_ref[...], preferred_element_type=F32)
              + jnp.dot(hi, wrl_ref[...], preferred_element_type=F32)) + br_ref[...]
    tm = logits.shape[0]
    lane = lax.broadcasted_iota(I32, (tm, ROUTER_LANES), 1)
    te = jnp.zeros((tm, ROUTER_LANES), F32)
    tr = jnp.zeros((tm, ROUTER_LANES), F32)
    tv = jnp.zeros((tm, ROUTER_LANES), F32)
    before = base_ref[...]
    tri = tri_ref[...]
    v0 = None
    for k in range(TOP_K):
        mx = jnp.max(logits, axis=-1, keepdims=True)
        idx = jnp.min(jnp.where(logits == mx, lane, ROUTER_LANES), axis=-1, keepdims=True)
        if k == 0:
            v0 = mx
        hit = lane == idx
        onehot = jnp.where(hit, 1.0, 0.0)
        prefix = jnp.dot(tri, onehot.astype(BF16), preferred_element_type=F32) + before
        rank = jnp.sum(onehot * prefix, axis=-1, keepdims=True)
        before = before + jnp.sum(onehot, axis=0, keepdims=True)
        te = jnp.where(lane == k, idx.astype(F32), te)
        tr = jnp.where(lane == k, rank, tr)
        tv = jnp.where(lane == k, jnp.exp(mx - v0), tv)
        logits = jnp.where(hit, NEG_INF, logits)
    base_ref[...] = before
    cnt_ref[...] = before
    tg_ref[0] = tv / jnp.sum(tv, axis=-1, keepdims=True)
    te_t = te.T
    tr_t = tr.T
    for k in range(TOP_K):
        te_ref[0, :, k * tm:(k + 1) * tm] = te_t[k:k + 1, :].astype(I32)
        tr_ref[0, :, k * tm:(k + 1) * tm] = tr_t[k:k + 1, :].astype(I32)


def _merge(x_all, y_hy, y_ga, y_wa, mod, branch_norm, w_out, b_out, norm_ffn, w_router, b_router, nt, ntile):
    bsz, _, d = x_all.shape
    tm = ROW_TILE
    rows = ntile * tm
    wr = jnp.zeros((d, ROUTER_LANES), F32).at[:, :N_EXPERTS].set(w_router)
    wrh = wr.astype(BF16)
    wrl = (wr - wrh.astype(F32)).astype(BF16)
    br = jnp.full((1, ROUTER_LANES), NEG_INF, F32).at[0, :N_EXPERTS].set(b_router)
    tri = (jnp.arange(tm, dtype=I32)[:, None] > jnp.arange(tm, dtype=I32)[None, :]).astype(BF16)
    tok = lambda w: pl.BlockSpec((1, tm, w), lambda b, i: (b, i, 0))
    const = lambda s: pl.BlockSpec(s, lambda b, i: (0,) * len(s))
    flat = pl.BlockSpec((1, 1, TOP_K * tm), lambda b, i: (b * ntile + i, 0, 0))
    return pl.pallas_call(
        _merge_kernel,
        grid=(bsz, ntile),
        in_specs=[
            tok(d), tok(HY_WIDTH), tok(Q_WIDTH), tok(Q_WIDTH),
            pl.BlockSpec((1, 1, 6, d), lambda b, i: (b, i // nt, 0, 0)),
            const((1, MIX_WIDTH)), const((MIX_WIDTH, d)), const((1, d)), const((1, d)),
            const((d, ROUTER_LANES)), const((d, ROUTER_LANES)), const((1, ROUTER_LANES)), const((tm, tm)),
        ],
        out_specs=[tok(d), pl.BlockSpec((None, tm * 8, LANES), lambda b, i: (b, i, 0)), tok(ROUTER_LANES), flat, flat,
                   const((1, ROUTER_LANES))],
        out_shape=[
            jax.ShapeDtypeStruct((bsz, rows, d), F32),
            jax.ShapeDtypeStruct((bsz, rows * 8, LANES), F32),
            jax.ShapeDtypeStruct((bsz, rows, ROUTER_LANES), F32),
            jax.ShapeDtypeStruct((bsz * ntile, 1, TOP_K * tm), I32),
            jax.ShapeDtypeStruct((bsz * ntile, 1, TOP_K * tm), I32),
            jax.ShapeDtypeStruct((1, ROUTER_LANES), F32),
        ],
        scratch_shapes=[pltpu.VMEM((1, ROUTER_LANES), F32)],
        compiler_params=_cp(("arbitrary", "arbitrary")),
        name="merge_router",
    )(x_all, y_hy, y_ga, y_wa, mod, branch_norm.reshape(1, MIX_WIDTH), w_out.astype(BF16), b_out.reshape(1, d),
      norm_ffn.reshape(1, d), wrh, wrl, br, tri)


def _moe_plan(te, tr, cnt, tm):
    steps, _, per = te.shape
    n_assign = steps * per
    n_blocks = -(-n_assign // tm) + N_EXPERTS
    counts = cnt[0, :N_EXPERTS].astype(I32)
    padded = (counts + tm - 1) // tm * tm
    ends = jnp.cumsum(padded)
    starts = ends - padded
    experts = jnp.arange(N_EXPERTS, dtype=I32)
    slot = jnp.sum(jnp.where(te[..., None] == experts, starts, 0), axis=-1) + tr
    tok = (jnp.arange(steps, dtype=I32)[:, None, None] * ROW_TILE + jnp.arange(per, dtype=I32)[None, None, :] % ROW_TILE)
    slot_tok = jnp.zeros((n_blocks * tm,), I32).at[slot.reshape(-1)].set(tok.reshape(-1), unique_indices=True)
    block_expert = jnp.minimum(jnp.searchsorted(ends, jnp.arange(n_blocks, dtype=I32) * tm, side='right'), N_EXPERTS - 1).astype(I32)
    n_used = (ends[-1] // tm).astype(I32).reshape(1)
    return slot, slot_tok.reshape(n_blocks, 1, tm), block_expert, n_used


def _ffn_kernel(be_ref, nu_ref, tok_ref, tokn_ref, h_hbm, w1_ref, b1_ref, w2_ref, b2_ref, y_ref,
                xbuf0, xbuf1, w1b, w2b, sem):
    i = pl.program_id(0)
    tm = xbuf0.shape[0] // 8
    n_used = nu_ref[0]
    f = w2_ref.shape[1]

    def issue(idx_ref, buf, s):
        for r in range(tm):
            src = pl.multiple_of(idx_ref[0, 0, r], 8)
            pltpu.make_async_copy(h_hbm.at[pl.ds(src, 8), :], buf.at[pl.ds(r * 8, 8), :], sem.at[s]).start()

    def wait(buf, s):
        pltpu.make_async_copy(h_hbm.at[pl.ds(0, tm * 8), :], buf, sem.at[s]).wait()

    @pl.when(i == 0)
    def _():
        issue(tok_ref, xbuf0, 0)

    def step(cur, nxt, sc, sn):
        wait(cur, sc)

        @pl.when(i < n_used)
        def _():
            @pl.when((i == 0) | (be_ref[i] != be_ref[jnp.maximum(i - 1, 0)]))
            def _():
                w1b[...] = w1_ref[0].astype(BF16)
                w2b[...] = w2_ref[0].astype(BF16)

            issue(tokn_ref, nxt, sn)
            a = jnp.dot(_load_row_tiles(cur, 0, tm).astype(BF16), w1b[...], preferred_element_type=F32) + b1_ref[0]
            glu = jnp.minimum(a[:, :f], SWIGLU_LIMIT)
            lin = jnp.clip(a[:, f:], -SWIGLU_LIMIT, SWIGLU_LIMIT)
            act = glu / (1.0 + jnp.exp(-SWIGLU_ALPHA * glu)) * (lin + 1.0)
            _store_row_tiles(y_ref, jnp.dot(act.astype(BF16), w2b[...], preferred_element_type=F32) + b2_ref[0])

        @pl.when(i >= n_used)
        def _():
            issue(tokn_ref, nxt, sn)
            y_ref[...] = jnp.zeros(y_ref.shape, F32)

        @pl.when(i == pl.num_programs(0) - 1)
        def _():
            wait(nxt, sn)

    @pl.when(i % 2 == 0)
    def _():
        step(xbuf0, xbuf1, 0, 1)

    @pl.when(i % 2 == 1)
    def _():
        step(xbuf1, xbuf0, 1, 0)


def _moe_ffn(h_rows, slot_tok, block_expert, n_used, w1, b1, w2, b2):
    n_blocks, _, tm = slot_tok.shape
    n_exp, d, f2 = w1.shape
    f = w2.shape[1]
    last = n_blocks - 1
    grid_spec = pltpu.PrefetchScalarGridSpec(
        num_scalar_prefetch=2,
        grid=(n_blocks,),
        in_specs=[
            pl.BlockSpec((1, 1, tm), lambda i, be, nu: (i, 0, 0), memory_space=pltpu.SMEM),
            pl.BlockSpec((1, 1, tm), lambda i, be, nu: (jnp.minimum(i + 1, last), 0, 0), memory_space=pltpu.SMEM),
            pl.BlockSpec(memory_space=pl.ANY),
            pl.BlockSpec((1, d, f2), lambda i, be, nu: (be[i], 0, 0)),
            pl.BlockSpec((1, 1, f2), lambda i, be, nu: (be[i], 0, 0)),
            pl.BlockSpec((1, f, d), lambda i, be, nu: (be[i], 0, 0)),
            pl.BlockSpec((1, 1, d), lambda i, be, nu: (be[i], 0, 0)),
        ],
        out_specs=pl.BlockSpec((tm * 8, LANES), lambda i, be, nu: (i, 0)),
        scratch_shapes=[
            pltpu.VMEM((tm * 8, LANES), F32),
            pltpu.VMEM((tm * 8, LANES), F32),
            pltpu.VMEM((d, f2), BF16),
            pltpu.VMEM((f, d), BF16),
            pltpu.SemaphoreType.DMA((2,)),
        ],
    )
    return pl.pallas_call(
        _ffn_kernel,
        grid_spec=grid_spec,
        out_shape=jax.ShapeDtypeStruct((n_blocks * tm * 8, LANES), F32),
        compiler_params=_cp(("arbitrary",)),
        name="moe_ffn",
    )(block_expert, n_used, slot_tok, slot_tok, h_rows, w1, b1.reshape(n_exp, 1, f2), w2, b2.reshape(n_exp, 1, d))


def _combine_kernel(slot_ref, slotn_ref, y_hbm, x_ref, mod_ref, gate_ref, nf_ref, o_ref, ybuf0, ybuf1, sem, *, final):
    b = pl.program_id(0)
    i = pl.program_id(1)
    step = b * pl.num_programs(1) + i
    n_steps = pl.num_programs(0) * pl.num_programs(1)
    tc = x_ref.shape[1]
    n_rows = TOP_K * tc

    def issue(idx_ref, buf, s):
        for j in range(n_rows):
            src = pl.multiple_of(idx_ref[0, 0, j], 8)
            pltpu.make_async_copy(y_hbm.at[pl.ds(src, 8), :], buf.at[pl.ds(j * 8, 8), :], sem.at[s]).start()

    def wait(buf, s):
        pltpu.make_async_copy(y_hbm.at[pl.ds(0, n_rows * 8), :], buf, sem.at[s]).wait()

    @pl.when(step == 0)
    def _():
        issue(slot_ref, ybuf0, 0)

    def run(cur, nxt, sc, sn):
        wait(cur, sc)
        issue(slotn_ref, nxt, sn)
        gates = gate_ref[0]
        acc = gates[:, 0:1] * _load_row_tiles(cur, 0, tc)
        for k in range(1, TOP_K):
            acc += gates[:, k:k + 1] * _load_row_tiles(cur, k * tc, tc)
        out = x_ref[0] + mod_ref[0, 0, 5:6, :] * acc
        if final:
            out = _rms(out, nf_ref[...])
        o_ref[0] = out

        @pl.when(step == n_steps - 1)
        def _():
            wait(nxt, sn)

    @pl.when(step % 2 == 0)
    def _():
        run(ybuf0, ybuf1, 0, 1)

    @pl.when(step % 2 == 1)
    def _():
        run(ybuf1, ybuf0, 1, 0)


def _combine(y_sorted, slot, x_mid, mod, gates, norm_final, nt, ntile, final):
    bsz, _, d = x_mid.shape
    tc = ROW_TILE
    n_steps = bsz * ntile
    slots = slot * 8
    tok = lambda w: pl.BlockSpec((1, tc, w), lambda b, i: (b, i, 0))
    return pl.pallas_call(
        functools.partial(_combine_kernel, final=final),
        grid=(bsz, ntile),
        in_specs=[
            pl.BlockSpec((1, 1, tc * TOP_K), lambda b, i: (b * ntile + i, 0, 0), memory_space=pltpu.SMEM),
            pl.BlockSpec((1, 1, tc * TOP_K), lambda b, i: (jnp.minimum(b * ntile + i + 1, n_steps - 1), 0, 0),
                         memory_space=pltpu.SMEM),
            pl.BlockSpec(memory_space=pl.ANY),
            tok(d),
            pl.BlockSpec((1, 1, 6, d), lambda b, i: (b, i // nt, 0, 0)),
            tok(ROUTER_LANES),
            pl.BlockSpec((1, d), lambda b, i: (0, 0)),
        ],
        out_specs=tok(d),
        out_shape=jax.ShapeDtypeStruct((bsz, ntile * tc, d), F32),
        scratch_shapes=[pltpu.VMEM((TOP_K * tc * 8, LANES), F32), pltpu.VMEM((TOP_K * tc * 8, LANES), F32),
                        pltpu.SemaphoreType.DMA((2,))],
        compiler_params=_cp(("arbitrary", "arbitrary")),
        name="moe_combine",
    )(slots, slots, y_sorted, x_mid, mod, gates, norm_final.reshape(1, d))


def _split_dot(a, w):
    ah = a.astype(BF16)
    al = (a - ah.astype(F32)).astype(BF16)
    wh = w.astype(BF16)
    wl = (w - wh.astype(F32)).astype(BF16)
    return (jnp.dot(ah, wh, preferred_element_type=F32) + jnp.dot(al, wh, preferred_element_type=F32)
            + jnp.dot(ah, wl, preferred_element_type=F32))


def _filt_kernel(z_ref, w1_ref, b1_ref, w2_ref, b2_ref, w3_ref, b3_ref, fr_ref, wo_ref, dec_ref, o_ref):
    fr = fr_ref[...]
    h = jnp.sin(fr * (_split_dot(z_ref[...], w1_ref[...]) + b1_ref[...]))
    h = jnp.sin(fr * (_split_dot(h, w2_ref[...]) + b2_ref[...]))
    h = jnp.sin(fr * (_split_dot(h, w3_ref[...]) + b3_ref[...]))
    o_ref[...] = _split_dot(h, wo_ref[...]) * dec_ref[...]


def _hyena_filters_pl(n, filt):
    w1, b1, w2, b2, w3, b3, freq, w_out = filt
    fw = w2.shape[0]
    t = jnp.linspace(0.0, 1.0, n, dtype=F32)[:, None]
    wpos = (2.0 * math.pi / n) * jnp.arange(n, dtype=F32)[:, None]
    bands = jnp.linspace(1e-4, HY_BANDS - 1, HY_BANDS, dtype=F32)
    z = jnp.concatenate([t, jnp.cos(wpos * bands), -jnp.sin(wpos * bands), jnp.zeros((n, fw - HY_EMB), F32)], axis=-1)
    w1p = jnp.concatenate([w1, jnp.zeros((fw - HY_EMB, fw), F32)], axis=0)
    max_decay = math.log(HY_TARGET) / HY_FAST_DECAY
    min_decay = math.log(HY_TARGET) / HY_SLOW_DECAY
    deltas = jnp.linspace(min_decay, max_decay, HY_WIDTH, dtype=F32)
    decay = jnp.tile(jnp.exp(-t * jnp.abs(deltas)), (1, 2))
    tm = min(n, 512)
    row = lambda w: pl.BlockSpec((tm, w), lambda i: (i, 0))
    const = lambda s: pl.BlockSpec(s, lambda i: (0, 0))
    return pl.pallas_call(
        _filt_kernel,
        grid=(n // tm,),
        in_specs=[row(fw), const((fw, fw)), const((1, fw)), const((fw, fw)), const((1, fw)), const((fw, fw)), const((1, fw)),
                  const((1, fw)), const((fw, 2 * HY_WIDTH)), row(2 * HY_WIDTH)],
        out_specs=row(2 * HY_WIDTH),
        out_shape=jax.ShapeDtypeStruct((n, 2 * HY_WIDTH), F32),
        compiler_params=_cp(("arbitrary",)),
        name="hyena_filters",
    )(z, w1p, b1.reshape(1, fw), w2, b2.reshape(1, fw), w3, b3.reshape(1, fw), freq.reshape(1, fw), w_out, decay)


def _gate_kernel(u_ref, prev_ref, next_ref, w_ref, b_ref, z_ref, x0_ref, *, nt):
    i = pl.program_id(1)
    u = u_ref[0]
    tm = u.shape[0]
    w = w_ref[...]
    has_prev = jnp.where((i != 0) & (i != nt), 1.0, 0.0)
    has_next = jnp.where((i != nt - 1) & (i != nt), 1.0, 0.0)
    row = lax.broadcasted_iota(I32, u.shape, 0)
    up = jnp.where(row == 0, prev_ref[0, 7:8, :] * has_prev, pltpu.roll(u, 1, 0))
    un = jnp.where(row == tm - 1, next_ref[0, 0:1, :] * has_next, pltpu.roll(u, tm - 1, 0))
    uc = up * w[0:1, :] + u * w[1:2, :] + un * w[2:3, :] + b_ref[...]
    x0_ref[0] = uc[:, :HY_WIDTH]
    z_ref[0] = uc[:, 2 * HY_WIDTH:] * uc[:, HY_WIDTH:2 * HY_WIDTH]


def _hyena_gate(hy, conv_w, conv_b, nt, ntile):
    bsz, rows, w3 = hy.shape
    tm = ROW_TILE
    sub = tm // 8
    last8 = rows // 8 - 1
    out = pl.BlockSpec((1, tm, HY_WIDTH), lambda b, i: (b, i, 0))
    return pl.pallas_call(
        functools.partial(_gate_kernel, nt=nt),
        grid=(bsz, ntile),
        in_specs=[
            pl.BlockSpec((1, tm, w3), lambda b, i: (b, i, 0)),
            pl.BlockSpec((1, 8, w3), lambda b, i: (b, jnp.maximum(i * sub - 1, 0), 0)),
            pl.BlockSpec((1, 8, w3), lambda b, i: (b, jnp.minimum((i + 1) * sub, last8), 0)),
            pl.BlockSpec((HY_SHORT, w3), lambda b, i: (0, 0)),
            pl.BlockSpec((1, w3), lambda b, i: (0, 0)),
        ],
        out_specs=[out, out],
        out_shape=[jax.ShapeDtypeStruct((bsz, ntile * tm, HY_WIDTH), F32)] * 2,
        compiler_params=_cp(("arbitrary", "arbitrary")),
        name="hyena_gate",
    )(hy, hy, hy, conv_w, conv_b.reshape(1, w3))


FFT_N2 = LANES


def _dft_tables(n_fft):
    n1 = n_fft // FFT_N2
    k1 = jnp.arange(n1, dtype=I32)
    a = (2.0 * math.pi / n1) * ((k1[:, None] * k1[None, :]) % n1).astype(F32)
    f1 = jnp.stack([jnp.cos(a), -jnp.sin(a)], axis=1).reshape(2 * n1, n1)
    g = (jnp.stack([jnp.cos(a), -jnp.sin(a)], axis=2).reshape(n1, 2 * n1) / n_fft)
    k2 = jnp.arange(FFT_N2, dtype=I32)
    m = (k2[None, None, :] * (k1[:, None, None] + n1 * k2[None, :, None])) % n_fft
    th = (2.0 * math.pi / n_fft) * m.astype(F32)
    mre, mim = jnp.cos(th), -jnp.sin(th)
    mr = jnp.concatenate([jnp.concatenate([mre, -mim], axis=2), jnp.concatenate([mim, mre], axis=2)], axis=1)
    return f1.astype(BF16), g.astype(BF16), mr.astype(BF16), jnp.swapaxes(mr, 1, 2).astype(BF16)


def _fft1_kernel(x_ref, f_ref, o_ref, *, n1_in):
    f = f_ref[...]
    n_rows = f.shape[0]

    def body(n2, carry):
        xs = x_ref[pl.ds(n2, n1_in, stride=FFT_N2), :].astype(BF16)
        o_ref[pl.ds(n2, n_rows, stride=FFT_N2), :] = jnp.dot(f, xs, preferred_element_type=F32)
        return carry
    lax.fori_loop(0, FFT_N2, body, 0, unroll=8)


def _fft1(x, f1, n1_in):
    bsz, _, ch = x.shape
    n_rows = f1.shape[0]
    return pl.pallas_call(
        functools.partial(_fft1_kernel, n1_in=n1_in),
        grid=(bsz, ch // LANES),
        in_specs=[pl.BlockSpec((None, n1_in * FFT_N2, LANES), lambda b, h: (b, 0, h)),
                  pl.BlockSpec((n_rows, n1_in), lambda b, h: (0, 0))],
        out_specs=pl.BlockSpec((None, n_rows * FFT_N2, LANES), lambda b, h: (b, 0, h)),
        out_shape=jax.ShapeDtypeStruct((bsz, n_rows * FFT_N2, ch), F32),
        compiler_params=_cp(("arbitrary", "arbitrary")),
        name="hyena_fft1",
    )(x, f1[:, :n1_in])


def _spec_kernel(a_ref, mr_ref, o_ref):
    o_ref[0] = jnp.dot(mr_ref[0], a_ref[0].astype(BF16), preferred_element_type=F32)


def _fft_spectrum(a, mr):
    _, rows, ch = a.shape
    blk = 2 * FFT_N2
    return pl.pallas_call(
        _spec_kernel,
        grid=(rows // blk,),
        in_specs=[pl.BlockSpec((1, blk, ch), lambda k: (0, k, 0)), pl.BlockSpec((1, blk, blk), lambda k: (k, 0, 0))],
        out_specs=pl.BlockSpec((1, blk, ch), lambda k: (0, k, 0)),
        out_shape=jax.ShapeDtypeStruct((1, rows, ch), F32),
        compiler_params=_cp(("arbitrary",)),
        name="hyena_filter_spectrum",
    )(a, mr)


def _mid_kernel(a_ref, mr_ref, mrt_ref, ks_ref, o_ref):
    kre, kim = ks_ref[0, :FFT_N2, :], ks_ref[0, FFT_N2:, :]
    for b in range(a_ref.shape[0]):
        x = jnp.dot(mr_ref[0], a_ref[b].astype(BF16), preferred_element_type=F32)
        xre, xim = x[:FFT_N2], x[FFT_N2:]
        y = jnp.concatenate([xre * kre - xim * kim, xre * kim + xim * kre], axis=0)
        o_ref[b] = jnp.dot(mrt_ref[0], y.astype(BF16), preferred_element_type=F32)


def _fft_mid(a, mr, mrt, kspec):
    bsz, rows, ch = a.shape
    blk = 2 * FFT_N2
    return pl.pallas_call(
        _mid_kernel,
        grid=(rows // blk,),
        in_specs=[pl.BlockSpec((bsz, blk, ch), lambda k: (0, k, 0)),
                  pl.BlockSpec((1, blk, blk), lambda k: (k, 0, 0)),
                  pl.BlockSpec((1, blk, blk), lambda k: (k, 0, 0)),
                  pl.BlockSpec((1, blk, ch), lambda k: (0, k, 0))],
        out_specs=pl.BlockSpec((bsz, blk, ch), lambda k: (0, k, 0)),
        out_shape=jax.ShapeDtypeStruct((bsz, rows, ch), F32),
        compiler_params=_cp(("arbitrary",)),
        name="hyena_fft_mid",
    )(a, mr, mrt, kspec)


def _ifft1_kernel(b_ref, g_ref, z_ref, x0_ref, skip_ref, o_ref, *, n1_out):
    g = g_ref[...]
    n_rows = g.shape[1]
    skip = skip_ref[...]

    def body(n2, carry):
        bs = b_ref[pl.ds(n2, n_rows, stride=FFT_N2), :].astype(BF16)
        y = jnp.dot(g, bs, preferred_element_type=F32)
        rows = pl.ds(n2, n1_out, stride=FFT_N2)
        zs = z_ref[rows, :]
        o_ref[rows, :] = x0_ref[rows, :] * (y + zs * skip)
        return carry
    lax.fori_loop(0, FFT_N2, body, 0, unroll=8)


def _ifft1(bmat, g, z, x0, skip, n1_out):
    bsz, rows, ch = bmat.shape
    n = n1_out * FFT_N2
    seq = pl.BlockSpec((None, n, LANES), lambda b, h: (b, 0, h))
    return pl.pallas_call(
        functools.partial(_ifft1_kernel, n1_out=n1_out),
        grid=(bsz, ch // LANES),
        in_specs=[pl.BlockSpec((None, rows, LANES), lambda b, h: (b, 0, h)),
                  pl.BlockSpec((n1_out, g.shape[1]), lambda b, h: (0, 0)),
                  seq, seq, pl.BlockSpec((1, LANES), lambda b, h: (0, h))],
        out_specs=seq,
        out_shape=jax.ShapeDtypeStruct((bsz, n, ch), F32),
        compiler_params=_cp(("arbitrary", "arbitrary")),
        name="hyena_ifft1",
    )(bmat, g[:n1_out], z, x0, skip.reshape(1, ch))


def _hyena_fftconv(z, x0, k2, skip):
    n = k2.shape[0]
    n_fft = 2 * n
    f1, g, mr, mrt = _dft_tables(n_fft)
    k_full = jnp.concatenate([k2[:, :HY_WIDTH], k2[::-1, HY_WIDTH:]], axis=0)[None]
    kspec = _fft_spectrum(_fft1(k_full, f1, n_fft // FFT_N2), mr)
    a = _fft1(z, f1, n // FFT_N2)
    bmat = _fft_mid(a, mr, mrt, kspec)
    return _ifft1(bmat, g, z, x0, skip, n // FFT_N2)


def _ctxconv_kernel(z_ref, x0_ref, kf_ref, fd_ref, gd_ref, skip_ref, o_ref, ks_ref):
    n = z_ref.shape[1]
    half = fd_ref.shape[0] // 2

    @pl.when(pl.program_id(0) == 0)
    def _():
        ks_ref[...] = jnp.dot(fd_ref[...], kf_ref[...].astype(BF16), preferred_element_type=F32)

    z = z_ref[0]
    x = jnp.dot(fd_ref[:, :n], z.astype(BF16), preferred_element_type=F32)
    xre, xim = x[:half], x[half:]
    kre, kim = ks_ref[:half, :], ks_ref[half:, :]
    y = jnp.concatenate([xre * kre - xim * kim, xre * kim + xim * kre], axis=0)
    conv = jnp.dot(gd_ref[...], y.astype(BF16), preferred_element_type=F32)
    o_ref[0] = x0_ref[0] * (conv + z * skip_ref[...])


def _hyena_ctxconv(z, x0, k2, skip, row0):
    n = k2.shape[0]
    bsz, _, ch = z.shape
    n_fft = 2 * n
    kk = jnp.arange(n_fft, dtype=I32)
    th = (2.0 * math.pi / n_fft) * ((kk[:, None] * kk[None, :]) % n_fft).astype(F32)
    fd = jnp.concatenate([jnp.cos(th), -jnp.sin(th)], axis=0).astype(BF16)
    gd = (jnp.concatenate([jnp.cos(th[:n]), -jnp.sin(th[:n])], axis=1) / n_fft).astype(BF16)
    k_full = jnp.concatenate([k2[:, :HY_WIDTH], k2[::-1, HY_WIDTH:]], axis=0)
    blk = row0 // n
    seq = pl.BlockSpec((1, n, ch), lambda b: (b, blk, 0))
    const = lambda s: pl.BlockSpec(s, lambda b: (0, 0))
    return pl.pallas_call(
        _ctxconv_kernel,
        grid=(bsz,),
        in_specs=[seq, seq, const((n_fft, ch)), const((2 * n_fft, n_fft)), const((n, 2 * n_fft)), const((1, ch))],
        out_specs=pl.BlockSpec((1, n, ch), lambda b: (b, 0, 0)),
        out_shape=jax.ShapeDtypeStruct((bsz, n, ch), F32),
        scratch_shapes=[pltpu.VMEM((2 * n_fft, ch), F32)],
        compiler_params=_cp(("arbitrary",)),
        name="hyena_ctxconv",
    )(z, x0, k_full, fd, gd, skip.reshape(1, ch))


def _rope_tables(n_lat, rows):
    nrow = n_lat // GRID_W
    row = jnp.repeat(jnp.arange(nrow, dtype=F32), GRID_W)
    col = jnp.tile(jnp.arange(GRID_W, dtype=F32), nrow)
    n_freq = HEAD_DIM // 4
    inv_freq = ROPE_THETA ** (-jnp.arange(n_freq, dtype=F32) / n_freq)
    ar = row[:, None] * inv_freq
    ac = col[:, None] * inv_freq
    cos64 = jnp.concatenate([jnp.cos(ar), jnp.cos(ar), jnp.cos(ac), jnp.cos(ac)], axis=-1)
    sin64 = jnp.concatenate([-jnp.sin(ar), jnp.sin(ar), -jnp.sin(ac), jnp.sin(ac)], axis=-1)
    pad = rows - n_lat
    cos_t = jnp.concatenate([jnp.tile(cos64, (1, LANES // HEAD_DIM)), jnp.ones((pad, LANES), F32)], axis=0)
    sin_t = jnp.concatenate([jnp.tile(sin64, (1, LANES // HEAD_DIM)), jnp.zeros((pad, LANES), F32)], axis=0)
    return cos_t, sin_t


def kernel(x, c, ctx, c_ctx, w_ada, b_ada, norm_mix, norm_ffn, w_in, b_in, hy_conv_w, hy_conv_b, hy_filt_w1, hy_filt_b1, hy_filt_w2, hy_filt_b2, hy_filt_w3, hy_filt_b3, hy_filt_freq, hy_filt_out, hy_skip, ga_q_norm, ga_k_norm, wa_sink, branch_norm, w_out, b_out, w_router, b_router, w_mlp1, b_mlp1, w_mlp2, b_mlp2, norm_final):
    bsz, n_lat, d = x.shape
    n_ctx = ctx.shape[1]
    depth = w_ada.shape[0]
    assert n_ctx == ROW_TILE and n_lat % KV_CHUNK == 0 and bsz < 16 and d == D_MODEL
    nt = n_lat // ROW_TILE
    rows = n_lat + n_ctx
    cos_t, sin_t = _rope_tables(n_lat, rows)

    c_all = jnp.zeros((16, d), F32).at[:bsz].set(c).at[bsz].set(c_ctx)
    mod_all = _ada_mod(c_all, w_ada, b_ada)
    x_all = jnp.concatenate([x, ctx], axis=1)

    for l in range(depth):
        last = l == depth - 1
        ntile = nt if last else nt + 1
        m = mod_all[l].reshape(16, 6, d)
        mod = jnp.stack([m[:bsz], jnp.broadcast_to(m[bsz], (bsz, 6, d))], axis=1)

        hy, gq, gk, gv, wq, wk, wv = _inproj(x_all, mod, norm_mix[l], w_in[l], b_in[l], cos_t, sin_t,
                                             ga_q_norm[l], ga_k_norm[l], nt)
        filt = (hy_filt_w1[l], hy_filt_b1[l], hy_filt_w2[l], hy_filt_b2[l], hy_filt_w3[l], hy_filt_b3[l],
                hy_filt_freq[l], hy_filt_out[l])
        z, x0 = _hyena_gate(hy, hy_conv_w[l], hy_conv_b[l], nt, ntile)
        y_hy = _hyena_fftconv(z, x0, _hyena_filters_pl(n_lat, filt), hy_skip[l])
        if not last:
            yc_hy = _hyena_ctxconv(z, x0, _hyena_filters_pl(n_ctx, filt), hy_skip[l], n_lat)
            y_hy = jnp.concatenate([y_hy, yc_hy], axis=1)
        y_ga = _global_attention(gq, gk, gv, nt, ntile)
        y_wa = _window_attention(wq, wk, wv, wa_sink[l], nt, ntile)
        x_mid, h2, top_g, top_e, top_r, counts = _merge(x_all, y_hy, y_ga, y_wa, mod, branch_norm[l], w_out[l], b_out[l],
                                                        norm_ffn[l], w_router[l], b_router[l], nt, ntile)
        slot, slot_tok, block_expert, n_used = _moe_plan(top_e, top_r, counts, MOE_TILE)
        y_sorted = _moe_ffn(h2.reshape(-1, LANES), slot_tok * 8, block_expert, n_used, w_mlp1[l], b_mlp1[l], w_mlp2[l], b_mlp2[l])
        x_all = _combine(y_sorted, slot, x_mid, mod, top_g, norm_final, nt, ntile, last)
    return x_all
```

```python
import functools
import math

import jax
import jax.numpy as jnp
from jax import lax
from jax.experimental import pallas as pl
from jax.experimental.pallas import tpu as pltpu

F32 = jnp.float32
BF16 = jnp.bfloat16
I32 = jnp.int32

D_MODEL = 1024
HEAD_DIM = 64
GRID_W = 64
HY_WIDTH = 256
GA_HEADS = 6
GA_KV_HEADS = 2
WA_HEADS = 6
WA_KV_HEADS = 2
GROUP = GA_HEADS // GA_KV_HEADS
Q_WIDTH = GA_HEADS * HEAD_DIM
KV_WIDTH = GA_KV_HEADS * HEAD_DIM
MIX_WIDTH = HY_WIDTH + 2 * Q_WIDTH
IN_WIDTH = 3 * HY_WIDTH + 2 * (Q_WIDTH + 2 * KV_WIDTH)
WINDOW = 128
ROPE_THETA = 10000.0
ATTN_SCALE = HEAD_DIM ** -0.5
HY_SHORT = 3
HY_EMB = 33
HY_BANDS = (HY_EMB - 1) // 2
HY_FAST_DECAY = 0.3
HY_SLOW_DECAY = 1.5
HY_TARGET = 1e-2
N_EXPERTS = 32
TOP_K = 4
SWIGLU_ALPHA = 1.702
SWIGLU_LIMIT = 7.0
EPS = 1e-6
NEG_INF = -1e30

LANES = 128
VMEM_LIMIT_V7X = 56 * 1024 * 1024

ROW_TILE = 256
KV_CHUNK = 512
MOE_TILE = 512
FFN_CHUNKS = 4
ROUTER_LANES = LANES


def _cp(sem, vmem=VMEM_LIMIT_V7X):
    return pltpu.CompilerParams(dimension_semantics=sem, vmem_limit_bytes=vmem)


def _rms(v, w):
    return v * lax.rsqrt(jnp.mean(v * v, axis=-1, keepdims=True) + EPS) * w


def _ada_kernel(c_ref, w_ref, b_ref, o_ref):
    c = c_ref[...]
    s = c / (1.0 + jnp.exp(-c))
    hi = s.astype(BF16)
    lo = (s - hi.astype(F32)).astype(BF16)
    w = w_ref[0]
    whi = w.astype(BF16)
    wlo = (w - whi.astype(F32)).astype(BF16)
    acc = jnp.dot(hi, whi, preferred_element_type=F32)
    acc += jnp.dot(lo, whi, preferred_element_type=F32)
    acc += jnp.dot(hi, wlo, preferred_element_type=F32)
    o_ref[0] = acc + b_ref[0]


def _ada_mod(c_all, w_ada, b_ada):
    n_layers, d, n6 = w_ada.shape
    tn = n6 // 4
    return pl.pallas_call(
        _ada_kernel,
        grid=(n_layers, n6 // tn),
        in_specs=[
            pl.BlockSpec((16, d), lambda l, j: (0, 0)),
            pl.BlockSpec((1, d, tn), lambda l, j: (l, 0, j)),
            pl.BlockSpec((1, 1, tn), lambda l, j: (l, 0, j)),
        ],
        out_specs=pl.BlockSpec((1, 16, tn), lambda l, j: (l, 0, j)),
        out_shape=jax.ShapeDtypeStruct((n_layers, 16, n6), F32),
        compiler_params=_cp(("arbitrary", "arbitrary")),
        name="ada_mod",
    )(c_all, w_ada, b_ada.reshape(n_layers, 1, n6))


def _inproj_kernel(x_ref, mod_ref, nw_ref, w_ref, b_ref, cos_ref, sin_ref, qn_ref, kn_ref, g_ref,
                   hy_ref, gq_ref, gk_ref, gv_ref, wq_ref, wk_ref, wv_ref):
    x = x_ref[0]
    tm = x.shape[0]
    shift = mod_ref[0, 0, 0:1, :]
    scale = mod_ref[0, 0, 1:2, :]
    h = _rms(x, nw_ref[...]) * (1.0 + scale) + shift
    p = jnp.dot(h.astype(BF16), w_ref[...], preferred_element_type=F32) + b_ref[...]
    hy_ref[0] = p[:, :3 * HY_WIDTH]

    cos = cos_ref[...]
    sin = sin_ref[...]
    lane = lax.broadcasted_iota(I32, (tm, LANES), 1)
    first = (lane % (HEAD_DIM // 2)) < (HEAD_DIM // 4)
    gmat = g_ref[...]

    def rope(v):
        sw = jnp.where(first, pltpu.roll(v, LANES - HEAD_DIM // 4, 1), pltpu.roll(v, HEAD_DIM // 4, 1))
        return v * cos + sw * sin

    def head_norm(v, w):
        t = v * v
        hi = t.astype(BF16)
        lo = (t - hi.astype(F32)).astype(BF16)
        ms = jnp.dot(hi, gmat, preferred_element_type=F32) + jnp.dot(lo, gmat, preferred_element_type=F32)
        return v * lax.rsqrt(ms + EPS) * w

    o = 3 * HY_WIDTH
    for u in range(Q_WIDTH // LANES):
        v = p[:, o + u * LANES:o + (u + 1) * LANES]
        v = rope(head_norm(v, qn_ref[...]))
        gq_ref[0, :, u * LANES:(u + 1) * LANES] = (v * ATTN_SCALE).astype(BF16)
    o += Q_WIDTH
    gk_ref[0] = rope(head_norm(p[:, o:o + KV_WIDTH], kn_ref[...])).astype(BF16)
    o += KV_WIDTH
    gv_ref[0] = p[:, o:o + KV_WIDTH].astype(BF16)
    o += KV_WIDTH
    for u in range(Q_WIDTH // LANES):
        v = rope(p[:, o + u * LANES:o + (u + 1) * LANES])
        wq_ref[0, :, u * LANES:(u + 1) * LANES] = (v * ATTN_SCALE).astype(BF16)
    o += Q_WIDTH
    wk_ref[0] = rope(p[:, o:o + KV_WIDTH]).astype(BF16)
    o += KV_WIDTH
    wv_ref[0] = p[:, o:o + KV_WIDTH].astype(BF16)


def _inproj(x_all, mod, norm_w, w_in, b_in, cos_t, sin_t, q_norm, k_norm, nt):
    bsz, rows, d = x_all.shape
    tm = ROW_TILE
    ntile = rows // tm
    gmat = jnp.kron(jnp.eye(LANES // HEAD_DIM, dtype=F32), jnp.full((HEAD_DIM, HEAD_DIM), 1.0 / HEAD_DIM, F32)).astype(BF16)
    qn = jnp.tile(q_norm, LANES // HEAD_DIM).reshape(1, LANES)
    kn = jnp.tile(k_norm, LANES // HEAD_DIM).reshape(1, LANES)
    tok = lambda w: pl.BlockSpec((1, tm, w), lambda b, i: (b, i, 0))
    const = lambda s: pl.BlockSpec(s, lambda b, i: (0,) * len(s))
    out_w = (3 * HY_WIDTH, Q_WIDTH, KV_WIDTH, KV_WIDTH, Q_WIDTH, KV_WIDTH, KV_WIDTH)
    out_dt = (F32, BF16, BF16, BF16, BF16, BF16, BF16)
    return pl.pallas_call(
        _inproj_kernel,
        grid=(bsz, ntile),
        in_specs=[
            tok(d),
            pl.BlockSpec((1, 1, 6, d), lambda b, i: (b, i // nt, 0, 0)),
            const((1, d)),
            const((d, IN_WIDTH)),
            const((1, IN_WIDTH)),
            pl.BlockSpec((tm, LANES), lambda b, i: (i, 0)),
            pl.BlockSpec((tm, LANES), lambda b, i: (i, 0)),
            const((1, LANES)),
            const((1, LANES)),
            const((LANES, LANES)),
        ],
        out_specs=[tok(w) for w in out_w],
        out_shape=[jax.ShapeDtypeStruct((bsz, rows, w), dt) for w, dt in zip(out_w, out_dt)],
        compiler_params=_cp(("arbitrary", "arbitrary")),
        name="inproj",
    )(x_all, mod, norm_w.reshape(1, d), w_in.astype(BF16), b_in.reshape(1, IN_WIDTH), cos_t, sin_t, qn, kn, gmat)


def _stack_q(q_ref, qs_ref):
    qf = q_ref[0].astype(F32)
    tq = qf.shape[0]
    for j in range(GA_KV_HEADS):
        for g in range(GROUP):
            h = GROUP * j + g
            blk = qf[:, (h // 2) * LANES:(h // 2 + 1) * LANES]
            if h % 2 != j:
                blk = pltpu.roll(blk, HEAD_DIM, 1)
            qs_ref[j, g * tq:(g + 1) * tq, :] = blk.astype(BF16)


def _unstack_o(norm_fn, o_ref, tq):
    lane = lax.broadcasted_iota(I32, (tq, LANES), 1)
    heads = []
    for j in range(GA_KV_HEADS):
        for g in range(GROUP):
            h = GROUP * j + g
            o = norm_fn(j, g)
            if h % 2 != j:
                o = pltpu.roll(o, HEAD_DIM, 1)
            heads.append(o)
    for u in range(Q_WIDTH // LANES):
        o_ref[0, :, u * LANES:(u + 1) * LANES] = jnp.where(lane < HEAD_DIM, heads[2 * u], heads[2 * u + 1])


def _half_masks(rows):
    lane = lax.broadcasted_iota(I32, (rows, LANES), 1)
    return [lane < HEAD_DIM, lane >= HEAD_DIM]


def _dot_t(a, b):
    return lax.dot_general(a, b, (((1,), (1,)), ((), ())), preferred_element_type=F32)


def _ga_kernel(q_ref, k_ref, v_ref, o_ref, qs_ref, m_ref, acc_ref, *, nt, n_lat):
    i = pl.program_id(1)
    tq = q_ref.shape[1]
    _stack_q(q_ref, qs_ref)
    m_ref[...] = jnp.full(m_ref.shape, NEG_INF, F32)
    acc_ref[...] = jnp.zeros(acc_ref.shape, F32)

    def chunk(kc, vc):
        masks = _half_masks(kc.shape[0])
        for j in range(GA_KV_HEADS):
            km = jnp.where(masks[j], kc, jnp.zeros_like(kc))
            vm = jnp.where(masks[j], vc, jnp.ones_like(vc))
            s = _dot_t(qs_ref[j], km)
            m_old = m_ref[j]
            m_new = jnp.maximum(m_old, jnp.max(s, axis=-1, keepdims=True))
            p = jnp.exp(s - jnp.tile(m_new, (1, s.shape[1] // LANES)))
            acc_ref[j] = jnp.exp(m_old - m_new) * acc_ref[j] + jnp.dot(p.astype(BF16), vm, preferred_element_type=F32)
            m_ref[j] = m_new

    @pl.when(i < nt)
    def _():
        def body(c, carry):
            st = pl.multiple_of(c * KV_CHUNK, KV_CHUNK)
            chunk(k_ref[0, pl.ds(st, KV_CHUNK), :], v_ref[0, pl.ds(st, KV_CHUNK), :])
            return carry
        lax.fori_loop(0, n_lat // KV_CHUNK, body, 0)

    chunk(k_ref[0, n_lat:, :], v_ref[0, n_lat:, :])

    def norm_fn(j, g):
        a = acc_ref[j, g * tq:(g + 1) * tq, :]
        return a / pltpu.roll(a, HEAD_DIM, 1)

    _unstack_o(norm_fn, o_ref, tq)


def _global_attention(gq, gk, gv, nt, ntile):
    bsz, rows, _ = gq.shape
    tq = ROW_TILE
    n_lat = nt * tq
    return pl.pallas_call(
        functools.partial(_ga_kernel, nt=nt, n_lat=n_lat),
        grid=(bsz, ntile),
        in_specs=[
            pl.BlockSpec((1, tq, Q_WIDTH), lambda b, i: (b, i, 0)),
            pl.BlockSpec((1, rows, KV_WIDTH), lambda b, i: (b, 0, 0)),
            pl.BlockSpec((1, rows, KV_WIDTH), lambda b, i: (b, 0, 0)),
        ],
        out_specs=pl.BlockSpec((1, tq, Q_WIDTH), lambda b, i: (b, i, 0)),
        out_shape=jax.ShapeDtypeStruct((bsz, ntile * tq, Q_WIDTH), F32),
        scratch_shapes=[
            pltpu.VMEM((GA_KV_HEADS, GROUP * tq, LANES), BF16),
            pltpu.VMEM((GA_KV_HEADS, GROUP * tq, LANES), F32),
            pltpu.VMEM((GA_KV_HEADS, GROUP * tq, LANES), F32),
        ],
        compiler_params=_cp(("arbitrary", "arbitrary")),
        name="global_attn",
    )(gq, gk, gv)


def _wa_kernel(q_ref, k_ref, v_ref, sink_ref, o_ref, qs_ref, acc_ref, l_ref, *, nt, n_lat):
    i = pl.program_id(1)
    tq = q_ref.shape[1]
    band = tq + 2 * WINDOW
    _stack_q(q_ref, qs_ref)
    st = pl.multiple_of(i * tq, tq)
    kb = k_ref[0, pl.ds(st, band), :]
    vb = v_ref[0, pl.ds(st, band), :]
    kc = k_ref[0, WINDOW + n_lat:WINDOW + n_lat + ROW_TILE, :]
    vc = v_ref[0, WINDOW + n_lat:WINDOW + n_lat + ROW_TILE, :]

    r = lax.broadcasted_iota(I32, (GROUP * tq, band), 0) % tq
    c = lax.broadcasted_iota(I32, (GROUP * tq, band), 1)
    rel = c - WINDOW - r
    kpos = c + (i * tq - WINDOW)
    valid = (jnp.abs(rel) <= WINDOW) & (kpos >= 0) & (kpos < n_lat) & (i < nt)

    mb = _half_masks(band)
    mc = _half_masks(ROW_TILE)
    for j in range(WA_KV_HEADS):
        q = qs_ref[j]
        s_loc = jnp.where(valid, _dot_t(q, jnp.where(mb[j], kb, jnp.zeros_like(kb))), NEG_INF)
        s_ctx = _dot_t(q, jnp.where(mc[j], kc, jnp.zeros_like(kc)))
        sink = jnp.concatenate(
            [jnp.broadcast_to(sink_ref[GROUP * j + g:GROUP * j + g + 1, :], (tq, LANES)) for g in range(GROUP)], axis=0)
        m = jnp.maximum(jnp.maximum(jnp.max(s_loc, axis=-1, keepdims=True), jnp.max(s_ctx, axis=-1, keepdims=True)), sink)
        p_loc = jnp.exp(s_loc - jnp.tile(m, (1, band // LANES))).astype(BF16)
        p_ctx = jnp.exp(s_ctx - jnp.tile(m, (1, ROW_TILE // LANES))).astype(BF16)
        acc_ref[j] = (jnp.dot(p_loc, jnp.where(mb[j], vb, jnp.ones_like(vb)), preferred_element_type=F32)
                      + jnp.dot(p_ctx, jnp.where(mc[j], vc, jnp.ones_like(vc)), preferred_element_type=F32))
        l_ref[j] = jnp.exp(sink - m)

    def norm_fn(j, g):
        a = acc_ref[j, g * tq:(g + 1) * tq, :]
        return a / (pltpu.roll(a, HEAD_DIM, 1) + l_ref[j, g * tq:(g + 1) * tq, :])

    _unstack_o(norm_fn, o_ref, tq)


def _window_attention(wq, wk, wv, sink, nt, ntile):
    bsz, rows, _ = wq.shape
    tq = ROW_TILE
    n_lat = nt * tq
    pad = ((0, 0), (WINDOW, WINDOW), (0, 0))
    kp = jnp.pad(wk, pad)
    vp = jnp.pad(wv, pad)
    sink_b = jnp.zeros((8, LANES), F32).at[:WA_HEADS].set(jnp.broadcast_to(sink[:, None], (WA_HEADS, LANES)))
    return pl.pallas_call(
        functools.partial(_wa_kernel, nt=nt, n_lat=n_lat),
        grid=(bsz, ntile),
        in_specs=[
            pl.BlockSpec((1, tq, Q_WIDTH), lambda b, i: (b, i, 0)),
            pl.BlockSpec((1, rows + 2 * WINDOW, KV_WIDTH), lambda b, i: (b, 0, 0)),
            pl.BlockSpec((1, rows + 2 * WINDOW, KV_WIDTH), lambda b, i: (b, 0, 0)),
            pl.BlockSpec((8, LANES), lambda b, i: (0, 0)),
        ],
        out_specs=pl.BlockSpec((1, tq, Q_WIDTH), lambda b, i: (b, i, 0)),
        out_shape=jax.ShapeDtypeStruct((bsz, ntile * tq, Q_WIDTH), F32),
        scratch_shapes=[
            pltpu.VMEM((WA_KV_HEADS, GROUP * tq, LANES), BF16),
            pltpu.VMEM((WA_KV_HEADS, GROUP * tq, LANES), F32),
            pltpu.VMEM((WA_KV_HEADS, GROUP * tq, LANES), F32),
        ],
        compiler_params=_cp(("arbitrary", "arbitrary")),
        name="window_attn",
    )(wq, kp, vp, sink_b)


def _store_row_tiles(ref, v):
    rows = v.shape[0]
    for s in range(v.shape[1] // LANES):
        ref[pl.ds(s, rows, stride=8), :] = v[:, s * LANES:(s + 1) * LANES]


def _load_row_tiles(ref, row0, rows):
    return jnp.concatenate([ref[pl.ds(row0 * 8 + s, rows, stride=8), :] for s in range(8)], axis=1)


def _merge_kernel(x_ref, hy_ref, ga_ref, wa_ref, mod_ref, bn_ref, wo_ref, bo_ref, nf_ref, wrh_ref, wrl_ref, br_ref, tri_ref,
                  xo_ref, h2_ref, tg_ref, te_ref, tr_ref, cnt_ref, base_ref):
    @pl.when((pl.program_id(0) == 0) & (pl.program_id(1) == 0))
    def _():
        base_ref[...] = jnp.zeros(base_ref.shape, F32)

    bn = bn_ref[...]
    g0 = HY_WIDTH
    g1 = HY_WIDTH + Q_WIDTH
    y = jnp.concatenate([_rms(hy_ref[0], bn[:, :g0]), _rms(ga_ref[0], bn[:, g0:g1]), _rms(wa_ref[0], bn[:, g1:])], axis=-1)
    o = jnp.dot(y.astype(BF16), wo_ref[...], preferred_element_type=F32) + bo_ref[...]
    x1 = x_ref[0] + mod_ref[0, 0, 2:3, :] * o
    xo_ref[0] = x1
    h2 = _rms(x1, nf_ref[...]) * (1.0 + mod_ref[0, 0, 4:5, :]) + mod_ref[0, 0, 3:4, :]
    _store_row_tiles(h2_ref, h2)

    hi = h2.astype(BF16)
    lo = (h2 - hi.astype(F32)).astype(BF16)
    logits = (jnp.dot(hi, wrh_ref[...], preferred_element_type=F32) + jnp.dot(lo, wrh_ref[...], preferred_element_type=F32)
              + jnp.dot(hi, wrl_ref[...], preferred_element_type=F32)) + br_ref[...]
    tm = logits.shape[0]
    lane = lax.broadcasted_iota(I32, (tm, ROUTER_LANES), 1)
    te = jnp.zeros((tm, ROUTER_LANES), F32)
    tr = jnp.zeros((tm, ROUTER_LANES), F32)
    tv = jnp.zeros((tm, ROUTER_LANES), F32)
    before = base_ref[...]
    tri = tri_ref[...]
    v0 = None
    for k in range(TOP_K):
        mx = jnp.max(logits, axis=-1, keepdims=True)
        idx = jnp.min(jnp.where(logits == mx, lane, ROUTER_LANES), axis=-1, keepdims=True)
        if k == 0:
            v0 = mx
        hit = lane == idx
        onehot = jnp.where(hit, 1.0, 0.0)
        prefix = jnp.dot(tri, onehot.astype(BF16), preferred_element_type=F32) + before
        rank = jnp.sum(onehot * prefix, axis=-1, keepdims=True)
        before = before + jnp.sum(onehot, axis=0, keepdims=True)
        te = jnp.where(lane == k, idx.astype(F32), te)
        tr = jnp.where(lane == k, rank, tr)
        tv = jnp.where(lane == k, jnp.exp(mx - v0), tv)
        logits = jnp.where(hit, NEG_INF, logits)
    base_ref[...] = before
    cnt_ref[...] = before
    tg_ref[0] = tv / jnp.sum(tv, axis=-1, keepdims=True)
    te_t = te.T
    tr_t = tr.T
    for k in range(TOP_K):
        te_ref[0, :, k * tm:(k + 1) * tm] = te_t[k:k + 1, :].astype(I32)
        tr_ref[0, :, k * tm:(k + 1) * tm] = tr_t[k:k + 1, :].astype(I32)


def _merge(x_all, y_hy, y_ga, y_wa, mod, branch_norm, w_out, b_out, norm_ffn, w_router, b_router, nt, ntile):
    bsz, _, d = x_all.shape
    tm = ROW_TILE
    rows = ntile * tm
    wr = jnp.zeros((d, ROUTER_LANES), F32).at[:, :N_EXPERTS].set(w_router)
    wrh = wr.astype(BF16)
    wrl = (wr - wrh.astype(F32)).astype(BF16)
    br = jnp.full((1, ROUTER_LANES), NEG_INF, F32).at[0, :N_EXPERTS].set(b_router)
    tri = (jnp.arange(tm, dtype=I32)[:, None] > jnp.arange(tm, dtype=I32)[None, :]).astype(BF16)
    tok = lambda w: pl.BlockSpec((1, tm, w), lambda b, i: (b, i, 0))
    const = lambda s: pl.BlockSpec(s, lambda b, i: (0,) * len(s))
    flat = pl.BlockSpec((1, 1, TOP_K * tm), lambda b, i: (b * ntile + i, 0, 0))
    return pl.pallas_call(
        _merge_kernel,
        grid=(bsz, ntile),
        in_specs=[
            tok(d), tok(HY_WIDTH), tok(Q_WIDTH), tok(Q_WIDTH),
            pl.BlockSpec((1, 1, 6, d), lambda b, i: (b, i // nt, 0, 0)),
            const((1, MIX_WIDTH)), const((MIX_WIDTH, d)), const((1, d)), const((1, d)),
            const((d, ROUTER_LANES)), const((d, ROUTER_LANES)), const((1, ROUTER_LANES)), const((tm, tm)),
        ],
        out_specs=[tok(d), pl.BlockSpec((None, tm * 8, LANES), lambda b, i: (b, i, 0)), tok(ROUTER_LANES), flat, flat,
                   const((1, ROUTER_LANES))],
        out_shape=[
            jax.ShapeDtypeStruct((bsz, rows, d), F32),
            jax.ShapeDtypeStruct((bsz, rows * 8, LANES), F32),
            jax.ShapeDtypeStruct((bsz, rows, ROUTER_LANES), F32),
            jax.ShapeDtypeStruct((bsz * ntile, 1, TOP_K * tm), I32),
            jax.ShapeDtypeStruct((bsz * ntile, 1, TOP_K * tm), I32),
            jax.ShapeDtypeStruct((1, ROUTER_LANES), F32),
        ],
        scratch_shapes=[pltpu.VMEM((1, ROUTER_LANES), F32)],
        compiler_params=_cp(("arbitrary", "arbitrary")),
        name="merge_router",
    )(x_all, y_hy, y_ga, y_wa, mod, branch_norm.reshape(1, MIX_WIDTH), w_out.astype(BF16), b_out.reshape(1, d),
      norm_ffn.reshape(1, d), wrh, wrl, br, tri)


def _moe_plan(te, tr, cnt, tm):
    steps, _, per = te.shape
    n_assign = steps * per
    n_blocks = -(-n_assign // tm) + N_EXPERTS
    counts = cnt[0, :N_EXPERTS].astype(I32)
    padded = (counts + tm - 1) // tm * tm
    ends = jnp.cumsum(padded)
    starts = ends - padded
    experts = jnp.arange(N_EXPERTS, dtype=I32)
    slot = jnp.sum(jnp.where(te[..., None] == experts, starts, 0), axis=-1) + tr
    tok = (jnp.arange(steps, dtype=I32)[:, None, None] * ROW_TILE + jnp.arange(per, dtype=I32)[None, None, :] % ROW_TILE)
    slot_tok = jnp.zeros((n_blocks * tm,), I32).at[slot.reshape(-1)].set(tok.reshape(-1), unique_indices=True)
    first_row = jnp.arange(n_blocks, dtype=I32) * tm
    block_expert = jnp.minimum(jnp.sum((ends[None, :] <= first_row[:, None]).astype(I32), axis=1), N_EXPERTS - 1)
    n_used = (ends[-1] // tm).astype(I32).reshape(1)
    return slot, slot_tok.reshape(n_blocks, 1, tm), block_expert, n_used


def _ffn_kernel(be_ref, nu_ref, tok_ref, tokn_ref, h_hbm, w1_ref, b1_ref, w2_ref, b2_ref, y_ref,
                xbuf0, xbuf1, w1b, b1c, w2b, xb, yacc, sem):
    i = pl.program_id(0)
    tm = xbuf0.shape[0] // 8
    n_used = nu_ref[0]
    f = w2_ref.shape[1]
    fc = f // FFN_CHUNKS
    rc = tm // FFN_CHUNKS

    def issue(idx_ref, buf, s, first, count):
        for j in range(count):
            src = pl.multiple_of(idx_ref[0, 0, first + j], 8)
            dst = pl.multiple_of((first + j) * 8, 8)
            pltpu.make_async_copy(h_hbm.at[pl.ds(src, 8), :], buf.at[pl.ds(dst, 8), :], sem.at[s]).start()

    def wait(buf, s):
        pltpu.make_async_copy(h_hbm.at[pl.ds(0, tm * 8), :], buf, sem.at[s]).wait()

    @pl.when(i == 0)
    def _():
        issue(tok_ref, xbuf0, 0, 0, tm)

    def step(cur, nxt, sc, sn):
        wait(cur, sc)

        @pl.when(i < n_used)
        def _():
            @pl.when((i == 0) | (be_ref[i] != be_ref[jnp.maximum(i - 1, 0)]))
            def _():
                for c in range(FFN_CHUNKS):
                    lo, hi = c * fc, (c + 1) * fc
                    w1b[c] = jnp.concatenate([w1_ref[0, :, lo:hi], w1_ref[0, :, f + lo:f + hi]], axis=1).astype(BF16)
                    b1c[c] = jnp.concatenate([b1_ref[0, :, lo:hi], b1_ref[0, :, f + lo:f + hi]], axis=1)
                    w2b[c] = w2_ref[0, lo:hi, :].astype(BF16)

            xb[...] = _load_row_tiles(cur, 0, tm).astype(BF16)
            yacc[...] = jnp.broadcast_to(b2_ref[0], yacc.shape)

            def chunk(c, carry):
                issue(tokn_ref, nxt, sn, c * rc, rc)
                a = jnp.dot(xb[...], w1b[c], preferred_element_type=F32) + b1c[c]
                glu = jnp.minimum(a[:, :fc], SWIGLU_LIMIT)
                lin = jnp.clip(a[:, fc:], -SWIGLU_LIMIT, SWIGLU_LIMIT)
                act = glu / (1.0 + jnp.exp(-SWIGLU_ALPHA * glu)) * (lin + 1.0)
                yacc[...] += jnp.dot(act.astype(BF16), w2b[c], preferred_element_type=F32)
                return carry
            lax.fori_loop(0, FFN_CHUNKS, chunk, 0)
            _store_row_tiles(y_ref, yacc[...])

        @pl.when(i >= n_used)
        def _():
            issue(tokn_ref, nxt, sn, 0, tm)
            y_ref[...] = jnp.zeros(y_ref.shape, F32)

        @pl.when(i == pl.num_programs(0) - 1)
        def _():
            wait(nxt, sn)

    @pl.when(i % 2 == 0)
    def _():
        step(xbuf0, xbuf1, 0, 1)

    @pl.when(i % 2 == 1)
    def _():
        step(xbuf1, xbuf0, 1, 0)


def _moe_ffn(h_rows, slot_tok, block_expert, n_used, w1, b1, w2, b2, layer):
    n_blocks, _, tm = slot_tok.shape
    _, n_exp, d, f2 = w1.shape
    f = w2.shape[2]
    last = n_blocks - 1
    grid_spec = pltpu.PrefetchScalarGridSpec(
        num_scalar_prefetch=2,
        grid=(n_blocks,),
        in_specs=[
            pl.BlockSpec((1, 1, tm), lambda i, be, nu: (i, 0, 0), memory_space=pltpu.SMEM),
            pl.BlockSpec((1, 1, tm), lambda i, be, nu: (jnp.minimum(i + 1, last), 0, 0), memory_space=pltpu.SMEM),
            pl.BlockSpec(memory_space=pl.ANY),
            pl.BlockSpec((None, 1, d, f2), lambda i, be, nu: (layer, be[i], 0, 0)),
            pl.BlockSpec((1, 1, f2), lambda i, be, nu: (be[i], 0, 0)),
            pl.BlockSpec((None, 1, f, d), lambda i, be, nu: (layer, be[i], 0, 0)),
            pl.BlockSpec((1, 1, d), lambda i, be, nu: (be[i], 0, 0)),
        ],
        out_specs=pl.BlockSpec((tm * 8, LANES), lambda i, be, nu: (i, 0)),
        scratch_shapes=[
            pltpu.VMEM((tm * 8, LANES), F32),
            pltpu.VMEM((tm * 8, LANES), F32),
            pltpu.VMEM((FFN_CHUNKS, d, f2 // FFN_CHUNKS), BF16),
            pltpu.VMEM((FFN_CHUNKS, 1, f2 // FFN_CHUNKS), F32),
            pltpu.VMEM((FFN_CHUNKS, f // FFN_CHUNKS, d), BF16),
            pltpu.VMEM((tm, d), BF16),
            pltpu.VMEM((tm, d), F32),
            pltpu.SemaphoreType.DMA((2,)),
        ],
    )
    return pl.pallas_call(
        _ffn_kernel,
        grid_spec=grid_spec,
        out_shape=jax.ShapeDtypeStruct((n_blocks * tm * 8, LANES), F32),
        compiler_params=_cp(("arbitrary",)),
        name="moe_ffn",
    )(block_expert, n_used, slot_tok, slot_tok, h_rows, w1, b1.reshape(n_exp, 1, f2), w2, b2.reshape(n_exp, 1, d))


def _combine_kernel(slot_ref, slotn_ref, y_hbm, x_ref, mod_ref, gate_ref, nf_ref, o_ref, ybuf0, ybuf1, sem, *, final):
    b = pl.program_id(0)
    i = pl.program_id(1)
    step = b * pl.num_programs(1) + i
    n_steps = pl.num_programs(0) * pl.num_programs(1)
    tc = x_ref.shape[1]
    n_rows = TOP_K * tc

    def issue(idx_ref, buf, s):
        for j in range(n_rows):
            src = pl.multiple_of(idx_ref[0, 0, j], 8)
            pltpu.make_async_copy(y_hbm.at[pl.ds(src, 8), :], buf.at[pl.ds(j * 8, 8), :], sem.at[s]).start()

    def wait(buf, s):
        pltpu.make_async_copy(y_hbm.at[pl.ds(0, n_rows * 8), :], buf, sem.at[s]).wait()

    @pl.when(step == 0)
    def _():
        issue(slot_ref, ybuf0, 0)

    def run(cur, nxt, sc, sn):
        wait(cur, sc)
        issue(slotn_ref, nxt, sn)
        gates = gate_ref[0]
        acc = gates[:, 0:1] * _load_row_tiles(cur, 0, tc)
        for k in range(1, TOP_K):
            acc += gates[:, k:k + 1] * _load_row_tiles(cur, k * tc, tc)
        out = x_ref[0] + mod_ref[0, 0, 5:6, :] * acc
        if final:
            out = _rms(out, nf_ref[...])
        o_ref[0] = out

        @pl.when(step == n_steps - 1)
        def _():
            wait(nxt, sn)

    @pl.when(step % 2 == 0)
    def _():
        run(ybuf0, ybuf1, 0, 1)

    @pl.when(step % 2 == 1)
    def _():
        run(ybuf1, ybuf0, 1, 0)


def _combine(y_sorted, slot, x_mid, mod, gates, norm_final, nt, ntile, final):
    bsz, _, d = x_mid.shape
    tc = ROW_TILE
    n_steps = bsz * ntile
    slots = slot * 8
    tok = lambda w: pl.BlockSpec((1, tc, w), lambda b, i: (b, i, 0))
    return pl.pallas_call(
        functools.partial(_combine_kernel, final=final),
        grid=(bsz, ntile),
        in_specs=[
            pl.BlockSpec((1, 1, tc * TOP_K), lambda b, i: (b * ntile + i, 0, 0), memory_space=pltpu.SMEM),
            pl.BlockSpec((1, 1, tc * TOP_K), lambda b, i: (jnp.minimum(b * ntile + i + 1, n_steps - 1), 0, 0),
                         memory_space=pltpu.SMEM),
            pl.BlockSpec(memory_space=pl.ANY),
            tok(d),
            pl.BlockSpec((1, 1, 6, d), lambda b, i: (b, i // nt, 0, 0)),
            tok(ROUTER_LANES),
            pl.BlockSpec((1, d), lambda b, i: (0, 0)),
        ],
        out_specs=tok(d),
        out_shape=jax.ShapeDtypeStruct((bsz, ntile * tc, d), F32),
        scratch_shapes=[pltpu.VMEM((TOP_K * tc * 8, LANES), F32), pltpu.VMEM((TOP_K * tc * 8, LANES), F32),
                        pltpu.SemaphoreType.DMA((2,))],
        compiler_params=_cp(("arbitrary", "arbitrary")),
        name="moe_combine",
    )(slots, slots, y_sorted, x_mid, mod, gates, norm_final.reshape(1, d))


def _split_dot(a, w):
    ah = a.astype(BF16)
    al = (a - ah.astype(F32)).astype(BF16)
    wh = w.astype(BF16)
    wl = (w - wh.astype(F32)).astype(BF16)
    return (jnp.dot(ah, wh, preferred_element_type=F32) + jnp.dot(al, wh, preferred_element_type=F32)
            + jnp.dot(ah, wl, preferred_element_type=F32))


def _filt_kernel(z_ref, w1_ref, b1_ref, w2_ref, b2_ref, w3_ref, b3_ref, fr_ref, wo_ref, dec_ref, o_ref):
    fr = fr_ref[...]
    h = jnp.sin(fr * (_split_dot(z_ref[...], w1_ref[...]) + b1_ref[...]))
    h = jnp.sin(fr * (_split_dot(h, w2_ref[...]) + b2_ref[...]))
    h = jnp.sin(fr * (_split_dot(h, w3_ref[...]) + b3_ref[...]))
    o_ref[...] = _split_dot(h, wo_ref[...]) * dec_ref[...]


def _hyena_filters_pl(n, filt):
    w1, b1, w2, b2, w3, b3, freq, w_out = filt
    fw = w2.shape[0]
    t = jnp.linspace(0.0, 1.0, n, dtype=F32)[:, None]
    wpos = (2.0 * math.pi / n) * jnp.arange(n, dtype=F32)[:, None]
    bands = jnp.linspace(1e-4, HY_BANDS - 1, HY_BANDS, dtype=F32)
    z = jnp.concatenate([t, jnp.cos(wpos * bands), -jnp.sin(wpos * bands), jnp.zeros((n, fw - HY_EMB), F32)], axis=-1)
    w1p = jnp.concatenate([w1, jnp.zeros((fw - HY_EMB, fw), F32)], axis=0)
    max_decay = math.log(HY_TARGET) / HY_FAST_DECAY
    min_decay = math.log(HY_TARGET) / HY_SLOW_DECAY
    deltas = jnp.linspace(min_decay, max_decay, HY_WIDTH, dtype=F32)
    decay = jnp.tile(jnp.exp(-t * jnp.abs(deltas)), (1, 2))
    tm = min(n, 512)
    row = lambda w: pl.BlockSpec((tm, w), lambda i: (i, 0))
    const = lambda s: pl.BlockSpec(s, lambda i: (0, 0))
    return pl.pallas_call(
        _filt_kernel,
        grid=(n // tm,),
        in_specs=[row(fw), const((fw, fw)), const((1, fw)), const((fw, fw)), const((1, fw)), const((fw, fw)), const((1, fw)),
                  const((1, fw)), const((fw, 2 * HY_WIDTH)), row(2 * HY_WIDTH)],
        out_specs=row(2 * HY_WIDTH),
        out_shape=jax.ShapeDtypeStruct((n, 2 * HY_WIDTH), F32),
        compiler_params=_cp(("arbitrary",)),
        name="hyena_filters",
    )(z, w1p, b1.reshape(1, fw), w2, b2.reshape(1, fw), w3, b3.reshape(1, fw), freq.reshape(1, fw), w_out, decay)


def _gate_kernel(u_ref, prev_ref, next_ref, w_ref, b_ref, z_ref, x0_ref, *, nt):
    i = pl.program_id(1)
    u = u_ref[0]
    tm = u.shape[0]
    w = w_ref[...]
    has_prev = jnp.where((i != 0) & (i != nt), 1.0, 0.0)
    has_next = jnp.where((i != nt - 1) & (i != nt), 1.0, 0.0)
    row = lax.broadcasted_iota(I32, u.shape, 0)
    up = jnp.where(row == 0, prev_ref[0, 7:8, :] * has_prev, pltpu.roll(u, 1, 0))
    un = jnp.where(row == tm - 1, next_ref[0, 0:1, :] * has_next, pltpu.roll(u, tm - 1, 0))
    uc = up * w[0:1, :] + u * w[1:2, :] + un * w[2:3, :] + b_ref[...]
    x0_ref[0] = uc[:, :HY_WIDTH]
    z_ref[0] = uc[:, 2 * HY_WIDTH:] * uc[:, HY_WIDTH:2 * HY_WIDTH]


def _hyena_gate(hy, conv_w, conv_b, nt, ntile):
    bsz, rows, w3 = hy.shape
    tm = ROW_TILE
    sub = tm // 8
    last8 = rows // 8 - 1
    out = pl.BlockSpec((1, tm, HY_WIDTH), lambda b, i: (b, i, 0))
    return pl.pallas_call(
        functools.partial(_gate_kernel, nt=nt),
        grid=(bsz, ntile),
        in_specs=[
            pl.BlockSpec((1, tm, w3), lambda b, i: (b, i, 0)),
            pl.BlockSpec((1, 8, w3), lambda b, i: (b, jnp.maximum(i * sub - 1, 0), 0)),
            pl.BlockSpec((1, 8, w3), lambda b, i: (b, jnp.minimum((i + 1) * sub, last8), 0)),
            pl.BlockSpec((HY_SHORT, w3), lambda b, i: (0, 0)),
            pl.BlockSpec((1, w3), lambda b, i: (0, 0)),
        ],
        out_specs=[out, out],
        out_shape=[jax.ShapeDtypeStruct((bsz, ntile * tm, HY_WIDTH), F32)] * 2,
        compiler_params=_cp(("arbitrary", "arbitrary")),
        name="hyena_gate",
    )(hy, hy, hy, conv_w, conv_b.reshape(1, w3))


FFT_N2 = LANES


def _dft_tables(n_fft):
    n1 = n_fft // FFT_N2
    k1 = jnp.arange(n1, dtype=I32)
    a = (2.0 * math.pi / n1) * ((k1[:, None] * k1[None, :]) % n1).astype(F32)
    f1 = jnp.stack([jnp.cos(a), -jnp.sin(a)], axis=1).reshape(2 * n1, n1)
    g = (jnp.stack([jnp.cos(a), -jnp.sin(a)], axis=2).reshape(n1, 2 * n1) / n_fft)
    k2 = jnp.arange(FFT_N2, dtype=I32)
    m = (k2[None, None, :] * (k1[:, None, None] + n1 * k2[None, :, None])) % n_fft
    th = (2.0 * math.pi / n_fft) * m.astype(F32)
    mre, mim = jnp.cos(th), -jnp.sin(th)
    mr = jnp.concatenate([jnp.concatenate([mre, -mim], axis=2), jnp.concatenate([mim, mre], axis=2)], axis=1)
    return f1.astype(BF16), g.astype(BF16), mr.astype(BF16), jnp.swapaxes(mr, 1, 2).astype(BF16)


def _fft1_kernel(x_ref, f_ref, o_ref, *, n1_in):
    f = f_ref[...]
    n_rows = f.shape[0]

    def body(n2, carry):
        xs = x_ref[pl.ds(n2, n1_in, stride=FFT_N2), :].astype(BF16)
        o_ref[pl.ds(n2, n_rows, stride=FFT_N2), :] = jnp.dot(f, xs, preferred_element_type=F32)
        return carry
    lax.fori_loop(0, FFT_N2, body, 0, unroll=8)


def _fft1(x, f1, n1_in):
    bsz, _, ch = x.shape
    n_rows = f1.shape[0]
    return pl.pallas_call(
        functools.partial(_fft1_kernel, n1_in=n1_in),
        grid=(bsz, ch // LANES),
        in_specs=[pl.BlockSpec((None, n1_in * FFT_N2, LANES), lambda b, h: (b, 0, h)),
                  pl.BlockSpec((n_rows, n1_in), lambda b, h: (0, 0))],
        out_specs=pl.BlockSpec((None, n_rows * FFT_N2, LANES), lambda b, h: (b, 0, h)),
        out_shape=jax.ShapeDtypeStruct((bsz, n_rows * FFT_N2, ch), F32),
        compiler_params=_cp(("arbitrary", "arbitrary")),
        name="hyena_fft1",
    )(x, f1[:, :n1_in])


def _spec_kernel(a_ref, mr_ref, o_ref):
    o_ref[0] = jnp.dot(mr_ref[0], a_ref[0].astype(BF16), preferred_element_type=F32)


def _fft_spectrum(a, mr):
    _, rows, ch = a.shape
    blk = 2 * FFT_N2
    return pl.pallas_call(
        _spec_kernel,
        grid=(rows // blk,),
        in_specs=[pl.BlockSpec((1, blk, ch), lambda k: (0, k, 0)), pl.BlockSpec((1, blk, blk), lambda k: (k, 0, 0))],
        out_specs=pl.BlockSpec((1, blk, ch), lambda k: (0, k, 0)),
        out_shape=jax.ShapeDtypeStruct((1, rows, ch), F32),
        compiler_params=_cp(("arbitrary",)),
        name="hyena_filter_spectrum",
    )(a, mr)


def _mid_kernel(a_ref, mr_ref, mrt_ref, ks_ref, o_ref):
    kre, kim = ks_ref[0, :FFT_N2, :], ks_ref[0, FFT_N2:, :]
    for b in range(a_ref.shape[0]):
        x = jnp.dot(mr_ref[0], a_ref[b].astype(BF16), preferred_element_type=F32)
        xre, xim = x[:FFT_N2], x[FFT_N2:]
        y = jnp.concatenate([xre * kre - xim * kim, xre * kim + xim * kre], axis=0)
        o_ref[b] = jnp.dot(mrt_ref[0], y.astype(BF16), preferred_element_type=F32)


def _fft_mid(a, mr, mrt, kspec):
    bsz, rows, ch = a.shape
    blk = 2 * FFT_N2
    return pl.pallas_call(
        _mid_kernel,
        grid=(rows // blk,),
        in_specs=[pl.BlockSpec((bsz, blk, ch), lambda k: (0, k, 0)),
                  pl.BlockSpec((1, blk, blk), lambda k: (k, 0, 0)),
                  pl.BlockSpec((1, blk, blk), lambda k: (k, 0, 0)),
                  pl.BlockSpec((1, blk, ch), lambda k: (0, k, 0))],
        out_specs=pl.BlockSpec((bsz, blk, ch), lambda k: (0, k, 0)),
        out_shape=jax.ShapeDtypeStruct((bsz, rows, ch), F32),
        compiler_params=_cp(("arbitrary",)),
        name="hyena_fft_mid",
    )(a, mr, mrt, kspec)


def _ifft1_kernel(b_ref, g_ref, z_ref, x0_ref, skip_ref, o_ref, *, n1_out):
    g = g_ref[...]
    n_rows = g.shape[1]
    skip = skip_ref[...]

    def body(n2, carry):
        bs = b_ref[pl.ds(n2, n_rows, stride=FFT_N2), :].astype(BF16)
        y = jnp.dot(g, bs, preferred_element_type=F32)
        rows = pl.ds(n2, n1_out, stride=FFT_N2)
        zs = z_ref[rows, :]
        o_ref[rows, :] = x0_ref[rows, :] * (y + zs * skip)
        return carry
    lax.fori_loop(0, FFT_N2, body, 0, unroll=8)


def _ifft1(bmat, g, z, x0, skip, n1_out):
    bsz, rows, ch = bmat.shape
    n = n1_out * FFT_N2
    seq = pl.BlockSpec((None, n, LANES), lambda b, h: (b, 0, h))
    return pl.pallas_call(
        functools.partial(_ifft1_kernel, n1_out=n1_out),
        grid=(bsz, ch // LANES),
        in_specs=[pl.BlockSpec((None, rows, LANES), lambda b, h: (b, 0, h)),
                  pl.BlockSpec((n1_out, g.shape[1]), lambda b, h: (0, 0)),
                  seq, seq, pl.BlockSpec((1, LANES), lambda b, h: (0, h))],
        out_specs=seq,
        out_shape=jax.ShapeDtypeStruct((bsz, n, ch), F32),
        compiler_params=_cp(("arbitrary", "arbitrary")),
        name="hyena_ifft1",
    )(bmat, g[:n1_out], z, x0, skip.reshape(1, ch))


def _hyena_fftconv(z, x0, k2, skip):
    n = k2.shape[0]
    n_fft = 2 * n
    f1, g, mr, mrt = _dft_tables(n_fft)
    k_full = jnp.concatenate([k2[:, :HY_WIDTH], k2[::-1, HY_WIDTH:]], axis=0)[None]
    kspec = _fft_spectrum(_fft1(k_full, f1, n_fft // FFT_N2), mr)
    a = _fft1(z, f1, n // FFT_N2)
    bmat = _fft_mid(a, mr, mrt, kspec)
    return _ifft1(bmat, g, z, x0, skip, n // FFT_N2)


def _ctxconv_kernel(z_ref, x0_ref, kf_ref, fd_ref, gd_ref, skip_ref, o_ref, ks_ref):
    n = z_ref.shape[1]
    half = fd_ref.shape[0] // 2

    @pl.when(pl.program_id(0) == 0)
    def _():
        ks_ref[...] = jnp.dot(fd_ref[...], kf_ref[...].astype(BF16), preferred_element_type=F32)

    z = z_ref[0]
    x = jnp.dot(fd_ref[:, :n], z.astype(BF16), preferred_element_type=F32)
    xre, xim = x[:half], x[half:]
    kre, kim = ks_ref[:half, :], ks_ref[half:, :]
    y = jnp.concatenate([xre * kre - xim * kim, xre * kim + xim * kre], axis=0)
    conv = jnp.dot(gd_ref[...], y.astype(BF16), preferred_element_type=F32)
    o_ref[0] = x0_ref[0] * (conv + z * skip_ref[...])


def _hyena_ctxconv(z, x0, k2, skip, row0):
    n = k2.shape[0]
    bsz, _, ch = z.shape
    n_fft = 2 * n
    kk = jnp.arange(n_fft, dtype=I32)
    th = (2.0 * math.pi / n_fft) * ((kk[:, None] * kk[None, :]) % n_fft).astype(F32)
    fd = jnp.concatenate([jnp.cos(th), -jnp.sin(th)], axis=0).astype(BF16)
    gd = (jnp.concatenate([jnp.cos(th[:n]), -jnp.sin(th[:n])], axis=1) / n_fft).astype(BF16)
    k_full = jnp.concatenate([k2[:, :HY_WIDTH], k2[::-1, HY_WIDTH:]], axis=0)
    blk = row0 // n
    seq = pl.BlockSpec((1, n, ch), lambda b: (b, blk, 0))
    const = lambda s: pl.BlockSpec(s, lambda b: (0, 0))
    return pl.pallas_call(
        _ctxconv_kernel,
        grid=(bsz,),
        in_specs=[seq, seq, const((n_fft, ch)), const((2 * n_fft, n_fft)), const((n, 2 * n_fft)), const((1, ch))],
        out_specs=pl.BlockSpec((1, n, ch), lambda b: (b, 0, 0)),
        out_shape=jax.ShapeDtypeStruct((bsz, n, ch), F32),
        scratch_shapes=[pltpu.VMEM((2 * n_fft, ch), F32)],
        compiler_params=_cp(("arbitrary",)),
        name="hyena_ctxconv",
    )(z, x0, k_full, fd, gd, skip.reshape(1, ch))


def _rope_tables(n_lat, rows):
    nrow = n_lat // GRID_W
    row = jnp.repeat(jnp.arange(nrow, dtype=F32), GRID_W)
    col = jnp.tile(jnp.arange(GRID_W, dtype=F32), nrow)
    n_freq = HEAD_DIM // 4
    inv_freq = ROPE_THETA ** (-jnp.arange(n_freq, dtype=F32) / n_freq)
    ar = row[:, None] * inv_freq
    ac = col[:, None] * inv_freq
    cos64 = jnp.concatenate([jnp.cos(ar), jnp.cos(ar), jnp.cos(ac), jnp.cos(ac)], axis=-1)
    sin64 = jnp.concatenate([-jnp.sin(ar), jnp.sin(ar), -jnp.sin(ac), jnp.sin(ac)], axis=-1)
    pad = rows - n_lat
    cos_t = jnp.concatenate([jnp.tile(cos64, (1, LANES // HEAD_DIM)), jnp.ones((pad, LANES), F32)], axis=0)
    sin_t = jnp.concatenate([jnp.tile(sin64, (1, LANES // HEAD_DIM)), jnp.zeros((pad, LANES), F32)], axis=0)
    return cos_t, sin_t


def kernel(x, c, ctx, c_ctx, w_ada, b_ada, norm_mix, norm_ffn, w_in, b_in, hy_conv_w, hy_conv_b, hy_filt_w1, hy_filt_b1, hy_filt_w2, hy_filt_b2, hy_filt_w3, hy_filt_b3, hy_filt_freq, hy_filt_out, hy_skip, ga_q_norm, ga_k_norm, wa_sink, branch_norm, w_out, b_out, w_router, b_router, w_mlp1, b_mlp1, w_mlp2, b_mlp2, norm_final):
    bsz, n_lat, d = x.shape
    n_ctx = ctx.shape[1]
    depth = w_ada.shape[0]
    assert n_ctx == ROW_TILE and n_lat % KV_CHUNK == 0 and bsz < 16 and d == D_MODEL
    nt = n_lat // ROW_TILE
    rows = n_lat + n_ctx
    cos_t, sin_t = _rope_tables(n_lat, rows)

    c_all = jnp.zeros((16, d), F32).at[:bsz].set(c).at[bsz].set(c_ctx)
    mod_all = _ada_mod(c_all, w_ada, b_ada)
    x_all = jnp.concatenate([x, ctx], axis=1)

    for l in range(depth):
        last = l == depth - 1
        ntile = nt if last else nt + 1
        m = mod_all[l].reshape(16, 6, d)
        mod = jnp.stack([m[:bsz], jnp.broadcast_to(m[bsz], (bsz, 6, d))], axis=1)

        hy, gq, gk, gv, wq, wk, wv = _inproj(x_all, mod, norm_mix[l], w_in[l], b_in[l], cos_t, sin_t,
                                             ga_q_norm[l], ga_k_norm[l], nt)
        filt = (hy_filt_w1[l], hy_filt_b1[l], hy_filt_w2[l], hy_filt_b2[l], hy_filt_w3[l], hy_filt_b3[l],
                hy_filt_freq[l], hy_filt_out[l])
        z, x0 = _hyena_gate(hy, hy_conv_w[l], hy_conv_b[l], nt, ntile)
        y_hy = _hyena_fftconv(z, x0, _hyena_filters_pl(n_lat, filt), hy_skip[l])
        if not last:
            yc_hy = _hyena_ctxconv(z, x0, _hyena_filters_pl(n_ctx, filt), hy_skip[l], n_lat)
            y_hy = jnp.concatenate([y_hy, yc_hy], axis=1)
        y_ga = _global_attention(gq, gk, gv, nt, ntile)
        y_wa = _window_attention(wq, wk, wv, wa_sink[l], nt, ntile)
        x_mid, h2, top_g, top_e, top_r, counts = _merge(x_all, y_hy, y_ga, y_wa, mod, branch_norm[l], w_out[l], b_out[l],
                                                        norm_ffn[l], w_router[l], b_router[l], nt, ntile)
        slot, slot_tok, block_expert, n_used = _moe_plan(top_e, top_r, counts, MOE_TILE)
        y_sorted = _moe_ffn(h2.reshape(-1, LANES), slot_tok * 8, block_expert, n_used, w_mlp1, b_mlp1[l], w_mlp2, b_mlp2[l], l)
        x_all = _combine(y_sorted, slot, x_mid, mod, top_g, norm_final, nt, ntile, last)
    return x_all
```

```python
import functools
import math

import jax
import jax.numpy as jnp
from jax import lax
from jax.experimental import pallas as pl
from jax.experimental.pallas import tpu as pltpu

F32 = jnp.float32
BF16 = jnp.bfloat16
I32 = jnp.int32

D_MODEL = 1024
HEAD_DIM = 64
GRID_W = 64
HY_WIDTH = 256
GA_HEADS = 6
GA_KV_HEADS = 2
WA_HEADS = 6
WA_KV_HEADS = 2
GROUP = GA_HEADS // GA_KV_HEADS
Q_WIDTH = GA_HEADS * HEAD_DIM
KV_WIDTH = GA_KV_HEADS * HEAD_DIM
MIX_WIDTH = HY_WIDTH + 2 * Q_WIDTH
IN_WIDTH = 3 * HY_WIDTH + 2 * (Q_WIDTH + 2 * KV_WIDTH)
WINDOW = 128
ROPE_THETA = 10000.0
ATTN_SCALE = HEAD_DIM ** -0.5
HY_SHORT = 3
HY_EMB = 33
HY_BANDS = (HY_EMB - 1) // 2
HY_FAST_DECAY = 0.3
HY_SLOW_DECAY = 1.5
HY_TARGET = 1e-2
N_EXPERTS = 32
TOP_K = 4
SWIGLU_ALPHA = 1.702
SWIGLU_LIMIT = 7.0
EPS = 1e-6
NEG_INF = -1e30

LANES = 128
VMEM_LIMIT_V7X = 56 * 1024 * 1024

ROW_TILE = 256
KV_CHUNK = 512
MOE_TILE = 512
ROUTER_LANES = LANES


def _cp(sem, vmem=VMEM_LIMIT_V7X):
    return pltpu.CompilerParams(dimension_semantics=sem, vmem_limit_bytes=vmem)


def _rms(v, w):
    return v * lax.rsqrt(jnp.mean(v * v, axis=-1, keepdims=True) + EPS) * w


def _ada_kernel(c_ref, w_ref, b_ref, o_ref):
    c = c_ref[...]
    s = c / (1.0 + jnp.exp(-c))
    hi = s.astype(BF16)
    lo = (s - hi.astype(F32)).astype(BF16)
    w = w_ref[0]
    whi = w.astype(BF16)
    wlo = (w - whi.astype(F32)).astype(BF16)
    acc = jnp.dot(hi, whi, preferred_element_type=F32)
    acc += jnp.dot(lo, whi, preferred_element_type=F32)
    acc += jnp.dot(hi, wlo, preferred_element_type=F32)
    o_ref[0] = acc + b_ref[0]


def _ada_mod(c_all, w_ada, b_ada):
    n_layers, d, n6 = w_ada.shape
    tn = n6 // 4
    return pl.pallas_call(
        _ada_kernel,
        grid=(n_layers, n6 // tn),
        in_specs=[
            pl.BlockSpec((16, d), lambda l, j: (0, 0)),
            pl.BlockSpec((1, d, tn), lambda l, j: (l, 0, j)),
            pl.BlockSpec((1, 1, tn), lambda l, j: (l, 0, j)),
        ],
        out_specs=pl.BlockSpec((1, 16, tn), lambda l, j: (l, 0, j)),
        out_shape=jax.ShapeDtypeStruct((n_layers, 16, n6), F32),
        compiler_params=_cp(("arbitrary", "arbitrary")),
        name="ada_mod",
    )(c_all, w_ada, b_ada.reshape(n_layers, 1, n6))


def _inproj_kernel(x_ref, mod_ref, nw_ref, w_ref, b_ref, cos_ref, sin_ref, qn_ref, kn_ref, g_ref,
                   hy_ref, gq_ref, gk_ref, gv_ref, wq_ref, wk_ref, wv_ref):
    x = x_ref[0]
    tm = x.shape[0]
    shift = mod_ref[0, 0, 0:1, :]
    scale = mod_ref[0, 0, 1:2, :]
    h = _rms(x, nw_ref[...]) * (1.0 + scale) + shift
    p = jnp.dot(h.astype(BF16), w_ref[...], preferred_element_type=F32) + b_ref[...]
    hy_ref[0] = p[:, :3 * HY_WIDTH]

    cos = cos_ref[...]
    sin = sin_ref[...]
    lane = lax.broadcasted_iota(I32, (tm, LANES), 1)
    first = (lane % (HEAD_DIM // 2)) < (HEAD_DIM // 4)
    gmat = g_ref[...]

    def rope(v):
        sw = jnp.where(first, pltpu.roll(v, LANES - HEAD_DIM // 4, 1), pltpu.roll(v, HEAD_DIM // 4, 1))
        return v * cos + sw * sin

    def head_norm(v, w):
        t = v * v
        hi = t.astype(BF16)
        lo = (t - hi.astype(F32)).astype(BF16)
        ms = jnp.dot(hi, gmat, preferred_element_type=F32) + jnp.dot(lo, gmat, preferred_element_type=F32)
        return v * lax.rsqrt(ms + EPS) * w

    o = 3 * HY_WIDTH
    for u in range(Q_WIDTH // LANES):
        v = p[:, o + u * LANES:o + (u + 1) * LANES]
        v = rope(head_norm(v, qn_ref[...]))
        gq_ref[0, :, u * LANES:(u + 1) * LANES] = (v * ATTN_SCALE).astype(BF16)
    o += Q_WIDTH
    gk_ref[0] = rope(head_norm(p[:, o:o + KV_WIDTH], kn_ref[...])).astype(BF16)
    o += KV_WIDTH
    gv_ref[0] = p[:, o:o + KV_WIDTH].astype(BF16)
    o += KV_WIDTH
    for u in range(Q_WIDTH // LANES):
        v = rope(p[:, o + u * LANES:o + (u + 1) * LANES])
        wq_ref[0, :, u * LANES:(u + 1) * LANES] = (v * ATTN_SCALE).astype(BF16)
    o += Q_WIDTH
    wk_ref[0] = rope(p[:, o:o + KV_WIDTH]).astype(BF16)
    o += KV_WIDTH
    wv_ref[0] = p[:, o:o + KV_WIDTH].astype(BF16)


def _inproj(x_all, mod, norm_w, w_in, b_in, cos_t, sin_t, q_norm, k_norm, nt):
    bsz, rows, d = x_all.shape
    tm = ROW_TILE
    ntile = rows // tm
    gmat = jnp.kron(jnp.eye(LANES // HEAD_DIM, dtype=F32), jnp.full((HEAD_DIM, HEAD_DIM), 1.0 / HEAD_DIM, F32)).astype(BF16)
    qn = jnp.tile(q_norm, LANES // HEAD_DIM).reshape(1, LANES)
    kn = jnp.tile(k_norm, LANES // HEAD_DIM).reshape(1, LANES)
    tok = lambda w: pl.BlockSpec((1, tm, w), lambda b, i: (b, i, 0))
    const = lambda s: pl.BlockSpec(s, lambda b, i: (0,) * len(s))
    out_w = (3 * HY_WIDTH, Q_WIDTH, KV_WIDTH, KV_WIDTH, Q_WIDTH, KV_WIDTH, KV_WIDTH)
    out_dt = (F32, BF16, BF16, BF16, BF16, BF16, BF16)
    return pl.pallas_call(
        _inproj_kernel,
        grid=(bsz, ntile),
        in_specs=[
            tok(d),
            pl.BlockSpec((1, 1, 6, d), lambda b, i: (b, i // nt, 0, 0)),
            const((1, d)),
            const((d, IN_WIDTH)),
            const((1, IN_WIDTH)),
            pl.BlockSpec((tm, LANES), lambda b, i: (i, 0)),
            pl.BlockSpec((tm, LANES), lambda b, i: (i, 0)),
            const((1, LANES)),
            const((1, LANES)),
            const((LANES, LANES)),
        ],
        out_specs=[tok(w) for w in out_w],
        out_shape=[jax.ShapeDtypeStruct((bsz, rows, w), dt) for w, dt in zip(out_w, out_dt)],
        compiler_params=_cp(("arbitrary", "arbitrary")),
        name="inproj",
    )(x_all, mod, norm_w.reshape(1, d), w_in.astype(BF16), b_in.reshape(1, IN_WIDTH), cos_t, sin_t, qn, kn, gmat)


def _stack_q(q_ref, qs_ref):
    qf = q_ref[0].astype(F32)
    tq = qf.shape[0]
    for j in range(GA_KV_HEADS):
        for g in range(GROUP):
            h = GROUP * j + g
            blk = qf[:, (h // 2) * LANES:(h // 2 + 1) * LANES]
            if h % 2 != j:
                blk = pltpu.roll(blk, HEAD_DIM, 1)
            qs_ref[j, g * tq:(g + 1) * tq, :] = blk.astype(BF16)


def _unstack_o(norm_fn, o_ref, tq):
    lane = lax.broadcasted_iota(I32, (tq, LANES), 1)
    heads = []
    for j in range(GA_KV_HEADS):
        for g in range(GROUP):
            h = GROUP * j + g
            o = norm_fn(j, g)
            if h % 2 != j:
                o = pltpu.roll(o, HEAD_DIM, 1)
            heads.append(o)
    for u in range(Q_WIDTH // LANES):
        o_ref[0, :, u * LANES:(u + 1) * LANES] = jnp.where(lane < HEAD_DIM, heads[2 * u], heads[2 * u + 1])


def _half_masks(rows):
    lane = lax.broadcasted_iota(I32, (rows, LANES), 1)
    return [lane < HEAD_DIM, lane >= HEAD_DIM]


def _dot_t(a, b):
    return lax.dot_general(a, b, (((1,), (1,)), ((), ())), preferred_element_type=F32)


def _ga_kernel(q_ref, k_ref, v_ref, o_ref, qs_ref, m_ref, acc_ref, *, nt, n_lat):
    i = pl.program_id(1)
    tq = q_ref.shape[1]
    _stack_q(q_ref, qs_ref)
    m_ref[...] = jnp.full(m_ref.shape, NEG_INF, F32)
    acc_ref[...] = jnp.zeros(acc_ref.shape, F32)

    def chunk(kc, vc):
        masks = _half_masks(kc.shape[0])
        for j in range(GA_KV_HEADS):
            km = jnp.where(masks[j], kc, jnp.zeros_like(kc))
            vm = jnp.where(masks[j], vc, jnp.ones_like(vc))
            s = _dot_t(qs_ref[j], km)
            m_old = m_ref[j]
            m_new = jnp.maximum(m_old, jnp.max(s, axis=-1, keepdims=True))
            p = jnp.exp((s - jnp.tile(m_new, (1, s.shape[1] // LANES))).astype(BF16))
            acc_ref[j] = jnp.exp(m_old - m_new) * acc_ref[j] + jnp.dot(p, vm, preferred_element_type=F32)
            m_ref[j] = m_new

    @pl.when(i < nt)
    def _():
        def body(c, carry):
            st = pl.multiple_of(c * KV_CHUNK, KV_CHUNK)
            chunk(k_ref[0, pl.ds(st, KV_CHUNK), :], v_ref[0, pl.ds(st, KV_CHUNK), :])
            return carry
        lax.fori_loop(0, n_lat // KV_CHUNK, body, 0)

    chunk(k_ref[0, n_lat:, :], v_ref[0, n_lat:, :])

    def norm_fn(j, g):
        a = acc_ref[j, g * tq:(g + 1) * tq, :]
        return a / pltpu.roll(a, HEAD_DIM, 1)

    _unstack_o(norm_fn, o_ref, tq)


def _global_attention(gq, gk, gv, nt, ntile):
    bsz, rows, _ = gq.shape
    tq = ROW_TILE
    n_lat = nt * tq
    return pl.pallas_call(
        functools.partial(_ga_kernel, nt=nt, n_lat=n_lat),
        grid=(bsz, ntile),
        in_specs=[
            pl.BlockSpec((1, tq, Q_WIDTH), lambda b, i: (b, i, 0)),
            pl.BlockSpec((1, rows, KV_WIDTH), lambda b, i: (b, 0, 0)),
            pl.BlockSpec((1, rows, KV_WIDTH), lambda b, i: (b, 0, 0)),
        ],
        out_specs=pl.BlockSpec((1, tq, Q_WIDTH), lambda b, i: (b, i, 0)),
        out_shape=jax.ShapeDtypeStruct((bsz, ntile * tq, Q_WIDTH), F32),
        scratch_shapes=[
            pltpu.VMEM((GA_KV_HEADS, GROUP * tq, LANES), BF16),
            pltpu.VMEM((GA_KV_HEADS, GROUP * tq, LANES), F32),
            pltpu.VMEM((GA_KV_HEADS, GROUP * tq, LANES), F32),
        ],
        compiler_params=_cp(("arbitrary", "arbitrary")),
        name="global_attn",
    )(gq, gk, gv)


def _wa_kernel(q_ref, k_ref, v_ref, sink_ref, o_ref, qs_ref, acc_ref, l_ref, *, nt, n_lat):
    i = pl.program_id(1)
    tq = q_ref.shape[1]
    band = tq + 2 * WINDOW
    _stack_q(q_ref, qs_ref)
    st = pl.multiple_of(i * tq, tq)
    kb = k_ref[0, pl.ds(st, band), :]
    vb = v_ref[0, pl.ds(st, band), :]
    kc = k_ref[0, WINDOW + n_lat:WINDOW + n_lat + ROW_TILE, :]
    vc = v_ref[0, WINDOW + n_lat:WINDOW + n_lat + ROW_TILE, :]

    r = lax.broadcasted_iota(I32, (GROUP * tq, band), 0) % tq
    c = lax.broadcasted_iota(I32, (GROUP * tq, band), 1)
    rel = c - WINDOW - r
    kpos = c + (i * tq - WINDOW)
    valid = (jnp.abs(rel) <= WINDOW) & (kpos >= 0) & (kpos < n_lat) & (i < nt)

    mb = _half_masks(band)
    mc = _half_masks(ROW_TILE)
    for j in range(WA_KV_HEADS):
        q = qs_ref[j]
        s_loc = jnp.where(valid, _dot_t(q, jnp.where(mb[j], kb, jnp.zeros_like(kb))), NEG_INF)
        s_ctx = _dot_t(q, jnp.where(mc[j], kc, jnp.zeros_like(kc)))
        sink = jnp.concatenate(
            [jnp.broadcast_to(sink_ref[GROUP * j + g:GROUP * j + g + 1, :], (tq, LANES)) for g in range(GROUP)], axis=0)
        m = jnp.maximum(jnp.maximum(jnp.max(s_loc, axis=-1, keepdims=True), jnp.max(s_ctx, axis=-1, keepdims=True)), sink)
        p_loc = jnp.exp(s_loc - jnp.tile(m, (1, band // LANES))).astype(BF16)
        p_ctx = jnp.exp(s_ctx - jnp.tile(m, (1, ROW_TILE // LANES))).astype(BF16)
        acc_ref[j] = (jnp.dot(p_loc, jnp.where(mb[j], vb, jnp.ones_like(vb)), preferred_element_type=F32)
                      + jnp.dot(p_ctx, jnp.where(mc[j], vc, jnp.ones_like(vc)), preferred_element_type=F32))
        l_ref[j] = jnp.exp(sink - m)

    def norm_fn(j, g):
        a = acc_ref[j, g * tq:(g + 1) * tq, :]
        return a / (pltpu.roll(a, HEAD_DIM, 1) + l_ref[j, g * tq:(g + 1) * tq, :])

    _unstack_o(norm_fn, o_ref, tq)


def _window_attention(wq, wk, wv, sink, nt, ntile):
    bsz, rows, _ = wq.shape
    tq = ROW_TILE
    n_lat = nt * tq
    pad = ((0, 0), (WINDOW, WINDOW), (0, 0))
    kp = jnp.pad(wk, pad)
    vp = jnp.pad(wv, pad)
    sink_b = jnp.zeros((8, LANES), F32).at[:WA_HEADS].set(jnp.broadcast_to(sink[:, None], (WA_HEADS, LANES)))
    return pl.pallas_call(
        functools.partial(_wa_kernel, nt=nt, n_lat=n_lat),
        grid=(bsz, ntile),
        in_specs=[
            pl.BlockSpec((1, tq, Q_WIDTH), lambda b, i: (b, i, 0)),
            pl.BlockSpec((1, rows + 2 * WINDOW, KV_WIDTH), lambda b, i: (b, 0, 0)),
            pl.BlockSpec((1, rows + 2 * WINDOW, KV_WIDTH), lambda b, i: (b, 0, 0)),
            pl.BlockSpec((8, LANES), lambda b, i: (0, 0)),
        ],
        out_specs=pl.BlockSpec((1, tq, Q_WIDTH), lambda b, i: (b, i, 0)),
        out_shape=jax.ShapeDtypeStruct((bsz, ntile * tq, Q_WIDTH), F32),
        scratch_shapes=[
            pltpu.VMEM((WA_KV_HEADS, GROUP * tq, LANES), BF16),
            pltpu.VMEM((WA_KV_HEADS, GROUP * tq, LANES), F32),
            pltpu.VMEM((WA_KV_HEADS, GROUP * tq, LANES), F32),
        ],
        compiler_params=_cp(("arbitrary", "arbitrary")),
        name="window_attn",
    )(wq, kp, vp, sink_b)


def _store_row_tiles(ref, v):
    rows = v.shape[0]
    for s in range(v.shape[1] // LANES):
        ref[pl.ds(s, rows, stride=8), :] = v[:, s * LANES:(s + 1) * LANES]


def _load_row_tiles(ref, row0, rows):
    return jnp.concatenate([ref[pl.ds(row0 * 8 + s, rows, stride=8), :] for s in range(8)], axis=1)


def _merge_kernel(x_ref, hy_ref, ga_ref, wa_ref, mod_ref, bn_ref, wo_ref, bo_ref, nf_ref, wrh_ref, wrl_ref, br_ref, tri_ref,
                  xo_ref, h2_ref, tg_ref, te_ref, tr_ref, cnt_ref, base_ref):
    @pl.when((pl.program_id(0) == 0) & (pl.program_id(1) == 0))
    def _():
        base_ref[...] = jnp.zeros(base_ref.shape, F32)

    bn = bn_ref[...]
    g0 = HY_WIDTH
    g1 = HY_WIDTH + Q_WIDTH
    y = jnp.concatenate([_rms(hy_ref[0], bn[:, :g0]), _rms(ga_ref[0], bn[:, g0:g1]), _rms(wa_ref[0], bn[:, g1:])], axis=-1)
    o = jnp.dot(y.astype(BF16), wo_ref[...], preferred_element_type=F32) + bo_ref[...]
    x1 = x_ref[0] + mod_ref[0, 0, 2:3, :] * o
    xo_ref[0] = x1
    h2 = _rms(x1, nf_ref[...]) * (1.0 + mod_ref[0, 0, 4:5, :]) + mod_ref[0, 0, 3:4, :]
    _store_row_tiles(h2_ref, h2)

    hi = h2.astype(BF16)
    lo = (h2 - hi.astype(F32)).astype(BF16)
    logits = (jnp.dot(hi, wrh_ref[...], preferred_element_type=F32) + jnp.dot(lo, wrh_ref[...], preferred_element_type=F32)
              + jnp.dot(hi, wrl_ref[...], preferred_element_type=F32)) + br_ref[...]
    tm = logits.shape[0]
    lane = lax.broadcasted_iota(I32, (tm, ROUTER_LANES), 1)
    te = jnp.zeros((tm, ROUTER_LANES), F32)
    tr = jnp.zeros((tm, ROUTER_LANES), F32)
    tv = jnp.zeros((tm, ROUTER_LANES), F32)
    before = base_ref[...]
    tri = tri_ref[...]
    v0 = None
    for k in range(TOP_K):
        mx = jnp.max(logits, axis=-1, keepdims=True)
        idx = jnp.min(jnp.where(logits == mx, lane, ROUTER_LANES), axis=-1, keepdims=True)
        if k == 0:
            v0 = mx
        hit = lane == idx
        onehot = jnp.where(hit, 1.0, 0.0)
        prefix = jnp.dot(tri, onehot.astype(BF16), preferred_element_type=F32) + before
        rank = jnp.sum(onehot * prefix, axis=-1, keepdims=True)
        before = before + jnp.sum(onehot, axis=0, keepdims=True)
        te = jnp.where(lane == k, idx.astype(F32), te)
        tr = jnp.where(lane == k, rank, tr)
        tv = jnp.where(lane == k, jnp.exp(mx - v0), tv)
        logits = jnp.where(hit, NEG_INF, logits)
    base_ref[...] = before
    cnt_ref[...] = before
    tg_ref[0] = tv / jnp.sum(tv, axis=-1, keepdims=True)
    te_t = te.T
    tr_t = tr.T
    for k in range(TOP_K):
        te_ref[0, :, k * tm:(k + 1) * tm] = te_t[k:k + 1, :].astype(I32)
        tr_ref[0, :, k * tm:(k + 1) * tm] = tr_t[k:k + 1, :].astype(I32)


def _merge(x_all, y_hy, y_ga, y_wa, mod, branch_norm, w_out, b_out, norm_ffn, w_router, b_router, nt, ntile):
    bsz, _, d = x_all.shape
    tm = ROW_TILE
    rows = ntile * tm
    wr = jnp.zeros((d, ROUTER_LANES), F32).at[:, :N_EXPERTS].set(w_router)
    wrh = wr.astype(BF16)
    wrl = (wr - wrh.astype(F32)).astype(BF16)
    br = jnp.full((1, ROUTER_LANES), NEG_INF, F32).at[0, :N_EXPERTS].set(b_router)
    tri = (jnp.arange(tm, dtype=I32)[:, None] > jnp.arange(tm, dtype=I32)[None, :]).astype(BF16)
    tok = lambda w: pl.BlockSpec((1, tm, w), lambda b, i: (b, i, 0))
    const = lambda s: pl.BlockSpec(s, lambda b, i: (0,) * len(s))
    flat = pl.BlockSpec((1, 1, TOP_K * tm), lambda b, i: (b * ntile + i, 0, 0))
    return pl.pallas_call(
        _merge_kernel,
        grid=(bsz, ntile),
        in_specs=[
            tok(d), tok(HY_WIDTH), tok(Q_WIDTH), tok(Q_WIDTH),
            pl.BlockSpec((1, 1, 6, d), lambda b, i: (b, i // nt, 0, 0)),
            const((1, MIX_WIDTH)), const((MIX_WIDTH, d)), const((1, d)), const((1, d)),
            const((d, ROUTER_LANES)), const((d, ROUTER_LANES)), const((1, ROUTER_LANES)), const((tm, tm)),
        ],
        out_specs=[tok(d), pl.BlockSpec((None, tm * 8, LANES), lambda b, i: (b, i, 0)), tok(ROUTER_LANES), flat, flat,
                   const((1, ROUTER_LANES))],
        out_shape=[
            jax.ShapeDtypeStruct((bsz, rows, d), F32),
            jax.ShapeDtypeStruct((bsz, rows * 8, LANES), F32),
            jax.ShapeDtypeStruct((bsz, rows, ROUTER_LANES), F32),
            jax.ShapeDtypeStruct((bsz * ntile, 1, TOP_K * tm), I32),
            jax.ShapeDtypeStruct((bsz * ntile, 1, TOP_K * tm), I32),
            jax.ShapeDtypeStruct((1, ROUTER_LANES), F32),
        ],
        scratch_shapes=[pltpu.VMEM((1, ROUTER_LANES), F32)],
        compiler_params=_cp(("arbitrary", "arbitrary")),
        name="merge_router",
    )(x_all, y_hy, y_ga, y_wa, mod, branch_norm.reshape(1, MIX_WIDTH), w_out.astype(BF16), b_out.reshape(1, d),
      norm_ffn.reshape(1, d), wrh, wrl, br, tri)


def _moe_plan(te, tr, cnt, tm):
    steps, _, per = te.shape
    n_blocks = -(-(steps * per) // tm) + N_EXPERTS
    counts = cnt[0, :N_EXPERTS].astype(I32)
    nblk = (counts + tm - 1) // tm
    cend = jnp.cumsum(nblk)
    cstart = cend - nblk
    experts = jnp.arange(N_EXPERTS, dtype=I32)
    row = jnp.sum(jnp.where(te[..., None] == experts, cstart * tm, 0), axis=-1) + tr
    j = jnp.arange(n_blocks, dtype=I32)
    block_expert = jnp.minimum(jnp.sum((cend[None, :] <= j[:, None]).astype(I32), axis=1), N_EXPERTS - 1)
    past = cend[-1] + experts
    zero_blocks = jnp.concatenate([jnp.where(nblk > 0, cend - 1, -1), jnp.where(past < n_blocks, past, -1)])
    return row, block_expert, cend[-1].reshape(1), zero_blocks


def _dispatch_kernel(zb_ref, row_ref, h_ref, x_hbm, buf0, buf1, sem, *, block_rows):
    step = pl.program_id(0)
    tm = h_ref.shape[0] // 8
    per_block = block_rows // tm

    def wait(buf, s):
        for k in range(TOP_K):
            pltpu.make_async_copy(buf, x_hbm.at[pl.ds(0, tm * 8), :], sem.at[s, k]).wait()

    @pl.when(step == 0)
    def _():
        buf1[...] = jnp.zeros(buf1.shape, F32)
        def piece(e, q):
            dst = pl.multiple_of((zb_ref[e] * per_block + q) * (tm * 8), tm * 8)
            return pltpu.make_async_copy(buf1, x_hbm.at[pl.ds(dst, tm * 8), :], sem.at[1, 0])

        for phase in ("start", "wait"):
            for e in range(zb_ref.shape[0]):
                @pl.when(zb_ref[e] >= 0)
                def _():
                    for q in range(per_block):
                        getattr(piece(e, q), phase)()

    def run(cur, prev, sc, sp):
        cur[...] = h_ref[...]
        for k in range(TOP_K):
            for r in range(tm):
                dst = pl.multiple_of(row_ref[0, 0, k * tm + r], 8)
                pltpu.make_async_copy(cur.at[pl.ds(r * 8, 8), :], x_hbm.at[pl.ds(dst, 8), :], sem.at[sc, k]).start()

        @pl.when(step > 0)
        def _():
            wait(prev, sp)

        @pl.when(step == pl.num_programs(0) - 1)
        def _():
            wait(cur, sc)

    @pl.when(step % 2 == 0)
    def _():
        run(buf0, buf1, 0, 1)

    @pl.when(step % 2 == 1)
    def _():
        run(buf1, buf0, 1, 0)


def _dispatch(h_tiles, rows_e, zero_blocks, n_blocks, block_rows):
    steps = rows_e.shape[0]
    tm = ROW_TILE
    grid_spec = pltpu.PrefetchScalarGridSpec(
        num_scalar_prefetch=1,
        grid=(steps,),
        in_specs=[
            pl.BlockSpec((1, 1, TOP_K * tm), lambda i, zb: (i, 0, 0), memory_space=pltpu.SMEM),
            pl.BlockSpec((tm * 8, LANES), lambda i, zb: (i, 0)),
        ],
        out_specs=pl.BlockSpec(memory_space=pl.ANY),
        scratch_shapes=[pltpu.VMEM((tm * 8, LANES), F32), pltpu.VMEM((tm * 8, LANES), F32),
                        pltpu.SemaphoreType.DMA((2, TOP_K))],
    )
    return pl.pallas_call(
        functools.partial(_dispatch_kernel, block_rows=block_rows),
        grid_spec=grid_spec,
        out_shape=jax.ShapeDtypeStruct((n_blocks * block_rows * 8, LANES), F32),
        compiler_params=_cp(("arbitrary",)),
        name="moe_dispatch",
    )(zero_blocks, rows_e * 8, h_tiles)


def _ffn_kernel(be_ref, nu_ref, x_ref, w1_ref, b1_ref, w2_ref, b2_ref, y_ref, w1b, w2b):
    i = pl.program_id(0)
    tm = x_ref.shape[0] // 8
    f = w2_ref.shape[1]

    @pl.when(i < nu_ref[0])
    def _():
        @pl.when((i == 0) | (be_ref[i] != be_ref[jnp.maximum(i - 1, 0)]))
        def _():
            w1b[...] = w1_ref[0].astype(BF16)
            w2b[...] = w2_ref[0].astype(BF16)

        a = jnp.dot(_load_row_tiles(x_ref, 0, tm).astype(BF16), w1b[...], preferred_element_type=F32) + b1_ref[0]
        glu = jnp.minimum(a[:, :f], SWIGLU_LIMIT)
        lin = jnp.clip(a[:, f:], -SWIGLU_LIMIT, SWIGLU_LIMIT)
        act = glu / (1.0 + jnp.exp(-SWIGLU_ALPHA * glu)) * (lin + 1.0)
        _store_row_tiles(y_ref, jnp.dot(act.astype(BF16), w2b[...], preferred_element_type=F32) + b2_ref[0])

    @pl.when(i >= nu_ref[0])
    def _():
        y_ref[...] = jnp.zeros(y_ref.shape, F32)


def _moe_ffn(x_buf, block_expert, n_used, w1, b1, w2, b2, layer, tm):
    n_blocks = block_expert.shape[0]
    _, n_exp, d, f2 = w1.shape
    f = w2.shape[2]
    grid_spec = pltpu.PrefetchScalarGridSpec(
        num_scalar_prefetch=2,
        grid=(n_blocks,),
        in_specs=[
            pl.BlockSpec((tm * 8, LANES), lambda i, be, nu: (i, 0)),
            pl.BlockSpec((None, 1, d, f2), lambda i, be, nu: (layer, be[i], 0, 0)),
            pl.BlockSpec((1, 1, f2), lambda i, be, nu: (be[i], 0, 0)),
            pl.BlockSpec((None, 1, f, d), lambda i, be, nu: (layer, be[i], 0, 0)),
            pl.BlockSpec((1, 1, d), lambda i, be, nu: (be[i], 0, 0)),
        ],
        out_specs=pl.BlockSpec((tm * 8, LANES), lambda i, be, nu: (i, 0)),
        scratch_shapes=[pltpu.VMEM((d, f2), BF16), pltpu.VMEM((f, d), BF16)],
    )
    return pl.pallas_call(
        _ffn_kernel,
        grid_spec=grid_spec,
        out_shape=jax.ShapeDtypeStruct((n_blocks * tm * 8, LANES), F32),
        compiler_params=_cp(("arbitrary",)),
        name="moe_ffn",
    )(block_expert, n_used, x_buf, w1, b1.reshape(n_exp, 1, f2), w2, b2.reshape(n_exp, 1, d))


def _combine_kernel(slot_ref, slotn_ref, y_hbm, x_ref, mod_ref, gate_ref, nf_ref, o_ref, ybuf0, ybuf1, sem, *, final):
    b = pl.program_id(0)
    i = pl.program_id(1)
    step = b * pl.num_programs(1) + i
    n_steps = pl.num_programs(0) * pl.num_programs(1)
    tc = x_ref.shape[1]
    n_rows = TOP_K * tc

    def issue(idx_ref, buf, s):
        for j in range(n_rows):
            src = pl.multiple_of(idx_ref[0, 0, j], 8)
            pltpu.make_async_copy(y_hbm.at[pl.ds(src, 8), :], buf.at[pl.ds(j * 8, 8), :], sem.at[s]).start()

    def wait(buf, s):
        pltpu.make_async_copy(y_hbm.at[pl.ds(0, n_rows * 8), :], buf, sem.at[s]).wait()

    @pl.when(step == 0)
    def _():
        issue(slot_ref, ybuf0, 0)

    def run(cur, nxt, sc, sn):
        wait(cur, sc)
        issue(slotn_ref, nxt, sn)
        gates = gate_ref[0]
        acc = gates[:, 0:1] * _load_row_tiles(cur, 0, tc)
        for k in range(1, TOP_K):
            acc += gates[:, k:k + 1] * _load_row_tiles(cur, k * tc, tc)
        out = x_ref[0] + mod_ref[0, 0, 5:6, :] * acc
        if final:
            out = _rms(out, nf_ref[...])
        o_ref[0] = out

        @pl.when(step == n_steps - 1)
        def _():
            wait(nxt, sn)

    @pl.when(step % 2 == 0)
    def _():
        run(ybuf0, ybuf1, 0, 1)

    @pl.when(step % 2 == 1)
    def _():
        run(ybuf1, ybuf0, 1, 0)


def _combine(y_sorted, slot, x_mid, mod, gates, norm_final, nt, ntile, final):
    bsz, _, d = x_mid.shape
    tc = ROW_TILE
    n_steps = bsz * ntile
    slots = slot * 8
    tok = lambda w: pl.BlockSpec((1, tc, w), lambda b, i: (b, i, 0))
    return pl.pallas_call(
        functools.partial(_combine_kernel, final=final),
        grid=(bsz, ntile),
        in_specs=[
            pl.BlockSpec((1, 1, tc * TOP_K), lambda b, i: (b * ntile + i, 0, 0), memory_space=pltpu.SMEM),
            pl.BlockSpec((1, 1, tc * TOP_K), lambda b, i: (jnp.minimum(b * ntile + i + 1, n_steps - 1), 0, 0),
                         memory_space=pltpu.SMEM),
            pl.BlockSpec(memory_space=pl.ANY),
            tok(d),
            pl.BlockSpec((1, 1, 6, d), lambda b, i: (b, i // nt, 0, 0)),
            tok(ROUTER_LANES),
            pl.BlockSpec((1, d), lambda b, i: (0, 0)),
        ],
        out_specs=tok(d),
        out_shape=jax.ShapeDtypeStruct((bsz, ntile * tc, d), F32),
        scratch_shapes=[pltpu.VMEM((TOP_K * tc * 8, LANES), F32), pltpu.VMEM((TOP_K * tc * 8, LANES), F32),
                        pltpu.SemaphoreType.DMA((2,))],
        compiler_params=_cp(("arbitrary", "arbitrary")),
        name="moe_combine",
    )(slots, slots, y_sorted, x_mid, mod, gates, norm_final.reshape(1, d))


def _split_dot(a, w):
    ah = a.astype(BF16)
    al = (a - ah.astype(F32)).astype(BF16)
    wh = w.astype(BF16)
    wl = (w - wh.astype(F32)).astype(BF16)
    return (jnp.dot(ah, wh, preferred_element_type=F32) + jnp.dot(al, wh, preferred_element_type=F32)
            + jnp.dot(ah, wl, preferred_element_type=F32))


def _filt_kernel(z_ref, w1_ref, b1_ref, w2_ref, b2_ref, w3_ref, b3_ref, fr_ref, wo_ref, dec_ref, o_ref):
    fr = fr_ref[...]
    h = jnp.sin(fr * (_split_dot(z_ref[...], w1_ref[...]) + b1_ref[...]))
    h = jnp.sin(fr * (_split_dot(h, w2_ref[...]) + b2_ref[...]))
    h = jnp.sin(fr * (_split_dot(h, w3_ref[...]) + b3_ref[...]))
    o_ref[...] = _split_dot(h, wo_ref[...]) * dec_ref[...]


def _hyena_filters_pl(n, filt):
    w1, b1, w2, b2, w3, b3, freq, w_out = filt
    fw = w2.shape[0]
    t = jnp.linspace(0.0, 1.0, n, dtype=F32)[:, None]
    wpos = (2.0 * math.pi / n) * jnp.arange(n, dtype=F32)[:, None]
    bands = jnp.linspace(1e-4, HY_BANDS - 1, HY_BANDS, dtype=F32)
    z = jnp.concatenate([t, jnp.cos(wpos * bands), -jnp.sin(wpos * bands), jnp.zeros((n, fw - HY_EMB), F32)], axis=-1)
    w1p = jnp.concatenate([w1, jnp.zeros((fw - HY_EMB, fw), F32)], axis=0)
    max_decay = math.log(HY_TARGET) / HY_FAST_DECAY
    min_decay = math.log(HY_TARGET) / HY_SLOW_DECAY
    deltas = jnp.linspace(min_decay, max_decay, HY_WIDTH, dtype=F32)
    decay = jnp.tile(jnp.exp(-t * jnp.abs(deltas)), (1, 2))
    tm = min(n, 512)
    row = lambda w: pl.BlockSpec((tm, w), lambda i: (i, 0))
    const = lambda s: pl.BlockSpec(s, lambda i: (0, 0))
    return pl.pallas_call(
        _filt_kernel,
        grid=(n // tm,),
        in_specs=[row(fw), const((fw, fw)), const((1, fw)), const((fw, fw)), const((1, fw)), const((fw, fw)), const((1, fw)),
                  const((1, fw)), const((fw, 2 * HY_WIDTH)), row(2 * HY_WIDTH)],
        out_specs=row(2 * HY_WIDTH),
        out_shape=jax.ShapeDtypeStruct((n, 2 * HY_WIDTH), F32),
        compiler_params=_cp(("arbitrary",)),
        name="hyena_filters",
    )(z, w1p, b1.reshape(1, fw), w2, b2.reshape(1, fw), w3, b3.reshape(1, fw), freq.reshape(1, fw), w_out, decay)


def _gate_kernel(u_ref, prev_ref, next_ref, w_ref, b_ref, z_ref, x0_ref, *, nt):
    i = pl.program_id(1)
    u = u_ref[0]
    tm = u.shape[0]
    w = w_ref[...]
    has_prev = jnp.where((i != 0) & (i != nt), 1.0, 0.0)
    has_next = jnp.where((i != nt - 1) & (i != nt), 1.0, 0.0)
    row = lax.broadcasted_iota(I32, u.shape, 0)
    up = jnp.where(row == 0, prev_ref[0, 7:8, :] * has_prev, pltpu.roll(u, 1, 0))
    un = jnp.where(row == tm - 1, next_ref[0, 0:1, :] * has_next, pltpu.roll(u, tm - 1, 0))
    uc = up * w[0:1, :] + u * w[1:2, :] + un * w[2:3, :] + b_ref[...]
    x0_ref[0] = uc[:, :HY_WIDTH]
    z_ref[0] = uc[:, 2 * HY_WIDTH:] * uc[:, HY_WIDTH:2 * HY_WIDTH]


def _hyena_gate(hy, conv_w, conv_b, nt, ntile):
    bsz, rows, w3 = hy.shape
    tm = ROW_TILE
    sub = tm // 8
    last8 = rows // 8 - 1
    out = pl.BlockSpec((1, tm, HY_WIDTH), lambda b, i: (b, i, 0))
    return pl.pallas_call(
        functools.partial(_gate_kernel, nt=nt),
        grid=(bsz, ntile),
        in_specs=[
            pl.BlockSpec((1, tm, w3), lambda b, i: (b, i, 0)),
            pl.BlockSpec((1, 8, w3), lambda b, i: (b, jnp.maximum(i * sub - 1, 0), 0)),
            pl.BlockSpec((1, 8, w3), lambda b, i: (b, jnp.minimum((i + 1) * sub, last8), 0)),
            pl.BlockSpec((HY_SHORT, w3), lambda b, i: (0, 0)),
            pl.BlockSpec((1, w3), lambda b, i: (0, 0)),
        ],
        out_specs=[out, out],
        out_shape=[jax.ShapeDtypeStruct((bsz, ntile * tm, HY_WIDTH), F32)] * 2,
        compiler_params=_cp(("arbitrary", "arbitrary")),
        name="hyena_gate",
    )(hy, hy, hy, conv_w, conv_b.reshape(1, w3))


FFT_N2 = LANES


def _dft_tables(n_fft):
    n1 = n_fft // FFT_N2
    k1 = jnp.arange(n1, dtype=I32)
    a = (2.0 * math.pi / n1) * ((k1[:, None] * k1[None, :]) % n1).astype(F32)
    f1 = jnp.stack([jnp.cos(a), -jnp.sin(a)], axis=1).reshape(2 * n1, n1)
    g = (jnp.stack([jnp.cos(a), -jnp.sin(a)], axis=2).reshape(n1, 2 * n1) / n_fft)
    k2 = jnp.arange(FFT_N2, dtype=I32)
    m = (k2[None, None, :] * (k1[:, None, None] + n1 * k2[None, :, None])) % n_fft
    th = (2.0 * math.pi / n_fft) * m.astype(F32)
    mre, mim = jnp.cos(th), -jnp.sin(th)
    mr = jnp.concatenate([jnp.concatenate([mre, -mim], axis=2), jnp.concatenate([mim, mre], axis=2)], axis=1)
    return f1.astype(BF16), g.astype(BF16), mr.astype(BF16), jnp.swapaxes(mr, 1, 2).astype(BF16)


def _fft1_kernel(x_ref, f_ref, o_ref, *, n1_in):
    f = f_ref[...]
    n_rows = f.shape[0]

    def body(n2, carry):
        xs = x_ref[pl.ds(n2, n1_in, stride=FFT_N2), :].astype(BF16)
        o_ref[pl.ds(n2, n_rows, stride=FFT_N2), :] = jnp.dot(f, xs, preferred_element_type=F32)
        return carry
    lax.fori_loop(0, FFT_N2, body, 0, unroll=8)


def _fft1(x, f1, n1_in):
    bsz, _, ch = x.shape
    n_rows = f1.shape[0]
    return pl.pallas_call(
        functools.partial(_fft1_kernel, n1_in=n1_in),
        grid=(bsz, ch // LANES),
        in_specs=[pl.BlockSpec((None, n1_in * FFT_N2, LANES), lambda b, h: (b, 0, h)),
                  pl.BlockSpec((n_rows, n1_in), lambda b, h: (0, 0))],
        out_specs=pl.BlockSpec((None, n_rows * FFT_N2, LANES), lambda b, h: (b, 0, h)),
        out_shape=jax.ShapeDtypeStruct((bsz, n_rows * FFT_N2, ch), F32),
        compiler_params=_cp(("arbitrary", "arbitrary")),
        name="hyena_fft1",
    )(x, f1[:, :n1_in])


def _spec_kernel(a_ref, mr_ref, o_ref):
    o_ref[0] = jnp.dot(mr_ref[0], a_ref[0].astype(BF16), preferred_element_type=F32)


def _fft_spectrum(a, mr):
    _, rows, ch = a.shape
    blk = 2 * FFT_N2
    return pl.pallas_call(
        _spec_kernel,
        grid=(rows // blk,),
        in_specs=[pl.BlockSpec((1, blk, ch), lambda k: (0, k, 0)), pl.BlockSpec((1, blk, blk), lambda k: (k, 0, 0))],
        out_specs=pl.BlockSpec((1, blk, ch), lambda k: (0, k, 0)),
        out_shape=jax.ShapeDtypeStruct((1, rows, ch), F32),
        compiler_params=_cp(("arbitrary",)),
        name="hyena_filter_spectrum",
    )(a, mr)


def _mid_kernel(a_ref, mr_ref, mrt_ref, ks_ref, o_ref):
    kre, kim = ks_ref[0, :FFT_N2, :], ks_ref[0, FFT_N2:, :]
    for b in range(a_ref.shape[0]):
        x = jnp.dot(mr_ref[0], a_ref[b].astype(BF16), preferred_element_type=F32)
        xre, xim = x[:FFT_N2], x[FFT_N2:]
        y = jnp.concatenate([xre * kre - xim * kim, xre * kim + xim * kre], axis=0)
        o_ref[b] = jnp.dot(mrt_ref[0], y.astype(BF16), preferred_element_type=F32)


def _fft_mid(a, mr, mrt, kspec):
    bsz, rows, ch = a.shape
    blk = 2 * FFT_N2
    return pl.pallas_call(
        _mid_kernel,
        grid=(rows // blk,),
        in_specs=[pl.BlockSpec((bsz, blk, ch), lambda k: (0, k, 0)),
                  pl.BlockSpec((1, blk, blk), lambda k: (k, 0, 0)),
                  pl.BlockSpec((1, blk, blk), lambda k: (k, 0, 0)),
                  pl.BlockSpec((1, blk, ch), lambda k: (0, k, 0))],
        out_specs=pl.BlockSpec((bsz, blk, ch), lambda k: (0, k, 0)),
        out_shape=jax.ShapeDtypeStruct((bsz, rows, ch), F32),
        compiler_params=_cp(("arbitrary",)),
        name="hyena_fft_mid",
    )(a, mr, mrt, kspec)


def _ifft1_kernel(b_ref, g_ref, z_ref, x0_ref, skip_ref, o_ref, *, n1_out):
    g = g_ref[...]
    n_rows = g.shape[1]
    skip = skip_ref[...]

    def body(n2, carry):
        bs = b_ref[pl.ds(n2, n_rows, stride=FFT_N2), :].astype(BF16)
        y = jnp.dot(g, bs, preferred_element_type=F32)
        rows = pl.ds(n2, n1_out, stride=FFT_N2)
        zs = z_ref[rows, :]
        o_ref[rows, :] = x0_ref[rows, :] * (y + zs * skip)
        return carry
    lax.fori_loop(0, FFT_N2, body, 0, unroll=8)


def _ifft1(bmat, g, z, x0, skip, n1_out):
    bsz, rows, ch = bmat.shape
    n = n1_out * FFT_N2
    seq = pl.BlockSpec((None, n, LANES), lambda b, h: (b, 0, h))
    return pl.pallas_call(
        functools.partial(_ifft1_kernel, n1_out=n1_out),
        grid=(bsz, ch // LANES),
        in_specs=[pl.BlockSpec((None, rows, LANES), lambda b, h: (b, 0, h)),
                  pl.BlockSpec((n1_out, g.shape[1]), lambda b, h: (0, 0)),
                  seq, seq, pl.BlockSpec((1, LANES), lambda b, h: (0, h))],
        out_specs=seq,
        out_shape=jax.ShapeDtypeStruct((bsz, n, ch), F32),
        compiler_params=_cp(("arbitrary", "arbitrary")),
        name="hyena_ifft1",
    )(bmat, g[:n1_out], z, x0, skip.reshape(1, ch))


def _hyena_fftconv(z, x0, k2, skip):
    n = k2.shape[0]
    n_fft = 2 * n
    f1, g, mr, mrt = _dft_tables(n_fft)
    k_full = jnp.concatenate([k2[:, :HY_WIDTH], k2[::-1, HY_WIDTH:]], axis=0)[None]
    kspec = _fft_spectrum(_fft1(k_full, f1, n_fft // FFT_N2), mr)
    a = _fft1(z, f1, n // FFT_N2)
    bmat = _fft_mid(a, mr, mrt, kspec)
    return _ifft1(bmat, g, z, x0, skip, n // FFT_N2)


def _ctxconv_kernel(z_ref, x0_ref, kf_ref, fd_ref, gd_ref, skip_ref, o_ref, ks_ref):
    n = z_ref.shape[1]
    half = fd_ref.shape[0] // 2

    @pl.when(pl.program_id(0) == 0)
    def _():
        ks_ref[...] = jnp.dot(fd_ref[...], kf_ref[...].astype(BF16), preferred_element_type=F32)

    z = z_ref[0]
    x = jnp.dot(fd_ref[:, :n], z.astype(BF16), preferred_element_type=F32)
    xre, xim = x[:half], x[half:]
    kre, kim = ks_ref[:half, :], ks_ref[half:, :]
    y = jnp.concatenate([xre * kre - xim * kim, xre * kim + xim * kre], axis=0)
    conv = jnp.dot(gd_ref[...], y.astype(BF16), preferred_element_type=F32)
    o_ref[0] = x0_ref[0] * (conv + z * skip_ref[...])


def _hyena_ctxconv(z, x0, k2, skip, row0):
    n = k2.shape[0]
    bsz, _, ch = z.shape
    n_fft = 2 * n
    kk = jnp.arange(n_fft, dtype=I32)
    th = (2.0 * math.pi / n_fft) * ((kk[:, None] * kk[None, :]) % n_fft).astype(F32)
    fd = jnp.concatenate([jnp.cos(th), -jnp.sin(th)], axis=0).astype(BF16)
    gd = (jnp.concatenate([jnp.cos(th[:n]), -jnp.sin(th[:n])], axis=1) / n_fft).astype(BF16)
    k_full = jnp.concatenate([k2[:, :HY_WIDTH], k2[::-1, HY_WIDTH:]], axis=0)
    blk = row0 // n
    seq = pl.BlockSpec((1, n, ch), lambda b: (b, blk, 0))
    const = lambda s: pl.BlockSpec(s, lambda b: (0, 0))
    return pl.pallas_call(
        _ctxconv_kernel,
        grid=(bsz,),
        in_specs=[seq, seq, const((n_fft, ch)), const((2 * n_fft, n_fft)), const((n, 2 * n_fft)), const((1, ch))],
        out_specs=pl.BlockSpec((1, n, ch), lambda b: (b, 0, 0)),
        out_shape=jax.ShapeDtypeStruct((bsz, n, ch), F32),
        scratch_shapes=[pltpu.VMEM((2 * n_fft, ch), F32)],
        compiler_params=_cp(("arbitrary",)),
        name="hyena_ctxconv",
    )(z, x0, k_full, fd, gd, skip.reshape(1, ch))


def _rope_tables(n_lat, rows):
    nrow = n_lat // GRID_W
    row = jnp.repeat(jnp.arange(nrow, dtype=F32), GRID_W)
    col = jnp.tile(jnp.arange(GRID_W, dtype=F32), nrow)
    n_freq = HEAD_DIM // 4
    inv_freq = ROPE_THETA ** (-jnp.arange(n_freq, dtype=F32) / n_freq)
    ar = row[:, None] * inv_freq
    ac = col[:, None] * inv_freq
    cos64 = jnp.concatenate([jnp.cos(ar), jnp.cos(ar), jnp.cos(ac), jnp.cos(ac)], axis=-1)
    sin64 = jnp.concatenate([-jnp.sin(ar), jnp.sin(ar), -jnp.sin(ac), jnp.sin(ac)], axis=-1)
    pad = rows - n_lat
    cos_t = jnp.concatenate([jnp.tile(cos64, (1, LANES // HEAD_DIM)), jnp.ones((pad, LANES), F32)], axis=0)
    sin_t = jnp.concatenate([jnp.tile(sin64, (1, LANES // HEAD_DIM)), jnp.zeros((pad, LANES), F32)], axis=0)
    return cos_t, sin_t


def kernel(x, c, ctx, c_ctx, w_ada, b_ada, norm_mix, norm_ffn, w_in, b_in, hy_conv_w, hy_conv_b, hy_filt_w1, hy_filt_b1, hy_filt_w2, hy_filt_b2, hy_filt_w3, hy_filt_b3, hy_filt_freq, hy_filt_out, hy_skip, ga_q_norm, ga_k_norm, wa_sink, branch_norm, w_out, b_out, w_router, b_router, w_mlp1, b_mlp1, w_mlp2, b_mlp2, norm_final):
    bsz, n_lat, d = x.shape
    n_ctx = ctx.shape[1]
    depth = w_ada.shape[0]
    assert n_ctx == ROW_TILE and n_lat % KV_CHUNK == 0 and bsz < 16 and d == D_MODEL
    nt = n_lat // ROW_TILE
    rows = n_lat + n_ctx
    cos_t, sin_t = _rope_tables(n_lat, rows)

    c_all = jnp.zeros((16, d), F32).at[:bsz].set(c).at[bsz].set(c_ctx)
    mod_all = _ada_mod(c_all, w_ada, b_ada)
    x_all = jnp.concatenate([x, ctx], axis=1)

    for l in range(depth):
        last = l == depth - 1
        ntile = nt if last else nt + 1
        m = mod_all[l].reshape(16, 6, d)
        mod = jnp.stack([m[:bsz], jnp.broadcast_to(m[bsz], (bsz, 6, d))], axis=1)

        hy, gq, gk, gv, wq, wk, wv = _inproj(x_all, mod, norm_mix[l], w_in[l], b_in[l], cos_t, sin_t,
                                             ga_q_norm[l], ga_k_norm[l], nt)
        filt = (hy_filt_w1[l], hy_filt_b1[l], hy_filt_w2[l], hy_filt_b2[l], hy_filt_w3[l], hy_filt_b3[l],
                hy_filt_freq[l], hy_filt_out[l])
        z, x0 = _hyena_gate(hy, hy_conv_w[l], hy_conv_b[l], nt, ntile)
        y_hy = _hyena_fftconv(z, x0, _hyena_filters_pl(n_lat, filt), hy_skip[l])
        if not last:
            yc_hy = _hyena_ctxconv(z, x0, _hyena_filters_pl(n_ctx, filt), hy_skip[l], n_lat)
            y_hy = jnp.concatenate([y_hy, yc_hy], axis=1)
        y_ga = _global_attention(gq, gk, gv, nt, ntile)
        y_wa = _window_attention(wq, wk, wv, wa_sink[l], nt, ntile)
        x_mid, h2, top_g, top_e, top_r, counts = _merge(x_all, y_hy, y_ga, y_wa, mod, branch_norm[l], w_out[l], b_out[l],
                                                        norm_ffn[l], w_router[l], b_router[l], nt, ntile)
        rows_e, block_expert, n_used, zero_blocks = _moe_plan(top_e, top_r, counts, MOE_TILE)
        x_buf = _dispatch(h2.reshape(-1, LANES), rows_e, zero_blocks, block_expert.shape[0], MOE_TILE)
        y_sorted = _moe_ffn(x_buf, block_expert, n_used, w_mlp1, b_mlp1[l], w_mlp2, b_mlp2[l], l, MOE_TILE)
        x_all = _combine(y_sorted, rows_e, x_mid, mod, top_g, norm_final, nt, ntile, last)
    return x_all
```

```python
import functools
import math

import jax
import jax.numpy as jnp
from jax import lax
from jax.experimental import pallas as pl
from jax.experimental.pallas import tpu as pltpu

F32 = jnp.float32
BF16 = jnp.bfloat16
I32 = jnp.int32

D_MODEL = 1024
HEAD_DIM = 64
GRID_W = 64
HY_WIDTH = 256
GA_HEADS = 6
GA_KV_HEADS = 2
WA_HEADS = 6
WA_KV_HEADS = 2
GROUP = GA_HEADS // GA_KV_HEADS
Q_WIDTH = GA_HEADS * HEAD_DIM
KV_WIDTH = GA_KV_HEADS * HEAD_DIM
MIX_WIDTH = HY_WIDTH + 2 * Q_WIDTH
IN_WIDTH = 3 * HY_WIDTH + 2 * (Q_WIDTH + 2 * KV_WIDTH)
WINDOW = 128
ROPE_THETA = 10000.0
ATTN_SCALE = HEAD_DIM ** -0.5
HY_SHORT = 3
HY_EMB = 33
HY_BANDS = (HY_EMB - 1) // 2
HY_FAST_DECAY = 0.3
HY_SLOW_DECAY = 1.5
HY_TARGET = 1e-2
N_EXPERTS = 32
TOP_K = 4
SWIGLU_ALPHA = 1.702
SWIGLU_LIMIT = 7.0
EPS = 1e-6
NEG_INF = -1e30

LANES = 128
VMEM_LIMIT_V7X = 56 * 1024 * 1024

ROW_TILE = 256
KV_CHUNK = 1024
MOE_TILE = 512
ROUTER_LANES = LANES


def _cp(sem, vmem=VMEM_LIMIT_V7X):
    return pltpu.CompilerParams(dimension_semantics=sem, vmem_limit_bytes=vmem)


def _rms(v, w):
    return v * lax.rsqrt(jnp.mean(v * v, axis=-1, keepdims=True) + EPS) * w


def _ada_kernel(c_ref, w_ref, b_ref, o_ref):
    c = c_ref[...]
    s = c / (1.0 + jnp.exp(-c))
    hi = s.astype(BF16)
    lo = (s - hi.astype(F32)).astype(BF16)
    w = w_ref[0]
    whi = w.astype(BF16)
    wlo = (w - whi.astype(F32)).astype(BF16)
    acc = jnp.dot(hi, whi, preferred_element_type=F32)
    acc += jnp.dot(lo, whi, preferred_element_type=F32)
    acc += jnp.dot(hi, wlo, preferred_element_type=F32)
    o_ref[0] = acc + b_ref[0]


def _ada_mod(c_all, w_ada, b_ada):
    n_layers, d, n6 = w_ada.shape
    tn = n6 // 4
    return pl.pallas_call(
        _ada_kernel,
        grid=(n_layers, n6 // tn),
        in_specs=[
            pl.BlockSpec((16, d), lambda l, j: (0, 0)),
            pl.BlockSpec((1, d, tn), lambda l, j: (l, 0, j)),
            pl.BlockSpec((1, 1, tn), lambda l, j: (l, 0, j)),
        ],
        out_specs=pl.BlockSpec((1, 16, tn), lambda l, j: (l, 0, j)),
        out_shape=jax.ShapeDtypeStruct((n_layers, 16, n6), F32),
        compiler_params=_cp(("arbitrary", "arbitrary")),
        name="ada_mod",
    )(c_all, w_ada, b_ada.reshape(n_layers, 1, n6))


def _inproj_kernel(x_ref, mod_ref, nw_ref, w_ref, b_ref, cos_ref, sin_ref, qn_ref, kn_ref, g_ref,
                   hy_ref, gq_ref, gk_ref, gv_ref, wq_ref, wk_ref, wv_ref):
    x = x_ref[0]
    tm = x.shape[0]
    shift = mod_ref[0, 0, 0:1, :]
    scale = mod_ref[0, 0, 1:2, :]
    h = _rms(x, nw_ref[...]) * (1.0 + scale) + shift
    p = jnp.dot(h.astype(BF16), w_ref[...], preferred_element_type=F32) + b_ref[...]
    hy_ref[0] = p[:, :3 * HY_WIDTH]

    cos = cos_ref[...]
    sin = sin_ref[...]
    lane = lax.broadcasted_iota(I32, (tm, LANES), 1)
    first = (lane % (HEAD_DIM // 2)) < (HEAD_DIM // 4)
    gmat = g_ref[...]

    def rope(v):
        sw = jnp.where(first, pltpu.roll(v, LANES - HEAD_DIM // 4, 1), pltpu.roll(v, HEAD_DIM // 4, 1))
        return v * cos + sw * sin

    def head_norm(v, w):
        t = v * v
        hi = t.astype(BF16)
        lo = (t - hi.astype(F32)).astype(BF16)
        ms = jnp.dot(hi, gmat, preferred_element_type=F32) + jnp.dot(lo, gmat, preferred_element_type=F32)
        return v * lax.rsqrt(ms + EPS) * w

    o = 3 * HY_WIDTH
    for u in range(Q_WIDTH // LANES):
        v = p[:, o + u * LANES:o + (u + 1) * LANES]
        v = rope(head_norm(v, qn_ref[...]))
        gq_ref[0, :, u * LANES:(u + 1) * LANES] = (v * ATTN_SCALE).astype(BF16)
    o += Q_WIDTH
    gk_ref[0] = rope(head_norm(p[:, o:o + KV_WIDTH], kn_ref[...])).astype(BF16)
    o += KV_WIDTH
    gv_ref[0] = p[:, o:o + KV_WIDTH].astype(BF16)
    o += KV_WIDTH
    for u in range(Q_WIDTH // LANES):
        v = rope(p[:, o + u * LANES:o + (u + 1) * LANES])
        wq_ref[0, :, u * LANES:(u + 1) * LANES] = (v * ATTN_SCALE).astype(BF16)
    o += Q_WIDTH
    wk_ref[0] = rope(p[:, o:o + KV_WIDTH]).astype(BF16)
    o += KV_WIDTH
    wv_ref[0] = p[:, o:o + KV_WIDTH].astype(BF16)


def _inproj(x_all, mod, norm_w, w_in, b_in, cos_t, sin_t, q_norm, k_norm, nt):
    bsz, rows, d = x_all.shape
    tm = ROW_TILE
    ntile = rows // tm
    gmat = jnp.kron(jnp.eye(LANES // HEAD_DIM, dtype=F32), jnp.full((HEAD_DIM, HEAD_DIM), 1.0 / HEAD_DIM, F32)).astype(BF16)
    qn = jnp.tile(q_norm, LANES // HEAD_DIM).reshape(1, LANES)
    kn = jnp.tile(k_norm, LANES // HEAD_DIM).reshape(1, LANES)
    tok = lambda w: pl.BlockSpec((1, tm, w), lambda b, i: (b, i, 0))
    const = lambda s: pl.BlockSpec(s, lambda b, i: (0,) * len(s))
    out_w = (3 * HY_WIDTH, Q_WIDTH, KV_WIDTH, KV_WIDTH, Q_WIDTH, KV_WIDTH, KV_WIDTH)
    out_dt = (F32, BF16, BF16, BF16, BF16, BF16, BF16)
    return pl.pallas_call(
        _inproj_kernel,
        grid=(bsz, ntile),
        in_specs=[
            tok(d),
            pl.BlockSpec((1, 1, 6, d), lambda b, i: (b, i // nt, 0, 0)),
            const((1, d)),
            const((d, IN_WIDTH)),
            const((1, IN_WIDTH)),
            pl.BlockSpec((tm, LANES), lambda b, i: (i, 0)),
            pl.BlockSpec((tm, LANES), lambda b, i: (i, 0)),
            const((1, LANES)),
            const((1, LANES)),
            const((LANES, LANES)),
        ],
        out_specs=[tok(w) for w in out_w],
        out_shape=[jax.ShapeDtypeStruct((bsz, rows, w), dt) for w, dt in zip(out_w, out_dt)],
        compiler_params=_cp(("arbitrary", "arbitrary")),
        name="inproj",
    )(x_all, mod, norm_w.reshape(1, d), w_in.astype(BF16), b_in.reshape(1, IN_WIDTH), cos_t, sin_t, qn, kn, gmat)


def _stack_q(q_ref, qs_ref):
    qf = q_ref[0].astype(F32)
    tq = qf.shape[0]
    for j in range(GA_KV_HEADS):
        for g in range(GROUP):
            h = GROUP * j + g
            blk = qf[:, (h // 2) * LANES:(h // 2 + 1) * LANES]
            if h % 2 != j:
                blk = pltpu.roll(blk, HEAD_DIM, 1)
            qs_ref[j, g * tq:(g + 1) * tq, :] = blk.astype(BF16)


def _unstack_o(norm_fn, o_ref, tq):
    lane = lax.broadcasted_iota(I32, (tq, LANES), 1)
    heads = []
    for j in range(GA_KV_HEADS):
        for g in range(GROUP):
            h = GROUP * j + g
            o = norm_fn(j, g)
            if h % 2 != j:
                o = pltpu.roll(o, HEAD_DIM, 1)
            heads.append(o)
    for u in range(Q_WIDTH // LANES):
        o_ref[0, :, u * LANES:(u + 1) * LANES] = jnp.where(lane < HEAD_DIM, heads[2 * u], heads[2 * u + 1])


def _half_masks(rows):
    lane = lax.broadcasted_iota(I32, (rows, LANES), 1)
    return [lane < HEAD_DIM, lane >= HEAD_DIM]


def _dot_t(a, b):
    return lax.dot_general(a, b, (((1,), (1,)), ((), ())), preferred_element_type=F32)


def _ga_kernel(q_ref, k_ref, v_ref, o_ref, qs_ref, m_ref, acc_ref, *, nt, n_lat):
    i = pl.program_id(1)
    tq = q_ref.shape[1]
    _stack_q(q_ref, qs_ref)
    m_ref[...] = jnp.full(m_ref.shape, NEG_INF, F32)
    acc_ref[...] = jnp.zeros(acc_ref.shape, F32)

    def chunk(kc, vc):
        masks = _half_masks(kc.shape[0])
        for j in range(GA_KV_HEADS):
            km = jnp.where(masks[j], kc, jnp.zeros_like(kc))
            vm = jnp.where(masks[j], vc, jnp.ones_like(vc))
            s = _dot_t(qs_ref[j], km)
            m_old = m_ref[j]
            m_new = jnp.maximum(m_old, jnp.max(s, axis=-1, keepdims=True))
            p = jnp.exp((s - jnp.tile(m_new, (1, s.shape[1] // LANES))).astype(BF16))
            acc_ref[j] = jnp.exp(m_old - m_new) * acc_ref[j] + jnp.dot(p, vm, preferred_element_type=F32)
            m_ref[j] = m_new

    @pl.when(i < nt)
    def _():
        def body(c, carry):
            st = pl.multiple_of(c * KV_CHUNK, KV_CHUNK)
            chunk(k_ref[0, pl.ds(st, KV_CHUNK), :], v_ref[0, pl.ds(st, KV_CHUNK), :])
            return carry
        lax.fori_loop(0, n_lat // KV_CHUNK, body, 0)

    chunk(k_ref[0, n_lat:, :], v_ref[0, n_lat:, :])

    def norm_fn(j, g):
        a = acc_ref[j, g * tq:(g + 1) * tq, :]
        return a / pltpu.roll(a, HEAD_DIM, 1)

    _unstack_o(norm_fn, o_ref, tq)


def _global_attention(gq, gk, gv, nt, ntile):
    bsz, rows, _ = gq.shape
    tq = ROW_TILE
    n_lat = nt * tq
    return pl.pallas_call(
        functools.partial(_ga_kernel, nt=nt, n_lat=n_lat),
        grid=(bsz, ntile),
        in_specs=[
            pl.BlockSpec((1, tq, Q_WIDTH), lambda b, i: (b, i, 0)),
            pl.BlockSpec((1, rows, KV_WIDTH), lambda b, i: (b, 0, 0)),
            pl.BlockSpec((1, rows, KV_WIDTH), lambda b, i: (b, 0, 0)),
        ],
        out_specs=pl.BlockSpec((1, tq, Q_WIDTH), lambda b, i: (b, i, 0)),
        out_shape=jax.ShapeDtypeStruct((bsz, ntile * tq, Q_WIDTH), F32),
        scratch_shapes=[
            pltpu.VMEM((GA_KV_HEADS, GROUP * tq, LANES), BF16),
            pltpu.VMEM((GA_KV_HEADS, GROUP * tq, LANES), F32),
            pltpu.VMEM((GA_KV_HEADS, GROUP * tq, LANES), F32),
        ],
        compiler_params=_cp(("arbitrary", "arbitrary")),
        name="global_attn",
    )(gq, gk, gv)


def _wa_kernel(q_ref, k_ref, v_ref, sink_ref, bias_ref, o_ref, qs_ref, acc_ref, l_ref, *, nt, n_lat):
    i = pl.program_id(1)
    tq = q_ref.shape[1]
    band = tq + 2 * WINDOW
    _stack_q(q_ref, qs_ref)
    st = pl.multiple_of(i * tq, tq)
    kb = k_ref[0, pl.ds(st, band), :]
    vb = v_ref[0, pl.ds(st, band), :]
    kc = k_ref[0, WINDOW + n_lat:WINDOW + n_lat + ROW_TILE, :]
    vc = v_ref[0, WINDOW + n_lat:WINDOW + n_lat + ROW_TILE, :]

    kpos = lax.broadcasted_iota(I32, (1, band), 1) + (i * tq - WINDOW)
    in_seq = (kpos >= 0) & (kpos < n_lat) & (i < nt)
    bias = bias_ref[...]

    mb = _half_masks(band)
    mc = _half_masks(ROW_TILE)
    for j in range(WA_KV_HEADS):
        q = qs_ref[j]
        s_loc = jnp.where(in_seq, _dot_t(q, jnp.where(mb[j], kb, jnp.zeros_like(kb))) + bias, NEG_INF)
        s_ctx = _dot_t(q, jnp.where(mc[j], kc, jnp.zeros_like(kc)))
        sink = jnp.concatenate(
            [jnp.broadcast_to(sink_ref[GROUP * j + g:GROUP * j + g + 1, :], (tq, LANES)) for g in range(GROUP)], axis=0)
        m = jnp.maximum(jnp.maximum(jnp.max(s_loc, axis=-1, keepdims=True), jnp.max(s_ctx, axis=-1, keepdims=True)), sink)
        p_loc = jnp.exp(s_loc - jnp.tile(m, (1, band // LANES))).astype(BF16)
        p_ctx = jnp.exp(s_ctx - jnp.tile(m, (1, ROW_TILE // LANES))).astype(BF16)
        acc_ref[j] = (jnp.dot(p_loc, jnp.where(mb[j], vb, jnp.ones_like(vb)), preferred_element_type=F32)
                      + jnp.dot(p_ctx, jnp.where(mc[j], vc, jnp.ones_like(vc)), preferred_element_type=F32))
        l_ref[j] = jnp.exp(sink - m)

    def norm_fn(j, g):
        a = acc_ref[j, g * tq:(g + 1) * tq, :]
        return a / (pltpu.roll(a, HEAD_DIM, 1) + l_ref[j, g * tq:(g + 1) * tq, :])

    _unstack_o(norm_fn, o_ref, tq)


def _window_attention(wq, wk, wv, sink, nt, ntile):
    bsz, rows, _ = wq.shape
    tq = ROW_TILE
    n_lat = nt * tq
    pad = ((0, 0), (WINDOW, WINDOW), (0, 0))
    kp = jnp.pad(wk, pad)
    vp = jnp.pad(wv, pad)
    sink_b = jnp.zeros((8, LANES), F32).at[:WA_HEADS].set(jnp.broadcast_to(sink[:, None], (WA_HEADS, LANES)))
    band = tq + 2 * WINDOW
    rel = jnp.arange(band, dtype=I32)[None, :] - WINDOW - (jnp.arange(GROUP * tq, dtype=I32) % tq)[:, None]
    band_bias = jnp.where(jnp.abs(rel) <= WINDOW, 0.0, NEG_INF).astype(F32)
    return pl.pallas_call(
        functools.partial(_wa_kernel, nt=nt, n_lat=n_lat),
        grid=(bsz, ntile),
        in_specs=[
            pl.BlockSpec((1, tq, Q_WIDTH), lambda b, i: (b, i, 0)),
            pl.BlockSpec((1, rows + 2 * WINDOW, KV_WIDTH), lambda b, i: (b, 0, 0)),
            pl.BlockSpec((1, rows + 2 * WINDOW, KV_WIDTH), lambda b, i: (b, 0, 0)),
            pl.BlockSpec((8, LANES), lambda b, i: (0, 0)),
            pl.BlockSpec((GROUP * tq, band), lambda b, i: (0, 0)),
        ],
        out_specs=pl.BlockSpec((1, tq, Q_WIDTH), lambda b, i: (b, i, 0)),
        out_shape=jax.ShapeDtypeStruct((bsz, ntile * tq, Q_WIDTH), F32),
        scratch_shapes=[
            pltpu.VMEM((WA_KV_HEADS, GROUP * tq, LANES), BF16),
            pltpu.VMEM((WA_KV_HEADS, GROUP * tq, LANES), F32),
            pltpu.VMEM((WA_KV_HEADS, GROUP * tq, LANES), F32),
        ],
        compiler_params=_cp(("arbitrary", "arbitrary")),
        name="window_attn",
    )(wq, kp, vp, sink_b, band_bias)


def _store_row_tiles(ref, v):
    rows = v.shape[0]
    for s in range(v.shape[1] // LANES):
        ref[pl.ds(s, rows, stride=8), :] = v[:, s * LANES:(s + 1) * LANES]


def _load_row_tiles(ref, row0, rows):
    return jnp.concatenate([ref[pl.ds(row0 * 8 + s, rows, stride=8), :] for s in range(8)], axis=1)


def _merge_kernel(x_ref, hy_ref, ga_ref, wa_ref, mod_ref, bn_ref, wo_ref, bo_ref, nf_ref, wrh_ref, wrl_ref, br_ref, tri_ref,
                  xo_ref, h2_ref, tg_ref, te_ref, tr_ref, cnt_ref, base_ref):
    @pl.when((pl.program_id(0) == 0) & (pl.program_id(1) == 0))
    def _():
        base_ref[...] = jnp.zeros(base_ref.shape, F32)

    bn = bn_ref[...]
    g0 = HY_WIDTH
    g1 = HY_WIDTH + Q_WIDTH
    y = jnp.concatenate([_rms(hy_ref[0], bn[:, :g0]), _rms(ga_ref[0], bn[:, g0:g1]), _rms(wa_ref[0], bn[:, g1:])], axis=-1)
    o = jnp.dot(y.astype(BF16), wo_ref[...], preferred_element_type=F32) + bo_ref[...]
    x1 = x_ref[0] + mod_ref[0, 0, 2:3, :] * o
    xo_ref[0] = x1
    h2 = _rms(x1, nf_ref[...]) * (1.0 + mod_ref[0, 0, 4:5, :]) + mod_ref[0, 0, 3:4, :]
    _store_row_tiles(h2_ref, h2)

    hi = h2.astype(BF16)
    lo = (h2 - hi.astype(F32)).astype(BF16)
    logits = (jnp.dot(hi, wrh_ref[...], preferred_element_type=F32) + jnp.dot(lo, wrh_ref[...], preferred_element_type=F32)
              + jnp.dot(hi, wrl_ref[...], preferred_element_type=F32)) + br_ref[...]
    tm = logits.shape[0]
    lane = lax.broadcasted_iota(I32, (tm, ROUTER_LANES), 1)
    te = jnp.zeros((tm, ROUTER_LANES), F32)
    tr = jnp.zeros((tm, ROUTER_LANES), F32)
    tv = jnp.zeros((tm, ROUTER_LANES), F32)
    before = base_ref[...]
    tri = tri_ref[...]
    v0 = None
    for k in range(TOP_K):
        mx = jnp.max(logits, axis=-1, keepdims=True)
        idx = jnp.min(jnp.where(logits == mx, lane, ROUTER_LANES), axis=-1, keepdims=True)
        if k == 0:
            v0 = mx
        hit = lane == idx
        onehot = jnp.where(hit, 1.0, 0.0)
        prefix = jnp.dot(tri, onehot.astype(BF16), preferred_element_type=F32) + before
        rank = jnp.sum(onehot * prefix, axis=-1, keepdims=True)
        before = before + jnp.sum(onehot, axis=0, keepdims=True)
        te = jnp.where(lane == k, idx.astype(F32), te)
        tr = jnp.where(lane == k, rank, tr)
        tv = jnp.where(lane == k, jnp.exp(mx - v0), tv)
        logits = jnp.where(hit, NEG_INF, logits)
    base_ref[...] = before
    cnt_ref[...] = before
    tg_ref[0] = tv / jnp.sum(tv, axis=-1, keepdims=True)
    te_t = te.T
    tr_t = tr.T
    for k in range(TOP_K):
        te_ref[0, :, k * tm:(k + 1) * tm] = te_t[k:k + 1, :].astype(I32)
        tr_ref[0, :, k * tm:(k + 1) * tm] = tr_t[k:k + 1, :].astype(I32)


def _merge(x_all, y_hy, y_ga, y_wa, mod, branch_norm, w_out, b_out, norm_ffn, w_router, b_router, nt, ntile):
    bsz, _, d = x_all.shape
    tm = ROW_TILE
    rows = ntile * tm
    wr = jnp.zeros((d, ROUTER_LANES), F32).at[:, :N_EXPERTS].set(w_router)
    wrh = wr.astype(BF16)
    wrl = (wr - wrh.astype(F32)).astype(BF16)
    br = jnp.full((1, ROUTER_LANES), NEG_INF, F32).at[0, :N_EXPERTS].set(b_router)
    tri = (jnp.arange(tm, dtype=I32)[:, None] > jnp.arange(tm, dtype=I32)[None, :]).astype(BF16)
    tok = lambda w: pl.BlockSpec((1, tm, w), lambda b, i: (b, i, 0))
    const = lambda s: pl.BlockSpec(s, lambda b, i: (0,) * len(s))
    flat = pl.BlockSpec((1, 1, TOP_K * tm), lambda b, i: (b * ntile + i, 0, 0))
    return pl.pallas_call(
        _merge_kernel,
        grid=(bsz, ntile),
        in_specs=[
            tok(d), tok(HY_WIDTH), tok(Q_WIDTH), tok(Q_WIDTH),
            pl.BlockSpec((1, 1, 6, d), lambda b, i: (b, i // nt, 0, 0)),
            const((1, MIX_WIDTH)), const((MIX_WIDTH, d)), const((1, d)), const((1, d)),
            const((d, ROUTER_LANES)), const((d, ROUTER_LANES)), const((1, ROUTER_LANES)), const((tm, tm)),
        ],
        out_specs=[tok(d), pl.BlockSpec((None, tm * 8, LANES), lambda b, i: (b, i, 0)), tok(ROUTER_LANES), flat, flat,
                   const((1, ROUTER_LANES))],
        out_shape=[
            jax.ShapeDtypeStruct((bsz, rows, d), F32),
            jax.ShapeDtypeStruct((bsz, rows * 8, LANES), F32),
            jax.ShapeDtypeStruct((bsz, rows, ROUTER_LANES), F32),
            jax.ShapeDtypeStruct((bsz * ntile, 1, TOP_K * tm), I32),
            jax.ShapeDtypeStruct((bsz * ntile, 1, TOP_K * tm), I32),
            jax.ShapeDtypeStruct((1, ROUTER_LANES), F32),
        ],
        scratch_shapes=[pltpu.VMEM((1, ROUTER_LANES), F32)],
        compiler_params=_cp(("arbitrary", "arbitrary")),
        name="merge_router",
    )(x_all, y_hy, y_ga, y_wa, mod, branch_norm.reshape(1, MIX_WIDTH), w_out.astype(BF16), b_out.reshape(1, d),
      norm_ffn.reshape(1, d), wrh, wrl, br, tri)


def _moe_plan(te, tr, cnt, tm):
    steps, _, per = te.shape
    n_blocks = -(-(steps * per) // tm) + N_EXPERTS
    counts = cnt[0, :N_EXPERTS].astype(I32)
    nblk = (counts + tm - 1) // tm
    cend = jnp.cumsum(nblk)
    cstart = cend - nblk
    experts = jnp.arange(N_EXPERTS, dtype=I32)
    row = jnp.sum(jnp.where(te[..., None] == experts, cstart * tm, 0), axis=-1) + tr
    j = jnp.arange(n_blocks, dtype=I32)
    block_expert = jnp.minimum(jnp.sum((cend[None, :] <= j[:, None]).astype(I32), axis=1), N_EXPERTS - 1)
    past = cend[-1] + experts
    zero_blocks = jnp.concatenate([jnp.where(nblk > 0, cend - 1, -1), jnp.where(past < n_blocks, past, -1)])
    return row, block_expert, cend[-1].reshape(1), zero_blocks


def _dispatch_kernel(zb_ref, row_ref, h_ref, x_hbm, buf0, buf1, sem, *, block_rows):
    step = pl.program_id(0)
    tm = h_ref.shape[0] // 8
    per_block = block_rows // tm

    def wait(buf, s):
        for k in range(TOP_K):
            pltpu.make_async_copy(buf, x_hbm.at[pl.ds(0, tm * 8), :], sem.at[s, k]).wait()

    @pl.when(step == 0)
    def _():
        buf1[...] = jnp.zeros(buf1.shape, F32)
        def piece(e, q):
            dst = pl.multiple_of((zb_ref[e] * per_block + q) * (tm * 8), tm * 8)
            return pltpu.make_async_copy(buf1, x_hbm.at[pl.ds(dst, tm * 8), :], sem.at[1, 0])

        for phase in ("start", "wait"):
            for e in range(zb_ref.shape[0]):
                @pl.when(zb_ref[e] >= 0)
                def _():
                    for q in range(per_block):
                        getattr(piece(e, q), phase)()

    def run(cur, prev, sc, sp):
        cur[...] = h_ref[...]
        for k in range(TOP_K):
            for r in range(tm):
                dst = pl.multiple_of(row_ref[0, 0, k * tm + r], 8)
                pltpu.make_async_copy(cur.at[pl.ds(r * 8, 8), :], x_hbm.at[pl.ds(dst, 8), :], sem.at[sc, k]).start()

        @pl.when(step > 0)
        def _():
            wait(prev, sp)

        @pl.when(step == pl.num_programs(0) - 1)
        def _():
            wait(cur, sc)

    @pl.when(step % 2 == 0)
    def _():
        run(buf0, buf1, 0, 1)

    @pl.when(step % 2 == 1)
    def _():
        run(buf1, buf0, 1, 0)


def _dispatch(h_tiles, rows_e, zero_blocks, n_blocks, block_rows):
    steps = rows_e.shape[0]
    tm = ROW_TILE
    grid_spec = pltpu.PrefetchScalarGridSpec(
        num_scalar_prefetch=1,
        grid=(steps,),
        in_specs=[
            pl.BlockSpec((1, 1, TOP_K * tm), lambda i, zb: (i, 0, 0), memory_space=pltpu.SMEM),
            pl.BlockSpec((tm * 8, LANES), lambda i, zb: (i, 0)),
        ],
        out_specs=pl.BlockSpec(memory_space=pl.ANY),
        scratch_shapes=[pltpu.VMEM((tm * 8, LANES), F32), pltpu.VMEM((tm * 8, LANES), F32),
                        pltpu.SemaphoreType.DMA((2, TOP_K))],
    )
    return pl.pallas_call(
        functools.partial(_dispatch_kernel, block_rows=block_rows),
        grid_spec=grid_spec,
        out_shape=jax.ShapeDtypeStruct((n_blocks * block_rows * 8, LANES), F32),
        compiler_params=_cp(("arbitrary",)),
        name="moe_dispatch",
    )(zero_blocks, rows_e * 8, h_tiles)


def _ffn_kernel(be_ref, nu_ref, x_ref, w1_ref, b1_ref, w2_ref, b2_ref, y_ref, w1b, w2b):
    i = pl.program_id(0)
    tm = x_ref.shape[0] // 8
    f = w2_ref.shape[1]

    @pl.when(i < nu_ref[0])
    def _():
        @pl.when((i == 0) | (be_ref[i] != be_ref[jnp.maximum(i - 1, 0)]))
        def _():
            w1b[...] = w1_ref[0].astype(BF16)
            w2b[...] = w2_ref[0].astype(BF16)

        a = jnp.dot(_load_row_tiles(x_ref, 0, tm).astype(BF16), w1b[...], preferred_element_type=F32) + b1_ref[0]
        glu = jnp.minimum(a[:, :f], SWIGLU_LIMIT)
        lin = jnp.clip(a[:, f:], -SWIGLU_LIMIT, SWIGLU_LIMIT)
        act = glu / (1.0 + jnp.exp(-SWIGLU_ALPHA * glu)) * (lin + 1.0)
        _store_row_tiles(y_ref, jnp.dot(act.astype(BF16), w2b[...], preferred_element_type=F32) + b2_ref[0])

    @pl.when(i >= nu_ref[0])
    def _():
        y_ref[...] = jnp.zeros(y_ref.shape, F32)


def _moe_ffn(x_buf, block_expert, n_used, w1, b1, w2, b2, layer, tm):
    n_blocks = block_expert.shape[0]
    _, n_exp, d, f2 = w1.shape
    f = w2.shape[2]
    grid_spec = pltpu.PrefetchScalarGridSpec(
        num_scalar_prefetch=2,
        grid=(n_blocks,),
        in_specs=[
            pl.BlockSpec((tm * 8, LANES), lambda i, be, nu: (i, 0)),
            pl.BlockSpec((None, 1, d, f2), lambda i, be, nu: (layer, be[i], 0, 0)),
            pl.BlockSpec((1, 1, f2), lambda i, be, nu: (be[i], 0, 0)),
            pl.BlockSpec((None, 1, f, d), lambda i, be, nu: (layer, be[i], 0, 0)),
            pl.BlockSpec((1, 1, d), lambda i, be, nu: (be[i], 0, 0)),
        ],
        out_specs=pl.BlockSpec((tm * 8, LANES), lambda i, be, nu: (i, 0)),
        scratch_shapes=[pltpu.VMEM((d, f2), BF16), pltpu.VMEM((f, d), BF16)],
    )
    return pl.pallas_call(
        _ffn_kernel,
        grid_spec=grid_spec,
        out_shape=jax.ShapeDtypeStruct((n_blocks * tm * 8, LANES), F32),
        compiler_params=_cp(("arbitrary",)),
        name="moe_ffn",
    )(block_expert, n_used, x_buf, w1, b1.reshape(n_exp, 1, f2), w2, b2.reshape(n_exp, 1, d))


def _combine_kernel(slot_ref, slotn_ref, y_hbm, x_ref, mod_ref, gate_ref, nf_ref, o_ref, ybuf0, ybuf1, sem, *, final):
    b = pl.program_id(0)
    i = pl.program_id(1)
    step = b * pl.num_programs(1) + i
    n_steps = pl.num_programs(0) * pl.num_programs(1)
    tc = x_ref.shape[1]
    n_rows = TOP_K * tc

    def issue(idx_ref, buf, s):
        for j in range(n_rows):
            src = pl.multiple_of(idx_ref[0, 0, j], 8)
            pltpu.make_async_copy(y_hbm.at[pl.ds(src, 8), :], buf.at[pl.ds(j * 8, 8), :], sem.at[s]).start()

    def wait(buf, s):
        pltpu.make_async_copy(y_hbm.at[pl.ds(0, n_rows * 8), :], buf, sem.at[s]).wait()

    @pl.when(step == 0)
    def _():
        issue(slot_ref, ybuf0, 0)

    def run(cur, nxt, sc, sn):
        wait(cur, sc)
        issue(slotn_ref, nxt, sn)
        gates = gate_ref[0]
        acc = gates[:, 0:1] * _load_row_tiles(cur, 0, tc)
        for k in range(1, TOP_K):
            acc += gates[:, k:k + 1] * _load_row_tiles(cur, k * tc, tc)
        out = x_ref[0] + mod_ref[0, 0, 5:6, :] * acc
        if final:
            out = _rms(out, nf_ref[...])
        o_ref[0] = out

        @pl.when(step == n_steps - 1)
        def _():
            wait(nxt, sn)

    @pl.when(step % 2 == 0)
    def _():
        run(ybuf0, ybuf1, 0, 1)

    @pl.when(step % 2 == 1)
    def _():
        run(ybuf1, ybuf0, 1, 0)


def _combine(y_sorted, slot, x_mid, mod, gates, norm_final, nt, ntile, final):
    bsz, _, d = x_mid.shape
    tc = ROW_TILE
    n_steps = bsz * ntile
    slots = slot * 8
    tok = lambda w: pl.BlockSpec((1, tc, w), lambda b, i: (b, i, 0))
    return pl.pallas_call(
        functools.partial(_combine_kernel, final=final),
        grid=(bsz, ntile),
        in_specs=[
            pl.BlockSpec((1, 1, tc * TOP_K), lambda b, i: (b * ntile + i, 0, 0), memory_space=pltpu.SMEM),
            pl.BlockSpec((1, 1, tc * TOP_K), lambda b, i: (jnp.minimum(b * ntile + i + 1, n_steps - 1), 0, 0),
                         memory_space=pltpu.SMEM),
            pl.BlockSpec(memory_space=pl.ANY),
            tok(d),
            pl.BlockSpec((1, 1, 6, d), lambda b, i: (b, i // nt, 0, 0)),
            tok(ROUTER_LANES),
            pl.BlockSpec((1, d), lambda b, i: (0, 0)),
        ],
        out_specs=tok(d),
        out_shape=jax.ShapeDtypeStruct((bsz, ntile * tc, d), F32),
        scratch_shapes=[pltpu.VMEM((TOP_K * tc * 8, LANES), F32), pltpu.VMEM((TOP_K * tc * 8, LANES), F32),
                        pltpu.SemaphoreType.DMA((2,))],
        compiler_params=_cp(("arbitrary", "arbitrary")),
        name="moe_combine",
    )(slots, slots, y_sorted, x_mid, mod, gates, norm_final.reshape(1, d))


def _split_dot(a, w):
    ah = a.astype(BF16)
    al = (a - ah.astype(F32)).astype(BF16)
    wh = w.astype(BF16)
    wl = (w - wh.astype(F32)).astype(BF16)
    return (jnp.dot(ah, wh, preferred_element_type=F32) + jnp.dot(al, wh, preferred_element_type=F32)
            + jnp.dot(ah, wl, preferred_element_type=F32))


def _filt_kernel(z_ref, w1_ref, b1_ref, w2_ref, b2_ref, w3_ref, b3_ref, fr_ref, wo_ref, dec_ref, o_ref):
    fr = fr_ref[...]
    h = jnp.sin(fr * (_split_dot(z_ref[...], w1_ref[...]) + b1_ref[...]))
    h = jnp.sin(fr * (_split_dot(h, w2_ref[...]) + b2_ref[...]))
    h = jnp.sin(fr * (_split_dot(h, w3_ref[...]) + b3_ref[...]))
    o_ref[...] = _split_dot(h, wo_ref[...]) * dec_ref[...]


def _hyena_filters_pl(n, filt):
    w1, b1, w2, b2, w3, b3, freq, w_out = filt
    fw = w2.shape[0]
    t = jnp.linspace(0.0, 1.0, n, dtype=F32)[:, None]
    wpos = (2.0 * math.pi / n) * jnp.arange(n, dtype=F32)[:, None]
    bands = jnp.linspace(1e-4, HY_BANDS - 1, HY_BANDS, dtype=F32)
    z = jnp.concatenate([t, jnp.cos(wpos * bands), -jnp.sin(wpos * bands), jnp.zeros((n, fw - HY_EMB), F32)], axis=-1)
    w1p = jnp.concatenate([w1, jnp.zeros((fw - HY_EMB, fw), F32)], axis=0)
    max_decay = math.log(HY_TARGET) / HY_FAST_DECAY
    min_decay = math.log(HY_TARGET) / HY_SLOW_DECAY
    deltas = jnp.linspace(min_decay, max_decay, HY_WIDTH, dtype=F32)
    decay = jnp.tile(jnp.exp(-t * jnp.abs(deltas)), (1, 2))
    tm = min(n, 512)
    row = lambda w: pl.BlockSpec((tm, w), lambda i: (i, 0))
    const = lambda s: pl.BlockSpec(s, lambda i: (0, 0))
    return pl.pallas_call(
        _filt_kernel,
        grid=(n // tm,),
        in_specs=[row(fw), const((fw, fw)), const((1, fw)), const((fw, fw)), const((1, fw)), const((fw, fw)), const((1, fw)),
                  const((1, fw)), const((fw, 2 * HY_WIDTH)), row(2 * HY_WIDTH)],
        out_specs=row(2 * HY_WIDTH),
        out_shape=jax.ShapeDtypeStruct((n, 2 * HY_WIDTH), F32),
        compiler_params=_cp(("arbitrary",)),
        name="hyena_filters",
    )(z, w1p, b1.reshape(1, fw), w2, b2.reshape(1, fw), w3, b3.reshape(1, fw), freq.reshape(1, fw), w_out, decay)


def _gate_kernel(u_ref, prev_ref, next_ref, w_ref, b_ref, z_ref, x0_ref, *, nt):
    i = pl.program_id(1)
    u = u_ref[0]
    tm = u.shape[0]
    w = w_ref[...]
    has_prev = jnp.where((i != 0) & (i != nt), 1.0, 0.0)
    has_next = jnp.where((i != nt - 1) & (i != nt), 1.0, 0.0)
    row = lax.broadcasted_iota(I32, u.shape, 0)
    up = jnp.where(row == 0, prev_ref[0, 7:8, :] * has_prev, pltpu.roll(u, 1, 0))
    un = jnp.where(row == tm - 1, next_ref[0, 0:1, :] * has_next, pltpu.roll(u, tm - 1, 0))
    uc = up * w[0:1, :] + u * w[1:2, :] + un * w[2:3, :] + b_ref[...]
    x0_ref[0] = uc[:, :HY_WIDTH]
    z_ref[0] = uc[:, 2 * HY_WIDTH:] * uc[:, HY_WIDTH:2 * HY_WIDTH]


def _hyena_gate(hy, conv_w, conv_b, nt, ntile):
    bsz, rows, w3 = hy.shape
    tm = ROW_TILE
    sub = tm // 8
    last8 = rows // 8 - 1
    out = pl.BlockSpec((1, tm, HY_WIDTH), lambda b, i: (b, i, 0))
    return pl.pallas_call(
        functools.partial(_gate_kernel, nt=nt),
        grid=(bsz, ntile),
        in_specs=[
            pl.BlockSpec((1, tm, w3), lambda b, i: (b, i, 0)),
            pl.BlockSpec((1, 8, w3), lambda b, i: (b, jnp.maximum(i * sub - 1, 0), 0)),
            pl.BlockSpec((1, 8, w3), lambda b, i: (b, jnp.minimum((i + 1) * sub, last8), 0)),
            pl.BlockSpec((HY_SHORT, w3), lambda b, i: (0, 0)),
            pl.BlockSpec((1, w3), lambda b, i: (0, 0)),
        ],
        out_specs=[out, out],
        out_shape=[jax.ShapeDtypeStruct((bsz, ntile * tm, HY_WIDTH), F32)] * 2,
        compiler_params=_cp(("arbitrary", "arbitrary")),
        name="hyena_gate",
    )(hy, hy, hy, conv_w, conv_b.reshape(1, w3))


FFT_N2 = LANES


def _dft_tables(n_fft):
    n1 = n_fft // FFT_N2
    k1 = jnp.arange(n1, dtype=I32)
    a = (2.0 * math.pi / n1) * ((k1[:, None] * k1[None, :]) % n1).astype(F32)
    f1 = jnp.stack([jnp.cos(a), -jnp.sin(a)], axis=1).reshape(2 * n1, n1)
    g = (jnp.stack([jnp.cos(a), -jnp.sin(a)], axis=2).reshape(n1, 2 * n1) / n_fft)
    k2 = jnp.arange(FFT_N2, dtype=I32)
    m = (k2[None, None, :] * (k1[:, None, None] + n1 * k2[None, :, None])) % n_fft
    th = (2.0 * math.pi / n_fft) * m.astype(F32)
    mre, mim = jnp.cos(th), -jnp.sin(th)
    mr = jnp.concatenate([jnp.concatenate([mre, -mim], axis=2), jnp.concatenate([mim, mre], axis=2)], axis=1)
    return f1.astype(BF16), g.astype(BF16), mr.astype(BF16), jnp.swapaxes(mr, 1, 2).astype(BF16)


def _fft1_kernel(x_ref, f_ref, o_ref, *, n1_in):
    f = f_ref[...]
    n_rows = f.shape[0]

    def body(n2, carry):
        xs = x_ref[pl.ds(n2, n1_in, stride=FFT_N2), :].astype(BF16)
        o_ref[pl.ds(n2, n_rows, stride=FFT_N2), :] = jnp.dot(f, xs, preferred_element_type=F32)
        return carry
    lax.fori_loop(0, FFT_N2, body, 0, unroll=8)


def _fft1(x, f1, n1_in):
    bsz, _, ch = x.shape
    n_rows = f1.shape[0]
    return pl.pallas_call(
        functools.partial(_fft1_kernel, n1_in=n1_in),
        grid=(bsz, ch // LANES),
        in_specs=[pl.BlockSpec((None, n1_in * FFT_N2, LANES), lambda b, h: (b, 0, h)),
                  pl.BlockSpec((n_rows, n1_in), lambda b, h: (0, 0))],
        out_specs=pl.BlockSpec((None, n_rows * FFT_N2, LANES), lambda b, h: (b, 0, h)),
        out_shape=jax.ShapeDtypeStruct((bsz, n_rows * FFT_N2, ch), F32),
        compiler_params=_cp(("arbitrary", "arbitrary")),
        name="hyena_fft1",
    )(x, f1[:, :n1_in])


def _spec_kernel(a_ref, mr_ref, o_ref):
    o_ref[0] = jnp.dot(mr_ref[0], a_ref[0].astype(BF16), preferred_element_type=F32)


def _fft_spectrum(a, mr):
    _, rows, ch = a.shape
    blk = 2 * FFT_N2
    return pl.pallas_call(
        _spec_kernel,
        grid=(rows // blk,),
        in_specs=[pl.BlockSpec((1, blk, ch), lambda k: (0, k, 0)), pl.BlockSpec((1, blk, blk), lambda k: (k, 0, 0))],
        out_specs=pl.BlockSpec((1, blk, ch), lambda k: (0, k, 0)),
        out_shape=jax.ShapeDtypeStruct((1, rows, ch), F32),
        compiler_params=_cp(("arbitrary",)),
        name="hyena_filter_spectrum",
    )(a, mr)


def _mid_kernel(a_ref, mr_ref, mrt_ref, ks_ref, o_ref):
    kre, kim = ks_ref[0, :FFT_N2, :], ks_ref[0, FFT_N2:, :]
    for b in range(a_ref.shape[0]):
        x = jnp.dot(mr_ref[0], a_ref[b].astype(BF16), preferred_element_type=F32)
        xre, xim = x[:FFT_N2], x[FFT_N2:]
        y = jnp.concatenate([xre * kre - xim * kim, xre * kim + xim * kre], axis=0)
        o_ref[b] = jnp.dot(mrt_ref[0], y.astype(BF16), preferred_element_type=F32)


def _fft_mid(a, mr, mrt, kspec):
    bsz, rows, ch = a.shape
    blk = 2 * FFT_N2
    return pl.pallas_call(
        _mid_kernel,
        grid=(rows // blk,),
        in_specs=[pl.BlockSpec((bsz, blk, ch), lambda k: (0, k, 0)),
                  pl.BlockSpec((1, blk, blk), lambda k: (k, 0, 0)),
                  pl.BlockSpec((1, blk, blk), lambda k: (k, 0, 0)),
                  pl.BlockSpec((1, blk, ch), lambda k: (0, k, 0))],
        out_specs=pl.BlockSpec((bsz, blk, ch), lambda k: (0, k, 0)),
        out_shape=jax.ShapeDtypeStruct((bsz, rows, ch), F32),
        compiler_params=_cp(("arbitrary",)),
        name="hyena_fft_mid",
    )(a, mr, mrt, kspec)


def _ifft1_kernel(b_ref, g_ref, z_ref, x0_ref, skip_ref, o_ref, *, n1_out):
    g = g_ref[...]
    n_rows = g.shape[1]
    skip = skip_ref[...]

    def body(n2, carry):
        bs = b_ref[pl.ds(n2, n_rows, stride=FFT_N2), :].astype(BF16)
        y = jnp.dot(g, bs, preferred_element_type=F32)
        rows = pl.ds(n2, n1_out, stride=FFT_N2)
        zs = z_ref[rows, :]
        o_ref[rows, :] = x0_ref[rows, :] * (y + zs * skip)
        return carry
    lax.fori_loop(0, FFT_N2, body, 0, unroll=8)


def _ifft1(bmat, g, z, x0, skip, n1_out):
    bsz, rows, ch = bmat.shape
    n = n1_out * FFT_N2
    seq = pl.BlockSpec((None, n, LANES), lambda b, h: (b, 0, h))
    return pl.pallas_call(
        functools.partial(_ifft1_kernel, n1_out=n1_out),
        grid=(bsz, ch // LANES),
        in_specs=[pl.BlockSpec((None, rows, LANES), lambda b, h: (b, 0, h)),
                  pl.BlockSpec((n1_out, g.shape[1]), lambda b, h: (0, 0)),
                  seq, seq, pl.BlockSpec((1, LANES), lambda b, h: (0, h))],
        out_specs=seq,
        out_shape=jax.ShapeDtypeStruct((bsz, n, ch), F32),
        compiler_params=_cp(("arbitrary", "arbitrary")),
        name="hyena_ifft1",
    )(bmat, g[:n1_out], z, x0, skip.reshape(1, ch))


def _hyena_fftconv(z, x0, k2, skip):
    n = k2.shape[0]
    n_fft = 2 * n
    f1, g, mr, mrt = _dft_tables(n_fft)
    k_full = jnp.concatenate([k2[:, :HY_WIDTH], k2[::-1, HY_WIDTH:]], axis=0)[None]
    kspec = _fft_spectrum(_fft1(k_full, f1, n_fft // FFT_N2), mr)
    a = _fft1(z, f1, n // FFT_N2)
    bmat = _fft_mid(a, mr, mrt, kspec)
    return _ifft1(bmat, g, z, x0, skip, n // FFT_N2)


def _ctxconv_kernel(z_ref, x0_ref, kf_ref, fd_ref, gd_ref, skip_ref, o_ref, ks_ref):
    n = z_ref.shape[1]
    half = fd_ref.shape[0] // 2

    @pl.when(pl.program_id(0) == 0)
    def _():
        ks_ref[...] = jnp.dot(fd_ref[...], kf_ref[...].astype(BF16), preferred_element_type=F32)

    z = z_ref[0]
    x = jnp.dot(fd_ref[:, :n], z.astype(BF16), preferred_element_type=F32)
    xre, xim = x[:half], x[half:]
    kre, kim = ks_ref[:half, :], ks_ref[half:, :]
    y = jnp.concatenate([xre * kre - xim * kim, xre * kim + xim * kre], axis=0)
    conv = jnp.dot(gd_ref[...], y.astype(BF16), preferred_element_type=F32)
    o_ref[0] = x0_ref[0] * (conv + z * skip_ref[...])


def _hyena_ctxconv(z, x0, k2, skip, row0):
    n = k2.shape[0]
    bsz, _, ch = z.shape
    n_fft = 2 * n
    kk = jnp.arange(n_fft, dtype=I32)
    th = (2.0 * math.pi / n_fft) * ((kk[:, None] * kk[None, :]) % n_fft).astype(F32)
    fd = jnp.concatenate([jnp.cos(th), -jnp.sin(th)], axis=0).astype(BF16)
    gd = (jnp.concatenate([jnp.cos(th[:n]), -jnp.sin(th[:n])], axis=1) / n_fft).astype(BF16)
    k_full = jnp.concatenate([k2[:, :HY_WIDTH], k2[::-1, HY_WIDTH:]], axis=0)
    blk = row0 // n
    seq = pl.BlockSpec((1, n, ch), lambda b: (b, blk, 0))
    const = lambda s: pl.BlockSpec(s, lambda b: (0, 0))
    return pl.pallas_call(
        _ctxconv_kernel,
        grid=(bsz,),
        in_specs=[seq, seq, const((n_fft, ch)), const((2 * n_fft, n_fft)), const((n, 2 * n_fft)), const((1, ch))],
        out_specs=pl.BlockSpec((1, n, ch), lambda b: (b, 0, 0)),
        out_shape=jax.ShapeDtypeStruct((bsz, n, ch), F32),
        scratch_shapes=[pltpu.VMEM((2 * n_fft, ch), F32)],
        compiler_params=_cp(("arbitrary",)),
        name="hyena_ctxconv",
    )(z, x0, k_full, fd, gd, skip.reshape(1, ch))


def _rope_tables(n_lat, rows):
    nrow = n_lat // GRID_W
    row = jnp.repeat(jnp.arange(nrow, dtype=F32), GRID_W)
    col = jnp.tile(jnp.arange(GRID_W, dtype=F32), nrow)
    n_freq = HEAD_DIM // 4
    inv_freq = ROPE_THETA ** (-jnp.arange(n_freq, dtype=F32) / n_freq)
    ar = row[:, None] * inv_freq
    ac = col[:, None] * inv_freq
    cos64 = jnp.concatenate([jnp.cos(ar), jnp.cos(ar), jnp.cos(ac), jnp.cos(ac)], axis=-1)
    sin64 = jnp.concatenate([-jnp.sin(ar), jnp.sin(ar), -jnp.sin(ac), jnp.sin(ac)], axis=-1)
    pad = rows - n_lat
    cos_t = jnp.concatenate([jnp.tile(cos64, (1, LANES // HEAD_DIM)), jnp.ones((pad, LANES), F32)], axis=0)
    sin_t = jnp.concatenate([jnp.tile(sin64, (1, LANES // HEAD_DIM)), jnp.zeros((pad, LANES), F32)], axis=0)
    return cos_t, sin_t


def kernel(x, c, ctx, c_ctx, w_ada, b_ada, norm_mix, norm_ffn, w_in, b_in, hy_conv_w, hy_conv_b, hy_filt_w1, hy_filt_b1, hy_filt_w2, hy_filt_b2, hy_filt_w3, hy_filt_b3, hy_filt_freq, hy_filt_out, hy_skip, ga_q_norm, ga_k_norm, wa_sink, branch_norm, w_out, b_out, w_router, b_router, w_mlp1, b_mlp1, w_mlp2, b_mlp2, norm_final):
    bsz, n_lat, d = x.shape
    n_ctx = ctx.shape[1]
    depth = w_ada.shape[0]
    assert n_ctx == ROW_TILE and n_lat % KV_CHUNK == 0 and bsz < 16 and d == D_MODEL
    nt = n_lat // ROW_TILE
    rows = n_lat + n_ctx
    cos_t, sin_t = _rope_tables(n_lat, rows)

    c_all = jnp.zeros((16, d), F32).at[:bsz].set(c).at[bsz].set(c_ctx)
    mod_all = _ada_mod(c_all, w_ada, b_ada)
    x_all = jnp.concatenate([x, ctx], axis=1)

    for l in range(depth):
        last = l == depth - 1
        ntile = nt if last else nt + 1
        m = mod_all[l].reshape(16, 6, d)
        mod = jnp.stack([m[:bsz], jnp.broadcast_to(m[bsz], (bsz, 6, d))], axis=1)

        hy, gq, gk, gv, wq, wk, wv = _inproj(x_all, mod, norm_mix[l], w_in[l], b_in[l], cos_t, sin_t,
                                             ga_q_norm[l], ga_k_norm[l], nt)
        filt = (hy_filt_w1[l], hy_filt_b1[l], hy_filt_w2[l], hy_filt_b2[l], hy_filt_w3[l], hy_filt_b3[l],
                hy_filt_freq[l], hy_filt_out[l])
        z, x0 = _hyena_gate(hy, hy_conv_w[l], hy_conv_b[l], nt, ntile)
        y_hy = _hyena_fftconv(z, x0, _hyena_filters_pl(n_lat, filt), hy_skip[l])
        if not last:
            yc_hy = _hyena_ctxconv(z, x0, _hyena_filters_pl(n_ctx, filt), hy_skip[l], n_lat)
            y_hy = jnp.concatenate([y_hy, yc_hy], axis=1)
        y_ga = _global_attention(gq, gk, gv, nt, ntile)
        y_wa = _window_attention(wq, wk, wv, wa_sink[l], nt, ntile)
        x_mid, h2, top_g, top_e, top_r, counts = _merge(x_all, y_hy, y_ga, y_wa, mod, branch_norm[l], w_out[l], b_out[l],
                                                        norm_ffn[l], w_router[l], b_router[l], nt, ntile)
        rows_e, block_expert, n_used, zero_blocks = _moe_plan(top_e, top_r, counts, MOE_TILE)
        x_buf = _dispatch(h2.reshape(-1, LANES), rows_e, zero_blocks, block_expert.shape[0], MOE_TILE)
        y_sorted = _moe_ffn(x_buf, block_expert, n_used, w_mlp1, b_mlp1[l], w_mlp2, b_mlp2[l], l, MOE_TILE)
        x_all = _combine(y_sorted, rows_e, x_mid, mod, top_g, norm_final, nt, ntile, last)
    return x_all
```

```python
import functools
import math

import jax
import jax.numpy as jnp
from jax import lax
from jax.experimental import pallas as pl
from jax.experimental.pallas import tpu as pltpu

F32 = jnp.float32
BF16 = jnp.bfloat16
I32 = jnp.int32

D_MODEL = 1024
HEAD_DIM = 64
GRID_W = 64
HY_WIDTH = 256
GA_HEADS = 6
GA_KV_HEADS = 2
WA_HEADS = 6
WA_KV_HEADS = 2
GROUP = GA_HEADS // GA_KV_HEADS
Q_WIDTH = GA_HEADS * HEAD_DIM
KV_WIDTH = GA_KV_HEADS * HEAD_DIM
MIX_WIDTH = HY_WIDTH + 2 * Q_WIDTH
IN_WIDTH = 3 * HY_WIDTH + 2 * (Q_WIDTH + 2 * KV_WIDTH)
WINDOW = 128
ROPE_THETA = 10000.0
ATTN_SCALE = HEAD_DIM ** -0.5
HY_SHORT = 3
HY_EMB = 33
HY_BANDS = (HY_EMB - 1) // 2
HY_FAST_DECAY = 0.3
HY_SLOW_DECAY = 1.5
HY_TARGET = 1e-2
N_EXPERTS = 32
TOP_K = 4
SWIGLU_ALPHA = 1.702
SWIGLU_LIMIT = 7.0
EPS = 1e-6
NEG_INF = -1e30

LANES = 128
VMEM_LIMIT_V7X = 56 * 1024 * 1024

ROW_TILE = 256
KV_CHUNK = 2048
MOE_TILE = 512
ROUTER_LANES = LANES


def _cp(sem, vmem=VMEM_LIMIT_V7X):
    return pltpu.CompilerParams(dimension_semantics=sem, vmem_limit_bytes=vmem)


def _rms(v, w):
    return v * lax.rsqrt(jnp.mean(v * v, axis=-1, keepdims=True) + EPS) * w


def _ada_kernel(c_ref, w_ref, b_ref, o_ref):
    c = c_ref[...]
    s = c / (1.0 + jnp.exp(-c))
    hi = s.astype(BF16)
    lo = (s - hi.astype(F32)).astype(BF16)
    w = w_ref[0]
    whi = w.astype(BF16)
    wlo = (w - whi.astype(F32)).astype(BF16)
    acc = jnp.dot(hi, whi, preferred_element_type=F32)
    acc += jnp.dot(lo, whi, preferred_element_type=F32)
    acc += jnp.dot(hi, wlo, preferred_element_type=F32)
    o_ref[0] = acc + b_ref[0]


def _ada_mod(c_all, w_ada, b_ada):
    n_layers, d, n6 = w_ada.shape
    tn = n6 // 4
    return pl.pallas_call(
        _ada_kernel,
        grid=(n_layers, n6 // tn),
        in_specs=[
            pl.BlockSpec((16, d), lambda l, j: (0, 0)),
            pl.BlockSpec((1, d, tn), lambda l, j: (l, 0, j)),
            pl.BlockSpec((1, 1, tn), lambda l, j: (l, 0, j)),
        ],
        out_specs=pl.BlockSpec((1, 16, tn), lambda l, j: (l, 0, j)),
        out_shape=jax.ShapeDtypeStruct((n_layers, 16, n6), F32),
        compiler_params=_cp(("arbitrary", "arbitrary")),
        name="ada_mod",
    )(c_all, w_ada, b_ada.reshape(n_layers, 1, n6))


def _inproj_kernel(x_ref, mod_ref, nw_ref, w_ref, b_ref, cos_ref, sin_ref, qn_ref, kn_ref, g_ref,
                   hy_ref, gq_ref, gk_ref, gv_ref, wq_ref, wk_ref, wv_ref):
    x = x_ref[0]
    tm = x.shape[0]
    shift = mod_ref[0, 0, 0:1, :]
    scale = mod_ref[0, 0, 1:2, :]
    h = _rms(x, nw_ref[...]) * (1.0 + scale) + shift
    p = jnp.dot(h.astype(BF16), w_ref[...], preferred_element_type=F32) + b_ref[...]
    hy_ref[0] = p[:, :3 * HY_WIDTH]

    cos = cos_ref[...]
    sin = sin_ref[...]
    lane = lax.broadcasted_iota(I32, (tm, LANES), 1)
    first = (lane % (HEAD_DIM // 2)) < (HEAD_DIM // 4)
    gmat = g_ref[...]

    def rope(v):
        sw = jnp.where(first, pltpu.roll(v, LANES - HEAD_DIM // 4, 1), pltpu.roll(v, HEAD_DIM // 4, 1))
        return v * cos + sw * sin

    def head_norm(v, w):
        t = v * v
        hi = t.astype(BF16)
        lo = (t - hi.astype(F32)).astype(BF16)
        ms = jnp.dot(hi, gmat, preferred_element_type=F32) + jnp.dot(lo, gmat, preferred_element_type=F32)
        return v * lax.rsqrt(ms + EPS) * w

    o = 3 * HY_WIDTH
    for u in range(Q_WIDTH // LANES):
        v = p[:, o + u * LANES:o + (u + 1) * LANES]
        v = rope(head_norm(v, qn_ref[...]))
        gq_ref[0, :, u * LANES:(u + 1) * LANES] = (v * ATTN_SCALE).astype(BF16)
    o += Q_WIDTH
    gk_ref[0] = rope(head_norm(p[:, o:o + KV_WIDTH], kn_ref[...])).astype(BF16)
    o += KV_WIDTH
    gv_ref[0] = p[:, o:o + KV_WIDTH].astype(BF16)
    o += KV_WIDTH
    for u in range(Q_WIDTH // LANES):
        v = rope(p[:, o + u * LANES:o + (u + 1) * LANES])
        wq_ref[0, :, u * LANES:(u + 1) * LANES] = (v * ATTN_SCALE).astype(BF16)
    o += Q_WIDTH
    wk_ref[0] = rope(p[:, o:o + KV_WIDTH]).astype(BF16)
    o += KV_WIDTH
    wv_ref[0] = p[:, o:o + KV_WIDTH].astype(BF16)


def _inproj(x_all, mod, norm_w, w_in, b_in, cos_t, sin_t, q_norm, k_norm, nt):
    bsz, rows, d = x_all.shape
    tm = ROW_TILE
    ntile = rows // tm
    gmat = jnp.kron(jnp.eye(LANES // HEAD_DIM, dtype=F32), jnp.full((HEAD_DIM, HEAD_DIM), 1.0 / HEAD_DIM, F32)).astype(BF16)
    qn = jnp.tile(q_norm, LANES // HEAD_DIM).reshape(1, LANES)
    kn = jnp.tile(k_norm, LANES // HEAD_DIM).reshape(1, LANES)
    tok = lambda w: pl.BlockSpec((1, tm, w), lambda b, i: (b, i, 0))
    const = lambda s: pl.BlockSpec(s, lambda b, i: (0,) * len(s))
    out_w = (3 * HY_WIDTH, Q_WIDTH, KV_WIDTH, KV_WIDTH, Q_WIDTH, KV_WIDTH, KV_WIDTH)
    out_dt = (F32, BF16, BF16, BF16, BF16, BF16, BF16)
    return pl.pallas_call(
        _inproj_kernel,
        grid=(bsz, ntile),
        in_specs=[
            tok(d),
            pl.BlockSpec((1, 1, 6, d), lambda b, i: (b, i // nt, 0, 0)),
            const((1, d)),
            const((d, IN_WIDTH)),
            const((1, IN_WIDTH)),
            pl.BlockSpec((tm, LANES), lambda b, i: (i, 0)),
            pl.BlockSpec((tm, LANES), lambda b, i: (i, 0)),
            const((1, LANES)),
            const((1, LANES)),
            const((LANES, LANES)),
        ],
        out_specs=[tok(w) for w in out_w],
        out_shape=[jax.ShapeDtypeStruct((bsz, rows, w), dt) for w, dt in zip(out_w, out_dt)],
        compiler_params=_cp(("arbitrary", "arbitrary")),
        name="inproj",
    )(x_all, mod, norm_w.reshape(1, d), w_in.astype(BF16), b_in.reshape(1, IN_WIDTH), cos_t, sin_t, qn, kn, gmat)


def _stack_q(q_ref, qs_ref):
    qf = q_ref[0].astype(F32)
    tq = qf.shape[0]
    for j in range(GA_KV_HEADS):
        for g in range(GROUP):
            h = GROUP * j + g
            blk = qf[:, (h // 2) * LANES:(h // 2 + 1) * LANES]
            if h % 2 != j:
                blk = pltpu.roll(blk, HEAD_DIM, 1)
            qs_ref[j, g * tq:(g + 1) * tq, :] = blk.astype(BF16)


def _unstack_o(norm_fn, o_ref, tq):
    lane = lax.broadcasted_iota(I32, (tq, LANES), 1)
    heads = []
    for j in range(GA_KV_HEADS):
        for g in range(GROUP):
            h = GROUP * j + g
            o = norm_fn(j, g)
            if h % 2 != j:
                o = pltpu.roll(o, HEAD_DIM, 1)
            heads.append(o)
    for u in range(Q_WIDTH // LANES):
        o_ref[0, :, u * LANES:(u + 1) * LANES] = jnp.where(lane < HEAD_DIM, heads[2 * u], heads[2 * u + 1])


def _half_masks(rows):
    lane = lax.broadcasted_iota(I32, (rows, LANES), 1)
    return [lane < HEAD_DIM, lane >= HEAD_DIM]


def _dot_t(a, b):
    return lax.dot_general(a, b, (((1,), (1,)), ((), ())), preferred_element_type=F32)


def _ga_kernel(q_ref, k_ref, v_ref, o_ref, qs_ref, m_ref, acc_ref, *, nt, n_lat):
    i = pl.program_id(1)
    tq = q_ref.shape[1]
    _stack_q(q_ref, qs_ref)
    m_ref[...] = jnp.full(m_ref.shape, NEG_INF, F32)
    acc_ref[...] = jnp.zeros(acc_ref.shape, F32)

    def chunk(kc, vc):
        masks = _half_masks(kc.shape[0])
        for j in range(GA_KV_HEADS):
            km = jnp.where(masks[j], kc, jnp.zeros_like(kc))
            vm = jnp.where(masks[j], vc, jnp.ones_like(vc))
            s = _dot_t(qs_ref[j], km)
            m_old = m_ref[j]
            m_new = jnp.maximum(m_old, jnp.max(s, axis=-1, keepdims=True))
            p = jnp.exp((s - jnp.tile(m_new, (1, s.shape[1] // LANES))).astype(BF16))
            acc_ref[j] = jnp.exp(m_old - m_new) * acc_ref[j] + jnp.dot(p, vm, preferred_element_type=F32)
            m_ref[j] = m_new

    @pl.when(i < nt)
    def _():
        rows = k_ref.shape[1]
        for st in range(0, n_lat, KV_CHUNK):
            en = rows if st + KV_CHUNK >= n_lat else st + KV_CHUNK
            chunk(k_ref[0, st:en, :], v_ref[0, st:en, :])

    @pl.when(i >= nt)
    def _():
        chunk(k_ref[0, n_lat:, :], v_ref[0, n_lat:, :])

    def norm_fn(j, g):
        a = acc_ref[j, g * tq:(g + 1) * tq, :]
        return a / pltpu.roll(a, HEAD_DIM, 1)

    _unstack_o(norm_fn, o_ref, tq)


def _global_attention(gq, gk, gv, nt, ntile):
    bsz, rows, _ = gq.shape
    tq = ROW_TILE
    n_lat = nt * tq
    return pl.pallas_call(
        functools.partial(_ga_kernel, nt=nt, n_lat=n_lat),
        grid=(bsz, ntile),
        in_specs=[
            pl.BlockSpec((1, tq, Q_WIDTH), lambda b, i: (b, i, 0)),
            pl.BlockSpec((1, rows, KV_WIDTH), lambda b, i: (b, 0, 0)),
            pl.BlockSpec((1, rows, KV_WIDTH), lambda b, i: (b, 0, 0)),
        ],
        out_specs=pl.BlockSpec((1, tq, Q_WIDTH), lambda b, i: (b, i, 0)),
        out_shape=jax.ShapeDtypeStruct((bsz, ntile * tq, Q_WIDTH), F32),
        scratch_shapes=[
            pltpu.VMEM((GA_KV_HEADS, GROUP * tq, LANES), BF16),
            pltpu.VMEM((GA_KV_HEADS, GROUP * tq, LANES), F32),
            pltpu.VMEM((GA_KV_HEADS, GROUP * tq, LANES), F32),
        ],
        compiler_params=_cp(("arbitrary", "arbitrary")),
        name="global_attn",
    )(gq, gk, gv)


def _wa_kernel(q_ref, k_ref, v_ref, sink_ref, bias_ref, o_ref, qs_ref, acc_ref, l_ref, *, nt, n_lat):
    i = pl.program_id(1)
    tq = q_ref.shape[1]
    band = tq + 2 * WINDOW
    _stack_q(q_ref, qs_ref)
    st = pl.multiple_of(i * tq, tq)
    kb = k_ref[0, pl.ds(st, band), :]
    vb = v_ref[0, pl.ds(st, band), :]
    kc = k_ref[0, WINDOW + n_lat:WINDOW + n_lat + ROW_TILE, :]
    vc = v_ref[0, WINDOW + n_lat:WINDOW + n_lat + ROW_TILE, :]

    kpos = lax.broadcasted_iota(I32, (1, band), 1) + (i * tq - WINDOW)
    in_seq = (kpos >= 0) & (kpos < n_lat) & (i < nt)
    bias = bias_ref[...]

    mb = _half_masks(band)
    mc = _half_masks(ROW_TILE)
    for j in range(WA_KV_HEADS):
        q = qs_ref[j]
        s_loc = jnp.where(in_seq, _dot_t(q, jnp.where(mb[j], kb, jnp.zeros_like(kb))) + bias, NEG_INF)
        s_ctx = _dot_t(q, jnp.where(mc[j], kc, jnp.zeros_like(kc)))
        sink = jnp.concatenate(
            [jnp.broadcast_to(sink_ref[GROUP * j + g:GROUP * j + g + 1, :], (tq, LANES)) for g in range(GROUP)], axis=0)
        m = jnp.maximum(jnp.maximum(jnp.max(s_loc, axis=-1, keepdims=True), jnp.max(s_ctx, axis=-1, keepdims=True)), sink)
        p_loc = jnp.exp(s_loc - jnp.tile(m, (1, band // LANES))).astype(BF16)
        p_ctx = jnp.exp(s_ctx - jnp.tile(m, (1, ROW_TILE // LANES))).astype(BF16)
        acc_ref[j] = (jnp.dot(p_loc, jnp.where(mb[j], vb, jnp.ones_like(vb)), preferred_element_type=F32)
                      + jnp.dot(p_ctx, jnp.where(mc[j], vc, jnp.ones_like(vc)), preferred_element_type=F32))
        l_ref[j] = jnp.exp(sink - m)

    def norm_fn(j, g):
        a = acc_ref[j, g * tq:(g + 1) * tq, :]
        return a / (pltpu.roll(a, HEAD_DIM, 1) + l_ref[j, g * tq:(g + 1) * tq, :])

    _unstack_o(norm_fn, o_ref, tq)


def _window_attention(wq, wk, wv, sink, nt, ntile):
    bsz, rows, _ = wq.shape
    tq = ROW_TILE
    n_lat = nt * tq
    pad = ((0, 0), (WINDOW, WINDOW), (0, 0))
    kp = jnp.pad(wk, pad)
    vp = jnp.pad(wv, pad)
    sink_b = jnp.zeros((8, LANES), F32).at[:WA_HEADS].set(jnp.broadcast_to(sink[:, None], (WA_HEADS, LANES)))
    band = tq + 2 * WINDOW
    rel = jnp.arange(band, dtype=I32)[None, :] - WINDOW - (jnp.arange(GROUP * tq, dtype=I32) % tq)[:, None]
    band_bias = jnp.where(jnp.abs(rel) <= WINDOW, 0.0, NEG_INF).astype(F32)
    return pl.pallas_call(
        functools.partial(_wa_kernel, nt=nt, n_lat=n_lat),
        grid=(bsz, ntile),
        in_specs=[
            pl.BlockSpec((1, tq, Q_WIDTH), lambda b, i: (b, i, 0)),
            pl.BlockSpec((1, rows + 2 * WINDOW, KV_WIDTH), lambda b, i: (b, 0, 0)),
            pl.BlockSpec((1, rows + 2 * WINDOW, KV_WIDTH), lambda b, i: (b, 0, 0)),
            pl.BlockSpec((8, LANES), lambda b, i: (0, 0)),
            pl.BlockSpec((GROUP * tq, band), lambda b, i: (0, 0)),
        ],
        out_specs=pl.BlockSpec((1, tq, Q_WIDTH), lambda b, i: (b, i, 0)),
        out_shape=jax.ShapeDtypeStruct((bsz, ntile * tq, Q_WIDTH), F32),
        scratch_shapes=[
            pltpu.VMEM((WA_KV_HEADS, GROUP * tq, LANES), BF16),
            pltpu.VMEM((WA_KV_HEADS, GROUP * tq, LANES), F32),
            pltpu.VMEM((WA_KV_HEADS, GROUP * tq, LANES), F32),
        ],
        compiler_params=_cp(("arbitrary", "arbitrary")),
        name="window_attn",
    )(wq, kp, vp, sink_b, band_bias)


def _store_row_tiles(ref, v):
    rows = v.shape[0]
    for s in range(v.shape[1] // LANES):
        ref[pl.ds(s, rows, stride=8), :] = v[:, s * LANES:(s + 1) * LANES]


def _load_row_tiles(ref, row0, rows):
    return jnp.concatenate([ref[pl.ds(row0 * 8 + s, rows, stride=8), :] for s in range(8)], axis=1)


def _merge_kernel(x_ref, hy_ref, ga_ref, wa_ref, mod_ref, bn_ref, wo_ref, bo_ref, nf_ref, wrh_ref, wrl_ref, br_ref, tri_ref,
                  xo_ref, h2_ref, tg_ref, te_ref, tr_ref, cnt_ref, base_ref):
    @pl.when((pl.program_id(0) == 0) & (pl.program_id(1) == 0))
    def _():
        base_ref[...] = jnp.zeros(base_ref.shape, F32)

    bn = bn_ref[...]
    g0 = HY_WIDTH
    g1 = HY_WIDTH + Q_WIDTH
    y = jnp.concatenate([_rms(hy_ref[0], bn[:, :g0]), _rms(ga_ref[0], bn[:, g0:g1]), _rms(wa_ref[0], bn[:, g1:])], axis=-1)
    o = jnp.dot(y.astype(BF16), wo_ref[...], preferred_element_type=F32) + bo_ref[...]
    x1 = x_ref[0] + mod_ref[0, 0, 2:3, :] * o
    xo_ref[0] = x1
    h2 = _rms(x1, nf_ref[...]) * (1.0 + mod_ref[0, 0, 4:5, :]) + mod_ref[0, 0, 3:4, :]
    _store_row_tiles(h2_ref, h2)

    hi = h2.astype(BF16)
    lo = (h2 - hi.astype(F32)).astype(BF16)
    logits = (jnp.dot(hi, wrh_ref[...], preferred_element_type=F32) + jnp.dot(lo, wrh_ref[...], preferred_element_type=F32)
              + jnp.dot(hi, wrl_ref[...], preferred_element_type=F32)) + br_ref[...]
    tm = logits.shape[0]
    lane = lax.broadcasted_iota(I32, (tm, ROUTER_LANES), 1)
    te = jnp.zeros((tm, ROUTER_LANES), F32)
    tr = jnp.zeros((tm, ROUTER_LANES), F32)
    tv = jnp.zeros((tm, ROUTER_LANES), F32)
    before = base_ref[...]
    tri = tri_ref[...]
    v0 = None
    for k in range(TOP_K):
        mx = jnp.max(logits, axis=-1, keepdims=True)
        idx = jnp.min(jnp.where(logits == mx, lane, ROUTER_LANES), axis=-1, keepdims=True)
        if k == 0:
            v0 = mx
        hit = lane == idx
        onehot = jnp.where(hit, 1.0, 0.0)
        prefix = jnp.dot(tri, onehot.astype(BF16), preferred_element_type=F32) + before
        rank = jnp.sum(onehot * prefix, axis=-1, keepdims=True)
        before = before + jnp.sum(onehot, axis=0, keepdims=True)
        te = jnp.where(lane == k, idx.astype(F32), te)
        tr = jnp.where(lane == k, rank, tr)
        tv = jnp.where(lane == k, jnp.exp(mx - v0), tv)
        logits = jnp.where(hit, NEG_INF, logits)
    base_ref[...] = before
    cnt_ref[...] = before
    tg_ref[0] = tv / jnp.sum(tv, axis=-1, keepdims=True)
    te_t = te.T
    tr_t = tr.T
    for k in range(TOP_K):
        te_ref[0, :, k * tm:(k + 1) * tm] = te_t[k:k + 1, :].astype(I32)
        tr_ref[0, :, k * tm:(k + 1) * tm] = tr_t[k:k + 1, :].astype(I32)


def _merge(x_all, y_hy, y_ga, y_wa, mod, branch_norm, w_out, b_out, norm_ffn, w_router, b_router, nt, ntile):
    bsz, _, d = x_all.shape
    tm = ROW_TILE
    rows = ntile * tm
    wr = jnp.zeros((d, ROUTER_LANES), F32).at[:, :N_EXPERTS].set(w_router)
    wrh = wr.astype(BF16)
    wrl = (wr - wrh.astype(F32)).astype(BF16)
    br = jnp.full((1, ROUTER_LANES), NEG_INF, F32).at[0, :N_EXPERTS].set(b_router)
    tri = (jnp.arange(tm, dtype=I32)[:, None] > jnp.arange(tm, dtype=I32)[None, :]).astype(BF16)
    tok = lambda w: pl.BlockSpec((1, tm, w), lambda b, i: (b, i, 0))
    const = lambda s: pl.BlockSpec(s, lambda b, i: (0,) * len(s))
    flat = pl.BlockSpec((1, 1, TOP_K * tm), lambda b, i: (b * ntile + i, 0, 0))
    return pl.pallas_call(
        _merge_kernel,
        grid=(bsz, ntile),
        in_specs=[
            tok(d), tok(HY_WIDTH), tok(Q_WIDTH), tok(Q_WIDTH),
            pl.BlockSpec((1, 1, 6, d), lambda b, i: (b, i // nt, 0, 0)),
            const((1, MIX_WIDTH)), const((MIX_WIDTH, d)), const((1, d)), const((1, d)),
            const((d, ROUTER_LANES)), const((d, ROUTER_LANES)), const((1, ROUTER_LANES)), const((tm, tm)),
        ],
        out_specs=[tok(d), pl.BlockSpec((None, tm * 8, LANES), lambda b, i: (b, i, 0)), tok(ROUTER_LANES), flat, flat,
                   const((1, ROUTER_LANES))],
        out_shape=[
            jax.ShapeDtypeStruct((bsz, rows, d), F32),
            jax.ShapeDtypeStruct((bsz, rows * 8, LANES), F32),
            jax.ShapeDtypeStruct((bsz, rows, ROUTER_LANES), F32),
            jax.ShapeDtypeStruct((bsz * ntile, 1, TOP_K * tm), I32),
            jax.ShapeDtypeStruct((bsz * ntile, 1, TOP_K * tm), I32),
            jax.ShapeDtypeStruct((1, ROUTER_LANES), F32),
        ],
        scratch_shapes=[pltpu.VMEM((1, ROUTER_LANES), F32)],
        compiler_params=_cp(("arbitrary", "arbitrary")),
        name="merge_router",
    )(x_all, y_hy, y_ga, y_wa, mod, branch_norm.reshape(1, MIX_WIDTH), w_out.astype(BF16), b_out.reshape(1, d),
      norm_ffn.reshape(1, d), wrh, wrl, br, tri)


def _moe_plan(te, tr, cnt, tm):
    steps, _, per = te.shape
    n_blocks = -(-(steps * per) // tm) + N_EXPERTS
    counts = cnt[0, :N_EXPERTS].astype(I32)
    nblk = (counts + tm - 1) // tm
    cend = jnp.cumsum(nblk)
    cstart = cend - nblk
    experts = jnp.arange(N_EXPERTS, dtype=I32)
    row = jnp.sum(jnp.where(te[..., None] == experts, cstart * tm, 0), axis=-1) + tr
    j = jnp.arange(n_blocks, dtype=I32)
    block_expert = jnp.minimum(jnp.sum((cend[None, :] <= j[:, None]).astype(I32), axis=1), N_EXPERTS - 1)
    past = cend[-1] + experts
    zero_blocks = jnp.concatenate([jnp.where(nblk > 0, cend - 1, -1), jnp.where(past < n_blocks, past, -1)])
    return row, block_expert, cend[-1].reshape(1), zero_blocks


def _dispatch_kernel(zb_ref, row_ref, h_ref, x_hbm, buf0, buf1, sem, *, block_rows):
    step = pl.program_id(0)
    tm = h_ref.shape[0] // 8
    per_block = block_rows // tm

    def wait(buf, s):
        for k in range(TOP_K):
            pltpu.make_async_copy(buf, x_hbm.at[pl.ds(0, tm * 8), :], sem.at[s, k]).wait()

    @pl.when(step == 0)
    def _():
        buf1[...] = jnp.zeros(buf1.shape, F32)
        def piece(e, q):
            dst = pl.multiple_of((zb_ref[e] * per_block + q) * (tm * 8), tm * 8)
            return pltpu.make_async_copy(buf1, x_hbm.at[pl.ds(dst, tm * 8), :], sem.at[1, 0])

        for phase in ("start", "wait"):
            for e in range(zb_ref.shape[0]):
                @pl.when(zb_ref[e] >= 0)
                def _():
                    for q in range(per_block):
                        getattr(piece(e, q), phase)()

    def run(cur, prev, sc, sp):
        cur[...] = h_ref[...]
        for k in range(TOP_K):
            for r in range(tm):
                dst = pl.multiple_of(row_ref[0, 0, k * tm + r], 8)
                pltpu.make_async_copy(cur.at[pl.ds(r * 8, 8), :], x_hbm.at[pl.ds(dst, 8), :], sem.at[sc, k]).start()

        @pl.when(step > 0)
        def _():
            wait(prev, sp)

        @pl.when(step == pl.num_programs(0) - 1)
        def _():
            wait(cur, sc)

    @pl.when(step % 2 == 0)
    def _():
        run(buf0, buf1, 0, 1)

    @pl.when(step % 2 == 1)
    def _():
        run(buf1, buf0, 1, 0)


def _dispatch(h_tiles, rows_e, zero_blocks, n_blocks, block_rows):
    steps = rows_e.shape[0]
    tm = ROW_TILE
    grid_spec = pltpu.PrefetchScalarGridSpec(
        num_scalar_prefetch=1,
        grid=(steps,),
        in_specs=[
            pl.BlockSpec((1, 1, TOP_K * tm), lambda i, zb: (i, 0, 0), memory_space=pltpu.SMEM),
            pl.BlockSpec((tm * 8, LANES), lambda i, zb: (i, 0)),
        ],
        out_specs=pl.BlockSpec(memory_space=pl.ANY),
        scratch_shapes=[pltpu.VMEM((tm * 8, LANES), F32), pltpu.VMEM((tm * 8, LANES), F32),
                        pltpu.SemaphoreType.DMA((2, TOP_K))],
    )
    return pl.pallas_call(
        functools.partial(_dispatch_kernel, block_rows=block_rows),
        grid_spec=grid_spec,
        out_shape=jax.ShapeDtypeStruct((n_blocks * block_rows * 8, LANES), F32),
        compiler_params=_cp(("arbitrary",)),
        name="moe_dispatch",
    )(zero_blocks, rows_e * 8, h_tiles)


def _ffn_kernel(be_ref, nu_ref, x_ref, w1_ref, b1_ref, w2_ref, b2_ref, y_ref, w1b, w2b):
    i = pl.program_id(0)
    tm = x_ref.shape[0] // 8
    f = w2_ref.shape[1]

    @pl.when(i < nu_ref[0])
    def _():
        @pl.when((i == 0) | (be_ref[i] != be_ref[jnp.maximum(i - 1, 0)]))
        def _():
            w1b[...] = w1_ref[0].astype(BF16)
            w2b[...] = w2_ref[0].astype(BF16)

        a = jnp.dot(_load_row_tiles(x_ref, 0, tm).astype(BF16), w1b[...], preferred_element_type=F32) + b1_ref[0]
        glu = jnp.minimum(a[:, :f], SWIGLU_LIMIT)
        lin = jnp.clip(a[:, f:], -SWIGLU_LIMIT, SWIGLU_LIMIT)
        act = glu / (1.0 + jnp.exp(-SWIGLU_ALPHA * glu)) * (lin + 1.0)
        _store_row_tiles(y_ref, jnp.dot(act.astype(BF16), w2b[...], preferred_element_type=F32) + b2_ref[0])

    @pl.when(i >= nu_ref[0])
    def _():
        y_ref[...] = jnp.zeros(y_ref.shape, F32)


def _moe_ffn(x_buf, block_expert, n_used, w1, b1, w2, b2, layer, tm):
    n_blocks = block_expert.shape[0]
    _, n_exp, d, f2 = w1.shape
    f = w2.shape[2]
    grid_spec = pltpu.PrefetchScalarGridSpec(
        num_scalar_prefetch=2,
        grid=(n_blocks,),
        in_specs=[
            pl.BlockSpec((tm * 8, LANES), lambda i, be, nu: (i, 0)),
            pl.BlockSpec((None, 1, d, f2), lambda i, be, nu: (layer, be[i], 0, 0)),
            pl.BlockSpec((1, 1, f2), lambda i, be, nu: (be[i], 0, 0)),
            pl.BlockSpec((None, 1, f, d), lambda i, be, nu: (layer, be[i], 0, 0)),
            pl.BlockSpec((1, 1, d), lambda i, be, nu: (be[i], 0, 0)),
        ],
        out_specs=pl.BlockSpec((tm * 8, LANES), lambda i, be, nu: (i, 0)),
        scratch_shapes=[pltpu.VMEM((d, f2), BF16), pltpu.VMEM((f, d), BF16)],
    )
    return pl.pallas_call(
        _ffn_kernel,
        grid_spec=grid_spec,
        out_shape=jax.ShapeDtypeStruct((n_blocks * tm * 8, LANES), F32),
        compiler_params=_cp(("arbitrary",)),
        name="moe_ffn",
    )(block_expert, n_used, x_buf, w1, b1.reshape(n_exp, 1, f2), w2, b2.reshape(n_exp, 1, d))


def _combine_kernel(slot_ref, slotn_ref, y_hbm, x_ref, mod_ref, gate_ref, nf_ref, o_ref, ybuf0, ybuf1, sem, *, final):
    b = pl.program_id(0)
    i = pl.program_id(1)
    step = b * pl.num_programs(1) + i
    n_steps = pl.num_programs(0) * pl.num_programs(1)
    tc = x_ref.shape[1]
    n_rows = TOP_K * tc

    def issue(idx_ref, buf, s):
        for j in range(n_rows):
            src = pl.multiple_of(idx_ref[0, 0, j], 8)
            pltpu.make_async_copy(y_hbm.at[pl.ds(src, 8), :], buf.at[pl.ds(j * 8, 8), :], sem.at[s]).start()

    def wait(buf, s):
        pltpu.make_async_copy(y_hbm.at[pl.ds(0, n_rows * 8), :], buf, sem.at[s]).wait()

    @pl.when(step == 0)
    def _():
        issue(slot_ref, ybuf0, 0)

    def run(cur, nxt, sc, sn):
        wait(cur, sc)
        issue(slotn_ref, nxt, sn)
        gates = gate_ref[0]
        acc = gates[:, 0:1] * _load_row_tiles(cur, 0, tc)
        for k in range(1, TOP_K):
            acc += gates[:, k:k + 1] * _load_row_tiles(cur, k * tc, tc)
        out = x_ref[0] + mod_ref[0, 0, 5:6, :] * acc
        if final:
            out = _rms(out, nf_ref[...])
        o_ref[0] = out

        @pl.when(step == n_steps - 1)
        def _():
            wait(nxt, sn)

    @pl.when(step % 2 == 0)
    def _():
        run(ybuf0, ybuf1, 0, 1)

    @pl.when(step % 2 == 1)
    def _():
        run(ybuf1, ybuf0, 1, 0)


def _combine(y_sorted, slot, x_mid, mod, gates, norm_final, nt, ntile, final):
    bsz, _, d = x_mid.shape
    tc = ROW_TILE
    n_steps = bsz * ntile
    slots = slot * 8
    tok = lambda w: pl.BlockSpec((1, tc, w), lambda b, i: (b, i, 0))
    return pl.pallas_call(
        functools.partial(_combine_kernel, final=final),
        grid=(bsz, ntile),
        in_specs=[
            pl.BlockSpec((1, 1, tc * TOP_K), lambda b, i: (b * ntile + i, 0, 0), memory_space=pltpu.SMEM),
            pl.BlockSpec((1, 1, tc * TOP_K), lambda b, i: (jnp.minimum(b * ntile + i + 1, n_steps - 1), 0, 0),
                         memory_space=pltpu.SMEM),
            pl.BlockSpec(memory_space=pl.ANY),
            tok(d),
            pl.BlockSpec((1, 1, 6, d), lambda b, i: (b, i // nt, 0, 0)),
            tok(ROUTER_LANES),
            pl.BlockSpec((1, d), lambda b, i: (0, 0)),
        ],
        out_specs=tok(d),
        out_shape=jax.ShapeDtypeStruct((bsz, ntile * tc, d), F32),
        scratch_shapes=[pltpu.VMEM((TOP_K * tc * 8, LANES), F32), pltpu.VMEM((TOP_K * tc * 8, LANES), F32),
                        pltpu.SemaphoreType.DMA((2,))],
        compiler_params=_cp(("arbitrary", "arbitrary")),
        name="moe_combine",
    )(slots, slots, y_sorted, x_mid, mod, gates, norm_final.reshape(1, d))


def _split_dot(a, w):
    ah = a.astype(BF16)
    al = (a - ah.astype(F32)).astype(BF16)
    wh = w.astype(BF16)
    wl = (w - wh.astype(F32)).astype(BF16)
    return (jnp.dot(ah, wh, preferred_element_type=F32) + jnp.dot(al, wh, preferred_element_type=F32)
            + jnp.dot(ah, wl, preferred_element_type=F32))


def _filt_kernel(z_ref, w1_ref, b1_ref, w2_ref, b2_ref, w3_ref, b3_ref, fr_ref, wo_ref, dec_ref, o_ref):
    fr = fr_ref[...]
    h = jnp.sin(fr * (_split_dot(z_ref[...], w1_ref[...]) + b1_ref[...]))
    h = jnp.sin(fr * (_split_dot(h, w2_ref[...]) + b2_ref[...]))
    h = jnp.sin(fr * (_split_dot(h, w3_ref[...]) + b3_ref[...]))
    o_ref[...] = _split_dot(h, wo_ref[...]) * dec_ref[...]


def _hyena_filters_pl(n, filt):
    w1, b1, w2, b2, w3, b3, freq, w_out = filt
    fw = w2.shape[0]
    t = jnp.linspace(0.0, 1.0, n, dtype=F32)[:, None]
    wpos = (2.0 * math.pi / n) * jnp.arange(n, dtype=F32)[:, None]
    bands = jnp.linspace(1e-4, HY_BANDS - 1, HY_BANDS, dtype=F32)
    z = jnp.concatenate([t, jnp.cos(wpos * bands), -jnp.sin(wpos * bands), jnp.zeros((n, fw - HY_EMB), F32)], axis=-1)
    w1p = jnp.concatenate([w1, jnp.zeros((fw - HY_EMB, fw), F32)], axis=0)
    max_decay = math.log(HY_TARGET) / HY_FAST_DECAY
    min_decay = math.log(HY_TARGET) / HY_SLOW_DECAY
    deltas = jnp.linspace(min_decay, max_decay, HY_WIDTH, dtype=F32)
    decay = jnp.tile(jnp.exp(-t * jnp.abs(deltas)), (1, 2))
    tm = min(n, 512)
    row = lambda w: pl.BlockSpec((tm, w), lambda i: (i, 0))
    const = lambda s: pl.BlockSpec(s, lambda i: (0, 0))
    return pl.pallas_call(
        _filt_kernel,
        grid=(n // tm,),
        in_specs=[row(fw), const((fw, fw)), const((1, fw)), const((fw, fw)), const((1, fw)), const((fw, fw)), const((1, fw)),
                  const((1, fw)), const((fw, 2 * HY_WIDTH)), row(2 * HY_WIDTH)],
        out_specs=row(2 * HY_WIDTH),
        out_shape=jax.ShapeDtypeStruct((n, 2 * HY_WIDTH), F32),
        compiler_params=_cp(("arbitrary",)),
        name="hyena_filters",
    )(z, w1p, b1.reshape(1, fw), w2, b2.reshape(1, fw), w3, b3.reshape(1, fw), freq.reshape(1, fw), w_out, decay)


def _gate_kernel(u_ref, prev_ref, next_ref, w_ref, b_ref, z_ref, x0_ref, *, nt):
    i = pl.program_id(1)
    u = u_ref[0]
    tm = u.shape[0]
    w = w_ref[...]
    has_prev = jnp.where((i != 0) & (i != nt), 1.0, 0.0)
    has_next = jnp.where((i != nt - 1) & (i != nt), 1.0, 0.0)
    row = lax.broadcasted_iota(I32, u.shape, 0)
    up = jnp.where(row == 0, prev_ref[0, 7:8, :] * has_prev, pltpu.roll(u, 1, 0))
    un = jnp.where(row == tm - 1, next_ref[0, 0:1, :] * has_next, pltpu.roll(u, tm - 1, 0))
    uc = up * w[0:1, :] + u * w[1:2, :] + un * w[2:3, :] + b_ref[...]
    x0_ref[0] = uc[:, :HY_WIDTH]
    z_ref[0] = uc[:, 2 * HY_WIDTH:] * uc[:, HY_WIDTH:2 * HY_WIDTH]


def _hyena_gate(hy, conv_w, conv_b, nt, ntile):
    bsz, rows, w3 = hy.shape
    tm = ROW_TILE
    sub = tm // 8
    last8 = rows // 8 - 1
    out = pl.BlockSpec((1, tm, HY_WIDTH), lambda b, i: (b, i, 0))
    return pl.pallas_call(
        functools.partial(_gate_kernel, nt=nt),
        grid=(bsz, ntile),
        in_specs=[
            pl.BlockSpec((1, tm, w3), lambda b, i: (b, i, 0)),
            pl.BlockSpec((1, 8, w3), lambda b, i: (b, jnp.maximum(i * sub - 1, 0), 0)),
            pl.BlockSpec((1, 8, w3), lambda b, i: (b, jnp.minimum((i + 1) * sub, last8), 0)),
            pl.BlockSpec((HY_SHORT, w3), lambda b, i: (0, 0)),
            pl.BlockSpec((1, w3), lambda b, i: (0, 0)),
        ],
        out_specs=[out, out],
        out_shape=[jax.ShapeDtypeStruct((bsz, ntile * tm, HY_WIDTH), F32)] * 2,
        compiler_params=_cp(("arbitrary", "arbitrary")),
        name="hyena_gate",
    )(hy, hy, hy, conv_w, conv_b.reshape(1, w3))


FFT_N2 = LANES


def _dft_tables(n_fft):
    n1 = n_fft // FFT_N2
    k1 = jnp.arange(n1, dtype=I32)
    a = (2.0 * math.pi / n1) * ((k1[:, None] * k1[None, :]) % n1).astype(F32)
    f1 = jnp.stack([jnp.cos(a), -jnp.sin(a)], axis=1).reshape(2 * n1, n1)
    g = (jnp.stack([jnp.cos(a), -jnp.sin(a)], axis=2).reshape(n1, 2 * n1) / n_fft)
    k2 = jnp.arange(FFT_N2, dtype=I32)
    m = (k2[None, None, :] * (k1[:, None, None] + n1 * k2[None, :, None])) % n_fft
    th = (2.0 * math.pi / n_fft) * m.astype(F32)
    mre, mim = jnp.cos(th), -jnp.sin(th)
    mr = jnp.concatenate([jnp.concatenate([mre, -mim], axis=2), jnp.concatenate([mim, mre], axis=2)], axis=1)
    return f1.astype(BF16), g.astype(BF16), mr.astype(BF16), jnp.swapaxes(mr, 1, 2).astype(BF16)


def _fft1_kernel(x_ref, f_ref, o_ref, *, n1_in):
    f = f_ref[...]
    n_rows = f.shape[0]

    def body(n2, carry):
        xs = x_ref[pl.ds(n2, n1_in, stride=FFT_N2), :].astype(BF16)
        o_ref[pl.ds(n2, n_rows, stride=FFT_N2), :] = jnp.dot(f, xs, preferred_element_type=F32)
        return carry
    lax.fori_loop(0, FFT_N2, body, 0, unroll=8)


def _fft1(x, f1, n1_in):
    bsz, _, ch = x.shape
    n_rows = f1.shape[0]
    return pl.pallas_call(
        functools.partial(_fft1_kernel, n1_in=n1_in),
        grid=(bsz, ch // LANES),
        in_specs=[pl.BlockSpec((None, n1_in * FFT_N2, LANES), lambda b, h: (b, 0, h)),
                  pl.BlockSpec((n_rows, n1_in), lambda b, h: (0, 0))],
        out_specs=pl.BlockSpec((None, n_rows * FFT_N2, LANES), lambda b, h: (b, 0, h)),
        out_shape=jax.ShapeDtypeStruct((bsz, n_rows * FFT_N2, ch), F32),
        compiler_params=_cp(("arbitrary", "arbitrary")),
        name="hyena_fft1",
    )(x, f1[:, :n1_in])


def _spec_kernel(a_ref, mr_ref, o_ref):
    o_ref[0] = jnp.dot(mr_ref[0], a_ref[0].astype(BF16), preferred_element_type=F32)


def _fft_spectrum(a, mr):
    _, rows, ch = a.shape
    blk = 2 * FFT_N2
    return pl.pallas_call(
        _spec_kernel,
        grid=(rows // blk,),
        in_specs=[pl.BlockSpec((1, blk, ch), lambda k: (0, k, 0)), pl.BlockSpec((1, blk, blk), lambda k: (k, 0, 0))],
        out_specs=pl.BlockSpec((1, blk, ch), lambda k: (0, k, 0)),
        out_shape=jax.ShapeDtypeStruct((1, rows, ch), F32),
        compiler_params=_cp(("arbitrary",)),
        name="hyena_filter_spectrum",
    )(a, mr)


def _mid_kernel(a_ref, mr_ref, mrt_ref, ks_ref, o_ref):
    kre, kim = ks_ref[0, :FFT_N2, :], ks_ref[0, FFT_N2:, :]
    for b in range(a_ref.shape[0]):
        x = jnp.dot(mr_ref[0], a_ref[b].astype(BF16), preferred_element_type=F32)
        xre, xim = x[:FFT_N2], x[FFT_N2:]
        y = jnp.concatenate([xre * kre - xim * kim, xre * kim + xim * kre], axis=0)
        o_ref[b] = jnp.dot(mrt_ref[0], y.astype(BF16), preferred_element_type=F32)


def _fft_mid(a, mr, mrt, kspec):
    bsz, rows, ch = a.shape
    blk = 2 * FFT_N2
    return pl.pallas_call(
        _mid_kernel,
        grid=(rows // blk,),
        in_specs=[pl.BlockSpec((bsz, blk, ch), lambda k: (0, k, 0)),
                  pl.BlockSpec((1, blk, blk), lambda k: (k, 0, 0)),
                  pl.BlockSpec((1, blk, blk), lambda k: (k, 0, 0)),
                  pl.BlockSpec((1, blk, ch), lambda k: (0, k, 0))],
        out_specs=pl.BlockSpec((bsz, blk, ch), lambda k: (0, k, 0)),
        out_shape=jax.ShapeDtypeStruct((bsz, rows, ch), F32),
        compiler_params=_cp(("arbitrary",)),
        name="hyena_fft_mid",
    )(a, mr, mrt, kspec)


def _ifft1_kernel(b_ref, g_ref, z_ref, x0_ref, skip_ref, o_ref, *, n1_out):
    g = g_ref[...]
    n_rows = g.shape[1]
    skip = skip_ref[...]

    def body(n2, carry):
        bs = b_ref[pl.ds(n2, n_rows, stride=FFT_N2), :].astype(BF16)
        y = jnp.dot(g, bs, preferred_element_type=F32)
        rows = pl.ds(n2, n1_out, stride=FFT_N2)
        zs = z_ref[rows, :]
        o_ref[rows, :] = x0_ref[rows, :] * (y + zs * skip)
        return carry
    lax.fori_loop(0, FFT_N2, body, 0, unroll=8)


def _ifft1(bmat, g, z, x0, skip, n1_out):
    bsz, rows, ch = bmat.shape
    n = n1_out * FFT_N2
    seq = pl.BlockSpec((None, n, LANES), lambda b, h: (b, 0, h))
    return pl.pallas_call(
        functools.partial(_ifft1_kernel, n1_out=n1_out),
        grid=(bsz, ch // LANES),
        in_specs=[pl.BlockSpec((None, rows, LANES), lambda b, h: (b, 0, h)),
                  pl.BlockSpec((n1_out, g.shape[1]), lambda b, h: (0, 0)),
                  seq, seq, pl.BlockSpec((1, LANES), lambda b, h: (0, h))],
        out_specs=seq,
        out_shape=jax.ShapeDtypeStruct((bsz, n, ch), F32),
        compiler_params=_cp(("arbitrary", "arbitrary")),
        name="hyena_ifft1",
    )(bmat, g[:n1_out], z, x0, skip.reshape(1, ch))


def _hyena_fftconv(z, x0, k2, skip):
    n = k2.shape[0]
    n_fft = 2 * n
    f1, g, mr, mrt = _dft_tables(n_fft)
    k_full = jnp.concatenate([k2[:, :HY_WIDTH], k2[::-1, HY_WIDTH:]], axis=0)[None]
    kspec = _fft_spectrum(_fft1(k_full, f1, n_fft // FFT_N2), mr)
    a = _fft1(z, f1, n // FFT_N2)
    bmat = _fft_mid(a, mr, mrt, kspec)
    return _ifft1(bmat, g, z, x0, skip, n // FFT_N2)


def _ctxconv_kernel(z_ref, x0_ref, kf_ref, fd_ref, gd_ref, skip_ref, o_ref, ks_ref):
    n = z_ref.shape[1]
    half = fd_ref.shape[0] // 2

    @pl.when(pl.program_id(0) == 0)
    def _():
        ks_ref[...] = jnp.dot(fd_ref[...], kf_ref[...].astype(BF16), preferred_element_type=F32)

    z = z_ref[0]
    x = jnp.dot(fd_ref[:, :n], z.astype(BF16), preferred_element_type=F32)
    xre, xim = x[:half], x[half:]
    kre, kim = ks_ref[:half, :], ks_ref[half:, :]
    y = jnp.concatenate([xre * kre - xim * kim, xre * kim + xim * kre], axis=0)
    conv = jnp.dot(gd_ref[...], y.astype(BF16), preferred_element_type=F32)
    o_ref[0] = x0_ref[0] * (conv + z * skip_ref[...])


def _hyena_ctxconv(z, x0, k2, skip, row0):
    n = k2.shape[0]
    bsz, _, ch = z.shape
    n_fft = 2 * n
    kk = jnp.arange(n_fft, dtype=I32)
    th = (2.0 * math.pi / n_fft) * ((kk[:, None] * kk[None, :]) % n_fft).astype(F32)
    fd = jnp.concatenate([jnp.cos(th), -jnp.sin(th)], axis=0).astype(BF16)
    gd = (jnp.concatenate([jnp.cos(th[:n]), -jnp.sin(th[:n])], axis=1) / n_fft).astype(BF16)
    k_full = jnp.concatenate([k2[:, :HY_WIDTH], k2[::-1, HY_WIDTH:]], axis=0)
    blk = row0 // n
    seq = pl.BlockSpec((1, n, ch), lambda b: (b, blk, 0))
    const = lambda s: pl.BlockSpec(s, lambda b: (0, 0))
    return pl.pallas_call(
        _ctxconv_kernel,
        grid=(bsz,),
        in_specs=[seq, seq, const((n_fft, ch)), const((2 * n_fft, n_fft)), const((n, 2 * n_fft)), const((1, ch))],
        out_specs=pl.BlockSpec((1, n, ch), lambda b: (b, 0, 0)),
        out_shape=jax.ShapeDtypeStruct((bsz, n, ch), F32),
        scratch_shapes=[pltpu.VMEM((2 * n_fft, ch), F32)],
        compiler_params=_cp(("arbitrary",)),
        name="hyena_ctxconv",
    )(z, x0, k_full, fd, gd, skip.reshape(1, ch))


def _rope_tables(n_lat, rows):
    nrow = n_lat // GRID_W
    row = jnp.repeat(jnp.arange(nrow, dtype=F32), GRID_W)
    col = jnp.tile(jnp.arange(GRID_W, dtype=F32), nrow)
    n_freq = HEAD_DIM // 4
    inv_freq = ROPE_THETA ** (-jnp.arange(n_freq, dtype=F32) / n_freq)
    ar = row[:, None] * inv_freq
    ac = col[:, None] * inv_freq
    cos64 = jnp.concatenate([jnp.cos(ar), jnp.cos(ar), jnp.cos(ac), jnp.cos(ac)], axis=-1)
    sin64 = jnp.concatenate([-jnp.sin(ar), jnp.sin(ar), -jnp.sin(ac), jnp.sin(ac)], axis=-1)
    pad = rows - n_lat
    cos_t = jnp.concatenate([jnp.tile(cos64, (1, LANES // HEAD_DIM)), jnp.ones((pad, LANES), F32)], axis=0)
    sin_t = jnp.concatenate([jnp.tile(sin64, (1, LANES // HEAD_DIM)), jnp.zeros((pad, LANES), F32)], axis=0)
    return cos_t, sin_t


def kernel(x, c, ctx, c_ctx, w_ada, b_ada, norm_mix, norm_ffn, w_in, b_in, hy_conv_w, hy_conv_b, hy_filt_w1, hy_filt_b1, hy_filt_w2, hy_filt_b2, hy_filt_w3, hy_filt_b3, hy_filt_freq, hy_filt_out, hy_skip, ga_q_norm, ga_k_norm, wa_sink, branch_norm, w_out, b_out, w_router, b_router, w_mlp1, b_mlp1, w_mlp2, b_mlp2, norm_final):
    bsz, n_lat, d = x.shape
    n_ctx = ctx.shape[1]
    depth = w_ada.shape[0]
    assert n_ctx == ROW_TILE and n_lat % KV_CHUNK == 0 and bsz < 16 and d == D_MODEL
    nt = n_lat // ROW_TILE
    rows = n_lat + n_ctx
    cos_t, sin_t = _rope_tables(n_lat, rows)

    c_all = jnp.zeros((16, d), F32).at[:bsz].set(c).at[bsz].set(c_ctx)
    mod_all = _ada_mod(c_all, w_ada, b_ada)
    x_all = jnp.concatenate([x, ctx], axis=1)

    for l in range(depth):
        last = l == depth - 1
        ntile = nt if last else nt + 1
        m = mod_all[l].reshape(16, 6, d)
        mod = jnp.stack([m[:bsz], jnp.broadcast_to(m[bsz], (bsz, 6, d))], axis=1)

        hy, gq, gk, gv, wq, wk, wv = _inproj(x_all, mod, norm_mix[l], w_in[l], b_in[l], cos_t, sin_t,
                                             ga_q_norm[l], ga_k_norm[l], nt)
        filt = (hy_filt_w1[l], hy_filt_b1[l], hy_filt_w2[l], hy_filt_b2[l], hy_filt_w3[l], hy_filt_b3[l],
                hy_filt_freq[l], hy_filt_out[l])
        z, x0 = _hyena_gate(hy, hy_conv_w[l], hy_conv_b[l], nt, ntile)
        y_hy = _hyena_fftconv(z, x0, _hyena_filters_pl(n_lat, filt), hy_skip[l])
        if not last:
            yc_hy = _hyena_ctxconv(z, x0, _hyena_filters_pl(n_ctx, filt), hy_skip[l], n_lat)
            y_hy = jnp.concatenate([y_hy, yc_hy], axis=1)
        y_ga = _global_attention(gq, gk, gv, nt, ntile)
        y_wa = _window_attention(wq, wk, wv, wa_sink[l], nt, ntile)
        x_mid, h2, top_g, top_e, top_r, counts = _merge(x_all, y_hy, y_ga, y_wa, mod, branch_norm[l], w_out[l], b_out[l],
                                                        norm_ffn[l], w_router[l], b_router[l], nt, ntile)
        rows_e, block_expert, n_used, zero_blocks = _moe_plan(top_e, top_r, counts, MOE_TILE)
        x_buf = _dispatch(h2.reshape(-1, LANES), rows_e, zero_blocks, block_expert.shape[0], MOE_TILE)
        y_sorted = _moe_ffn(x_buf, block_expert, n_used, w_mlp1, b_mlp1[l], w_mlp2, b_mlp2[l], l, MOE_TILE)
        x_all = _combine(y_sorted, rows_e, x_mid, mod, top_g, norm_final, nt, ntile, last)
    return x_all
```

```python
import functools
import math

import jax
import jax.numpy as jnp
from jax import lax
from jax.experimental import pallas as pl
from jax.experimental.pallas import tpu as pltpu

F32 = jnp.float32
BF16 = jnp.bfloat16
I32 = jnp.int32

D_MODEL = 1024
HEAD_DIM = 64
GRID_W = 64
HY_WIDTH = 256
GA_HEADS = 6
GA_KV_HEADS = 2
WA_HEADS = 6
WA_KV_HEADS = 2
GROUP = GA_HEADS // GA_KV_HEADS
Q_WIDTH = GA_HEADS * HEAD_DIM
KV_WIDTH = GA_KV_HEADS * HEAD_DIM
MIX_WIDTH = HY_WIDTH + 2 * Q_WIDTH
IN_WIDTH = 3 * HY_WIDTH + 2 * (Q_WIDTH + 2 * KV_WIDTH)
WINDOW = 128
ROPE_THETA = 10000.0
ATTN_SCALE = HEAD_DIM ** -0.5
HY_SHORT = 3
HY_EMB = 33
HY_BANDS = (HY_EMB - 1) // 2
HY_FAST_DECAY = 0.3
HY_SLOW_DECAY = 1.5
HY_TARGET = 1e-2
N_EXPERTS = 32
TOP_K = 4
SWIGLU_ALPHA = 1.702
SWIGLU_LIMIT = 7.0
EPS = 1e-6
NEG_INF = -1e30

LANES = 128
VMEM_LIMIT_V7X = 56 * 1024 * 1024

ROW_TILE = 256
KV_CHUNK = 2048
MOE_TILE = 512
ROUTER_LANES = LANES


def _cp(sem, vmem=VMEM_LIMIT_V7X):
    return pltpu.CompilerParams(dimension_semantics=sem, vmem_limit_bytes=vmem)


def _rms(v, w):
    return v * lax.rsqrt(jnp.mean(v * v, axis=-1, keepdims=True) + EPS) * w


def _ada_kernel(c_ref, w_ref, b_ref, o_ref):
    c = c_ref[...]
    s = c / (1.0 + jnp.exp(-c))
    hi = s.astype(BF16)
    lo = (s - hi.astype(F32)).astype(BF16)
    w = w_ref[0]
    whi = w.astype(BF16)
    wlo = (w - whi.astype(F32)).astype(BF16)
    acc = jnp.dot(hi, whi, preferred_element_type=F32)
    acc += jnp.dot(lo, whi, preferred_element_type=F32)
    acc += jnp.dot(hi, wlo, preferred_element_type=F32)
    o_ref[0] = acc + b_ref[0]


def _ada_mod(c_all, w_ada, b_ada):
    n_layers, d, n6 = w_ada.shape
    tn = n6 // 4
    return pl.pallas_call(
        _ada_kernel,
        grid=(n_layers, n6 // tn),
        in_specs=[
            pl.BlockSpec((16, d), lambda l, j: (0, 0)),
            pl.BlockSpec((1, d, tn), lambda l, j: (l, 0, j)),
            pl.BlockSpec((1, 1, tn), lambda l, j: (l, 0, j)),
        ],
        out_specs=pl.BlockSpec((1, 16, tn), lambda l, j: (l, 0, j)),
        out_shape=jax.ShapeDtypeStruct((n_layers, 16, n6), F32),
        compiler_params=_cp(("arbitrary", "arbitrary")),
        name="ada_mod",
    )(c_all, w_ada, b_ada.reshape(n_layers, 1, n6))


def _inproj_kernel(x_ref, mod_ref, nw_ref, w_ref, b_ref, cos_ref, sin_ref, qn_ref, kn_ref, g_ref,
                   hy_ref, gq_ref, gk_ref, gv_ref, wq_ref, wk_ref, wv_ref):
    x = x_ref[0]
    tm = x.shape[0]
    shift = mod_ref[0, 0, 0:1, :]
    scale = mod_ref[0, 0, 1:2, :]
    h = _rms(x, nw_ref[...]) * (1.0 + scale) + shift
    p = jnp.dot(h.astype(BF16), w_ref[...], preferred_element_type=F32) + b_ref[...]
    hy_ref[0] = p[:, :3 * HY_WIDTH]

    cos = cos_ref[...]
    sin = sin_ref[...]
    lane = lax.broadcasted_iota(I32, (tm, LANES), 1)
    first = (lane % (HEAD_DIM // 2)) < (HEAD_DIM // 4)
    gmat = g_ref[...]

    def rope(v):
        sw = jnp.where(first, pltpu.roll(v, LANES - HEAD_DIM // 4, 1), pltpu.roll(v, HEAD_DIM // 4, 1))
        return v * cos + sw * sin

    def head_norm(v, w):
        t = v * v
        hi = t.astype(BF16)
        lo = (t - hi.astype(F32)).astype(BF16)
        ms = jnp.dot(hi, gmat, preferred_element_type=F32) + jnp.dot(lo, gmat, preferred_element_type=F32)
        return v * lax.rsqrt(ms + EPS) * w

    o = 3 * HY_WIDTH
    for u in range(Q_WIDTH // LANES):
        v = p[:, o + u * LANES:o + (u + 1) * LANES]
        v = rope(head_norm(v, qn_ref[...]))
        gq_ref[0, :, u * LANES:(u + 1) * LANES] = (v * ATTN_SCALE).astype(BF16)
    o += Q_WIDTH
    gk_ref[0] = rope(head_norm(p[:, o:o + KV_WIDTH], kn_ref[...])).astype(BF16)
    o += KV_WIDTH
    gv_ref[0] = p[:, o:o + KV_WIDTH].astype(BF16)
    o += KV_WIDTH
    for u in range(Q_WIDTH // LANES):
        v = rope(p[:, o + u * LANES:o + (u + 1) * LANES])
        wq_ref[0, :, u * LANES:(u + 1) * LANES] = (v * ATTN_SCALE).astype(BF16)
    o += Q_WIDTH
    wk_ref[0] = rope(p[:, o:o + KV_WIDTH]).astype(BF16)
    o += KV_WIDTH
    wv_ref[0] = p[:, o:o + KV_WIDTH].astype(BF16)


def _inproj(x_all, mod, norm_w, w_in, b_in, cos_t, sin_t, q_norm, k_norm, nt):
    bsz, rows, d = x_all.shape
    tm = ROW_TILE
    ntile = rows // tm
    gmat = jnp.kron(jnp.eye(LANES // HEAD_DIM, dtype=F32), jnp.full((HEAD_DIM, HEAD_DIM), 1.0 / HEAD_DIM, F32)).astype(BF16)
    qn = jnp.tile(q_norm, LANES // HEAD_DIM).reshape(1, LANES)
    kn = jnp.tile(k_norm, LANES // HEAD_DIM).reshape(1, LANES)
    tok = lambda w: pl.BlockSpec((1, tm, w), lambda b, i: (b, i, 0))
    const = lambda s: pl.BlockSpec(s, lambda b, i: (0,) * len(s))
    out_w = (3 * HY_WIDTH, Q_WIDTH, KV_WIDTH, KV_WIDTH, Q_WIDTH, KV_WIDTH, KV_WIDTH)
    out_dt = (F32, BF16, BF16, BF16, BF16, BF16, BF16)
    return pl.pallas_call(
        _inproj_kernel,
        grid=(bsz, ntile),
        in_specs=[
            tok(d),
            pl.BlockSpec((1, 1, 6, d), lambda b, i: (b, i // nt, 0, 0)),
            const((1, d)),
            const((d, IN_WIDTH)),
            const((1, IN_WIDTH)),
            pl.BlockSpec((tm, LANES), lambda b, i: (i, 0)),
            pl.BlockSpec((tm, LANES), lambda b, i: (i, 0)),
            const((1, LANES)),
            const((1, LANES)),
            const((LANES, LANES)),
        ],
        out_specs=[tok(w) for w in out_w],
        out_shape=[jax.ShapeDtypeStruct((bsz, rows, w), dt) for w, dt in zip(out_w, out_dt)],
        compiler_params=_cp(("arbitrary", "arbitrary")),
        name="inproj",
    )(x_all, mod, norm_w.reshape(1, d), w_in.astype(BF16), b_in.reshape(1, IN_WIDTH), cos_t, sin_t, qn, kn, gmat)


def _stack_q(q_ref, qs_ref):
    qf = q_ref[0].astype(F32)
    tq = qf.shape[0]
    for j in range(GA_KV_HEADS):
        for g in range(GROUP):
            h = GROUP * j + g
            blk = qf[:, (h // 2) * LANES:(h // 2 + 1) * LANES]
            if h % 2 != j:
                blk = pltpu.roll(blk, HEAD_DIM, 1)
            qs_ref[j, g * tq:(g + 1) * tq, :] = blk.astype(BF16)


def _unstack_o(norm_fn, o_ref, tq):
    lane = lax.broadcasted_iota(I32, (tq, LANES), 1)
    heads = []
    for j in range(GA_KV_HEADS):
        for g in range(GROUP):
            h = GROUP * j + g
            o = norm_fn(j, g)
            if h % 2 != j:
                o = pltpu.roll(o, HEAD_DIM, 1)
            heads.append(o)
    for u in range(Q_WIDTH // LANES):
        o_ref[0, :, u * LANES:(u + 1) * LANES] = jnp.where(lane < HEAD_DIM, heads[2 * u], heads[2 * u + 1])


def _half_masks(rows):
    lane = lax.broadcasted_iota(I32, (rows, LANES), 1)
    return [lane < HEAD_DIM, lane >= HEAD_DIM]


def _dot_t(a, b):
    return lax.dot_general(a, b, (((1,), (1,)), ((), ())), preferred_element_type=F32)


def _ga_kernel(q_ref, k_ref, v_ref, o_ref, qs_ref, m_ref, acc_ref, *, nt, n_lat):
    i = pl.program_id(1)
    tq = q_ref.shape[1]
    _stack_q(q_ref, qs_ref)
    m_ref[...] = jnp.full(m_ref.shape, NEG_INF, F32)
    acc_ref[...] = jnp.zeros(acc_ref.shape, F32)

    def chunk(kc, vc):
        masks = _half_masks(kc.shape[0])
        for j in range(GA_KV_HEADS):
            km = jnp.where(masks[j], kc, jnp.zeros_like(kc))
            vm = jnp.where(masks[j], vc, jnp.ones_like(vc))
            s = _dot_t(qs_ref[j], km)
            m_old = m_ref[j]
            m_new = jnp.maximum(m_old, jnp.max(s, axis=-1, keepdims=True))
            p = jnp.exp((s - jnp.tile(m_new, (1, s.shape[1] // LANES))).astype(BF16))
            acc_ref[j] = jnp.exp(m_old - m_new) * acc_ref[j] + jnp.dot(p, vm, preferred_element_type=F32)
            m_ref[j] = m_new

    @pl.when(i < nt)
    def _():
        rows = k_ref.shape[1]
        for st in range(0, n_lat, KV_CHUNK):
            en = rows if st + KV_CHUNK >= n_lat else st + KV_CHUNK
            chunk(k_ref[0, st:en, :], v_ref[0, st:en, :])

    @pl.when(i >= nt)
    def _():
        chunk(k_ref[0, n_lat:, :], v_ref[0, n_lat:, :])

    def norm_fn(j, g):
        a = acc_ref[j, g * tq:(g + 1) * tq, :]
        return a / pltpu.roll(a, HEAD_DIM, 1)

    _unstack_o(norm_fn, o_ref, tq)


def _global_attention(gq, gk, gv, nt, ntile):
    bsz, rows, _ = gq.shape
    tq = ROW_TILE
    n_lat = nt * tq
    return pl.pallas_call(
        functools.partial(_ga_kernel, nt=nt, n_lat=n_lat),
        grid=(bsz, ntile),
        in_specs=[
            pl.BlockSpec((1, tq, Q_WIDTH), lambda b, i: (b, i, 0)),
            pl.BlockSpec((1, rows, KV_WIDTH), lambda b, i: (b, 0, 0)),
            pl.BlockSpec((1, rows, KV_WIDTH), lambda b, i: (b, 0, 0)),
        ],
        out_specs=pl.BlockSpec((1, tq, Q_WIDTH), lambda b, i: (b, i, 0)),
        out_shape=jax.ShapeDtypeStruct((bsz, ntile * tq, Q_WIDTH), F32),
        scratch_shapes=[
            pltpu.VMEM((GA_KV_HEADS, GROUP * tq, LANES), BF16),
            pltpu.VMEM((GA_KV_HEADS, GROUP * tq, LANES), F32),
            pltpu.VMEM((GA_KV_HEADS, GROUP * tq, LANES), F32),
        ],
        compiler_params=_cp(("arbitrary", "arbitrary")),
        name="global_attn",
    )(gq, gk, gv)


def _wa_kernel(q_ref, k_ref, v_ref, sink_ref, bias_ref, o_ref, qs_ref, acc_ref, l_ref, *, nt, n_lat):
    i = pl.program_id(1)
    tq = q_ref.shape[1]
    band = tq + 2 * WINDOW
    _stack_q(q_ref, qs_ref)
    st = pl.multiple_of(i * tq, tq)
    kb = k_ref[0, pl.ds(st, band), :]
    vb = v_ref[0, pl.ds(st, band), :]
    kc = k_ref[0, WINDOW + n_lat:WINDOW + n_lat + ROW_TILE, :]
    vc = v_ref[0, WINDOW + n_lat:WINDOW + n_lat + ROW_TILE, :]

    kpos = lax.broadcasted_iota(I32, (1, band), 1) + (i * tq - WINDOW)
    in_seq = (kpos >= 0) & (kpos < n_lat) & (i < nt)
    bias = bias_ref[...]

    mb = _half_masks(band)
    mc = _half_masks(ROW_TILE)
    for j in range(WA_KV_HEADS):
        q = qs_ref[j]
        s_loc = jnp.where(in_seq, _dot_t(q, jnp.where(mb[j], kb, jnp.zeros_like(kb))) + bias, NEG_INF)
        s_ctx = _dot_t(q, jnp.where(mc[j], kc, jnp.zeros_like(kc)))
        sink = jnp.concatenate(
            [jnp.broadcast_to(sink_ref[GROUP * j + g:GROUP * j + g + 1, :], (tq, LANES)) for g in range(GROUP)], axis=0)
        m = jnp.maximum(jnp.maximum(jnp.max(s_loc, axis=-1, keepdims=True), jnp.max(s_ctx, axis=-1, keepdims=True)), sink)
        p_loc = jnp.exp(s_loc - jnp.tile(m, (1, band // LANES))).astype(BF16)
        p_ctx = jnp.exp(s_ctx - jnp.tile(m, (1, ROW_TILE // LANES))).astype(BF16)
        acc_ref[j] = (jnp.dot(p_loc, jnp.where(mb[j], vb, jnp.ones_like(vb)), preferred_element_type=F32)
                      + jnp.dot(p_ctx, jnp.where(mc[j], vc, jnp.ones_like(vc)), preferred_element_type=F32))
        l_ref[j] = jnp.exp(sink - m)

    def norm_fn(j, g):
        a = acc_ref[j, g * tq:(g + 1) * tq, :]
        return a / (pltpu.roll(a, HEAD_DIM, 1) + l_ref[j, g * tq:(g + 1) * tq, :])

    _unstack_o(norm_fn, o_ref, tq)


def _window_attention(wq, wk, wv, sink, nt, ntile):
    bsz, rows, _ = wq.shape
    tq = ROW_TILE
    n_lat = nt * tq
    pad = ((0, 0), (WINDOW, WINDOW), (0, 0))
    kp = jnp.pad(wk, pad)
    vp = jnp.pad(wv, pad)
    sink_b = jnp.zeros((8, LANES), F32).at[:WA_HEADS].set(jnp.broadcast_to(sink[:, None], (WA_HEADS, LANES)))
    band = tq + 2 * WINDOW
    rel = jnp.arange(band, dtype=I32)[None, :] - WINDOW - (jnp.arange(GROUP * tq, dtype=I32) % tq)[:, None]
    band_bias = jnp.where(jnp.abs(rel) <= WINDOW, 0.0, NEG_INF).astype(F32)
    return pl.pallas_call(
        functools.partial(_wa_kernel, nt=nt, n_lat=n_lat),
        grid=(bsz, ntile),
        in_specs=[
            pl.BlockSpec((1, tq, Q_WIDTH), lambda b, i: (b, i, 0)),
            pl.BlockSpec((1, rows + 2 * WINDOW, KV_WIDTH), lambda b, i: (b, 0, 0)),
            pl.BlockSpec((1, rows + 2 * WINDOW, KV_WIDTH), lambda b, i: (b, 0, 0)),
            pl.BlockSpec((8, LANES), lambda b, i: (0, 0)),
            pl.BlockSpec((GROUP * tq, band), lambda b, i: (0, 0)),
        ],
        out_specs=pl.BlockSpec((1, tq, Q_WIDTH), lambda b, i: (b, i, 0)),
        out_shape=jax.ShapeDtypeStruct((bsz, ntile * tq, Q_WIDTH), F32),
        scratch_shapes=[
            pltpu.VMEM((WA_KV_HEADS, GROUP * tq, LANES), BF16),
            pltpu.VMEM((WA_KV_HEADS, GROUP * tq, LANES), F32),
            pltpu.VMEM((WA_KV_HEADS, GROUP * tq, LANES), F32),
        ],
        compiler_params=_cp(("arbitrary", "arbitrary")),
        name="window_attn",
    )(wq, kp, vp, sink_b, band_bias)


def _store_row_tiles(ref, v):
    rows = v.shape[0]
    for s in range(v.shape[1] // LANES):
        ref[pl.ds(s, rows, stride=8), :] = v[:, s * LANES:(s + 1) * LANES]


def _load_row_tiles(ref, row0, rows):
    return jnp.concatenate([ref[pl.ds(row0 * 8 + s, rows, stride=8), :] for s in range(8)], axis=1)


def _merge_kernel(x_ref, hy_ref, ga_ref, wa_ref, mod_ref, bn_ref, wo_ref, bo_ref, nf_ref, wrh_ref, wrl_ref, br_ref, tri_ref,
                  xo_ref, h2_ref, tg_ref, te_ref, tr_ref, cnt_ref, base_ref):
    @pl.when((pl.program_id(0) == 0) & (pl.program_id(1) == 0))
    def _():
        base_ref[...] = jnp.zeros(base_ref.shape, F32)

    bn = bn_ref[...]
    g0 = HY_WIDTH
    g1 = HY_WIDTH + Q_WIDTH
    y = jnp.concatenate([_rms(hy_ref[0], bn[:, :g0]), _rms(ga_ref[0], bn[:, g0:g1]), _rms(wa_ref[0], bn[:, g1:])], axis=-1)
    o = jnp.dot(y.astype(BF16), wo_ref[...], preferred_element_type=F32) + bo_ref[...]
    x1 = x_ref[0] + mod_ref[0, 0, 2:3, :] * o
    xo_ref[0] = x1
    h2 = _rms(x1, nf_ref[...]) * (1.0 + mod_ref[0, 0, 4:5, :]) + mod_ref[0, 0, 3:4, :]
    _store_row_tiles(h2_ref, h2)

    hi = h2.astype(BF16)
    lo = (h2 - hi.astype(F32)).astype(BF16)
    logits = (jnp.dot(hi, wrh_ref[...], preferred_element_type=F32) + jnp.dot(lo, wrh_ref[...], preferred_element_type=F32)
              + jnp.dot(hi, wrl_ref[...], preferred_element_type=F32)) + br_ref[...]
    tm = logits.shape[0]
    lane = lax.broadcasted_iota(I32, (tm, ROUTER_LANES), 1)
    te = jnp.zeros((tm, ROUTER_LANES), F32)
    tr = jnp.zeros((tm, ROUTER_LANES), F32)
    tv = jnp.zeros((tm, ROUTER_LANES), F32)
    before = base_ref[...]
    tri = tri_ref[...]
    v0 = None
    for k in range(TOP_K):
        mx = jnp.max(logits, axis=-1, keepdims=True)
        idx = jnp.min(jnp.where(logits == mx, lane, ROUTER_LANES), axis=-1, keepdims=True)
        if k == 0:
            v0 = mx
        hit = lane == idx
        onehot = jnp.where(hit, 1.0, 0.0)
        prefix = jnp.dot(tri, onehot.astype(BF16), preferred_element_type=F32) + before
        rank = jnp.sum(onehot * prefix, axis=-1, keepdims=True)
        before = before + jnp.sum(onehot, axis=0, keepdims=True)
        te = jnp.where(lane == k, idx.astype(F32), te)
        tr = jnp.where(lane == k, rank, tr)
        tv = jnp.where(lane == k, jnp.exp(mx - v0), tv)
        logits = jnp.where(hit, NEG_INF, logits)
    base_ref[...] = before
    cnt_ref[...] = before
    tg_ref[0] = tv / jnp.sum(tv, axis=-1, keepdims=True)
    te_t = te.T
    tr_t = tr.T
    for k in range(TOP_K):
        te_ref[0, :, k * tm:(k + 1) * tm] = te_t[k:k + 1, :].astype(I32)
        tr_ref[0, :, k * tm:(k + 1) * tm] = tr_t[k:k + 1, :].astype(I32)


def _merge(x_all, y_hy, y_ga, y_wa, mod, branch_norm, w_out, b_out, norm_ffn, w_router, b_router, nt, ntile):
    bsz, _, d = x_all.shape
    tm = ROW_TILE
    rows = ntile * tm
    wr = jnp.zeros((d, ROUTER_LANES), F32).at[:, :N_EXPERTS].set(w_router)
    wrh = wr.astype(BF16)
    wrl = (wr - wrh.astype(F32)).astype(BF16)
    br = jnp.full((1, ROUTER_LANES), NEG_INF, F32).at[0, :N_EXPERTS].set(b_router)
    tri = (jnp.arange(tm, dtype=I32)[:, None] > jnp.arange(tm, dtype=I32)[None, :]).astype(BF16)
    tok = lambda w: pl.BlockSpec((1, tm, w), lambda b, i: (b, i, 0))
    const = lambda s: pl.BlockSpec(s, lambda b, i: (0,) * len(s))
    flat = pl.BlockSpec((1, 1, TOP_K * tm), lambda b, i: (b * ntile + i, 0, 0))
    return pl.pallas_call(
        _merge_kernel,
        grid=(bsz, ntile),
        in_specs=[
            tok(d), tok(HY_WIDTH), tok(Q_WIDTH), tok(Q_WIDTH),
            pl.BlockSpec((1, 1, 6, d), lambda b, i: (b, i // nt, 0, 0)),
            const((1, MIX_WIDTH)), const((MIX_WIDTH, d)), const((1, d)), const((1, d)),
            const((d, ROUTER_LANES)), const((d, ROUTER_LANES)), const((1, ROUTER_LANES)), const((tm, tm)),
        ],
        out_specs=[tok(d), pl.BlockSpec((None, tm * 8, LANES), lambda b, i: (b, i, 0)), tok(ROUTER_LANES), flat, flat,
                   const((1, ROUTER_LANES))],
        out_shape=[
            jax.ShapeDtypeStruct((bsz, rows, d), F32),
            jax.ShapeDtypeStruct((bsz, rows * 8, LANES), F32),
            jax.ShapeDtypeStruct((bsz, rows, ROUTER_LANES), F32),
            jax.ShapeDtypeStruct((bsz * ntile, 1, TOP_K * tm), I32),
            jax.ShapeDtypeStruct((bsz * ntile, 1, TOP_K * tm), I32),
            jax.ShapeDtypeStruct((1, ROUTER_LANES), F32),
        ],
        scratch_shapes=[pltpu.VMEM((1, ROUTER_LANES), F32)],
        compiler_params=_cp(("arbitrary", "arbitrary")),
        name="merge_router",
    )(x_all, y_hy, y_ga, y_wa, mod, branch_norm.reshape(1, MIX_WIDTH), w_out.astype(BF16), b_out.reshape(1, d),
      norm_ffn.reshape(1, d), wrh, wrl, br, tri)


def _moe_plan(te, tr, cnt, tm):
    steps, _, per = te.shape
    n_blocks = -(-(steps * per) // tm) + N_EXPERTS
    counts = cnt[0, :N_EXPERTS].astype(I32)
    nblk = (counts + tm - 1) // tm
    cend = jnp.cumsum(nblk)
    cstart = cend - nblk
    experts = jnp.arange(N_EXPERTS, dtype=I32)
    row = jnp.sum(jnp.where(te[..., None] == experts, cstart * tm, 0), axis=-1) + tr
    j = jnp.arange(n_blocks, dtype=I32)
    block_expert = jnp.minimum(jnp.sum((cend[None, :] <= j[:, None]).astype(I32), axis=1), N_EXPERTS - 1)
    past = cend[-1] + experts
    zero_blocks = jnp.concatenate([jnp.where(nblk > 0, cend - 1, -1), jnp.where(past < n_blocks, past, -1)])
    return row, block_expert, cend[-1].reshape(1), zero_blocks


def _dispatch_kernel(zb_ref, row_ref, h_ref, x_hbm, buf0, buf1, sem, *, block_rows):
    step = pl.program_id(0)
    tm = h_ref.shape[0] // 8
    per_block = block_rows // tm

    def wait(buf, s):
        for k in range(TOP_K):
            pltpu.make_async_copy(buf, x_hbm.at[pl.ds(0, tm * 8), :], sem.at[s, k]).wait()

    @pl.when(step == 0)
    def _():
        buf1[...] = jnp.zeros(buf1.shape, F32)
        def piece(e, q):
            dst = pl.multiple_of((zb_ref[e] * per_block + q) * (tm * 8), tm * 8)
            return pltpu.make_async_copy(buf1, x_hbm.at[pl.ds(dst, tm * 8), :], sem.at[1, 0])

        for phase in ("start", "wait"):
            for e in range(zb_ref.shape[0]):
                @pl.when(zb_ref[e] >= 0)
                def _():
                    for q in range(per_block):
                        getattr(piece(e, q), phase)()

    def run(cur, prev, sc, sp):
        cur[...] = h_ref[...]
        for k in range(TOP_K):
            for r in range(tm):
                dst = pl.multiple_of(row_ref[0, 0, k * tm + r], 8)
                pltpu.make_async_copy(cur.at[pl.ds(r * 8, 8), :], x_hbm.at[pl.ds(dst, 8), :], sem.at[sc, k]).start()

        @pl.when(step > 0)
        def _():
            wait(prev, sp)

        @pl.when(step == pl.num_programs(0) - 1)
        def _():
            wait(cur, sc)

    @pl.when(step % 2 == 0)
    def _():
        run(buf0, buf1, 0, 1)

    @pl.when(step % 2 == 1)
    def _():
        run(buf1, buf0, 1, 0)


def _dispatch(h_tiles, rows_e, zero_blocks, n_blocks, block_rows):
    steps = rows_e.shape[0]
    tm = ROW_TILE
    grid_spec = pltpu.PrefetchScalarGridSpec(
        num_scalar_prefetch=1,
        grid=(steps,),
        in_specs=[
            pl.BlockSpec((1, 1, TOP_K * tm), lambda i, zb: (i, 0, 0), memory_space=pltpu.SMEM),
            pl.BlockSpec((tm * 8, LANES), lambda i, zb: (i, 0)),
        ],
        out_specs=pl.BlockSpec(memory_space=pl.ANY),
        scratch_shapes=[pltpu.VMEM((tm * 8, LANES), F32), pltpu.VMEM((tm * 8, LANES), F32),
                        pltpu.SemaphoreType.DMA((2, TOP_K))],
    )
    return pl.pallas_call(
        functools.partial(_dispatch_kernel, block_rows=block_rows),
        grid_spec=grid_spec,
        out_shape=jax.ShapeDtypeStruct((n_blocks * block_rows * 8, LANES), F32),
        compiler_params=_cp(("arbitrary",)),
        name="moe_dispatch",
    )(zero_blocks, rows_e * 8, h_tiles)


def _ffn_kernel(be_ref, nu_ref, x_ref, w1_ref, b1_ref, w2_ref, b2_ref, y_ref, w1b, w2b):
    i = pl.program_id(0)
    tm = x_ref.shape[0] // 8
    f = w2_ref.shape[1]

    @pl.when(i < nu_ref[0])
    def _():
        @pl.when((i == 0) | (be_ref[i] != be_ref[jnp.maximum(i - 1, 0)]))
        def _():
            w1b[...] = w1_ref[0].astype(BF16)
            w2b[...] = w2_ref[0].astype(BF16)

        a = jnp.dot(_load_row_tiles(x_ref, 0, tm).astype(BF16), w1b[...], preferred_element_type=F32) + b1_ref[0]
        glu = jnp.minimum(a[:, :f], SWIGLU_LIMIT)
        lin = jnp.clip(a[:, f:], -SWIGLU_LIMIT, SWIGLU_LIMIT)
        act = glu / (1.0 + jnp.exp(-SWIGLU_ALPHA * glu)) * (lin + 1.0)
        _store_row_tiles(y_ref, jnp.dot(act.astype(BF16), w2b[...], preferred_element_type=F32) + b2_ref[0])

    @pl.when(i >= nu_ref[0])
    def _():
        y_ref[...] = jnp.zeros(y_ref.shape, F32)


def _moe_ffn(x_buf, block_expert, n_used, w1, b1, w2, b2, layer, tm):
    n_blocks = block_expert.shape[0]
    _, n_exp, d, f2 = w1.shape
    f = w2.shape[2]
    grid_spec = pltpu.PrefetchScalarGridSpec(
        num_scalar_prefetch=2,
        grid=(n_blocks,),
        in_specs=[
            pl.BlockSpec((tm * 8, LANES), lambda i, be, nu: (i, 0)),
            pl.BlockSpec((None, 1, d, f2), lambda i, be, nu: (layer, be[i], 0, 0)),
            pl.BlockSpec((1, 1, f2), lambda i, be, nu: (be[i], 0, 0)),
            pl.BlockSpec((None, 1, f, d), lambda i, be, nu: (layer, be[i], 0, 0)),
            pl.BlockSpec((1, 1, d), lambda i, be, nu: (be[i], 0, 0)),
        ],
        out_specs=pl.BlockSpec((tm * 8, LANES), lambda i, be, nu: (i, 0)),
        scratch_shapes=[pltpu.VMEM((d, f2), BF16), pltpu.VMEM((f, d), BF16)],
    )
    return pl.pallas_call(
        _ffn_kernel,
        grid_spec=grid_spec,
        out_shape=jax.ShapeDtypeStruct((n_blocks * tm * 8, LANES), F32),
        compiler_params=_cp(("arbitrary",)),
        name="moe_ffn",
    )(block_expert, n_used, x_buf, w1, b1.reshape(n_exp, 1, f2), w2, b2.reshape(n_exp, 1, d))


def _combine_kernel(slot_ref, slotn_ref, y_hbm, x_ref, mod_ref, gate_ref, nf_ref, o_ref, ybuf0, ybuf1, sem, *, final):
    b = pl.program_id(0)
    i = pl.program_id(1)
    step = b * pl.num_programs(1) + i
    n_steps = pl.num_programs(0) * pl.num_programs(1)
    tc = x_ref.shape[1]
    n_rows = TOP_K * tc

    def issue(idx_ref, buf, s):
        for j in range(n_rows):
            src = pl.multiple_of(idx_ref[0, 0, j], 8)
            pltpu.make_async_copy(y_hbm.at[pl.ds(src, 8), :], buf.at[pl.ds(j * 8, 8), :], sem.at[s]).start()

    def wait(buf, s):
        pltpu.make_async_copy(y_hbm.at[pl.ds(0, n_rows * 8), :], buf, sem.at[s]).wait()

    @pl.when(step == 0)
    def _():
        issue(slot_ref, ybuf0, 0)

    def run(cur, nxt, sc, sn):
        wait(cur, sc)
        issue(slotn_ref, nxt, sn)
        gates = gate_ref[0]
        acc = gates[:, 0:1] * _load_row_tiles(cur, 0, tc)
        for k in range(1, TOP_K):
            acc += gates[:, k:k + 1] * _load_row_tiles(cur, k * tc, tc)
        out = x_ref[0] + mod_ref[0, 0, 5:6, :] * acc
        if final:
            out = _rms(out, nf_ref[...])
        o_ref[0] = out

        @pl.when(step == n_steps - 1)
        def _():
            wait(nxt, sn)

    @pl.when(step % 2 == 0)
    def _():
        run(ybuf0, ybuf1, 0, 1)

    @pl.when(step % 2 == 1)
    def _():
        run(ybuf1, ybuf0, 1, 0)


def _combine(y_sorted, slot, x_mid, mod, gates, norm_final, nt, ntile, final):
    bsz, _, d = x_mid.shape
    tc = ROW_TILE
    n_steps = bsz * ntile
    slots = slot * 8
    tok = lambda w: pl.BlockSpec((1, tc, w), lambda b, i: (b, i, 0))
    return pl.pallas_call(
        functools.partial(_combine_kernel, final=final),
        grid=(bsz, ntile),
        in_specs=[
            pl.BlockSpec((1, 1, tc * TOP_K), lambda b, i: (b * ntile + i, 0, 0), memory_space=pltpu.SMEM),
            pl.BlockSpec((1, 1, tc * TOP_K), lambda b, i: (jnp.minimum(b * ntile + i + 1, n_steps - 1), 0, 0),
                         memory_space=pltpu.SMEM),
            pl.BlockSpec(memory_space=pl.ANY),
            tok(d),
            pl.BlockSpec((1, 1, 6, d), lambda b, i: (b, i // nt, 0, 0)),
            tok(ROUTER_LANES),
            pl.BlockSpec((1, d), lambda b, i: (0, 0)),
        ],
        out_specs=tok(d),
        out_shape=jax.ShapeDtypeStruct((bsz, ntile * tc, d), F32),
        scratch_shapes=[pltpu.VMEM((TOP_K * tc * 8, LANES), F32), pltpu.VMEM((TOP_K * tc * 8, LANES), F32),
                        pltpu.SemaphoreType.DMA((2,))],
        compiler_params=_cp(("arbitrary", "arbitrary")),
        name="moe_combine",
    )(slots, slots, y_sorted, x_mid, mod, gates, norm_final.reshape(1, d))


def _split_dot(a, w):
    ah = a.astype(BF16)
    al = (a - ah.astype(F32)).astype(BF16)
    wh = w.astype(BF16)
    wl = (w - wh.astype(F32)).astype(BF16)
    return (jnp.dot(ah, wh, preferred_element_type=F32) + jnp.dot(al, wh, preferred_element_type=F32)
            + jnp.dot(ah, wl, preferred_element_type=F32))


def _filt_kernel(z_ref, w1_ref, b1_ref, w2_ref, b2_ref, w3_ref, b3_ref, fr_ref, wo_ref, dec_ref, o_ref):
    fr = fr_ref[...]
    h = jnp.sin(fr * (_split_dot(z_ref[...], w1_ref[...]) + b1_ref[...]))
    h = jnp.sin(fr * (_split_dot(h, w2_ref[...]) + b2_ref[...]))
    h = jnp.sin(fr * (_split_dot(h, w3_ref[...]) + b3_ref[...]))
    o_ref[...] = _split_dot(h, wo_ref[...]) * dec_ref[...]


def _hyena_filters_pl(n, filt):
    w1, b1, w2, b2, w3, b3, freq, w_out = filt
    fw = w2.shape[0]
    t = jnp.linspace(0.0, 1.0, n, dtype=F32)[:, None]
    wpos = (2.0 * math.pi / n) * jnp.arange(n, dtype=F32)[:, None]
    bands = jnp.linspace(1e-4, HY_BANDS - 1, HY_BANDS, dtype=F32)
    z = jnp.concatenate([t, jnp.cos(wpos * bands), -jnp.sin(wpos * bands), jnp.zeros((n, fw - HY_EMB), F32)], axis=-1)
    w1p = jnp.concatenate([w1, jnp.zeros((fw - HY_EMB, fw), F32)], axis=0)
    max_decay = math.log(HY_TARGET) / HY_FAST_DECAY
    min_decay = math.log(HY_TARGET) / HY_SLOW_DECAY
    deltas = jnp.linspace(min_decay, max_decay, HY_WIDTH, dtype=F32)
    decay = jnp.tile(jnp.exp(-t * jnp.abs(deltas)), (1, 2))
    tm = min(n, 512)
    row = lambda w: pl.BlockSpec((tm, w), lambda i: (i, 0))
    const = lambda s: pl.BlockSpec(s, lambda i: (0, 0))
    return pl.pallas_call(
        _filt_kernel,
        grid=(n // tm,),
        in_specs=[row(fw), const((fw, fw)), const((1, fw)), const((fw, fw)), const((1, fw)), const((fw, fw)), const((1, fw)),
                  const((1, fw)), const((fw, 2 * HY_WIDTH)), row(2 * HY_WIDTH)],
        out_specs=row(2 * HY_WIDTH),
        out_shape=jax.ShapeDtypeStruct((n, 2 * HY_WIDTH), F32),
        compiler_params=_cp(("arbitrary",)),
        name="hyena_filters",
    )(z, w1p, b1.reshape(1, fw), w2, b2.reshape(1, fw), w3, b3.reshape(1, fw), freq.reshape(1, fw), w_out, decay)


def _gate_kernel(u_ref, prev_ref, next_ref, w_ref, b_ref, z_ref, x0_ref, *, nt):
    i = pl.program_id(1)
    u = u_ref[0]
    tm = u.shape[0]
    w = w_ref[...]
    has_prev = jnp.where((i != 0) & (i != nt), 1.0, 0.0)
    has_next = jnp.where((i != nt - 1) & (i != nt), 1.0, 0.0)
    row = lax.broadcasted_iota(I32, u.shape, 0)
    up = jnp.where(row == 0, prev_ref[0, 7:8, :] * has_prev, pltpu.roll(u, 1, 0))
    un = jnp.where(row == tm - 1, next_ref[0, 0:1, :] * has_next, pltpu.roll(u, tm - 1, 0))
    uc = up * w[0:1, :] + u * w[1:2, :] + un * w[2:3, :] + b_ref[...]
    x0_ref[0] = uc[:, :HY_WIDTH]
    z_ref[0] = uc[:, 2 * HY_WIDTH:] * uc[:, HY_WIDTH:2 * HY_WIDTH]


def _hyena_gate(hy, conv_w, conv_b, nt, ntile):
    bsz, rows, w3 = hy.shape
    tm = ROW_TILE
    sub = tm // 8
    last8 = rows // 8 - 1
    out = pl.BlockSpec((1, tm, HY_WIDTH), lambda b, i: (b, i, 0))
    return pl.pallas_call(
        functools.partial(_gate_kernel, nt=nt),
        grid=(bsz, ntile),
        in_specs=[
            pl.BlockSpec((1, tm, w3), lambda b, i: (b, i, 0)),
            pl.BlockSpec((1, 8, w3), lambda b, i: (b, jnp.maximum(i * sub - 1, 0), 0)),
            pl.BlockSpec((1, 8, w3), lambda b, i: (b, jnp.minimum((i + 1) * sub, last8), 0)),
            pl.BlockSpec((HY_SHORT, w3), lambda b, i: (0, 0)),
            pl.BlockSpec((1, w3), lambda b, i: (0, 0)),
        ],
        out_specs=[out, out],
        out_shape=[jax.ShapeDtypeStruct((bsz, ntile * tm, HY_WIDTH), F32)] * 2,
        compiler_params=_cp(("arbitrary", "arbitrary")),
        name="hyena_gate",
    )(hy, hy, hy, conv_w, conv_b.reshape(1, w3))


FFT_N2 = LANES


def _dft_tables(n_fft):
    n1 = n_fft // FFT_N2
    nk = min(n1, -(-(n1 // 2 + 1) // 8) * 8)
    k1 = jnp.arange(nk, dtype=I32)
    nn = jnp.arange(n1, dtype=I32)
    a = (2.0 * math.pi / n1) * ((k1[:, None] * nn[None, :]) % n1).astype(F32)
    f1 = jnp.stack([jnp.cos(a), -jnp.sin(a)], axis=1).reshape(2 * nk, n1)
    w = jnp.where((k1 == 0) | (2 * k1 == n1), 1.0, jnp.where(2 * k1 < n1, 2.0, 0.0)) / n_fft
    g = jnp.stack([jnp.cos(a.T) * w, -jnp.sin(a.T) * w], axis=2).reshape(n1, 2 * nk)
    k2 = jnp.arange(FFT_N2, dtype=I32)
    m = (k2[None, None, :] * (k1[:, None, None] + n1 * k2[None, :, None])) % n_fft
    th = (2.0 * math.pi / n_fft) * m.astype(F32)
    mre, mim = jnp.cos(th), -jnp.sin(th)
    mr = jnp.concatenate([jnp.concatenate([mre, -mim], axis=2), jnp.concatenate([mim, mre], axis=2)], axis=1)
    return f1.astype(BF16), g.astype(BF16), mr.astype(BF16), jnp.swapaxes(mr, 1, 2).astype(BF16)


def _fft1_kernel(x_ref, f_ref, o_ref, *, n1_in):
    f = f_ref[...]
    n_rows = f.shape[0]

    def body(n2, carry):
        xs = x_ref[pl.ds(n2, n1_in, stride=FFT_N2), :].astype(BF16)
        o_ref[pl.ds(n2, n_rows, stride=FFT_N2), :] = jnp.dot(f, xs, preferred_element_type=F32)
        return carry
    lax.fori_loop(0, FFT_N2, body, 0, unroll=8)


def _fft1(x, f1, n1_in):
    bsz, _, ch = x.shape
    n_rows = f1.shape[0]
    return pl.pallas_call(
        functools.partial(_fft1_kernel, n1_in=n1_in),
        grid=(bsz, ch // LANES),
        in_specs=[pl.BlockSpec((None, n1_in * FFT_N2, LANES), lambda b, h: (b, 0, h)),
                  pl.BlockSpec((n_rows, n1_in), lambda b, h: (0, 0))],
        out_specs=pl.BlockSpec((None, n_rows * FFT_N2, LANES), lambda b, h: (b, 0, h)),
        out_shape=jax.ShapeDtypeStruct((bsz, n_rows * FFT_N2, ch), F32),
        compiler_params=_cp(("arbitrary", "arbitrary")),
        name="hyena_fft1",
    )(x, f1[:, :n1_in])


def _spec_kernel(a_ref, mr_ref, o_ref):
    o_ref[0] = jnp.dot(mr_ref[0], a_ref[0].astype(BF16), preferred_element_type=F32)


def _fft_spectrum(a, mr):
    _, rows, ch = a.shape
    blk = 2 * FFT_N2
    return pl.pallas_call(
        _spec_kernel,
        grid=(rows // blk,),
        in_specs=[pl.BlockSpec((1, blk, ch), lambda k: (0, k, 0)), pl.BlockSpec((1, blk, blk), lambda k: (k, 0, 0))],
        out_specs=pl.BlockSpec((1, blk, ch), lambda k: (0, k, 0)),
        out_shape=jax.ShapeDtypeStruct((1, rows, ch), F32),
        compiler_params=_cp(("arbitrary",)),
        name="hyena_filter_spectrum",
    )(a, mr)


def _mid_kernel(a_ref, mr_ref, mrt_ref, ks_ref, o_ref):
    kre, kim = ks_ref[0, :FFT_N2, :], ks_ref[0, FFT_N2:, :]
    for b in range(a_ref.shape[0]):
        x = jnp.dot(mr_ref[0], a_ref[b].astype(BF16), preferred_element_type=F32)
        xre, xim = x[:FFT_N2], x[FFT_N2:]
        y = jnp.concatenate([xre * kre - xim * kim, xre * kim + xim * kre], axis=0)
        o_ref[b] = jnp.dot(mrt_ref[0], y.astype(BF16), preferred_element_type=F32)


def _fft_mid(a, mr, mrt, kspec):
    bsz, rows, ch = a.shape
    blk = 2 * FFT_N2
    return pl.pallas_call(
        _mid_kernel,
        grid=(rows // blk,),
        in_specs=[pl.BlockSpec((bsz, blk, ch), lambda k: (0, k, 0)),
                  pl.BlockSpec((1, blk, blk), lambda k: (k, 0, 0)),
                  pl.BlockSpec((1, blk, blk), lambda k: (k, 0, 0)),
                  pl.BlockSpec((1, blk, ch), lambda k: (0, k, 0))],
        out_specs=pl.BlockSpec((bsz, blk, ch), lambda k: (0, k, 0)),
        out_shape=jax.ShapeDtypeStruct((bsz, rows, ch), F32),
        compiler_params=_cp(("arbitrary",)),
        name="hyena_fft_mid",
    )(a, mr, mrt, kspec)


def _ifft1_kernel(b_ref, g_ref, z_ref, x0_ref, skip_ref, o_ref, *, n1_out):
    g = g_ref[...]
    n_rows = g.shape[1]
    skip = skip_ref[...]

    def body(n2, carry):
        bs = b_ref[pl.ds(n2, n_rows, stride=FFT_N2), :].astype(BF16)
        y = jnp.dot(g, bs, preferred_element_type=F32)
        rows = pl.ds(n2, n1_out, stride=FFT_N2)
        zs = z_ref[rows, :]
        o_ref[rows, :] = x0_ref[rows, :] * (y + zs * skip)
        return carry
    lax.fori_loop(0, FFT_N2, body, 0, unroll=8)


def _ifft1(bmat, g, z, x0, skip, n1_out):
    bsz, rows, ch = bmat.shape
    n = n1_out * FFT_N2
    seq = pl.BlockSpec((None, n, LANES), lambda b, h: (b, 0, h))
    return pl.pallas_call(
        functools.partial(_ifft1_kernel, n1_out=n1_out),
        grid=(bsz, ch // LANES),
        in_specs=[pl.BlockSpec((None, rows, LANES), lambda b, h: (b, 0, h)),
                  pl.BlockSpec((n1_out, g.shape[1]), lambda b, h: (0, 0)),
                  seq, seq, pl.BlockSpec((1, LANES), lambda b, h: (0, h))],
        out_specs=seq,
        out_shape=jax.ShapeDtypeStruct((bsz, n, ch), F32),
        compiler_params=_cp(("arbitrary", "arbitrary")),
        name="hyena_ifft1",
    )(bmat, g[:n1_out], z, x0, skip.reshape(1, ch))


def _hyena_fftconv(z, x0, k2, skip):
    n = k2.shape[0]
    n_fft = 2 * n
    f1, g, mr, mrt = _dft_tables(n_fft)
    k_full = jnp.concatenate([k2[:, :HY_WIDTH], k2[::-1, HY_WIDTH:]], axis=0)[None]
    kspec = _fft_spectrum(_fft1(k_full, f1, n_fft // FFT_N2), mr)
    a = _fft1(z, f1, n // FFT_N2)
    bmat = _fft_mid(a, mr, mrt, kspec)
    return _ifft1(bmat, g, z, x0, skip, n // FFT_N2)


def _ctxconv_kernel(z_ref, x0_ref, kf_ref, fd_ref, gd_ref, skip_ref, o_ref, ks_ref):
    n = z_ref.shape[1]
    half = fd_ref.shape[0] // 2

    @pl.when(pl.program_id(0) == 0)
    def _():
        ks_ref[...] = jnp.dot(fd_ref[...], kf_ref[...].astype(BF16), preferred_element_type=F32)

    z = z_ref[0]
    x = jnp.dot(fd_ref[:, :n], z.astype(BF16), preferred_element_type=F32)
    xre, xim = x[:half], x[half:]
    kre, kim = ks_ref[:half, :], ks_ref[half:, :]
    y = jnp.concatenate([xre * kre - xim * kim, xre * kim + xim * kre], axis=0)
    conv = jnp.dot(gd_ref[...], y.astype(BF16), preferred_element_type=F32)
    o_ref[0] = x0_ref[0] * (conv + z * skip_ref[...])


def _hyena_ctxconv(z, x0, k2, skip, row0):
    n = k2.shape[0]
    bsz, _, ch = z.shape
    n_fft = 2 * n
    kk = jnp.arange(n_fft, dtype=I32)
    th = (2.0 * math.pi / n_fft) * ((kk[:, None] * kk[None, :]) % n_fft).astype(F32)
    fd = jnp.concatenate([jnp.cos(th), -jnp.sin(th)], axis=0).astype(BF16)
    gd = (jnp.concatenate([jnp.cos(th[:n]), -jnp.sin(th[:n])], axis=1) / n_fft).astype(BF16)
    k_full = jnp.concatenate([k2[:, :HY_WIDTH], k2[::-1, HY_WIDTH:]], axis=0)
    blk = row0 // n
    seq = pl.BlockSpec((1, n, ch), lambda b: (b, blk, 0))
    const = lambda s: pl.BlockSpec(s, lambda b: (0, 0))
    return pl.pallas_call(
        _ctxconv_kernel,
        grid=(bsz,),
        in_specs=[seq, seq, const((n_fft, ch)), const((2 * n_fft, n_fft)), const((n, 2 * n_fft)), const((1, ch))],
        out_specs=pl.BlockSpec((1, n, ch), lambda b: (b, 0, 0)),
        out_shape=jax.ShapeDtypeStruct((bsz, n, ch), F32),
        scratch_shapes=[pltpu.VMEM((2 * n_fft, ch), F32)],
        compiler_params=_cp(("arbitrary",)),
        name="hyena_ctxconv",
    )(z, x0, k_full, fd, gd, skip.reshape(1, ch))


def _rope_tables(n_lat, rows):
    nrow = n_lat // GRID_W
    row = jnp.repeat(jnp.arange(nrow, dtype=F32), GRID_W)
    col = jnp.tile(jnp.arange(GRID_W, dtype=F32), nrow)
    n_freq = HEAD_DIM // 4
    inv_freq = ROPE_THETA ** (-jnp.arange(n_freq, dtype=F32) / n_freq)
    ar = row[:, None] * inv_freq
    ac = col[:, None] * inv_freq
    cos64 = jnp.concatenate([jnp.cos(ar), jnp.cos(ar), jnp.cos(ac), jnp.cos(ac)], axis=-1)
    sin64 = jnp.concatenate([-jnp.sin(ar), jnp.sin(ar), -jnp.sin(ac), jnp.sin(ac)], axis=-1)
    pad = rows - n_lat
    cos_t = jnp.concatenate([jnp.tile(cos64, (1, LANES // HEAD_DIM)), jnp.ones((pad, LANES), F32)], axis=0)
    sin_t = jnp.concatenate([jnp.tile(sin64, (1, LANES // HEAD_DIM)), jnp.zeros((pad, LANES), F32)], axis=0)
    return cos_t, sin_t


def kernel(x, c, ctx, c_ctx, w_ada, b_ada, norm_mix, norm_ffn, w_in, b_in, hy_conv_w, hy_conv_b, hy_filt_w1, hy_filt_b1, hy_filt_w2, hy_filt_b2, hy_filt_w3, hy_filt_b3, hy_filt_freq, hy_filt_out, hy_skip, ga_q_norm, ga_k_norm, wa_sink, branch_norm, w_out, b_out, w_router, b_router, w_mlp1, b_mlp1, w_mlp2, b_mlp2, norm_final):
    bsz, n_lat, d = x.shape
    n_ctx = ctx.shape[1]
    depth = w_ada.shape[0]
    assert n_ctx == ROW_TILE and n_lat % KV_CHUNK == 0 and bsz < 16 and d == D_MODEL
    nt = n_lat // ROW_TILE
    rows = n_lat + n_ctx
    cos_t, sin_t = _rope_tables(n_lat, rows)

    c_all = jnp.zeros((16, d), F32).at[:bsz].set(c).at[bsz].set(c_ctx)
    mod_all = _ada_mod(c_all, w_ada, b_ada)
    x_all = jnp.concatenate([x, ctx], axis=1)

    for l in range(depth):
        last = l == depth - 1
        ntile = nt if last else nt + 1
        m = mod_all[l].reshape(16, 6, d)
        mod = jnp.stack([m[:bsz], jnp.broadcast_to(m[bsz], (bsz, 6, d))], axis=1)

        hy, gq, gk, gv, wq, wk, wv = _inproj(x_all, mod, norm_mix[l], w_in[l], b_in[l], cos_t, sin_t,
                                             ga_q_norm[l], ga_k_norm[l], nt)
        filt = (hy_filt_w1[l], hy_filt_b1[l], hy_filt_w2[l], hy_filt_b2[l], hy_filt_w3[l], hy_filt_b3[l],
                hy_filt_freq[l], hy_filt_out[l])
        z, x0 = _hyena_gate(hy, hy_conv_w[l], hy_conv_b[l], nt, ntile)
        y_hy = _hyena_fftconv(z, x0, _hyena_filters_pl(n_lat, filt), hy_skip[l])
        if not last:
            yc_hy = _hyena_ctxconv(z, x0, _hyena_filters_pl(n_ctx, filt), hy_skip[l], n_lat)
            y_hy = jnp.concatenate([y_hy, yc_hy], axis=1)
        y_ga = _global_attention(gq, gk, gv, nt, ntile)
        y_wa = _window_attention(wq, wk, wv, wa_sink[l], nt, ntile)
        x_mid, h2, top_g, top_e, top_r, counts = _merge(x_all, y_hy, y_ga, y_wa, mod, branch_norm[l], w_out[l], b_out[l],
                                                        norm_ffn[l], w_router[l], b_router[l], nt, ntile)
        rows_e, block_expert, n_used, zero_blocks = _moe_plan(top_e, top_r, counts, MOE_TILE)
        x_buf = _dispatch(h2.reshape(-1, LANES), rows_e, zero_blocks, block_expert.shape[0], MOE_TILE)
        y_sorted = _moe_ffn(x_buf, block_expert, n_used, w_mlp1, b_mlp1[l], w_mlp2, b_mlp2[l], l, MOE_TILE)
        x_all = _combine(y_sorted, rows_e, x_mid, mod, top_g, norm_final, nt, ntile, last)
    return x_all
```

```python
import functools
import math

import jax
import jax.numpy as jnp
from jax import lax
from jax.experimental import pallas as pl
from jax.experimental.pallas import tpu as pltpu

F32 = jnp.float32
BF16 = jnp.bfloat16
I32 = jnp.int32

D_MODEL = 1024
HEAD_DIM = 64
GRID_W = 64
HY_WIDTH = 256
GA_HEADS = 6
GA_KV_HEADS = 2
WA_HEADS = 6
WA_KV_HEADS = 2
GROUP = GA_HEADS // GA_KV_HEADS
Q_WIDTH = GA_HEADS * HEAD_DIM
KV_WIDTH = GA_KV_HEADS * HEAD_DIM
MIX_WIDTH = HY_WIDTH + 2 * Q_WIDTH
IN_WIDTH = 3 * HY_WIDTH + 2 * (Q_WIDTH + 2 * KV_WIDTH)
WINDOW = 128
ROPE_THETA = 10000.0
ATTN_SCALE = HEAD_DIM ** -0.5
HY_SHORT = 3
HY_EMB = 33
HY_BANDS = (HY_EMB - 1) // 2
HY_FAST_DECAY = 0.3
HY_SLOW_DECAY = 1.5
HY_TARGET = 1e-2
N_EXPERTS = 32
TOP_K = 4
SWIGLU_ALPHA = 1.702
SWIGLU_LIMIT = 7.0
EPS = 1e-6
NEG_INF = -1e30

LANES = 128
VMEM_LIMIT_V7X = 56 * 1024 * 1024

ROW_TILE = 256
KV_CHUNK = 2048
MOE_TILE = 512
ROUTER_LANES = LANES


def _cp(sem, vmem=VMEM_LIMIT_V7X):
    return pltpu.CompilerParams(dimension_semantics=sem, vmem_limit_bytes=vmem)


def _rms(v, w):
    return v * lax.rsqrt(jnp.mean(v * v, axis=-1, keepdims=True) + EPS) * w


def _ada_kernel(c_ref, w_ref, b_ref, o_ref):
    c = c_ref[...]
    s = c / (1.0 + jnp.exp(-c))
    hi = s.astype(BF16)
    lo = (s - hi.astype(F32)).astype(BF16)
    w = w_ref[0]
    whi = w.astype(BF16)
    wlo = (w - whi.astype(F32)).astype(BF16)
    acc = jnp.dot(hi, whi, preferred_element_type=F32)
    acc += jnp.dot(lo, whi, preferred_element_type=F32)
    acc += jnp.dot(hi, wlo, preferred_element_type=F32)
    o_ref[0] = acc + b_ref[0]


def _ada_mod(c_all, w_ada, b_ada):
    n_layers, d, n6 = w_ada.shape
    tn = n6 // 4
    return pl.pallas_call(
        _ada_kernel,
        grid=(n_layers, n6 // tn),
        in_specs=[
            pl.BlockSpec((16, d), lambda l, j: (0, 0)),
            pl.BlockSpec((1, d, tn), lambda l, j: (l, 0, j)),
            pl.BlockSpec((1, 1, tn), lambda l, j: (l, 0, j)),
        ],
        out_specs=pl.BlockSpec((1, 16, tn), lambda l, j: (l, 0, j)),
        out_shape=jax.ShapeDtypeStruct((n_layers, 16, n6), F32),
        compiler_params=_cp(("arbitrary", "arbitrary")),
        name="ada_mod",
    )(c_all, w_ada, b_ada.reshape(n_layers, 1, n6))


def _inproj_kernel(x_ref, mod_ref, nw_ref, w_ref, b_ref, cos_ref, sin_ref, qn_ref, kn_ref, g_ref,
                   hy_ref, gq_ref, gk_ref, gv_ref, wq_ref, wk_ref, wv_ref):
    x = x_ref[0]
    tm = x.shape[0]
    shift = mod_ref[0, 0, 0:1, :]
    scale = mod_ref[0, 0, 1:2, :]
    h = _rms(x, nw_ref[...]) * (1.0 + scale) + shift
    p = jnp.dot(h.astype(BF16), w_ref[...], preferred_element_type=F32) + b_ref[...]
    hy_ref[0] = p[:, :3 * HY_WIDTH]

    cos = cos_ref[...]
    sin = sin_ref[...]
    lane = lax.broadcasted_iota(I32, (tm, LANES), 1)
    first = (lane % (HEAD_DIM // 2)) < (HEAD_DIM // 4)
    gmat = g_ref[...]

    def rope(v):
        sw = jnp.where(first, pltpu.roll(v, LANES - HEAD_DIM // 4, 1), pltpu.roll(v, HEAD_DIM // 4, 1))
        return v * cos + sw * sin

    def head_norm(v, w):
        t = v * v
        hi = t.astype(BF16)
        lo = (t - hi.astype(F32)).astype(BF16)
        ms = jnp.dot(hi, gmat, preferred_element_type=F32) + jnp.dot(lo, gmat, preferred_element_type=F32)
        return v * lax.rsqrt(ms + EPS) * w

    o = 3 * HY_WIDTH
    for u in range(Q_WIDTH // LANES):
        v = p[:, o + u * LANES:o + (u + 1) * LANES]
        v = rope(head_norm(v, qn_ref[...]))
        gq_ref[0, :, u * LANES:(u + 1) * LANES] = (v * ATTN_SCALE).astype(BF16)
    o += Q_WIDTH
    gk_ref[0] = rope(head_norm(p[:, o:o + KV_WIDTH], kn_ref[...])).astype(BF16)
    o += KV_WIDTH
    gv_ref[0] = p[:, o:o + KV_WIDTH].astype(BF16)
    o += KV_WIDTH
    for u in range(Q_WIDTH // LANES):
        v = rope(p[:, o + u * LANES:o + (u + 1) * LANES])
        wq_ref[0, :, u * LANES:(u + 1) * LANES] = (v * ATTN_SCALE).astype(BF16)
    o += Q_WIDTH
    wk_ref[0] = rope(p[:, o:o + KV_WIDTH]).astype(BF16)
    o += KV_WIDTH
    wv_ref[0] = p[:, o:o + KV_WIDTH].astype(BF16)


def _inproj(x_all, mod, norm_w, w_in, b_in, cos_t, sin_t, q_norm, k_norm, nt):
    bsz, rows, d = x_all.shape
    tm = ROW_TILE
    ntile = rows // tm
    gmat = jnp.kron(jnp.eye(LANES // HEAD_DIM, dtype=F32), jnp.full((HEAD_DIM, HEAD_DIM), 1.0 / HEAD_DIM, F32)).astype(BF16)
    qn = jnp.tile(q_norm, LANES // HEAD_DIM).reshape(1, LANES)
    kn = jnp.tile(k_norm, LANES // HEAD_DIM).reshape(1, LANES)
    tok = lambda w: pl.BlockSpec((1, tm, w), lambda b, i: (b, i, 0))
    const = lambda s: pl.BlockSpec(s, lambda b, i: (0,) * len(s))
    out_w = (3 * HY_WIDTH, Q_WIDTH, KV_WIDTH, KV_WIDTH, Q_WIDTH, KV_WIDTH, KV_WIDTH)
    out_dt = (F32, BF16, BF16, BF16, BF16, BF16, BF16)
    return pl.pallas_call(
        _inproj_kernel,
        grid=(bsz, ntile),
        in_specs=[
            tok(d),
            pl.BlockSpec((1, 1, 6, d), lambda b, i: (b, i // nt, 0, 0)),
            const((1, d)),
            const((d, IN_WIDTH)),
            const((1, IN_WIDTH)),
            pl.BlockSpec((tm, LANES), lambda b, i: (i, 0)),
            pl.BlockSpec((tm, LANES), lambda b, i: (i, 0)),
            const((1, LANES)),
            const((1, LANES)),
            const((LANES, LANES)),
        ],
        out_specs=[tok(w) for w in out_w],
        out_shape=[jax.ShapeDtypeStruct((bsz, rows, w), dt) for w, dt in zip(out_w, out_dt)],
        compiler_params=_cp(("arbitrary", "arbitrary")),
        name="inproj",
    )(x_all, mod, norm_w.reshape(1, d), w_in.astype(BF16), b_in.reshape(1, IN_WIDTH), cos_t, sin_t, qn, kn, gmat)


def _stack_q(q_ref, qs_ref):
    qf = q_ref[0].astype(F32)
    tq = qf.shape[0]
    for j in range(GA_KV_HEADS):
        for g in range(GROUP):
            h = GROUP * j + g
            blk = qf[:, (h // 2) * LANES:(h // 2 + 1) * LANES]
            if h % 2 != j:
                blk = pltpu.roll(blk, HEAD_DIM, 1)
            qs_ref[j, g * tq:(g + 1) * tq, :] = blk.astype(BF16)


def _unstack_o(norm_fn, o_ref, tq):
    lane = lax.broadcasted_iota(I32, (tq, LANES), 1)
    heads = []
    for j in range(GA_KV_HEADS):
        for g in range(GROUP):
            h = GROUP * j + g
            o = norm_fn(j, g)
            if h % 2 != j:
                o = pltpu.roll(o, HEAD_DIM, 1)
            heads.append(o)
    for u in range(Q_WIDTH // LANES):
        o_ref[0, :, u * LANES:(u + 1) * LANES] = jnp.where(lane < HEAD_DIM, heads[2 * u], heads[2 * u + 1])


def _half_masks(rows):
    lane = lax.broadcasted_iota(I32, (rows, LANES), 1)
    return [lane < HEAD_DIM, lane >= HEAD_DIM]


def _dot_t(a, b):
    return lax.dot_general(a, b, (((1,), (1,)), ((), ())), preferred_element_type=F32)


def _ga_kernel(q_ref, k_ref, v_ref, o_ref, qs_ref, m_ref, acc_ref, *, nt, n_lat):
    i = pl.program_id(1)
    tq = q_ref.shape[1]
    _stack_q(q_ref, qs_ref)
    m_ref[...] = jnp.full(m_ref.shape, NEG_INF, F32)
    acc_ref[...] = jnp.zeros(acc_ref.shape, F32)

    def chunk(kc, vc):
        masks = _half_masks(kc.shape[0])
        for j in range(GA_KV_HEADS):
            km = jnp.where(masks[j], kc, jnp.zeros_like(kc))
            vm = jnp.where(masks[j], vc, jnp.ones_like(vc))
            s = _dot_t(qs_ref[j], km)
            m_old = m_ref[j]
            m_new = jnp.maximum(m_old, jnp.max(s, axis=-1, keepdims=True))
            p = jnp.exp((s - jnp.tile(m_new, (1, s.shape[1] // LANES))).astype(BF16))
            acc_ref[j] = jnp.exp(m_old - m_new) * acc_ref[j] + jnp.dot(p, vm, preferred_element_type=F32)
            m_ref[j] = m_new

    @pl.when(i < nt)
    def _():
        rows = k_ref.shape[1]
        for st in range(0, n_lat, KV_CHUNK):
            en = rows if st + KV_CHUNK >= n_lat else st + KV_CHUNK
            chunk(k_ref[0, st:en, :], v_ref[0, st:en, :])

    @pl.when(i >= nt)
    def _():
        chunk(k_ref[0, n_lat:, :], v_ref[0, n_lat:, :])

    def norm_fn(j, g):
        a = acc_ref[j, g * tq:(g + 1) * tq, :]
        return a / pltpu.roll(a, HEAD_DIM, 1)

    _unstack_o(norm_fn, o_ref, tq)


def _global_attention(gq, gk, gv, nt, ntile):
    bsz, rows, _ = gq.shape
    tq = ROW_TILE
    n_lat = nt * tq
    return pl.pallas_call(
        functools.partial(_ga_kernel, nt=nt, n_lat=n_lat),
        grid=(bsz, ntile),
        in_specs=[
            pl.BlockSpec((1, tq, Q_WIDTH), lambda b, i: (b, i, 0)),
            pl.BlockSpec((1, rows, KV_WIDTH), lambda b, i: (b, 0, 0)),
            pl.BlockSpec((1, rows, KV_WIDTH), lambda b, i: (b, 0, 0)),
        ],
        out_specs=pl.BlockSpec((1, tq, Q_WIDTH), lambda b, i: (b, i, 0)),
        out_shape=jax.ShapeDtypeStruct((bsz, ntile * tq, Q_WIDTH), F32),
        scratch_shapes=[
            pltpu.VMEM((GA_KV_HEADS, GROUP * tq, LANES), BF16),
            pltpu.VMEM((GA_KV_HEADS, GROUP * tq, LANES), F32),
            pltpu.VMEM((GA_KV_HEADS, GROUP * tq, LANES), F32),
        ],
        compiler_params=_cp(("arbitrary", "arbitrary")),
        name="global_attn",
    )(gq, gk, gv)


def _wa_kernel(q_ref, k_ref, v_ref, sink_ref, bias_ref, o_ref, qs_ref, acc_ref, l_ref, *, nt, n_lat):
    i = pl.program_id(1)
    tq = q_ref.shape[1]
    band = tq + 2 * WINDOW
    _stack_q(q_ref, qs_ref)
    st = pl.multiple_of(i * tq, tq)
    kb = k_ref[0, pl.ds(st, band), :]
    vb = v_ref[0, pl.ds(st, band), :]
    kc = k_ref[0, WINDOW + n_lat:WINDOW + n_lat + ROW_TILE, :]
    vc = v_ref[0, WINDOW + n_lat:WINDOW + n_lat + ROW_TILE, :]

    kpos = lax.broadcasted_iota(I32, (1, band), 1) + (i * tq - WINDOW)
    in_seq = (kpos >= 0) & (kpos < n_lat) & (i < nt)
    bias = bias_ref[...]

    mb = _half_masks(band)
    mc = _half_masks(ROW_TILE)
    for j in range(WA_KV_HEADS):
        q = qs_ref[j]
        s_loc = jnp.where(in_seq, _dot_t(q, jnp.where(mb[j], kb, jnp.zeros_like(kb))) + bias, NEG_INF)
        s_ctx = _dot_t(q, jnp.where(mc[j], kc, jnp.zeros_like(kc)))
        sink = jnp.concatenate(
            [jnp.broadcast_to(sink_ref[GROUP * j + g:GROUP * j + g + 1, :], (tq, LANES)) for g in range(GROUP)], axis=0)
        m = jnp.maximum(jnp.maximum(jnp.max(s_loc, axis=-1, keepdims=True), jnp.max(s_ctx, axis=-1, keepdims=True)), sink)
        p_loc = jnp.exp(s_loc - jnp.tile(m, (1, band // LANES))).astype(BF16)
        p_ctx = jnp.exp(s_ctx - jnp.tile(m, (1, ROW_TILE // LANES))).astype(BF16)
        acc_ref[j] = (jnp.dot(p_loc, jnp.where(mb[j], vb, jnp.ones_like(vb)), preferred_element_type=F32)
                      + jnp.dot(p_ctx, jnp.where(mc[j], vc, jnp.ones_like(vc)), preferred_element_type=F32))
        l_ref[j] = jnp.exp(sink - m)

    def norm_fn(j, g):
        a = acc_ref[j, g * tq:(g + 1) * tq, :]
        return a / (pltpu.roll(a, HEAD_DIM, 1) + l_ref[j, g * tq:(g + 1) * tq, :])

    _unstack_o(norm_fn, o_ref, tq)


def _window_attention(wq, wk, wv, sink, nt, ntile):
    bsz, rows, _ = wq.shape
    tq = ROW_TILE
    n_lat = nt * tq
    pad = ((0, 0), (WINDOW, WINDOW), (0, 0))
    kp = jnp.pad(wk, pad)
    vp = jnp.pad(wv, pad)
    sink_b = jnp.zeros((8, LANES), F32).at[:WA_HEADS].set(jnp.broadcast_to(sink[:, None], (WA_HEADS, LANES)))
    band = tq + 2 * WINDOW
    rel = jnp.arange(band, dtype=I32)[None, :] - WINDOW - (jnp.arange(GROUP * tq, dtype=I32) % tq)[:, None]
    band_bias = jnp.where(jnp.abs(rel) <= WINDOW, 0.0, NEG_INF).astype(F32)
    return pl.pallas_call(
        functools.partial(_wa_kernel, nt=nt, n_lat=n_lat),
        grid=(bsz, ntile),
        in_specs=[
            pl.BlockSpec((1, tq, Q_WIDTH), lambda b, i: (b, i, 0)),
            pl.BlockSpec((1, rows + 2 * WINDOW, KV_WIDTH), lambda b, i: (b, 0, 0)),
            pl.BlockSpec((1, rows + 2 * WINDOW, KV_WIDTH), lambda b, i: (b, 0, 0)),
            pl.BlockSpec((8, LANES), lambda b, i: (0, 0)),
            pl.BlockSpec((GROUP * tq, band), lambda b, i: (0, 0)),
        ],
        out_specs=pl.BlockSpec((1, tq, Q_WIDTH), lambda b, i: (b, i, 0)),
        out_shape=jax.ShapeDtypeStruct((bsz, ntile * tq, Q_WIDTH), F32),
        scratch_shapes=[
            pltpu.VMEM((WA_KV_HEADS, GROUP * tq, LANES), BF16),
            pltpu.VMEM((WA_KV_HEADS, GROUP * tq, LANES), F32),
            pltpu.VMEM((WA_KV_HEADS, GROUP * tq, LANES), F32),
        ],
        compiler_params=_cp(("arbitrary", "arbitrary")),
        name="window_attn",
    )(wq, kp, vp, sink_b, band_bias)


def _store_row_tiles(ref, v):
    rows = v.shape[0]
    for s in range(v.shape[1] // LANES):
        ref[pl.ds(s, rows, stride=8), :] = v[:, s * LANES:(s + 1) * LANES]


def _load_row_tiles(ref, row0, rows):
    return jnp.concatenate([ref[pl.ds(row0 * 8 + s, rows, stride=8), :] for s in range(8)], axis=1)


def _merge_kernel(x_ref, hy_ref, ga_ref, wa_ref, mod_ref, bn_ref, wo_ref, bo_ref, nf_ref, wrh_ref, wrl_ref, br_ref, tri_ref,
                  xo_ref, h2_ref, tg_ref, te_ref, tr_ref, cnt_ref, base_ref):
    @pl.when((pl.program_id(0) == 0) & (pl.program_id(1) == 0))
    def _():
        base_ref[...] = jnp.zeros(base_ref.shape, F32)

    bn = bn_ref[...]
    g0 = HY_WIDTH
    g1 = HY_WIDTH + Q_WIDTH
    y = jnp.concatenate([_rms(hy_ref[0], bn[:, :g0]), _rms(ga_ref[0], bn[:, g0:g1]), _rms(wa_ref[0], bn[:, g1:])], axis=-1)
    o = jnp.dot(y.astype(BF16), wo_ref[...], preferred_element_type=F32) + bo_ref[...]
    x1 = x_ref[0] + mod_ref[0, 0, 2:3, :] * o
    xo_ref[0] = x1
    h2 = _rms(x1, nf_ref[...]) * (1.0 + mod_ref[0, 0, 4:5, :]) + mod_ref[0, 0, 3:4, :]
    _store_row_tiles(h2_ref, h2)

    hi = h2.astype(BF16)
    lo = (h2 - hi.astype(F32)).astype(BF16)
    logits = (jnp.dot(hi, wrh_ref[...], preferred_element_type=F32) + jnp.dot(lo, wrh_ref[...], preferred_element_type=F32)
              + jnp.dot(hi, wrl_ref[...], preferred_element_type=F32)) + br_ref[...]
    tm = logits.shape[0]
    lane = lax.broadcasted_iota(I32, (tm, ROUTER_LANES), 1)
    te = jnp.zeros((tm, ROUTER_LANES), F32)
    tr = jnp.zeros((tm, ROUTER_LANES), F32)
    tv = jnp.zeros((tm, ROUTER_LANES), F32)
    before = base_ref[...]
    tri = tri_ref[...]
    v0 = None
    for k in range(TOP_K):
        mx = jnp.max(logits, axis=-1, keepdims=True)
        idx = jnp.min(jnp.where(logits == mx, lane, ROUTER_LANES), axis=-1, keepdims=True)
        if k == 0:
            v0 = mx
        hit = lane == idx
        onehot = jnp.where(hit, 1.0, 0.0)
        prefix = jnp.dot(tri, onehot.astype(BF16), preferred_element_type=F32) + before
        rank = jnp.sum(onehot * prefix, axis=-1, keepdims=True)
        before = before + jnp.sum(onehot, axis=0, keepdims=True)
        te = jnp.where(lane == k, idx.astype(F32), te)
        tr = jnp.where(lane == k, rank, tr)
        tv = jnp.where(lane == k, jnp.exp(mx - v0), tv)
        logits = jnp.where(hit, NEG_INF, logits)
    base_ref[...] = before
    cnt_ref[...] = before
    tg_ref[0] = tv / jnp.sum(tv, axis=-1, keepdims=True)
    te_t = te.T
    tr_t = tr.T
    for k in range(TOP_K):
        te_ref[0, :, k * tm:(k + 1) * tm] = te_t[k:k + 1, :].astype(I32)
        tr_ref[0, :, k * tm:(k + 1) * tm] = tr_t[k:k + 1, :].astype(I32)


def _merge(x_all, y_hy, y_ga, y_wa, mod, branch_norm, w_out, b_out, norm_ffn, w_router, b_router, nt, ntile):
    bsz, _, d = x_all.shape
    tm = ROW_TILE
    rows = ntile * tm
    wr = jnp.zeros((d, ROUTER_LANES), F32).at[:, :N_EXPERTS].set(w_router)
    wrh = wr.astype(BF16)
    wrl = (wr - wrh.astype(F32)).astype(BF16)
    br = jnp.full((1, ROUTER_LANES), NEG_INF, F32).at[0, :N_EXPERTS].set(b_router)
    tri = (jnp.arange(tm, dtype=I32)[:, None] > jnp.arange(tm, dtype=I32)[None, :]).astype(BF16)
    tok = lambda w: pl.BlockSpec((1, tm, w), lambda b, i: (b, i, 0))
    const = lambda s: pl.BlockSpec(s, lambda b, i: (0,) * len(s))
    flat = pl.BlockSpec((1, 1, TOP_K * tm), lambda b, i: (b * ntile + i, 0, 0))
    return pl.pallas_call(
        _merge_kernel,
        grid=(bsz, ntile),
        in_specs=[
            tok(d), tok(HY_WIDTH), tok(Q_WIDTH), tok(Q_WIDTH),
            pl.BlockSpec((1, 1, 6, d), lambda b, i: (b, i // nt, 0, 0)),
            const((1, MIX_WIDTH)), const((MIX_WIDTH, d)), const((1, d)), const((1, d)),
            const((d, ROUTER_LANES)), const((d, ROUTER_LANES)), const((1, ROUTER_LANES)), const((tm, tm)),
        ],
        out_specs=[tok(d), pl.BlockSpec((None, tm * 8, LANES), lambda b, i: (b, i, 0)), tok(ROUTER_LANES), flat, flat,
                   const((1, ROUTER_LANES))],
        out_shape=[
            jax.ShapeDtypeStruct((bsz, rows, d), F32),
            jax.ShapeDtypeStruct((bsz, rows * 8, LANES), F32),
            jax.ShapeDtypeStruct((bsz, rows, ROUTER_LANES), F32),
            jax.ShapeDtypeStruct((bsz * ntile, 1, TOP_K * tm), I32),
            jax.ShapeDtypeStruct((bsz * ntile, 1, TOP_K * tm), I32),
            jax.ShapeDtypeStruct((1, ROUTER_LANES), F32),
        ],
        scratch_shapes=[pltpu.VMEM((1, ROUTER_LANES), F32)],
        compiler_params=_cp(("arbitrary", "arbitrary")),
        name="merge_router",
    )(x_all, y_hy, y_ga, y_wa, mod, branch_norm.reshape(1, MIX_WIDTH), w_out.astype(BF16), b_out.reshape(1, d),
      norm_ffn.reshape(1, d), wrh, wrl, br, tri)


def _moe_plan(te, tr, cnt, tm):
    steps, _, per = te.shape
    n_blocks = -(-(steps * per) // tm) + N_EXPERTS
    counts = cnt[0, :N_EXPERTS].astype(I32)
    nblk = (counts + tm - 1) // tm
    cend = jnp.cumsum(nblk)
    cstart = cend - nblk
    experts = jnp.arange(N_EXPERTS, dtype=I32)
    row = jnp.sum(jnp.where(te[..., None] == experts, cstart * tm, 0), axis=-1) + tr
    j = jnp.arange(n_blocks, dtype=I32)
    block_expert = jnp.minimum(jnp.sum((cend[None, :] <= j[:, None]).astype(I32), axis=1), N_EXPERTS - 1)
    past = cend[-1] + experts
    zero_blocks = jnp.concatenate([jnp.where(nblk > 0, cend - 1, -1), jnp.where(past < n_blocks, past, -1)])
    return row, block_expert, cend[-1].reshape(1), zero_blocks


def _dispatch_kernel(zb_ref, row_ref, h_ref, x_hbm, buf0, buf1, sem, *, block_rows):
    step = pl.program_id(0)
    tm = h_ref.shape[0] // 8
    per_block = block_rows // tm

    def wait(buf, s):
        for k in range(TOP_K):
            pltpu.make_async_copy(buf, x_hbm.at[pl.ds(0, tm * 8), :], sem.at[s, k]).wait()

    @pl.when(step == 0)
    def _():
        buf1[...] = jnp.zeros(buf1.shape, F32)
        def piece(e, q):
            dst = pl.multiple_of((zb_ref[e] * per_block + q) * (tm * 8), tm * 8)
            return pltpu.make_async_copy(buf1, x_hbm.at[pl.ds(dst, tm * 8), :], sem.at[1, 0])

        for phase in ("start", "wait"):
            for e in range(zb_ref.shape[0]):
                @pl.when(zb_ref[e] >= 0)
                def _():
                    for q in range(per_block):
                        getattr(piece(e, q), phase)()

    def run(cur, prev, sc, sp):
        cur[...] = h_ref[...]
        for k in range(TOP_K):
            for r in range(tm):
                dst = pl.multiple_of(row_ref[0, 0, k * tm + r], 8)
                pltpu.make_async_copy(cur.at[pl.ds(r * 8, 8), :], x_hbm.at[pl.ds(dst, 8), :], sem.at[sc, k]).start()

        @pl.when(step > 0)
        def _():
            wait(prev, sp)

        @pl.when(step == pl.num_programs(0) - 1)
        def _():
            wait(cur, sc)

    @pl.when(step % 2 == 0)
    def _():
        run(buf0, buf1, 0, 1)

    @pl.when(step % 2 == 1)
    def _():
        run(buf1, buf0, 1, 0)


def _dispatch(h_tiles, rows_e, zero_blocks, n_blocks, block_rows):
    steps = rows_e.shape[0]
    tm = ROW_TILE
    grid_spec = pltpu.PrefetchScalarGridSpec(
        num_scalar_prefetch=1,
        grid=(steps,),
        in_specs=[
            pl.BlockSpec((1, 1, TOP_K * tm), lambda i, zb: (i, 0, 0), memory_space=pltpu.SMEM),
            pl.BlockSpec((tm * 8, LANES), lambda i, zb: (i, 0)),
        ],
        out_specs=pl.BlockSpec(memory_space=pl.ANY),
        scratch_shapes=[pltpu.VMEM((tm * 8, LANES), F32), pltpu.VMEM((tm * 8, LANES), F32),
                        pltpu.SemaphoreType.DMA((2, TOP_K))],
    )
    return pl.pallas_call(
        functools.partial(_dispatch_kernel, block_rows=block_rows),
        grid_spec=grid_spec,
        out_shape=jax.ShapeDtypeStruct((n_blocks * block_rows * 8, LANES), F32),
        compiler_params=_cp(("arbitrary",)),
        name="moe_dispatch",
    )(zero_blocks, rows_e * 8, h_tiles)


def _ffn_kernel(be_ref, nu_ref, x_ref, w1_ref, b1_ref, w2_ref, b2_ref, y_ref, w1b, w2b):
    i = pl.program_id(0)
    tm = x_ref.shape[0] // 8
    f = w2_ref.shape[1]

    @pl.when(i < nu_ref[0])
    def _():
        @pl.when((i == 0) | (be_ref[i] != be_ref[jnp.maximum(i - 1, 0)]))
        def _():
            w1b[...] = w1_ref[0].astype(BF16)
            w2b[...] = w2_ref[0].astype(BF16)

        a = jnp.dot(_load_row_tiles(x_ref, 0, tm).astype(BF16), w1b[...], preferred_element_type=F32) + b1_ref[0]
        glu = jnp.minimum(a[:, :f], SWIGLU_LIMIT)
        lin = jnp.clip(a[:, f:], -SWIGLU_LIMIT, SWIGLU_LIMIT)
        act = glu / (1.0 + jnp.exp(-SWIGLU_ALPHA * glu)) * (lin + 1.0)
        _store_row_tiles(y_ref, jnp.dot(act.astype(BF16), w2b[...], preferred_element_type=F32) + b2_ref[0])

    @pl.when(i >= nu_ref[0])
    def _():
        y_ref[...] = jnp.zeros(y_ref.shape, F32)


def _moe_ffn(x_buf, block_expert, n_used, w1, b1, w2, b2, layer, tm):
    n_blocks = block_expert.shape[0]
    _, n_exp, d, f2 = w1.shape
    f = w2.shape[2]
    grid_spec = pltpu.PrefetchScalarGridSpec(
        num_scalar_prefetch=2,
        grid=(n_blocks,),
        in_specs=[
            pl.BlockSpec((tm * 8, LANES), lambda i, be, nu: (i, 0)),
            pl.BlockSpec((None, 1, d, f2), lambda i, be, nu: (layer, be[i], 0, 0)),
            pl.BlockSpec((1, 1, f2), lambda i, be, nu: (be[i], 0, 0)),
            pl.BlockSpec((None, 1, f, d), lambda i, be, nu: (layer, be[i], 0, 0)),
            pl.BlockSpec((1, 1, d), lambda i, be, nu: (be[i], 0, 0)),
        ],
        out_specs=pl.BlockSpec((tm * 8, LANES), lambda i, be, nu: (i, 0)),
        scratch_shapes=[pltpu.VMEM((d, f2), BF16), pltpu.VMEM((f, d), BF16)],
    )
    return pl.pallas_call(
        _ffn_kernel,
        grid_spec=grid_spec,
        out_shape=jax.ShapeDtypeStruct((n_blocks * tm * 8, LANES), F32),
        compiler_params=_cp(("arbitrary",)),
        name="moe_ffn",
    )(block_expert, n_used, x_buf, w1, b1.reshape(n_exp, 1, f2), w2, b2.reshape(n_exp, 1, d))


def _combine_kernel(slot_ref, slotn_ref, y_hbm, x_ref, mod_ref, gate_ref, nf_ref, o_ref, ybuf0, ybuf1, sem, *, final):
    b = pl.program_id(0)
    i = pl.program_id(1)
    step = b * pl.num_programs(1) + i
    n_steps = pl.num_programs(0) * pl.num_programs(1)
    tc = x_ref.shape[1]
    n_rows = TOP_K * tc

    def issue(idx_ref, buf, s):
        for j in range(n_rows):
            src = pl.multiple_of(idx_ref[0, 0, j], 8)
            pltpu.make_async_copy(y_hbm.at[pl.ds(src, 8), :], buf.at[pl.ds(j * 8, 8), :], sem.at[s]).start()

    def wait(buf, s):
        pltpu.make_async_copy(y_hbm.at[pl.ds(0, n_rows * 8), :], buf, sem.at[s]).wait()

    @pl.when(step == 0)
    def _():
        issue(slot_ref, ybuf0, 0)

    def run(cur, nxt, sc, sn):
        wait(cur, sc)
        issue(slotn_ref, nxt, sn)
        gates = gate_ref[0]
        acc = gates[:, 0:1] * _load_row_tiles(cur, 0, tc)
        for k in range(1, TOP_K):
            acc += gates[:, k:k + 1] * _load_row_tiles(cur, k * tc, tc)
        out = x_ref[0] + mod_ref[0, 0, 5:6, :] * acc
        if final:
            out = _rms(out, nf_ref[...])
        o_ref[0] = out

        @pl.when(step == n_steps - 1)
        def _():
            wait(nxt, sn)

    @pl.when(step % 2 == 0)
    def _():
        run(ybuf0, ybuf1, 0, 1)

    @pl.when(step % 2 == 1)
    def _():
        run(ybuf1, ybuf0, 1, 0)


def _combine(y_sorted, slot, x_mid, mod, gates, norm_final, nt, ntile, final):
    bsz, _, d = x_mid.shape
    tc = ROW_TILE
    n_steps = bsz * ntile
    slots = slot * 8
    tok = lambda w: pl.BlockSpec((1, tc, w), lambda b, i: (b, i, 0))
    return pl.pallas_call(
        functools.partial(_combine_kernel, final=final),
        grid=(bsz, ntile),
        in_specs=[
            pl.BlockSpec((1, 1, tc * TOP_K), lambda b, i: (b * ntile + i, 0, 0), memory_space=pltpu.SMEM),
            pl.BlockSpec((1, 1, tc * TOP_K), lambda b, i: (jnp.minimum(b * ntile + i + 1, n_steps - 1), 0, 0),
                         memory_space=pltpu.SMEM),
            pl.BlockSpec(memory_space=pl.ANY),
            tok(d),
            pl.BlockSpec((1, 1, 6, d), lambda b, i: (b, i // nt, 0, 0)),
            tok(ROUTER_LANES),
            pl.BlockSpec((1, d), lambda b, i: (0, 0)),
        ],
        out_specs=tok(d),
        out_shape=jax.ShapeDtypeStruct((bsz, ntile * tc, d), F32),
        scratch_shapes=[pltpu.VMEM((TOP_K * tc * 8, LANES), F32), pltpu.VMEM((TOP_K * tc * 8, LANES), F32),
                        pltpu.SemaphoreType.DMA((2,))],
        compiler_params=_cp(("arbitrary", "arbitrary")),
        name="moe_combine",
    )(slots, slots, y_sorted, x_mid, mod, gates, norm_final.reshape(1, d))


def _split_dot(a, w):
    ah = a.astype(BF16)
    al = (a - ah.astype(F32)).astype(BF16)
    wh = w.astype(BF16)
    wl = (w - wh.astype(F32)).astype(BF16)
    return (jnp.dot(ah, wh, preferred_element_type=F32) + jnp.dot(al, wh, preferred_element_type=F32)
            + jnp.dot(ah, wl, preferred_element_type=F32))


def _filt_kernel(z_ref, w1_ref, b1_ref, w2_ref, b2_ref, w3_ref, b3_ref, fr_ref, wo_ref, dec_ref, o_ref):
    fr = fr_ref[...]
    h = jnp.sin(fr * (_split_dot(z_ref[...], w1_ref[...]) + b1_ref[...]))
    h = jnp.sin(fr * (_split_dot(h, w2_ref[...]) + b2_ref[...]))
    h = jnp.sin(fr * (_split_dot(h, w3_ref[...]) + b3_ref[...]))
    o_ref[...] = _split_dot(h, wo_ref[...]) * dec_ref[...]


def _hyena_filters_pl(n, filt):
    w1, b1, w2, b2, w3, b3, freq, w_out = filt
    fw = w2.shape[0]
    t = jnp.linspace(0.0, 1.0, n, dtype=F32)[:, None]
    wpos = (2.0 * math.pi / n) * jnp.arange(n, dtype=F32)[:, None]
    bands = jnp.linspace(1e-4, HY_BANDS - 1, HY_BANDS, dtype=F32)
    z = jnp.concatenate([t, jnp.cos(wpos * bands), -jnp.sin(wpos * bands), jnp.zeros((n, fw - HY_EMB), F32)], axis=-1)
    w1p = jnp.concatenate([w1, jnp.zeros((fw - HY_EMB, fw), F32)], axis=0)
    max_decay = math.log(HY_TARGET) / HY_FAST_DECAY
    min_decay = math.log(HY_TARGET) / HY_SLOW_DECAY
    deltas = jnp.linspace(min_decay, max_decay, HY_WIDTH, dtype=F32)
    decay = jnp.tile(jnp.exp(-t * jnp.abs(deltas)), (1, 2))
    tm = min(n, 512)
    row = lambda w: pl.BlockSpec((tm, w), lambda i: (i, 0))
    const = lambda s: pl.BlockSpec(s, lambda i: (0, 0))
    return pl.pallas_call(
        _filt_kernel,
        grid=(n // tm,),
        in_specs=[row(fw), const((fw, fw)), const((1, fw)), const((fw, fw)), const((1, fw)), const((fw, fw)), const((1, fw)),
                  const((1, fw)), const((fw, 2 * HY_WIDTH)), row(2 * HY_WIDTH)],
        out_specs=row(2 * HY_WIDTH),
        out_shape=jax.ShapeDtypeStruct((n, 2 * HY_WIDTH), F32),
        compiler_params=_cp(("arbitrary",)),
        name="hyena_filters",
    )(z, w1p, b1.reshape(1, fw), w2, b2.reshape(1, fw), w3, b3.reshape(1, fw), freq.reshape(1, fw), w_out, decay)


GATE_STEPS = 4


def _gate_kernel(u_ref, prev_ref, next_ref, w_ref, b_ref, z_ref, x0_ref, *, n_lat):
    i = pl.program_id(1)
    u = u_ref[0]
    tm = u.shape[0]
    n_rows = pl.num_programs(1) * tm
    w = w_ref[...]
    row = lax.broadcasted_iota(I32, u.shape, 0)
    pos = row + i * tm
    up = jnp.where(row == 0, prev_ref[0, 7:8, :], pltpu.roll(u, 1, 0))
    up = jnp.where((pos == 0) | (pos == n_lat), 0.0, up)
    un = jnp.where(row == tm - 1, next_ref[0, 0:1, :], pltpu.roll(u, tm - 1, 0))
    un = jnp.where((pos == n_lat - 1) | (pos == n_rows - 1), 0.0, un)
    uc = up * w[0:1, :] + u * w[1:2, :] + un * w[2:3, :] + b_ref[...]
    x0_ref[0] = uc[:, :HY_WIDTH]
    z_ref[0] = uc[:, 2 * HY_WIDTH:] * uc[:, HY_WIDTH:2 * HY_WIDTH]


def _hyena_gate(hy, conv_w, conv_b, n_lat):
    bsz, rows, w3 = hy.shape
    tm = rows // GATE_STEPS
    sub = tm // 8
    last8 = rows // 8 - 1
    out = pl.BlockSpec((1, tm, HY_WIDTH), lambda b, i: (b, i, 0))
    return pl.pallas_call(
        functools.partial(_gate_kernel, n_lat=n_lat),
        grid=(bsz, GATE_STEPS),
        in_specs=[
            pl.BlockSpec((1, tm, w3), lambda b, i: (b, i, 0)),
            pl.BlockSpec((1, 8, w3), lambda b, i: (b, jnp.maximum(i * sub - 1, 0), 0)),
            pl.BlockSpec((1, 8, w3), lambda b, i: (b, jnp.minimum((i + 1) * sub, last8), 0)),
            pl.BlockSpec((HY_SHORT, w3), lambda b, i: (0, 0)),
            pl.BlockSpec((1, w3), lambda b, i: (0, 0)),
        ],
        out_specs=[out, out],
        out_shape=[jax.ShapeDtypeStruct((bsz, rows, HY_WIDTH), F32)] * 2,
        compiler_params=_cp(("arbitrary", "arbitrary")),
        name="hyena_gate",
    )(hy, hy, hy, conv_w, conv_b.reshape(1, w3))


FFT_N2 = LANES


def _dft_tables(n_fft):
    n1 = n_fft // FFT_N2
    nk = min(n1, -(-(n1 // 2 + 1) // 8) * 8)
    k1 = jnp.arange(nk, dtype=I32)
    nn = jnp.arange(n1, dtype=I32)
    a = (2.0 * math.pi / n1) * ((k1[:, None] * nn[None, :]) % n1).astype(F32)
    f1 = jnp.stack([jnp.cos(a), -jnp.sin(a)], axis=1).reshape(2 * nk, n1)
    w = jnp.where((k1 == 0) | (2 * k1 == n1), 1.0, jnp.where(2 * k1 < n1, 2.0, 0.0)) / n_fft
    g = jnp.stack([jnp.cos(a.T) * w, -jnp.sin(a.T) * w], axis=2).reshape(n1, 2 * nk)
    k2 = jnp.arange(FFT_N2, dtype=I32)
    m = (k2[None, None, :] * (k1[:, None, None] + n1 * k2[None, :, None])) % n_fft
    th = (2.0 * math.pi / n_fft) * m.astype(F32)
    mre, mim = jnp.cos(th), -jnp.sin(th)
    mr = jnp.concatenate([jnp.concatenate([mre, -mim], axis=2), jnp.concatenate([mim, mre], axis=2)], axis=1)
    return f1.astype(BF16), g.astype(BF16), mr.astype(BF16), jnp.swapaxes(mr, 1, 2).astype(BF16)


def _fft1_kernel(x_ref, f_ref, o_ref, *, n1_in):
    f = f_ref[...]
    n_rows = f.shape[0]

    def body(n2, carry):
        xs = x_ref[pl.ds(n2, n1_in, stride=FFT_N2), :].astype(BF16)
        o_ref[pl.ds(n2, n_rows, stride=FFT_N2), :] = jnp.dot(f, xs, preferred_element_type=F32)
        return carry
    lax.fori_loop(0, FFT_N2, body, 0, unroll=8)


def _fft1(x, f1, n1_in):
    bsz, _, ch = x.shape
    n_rows = f1.shape[0]
    return pl.pallas_call(
        functools.partial(_fft1_kernel, n1_in=n1_in),
        grid=(bsz, ch // LANES),
        in_specs=[pl.BlockSpec((None, n1_in * FFT_N2, LANES), lambda b, h: (b, 0, h)),
                  pl.BlockSpec((n_rows, n1_in), lambda b, h: (0, 0))],
        out_specs=pl.BlockSpec((None, n_rows * FFT_N2, LANES), lambda b, h: (b, 0, h)),
        out_shape=jax.ShapeDtypeStruct((bsz, n_rows * FFT_N2, ch), F32),
        compiler_params=_cp(("arbitrary", "arbitrary")),
        name="hyena_fft1",
    )(x, f1[:, :n1_in])


def _spec_kernel(a_ref, mr_ref, o_ref):
    o_ref[0] = jnp.dot(mr_ref[0], a_ref[0].astype(BF16), preferred_element_type=F32)


def _fft_spectrum(a, mr):
    _, rows, ch = a.shape
    blk = 2 * FFT_N2
    return pl.pallas_call(
        _spec_kernel,
        grid=(rows // blk,),
        in_specs=[pl.BlockSpec((1, blk, ch), lambda k: (0, k, 0)), pl.BlockSpec((1, blk, blk), lambda k: (k, 0, 0))],
        out_specs=pl.BlockSpec((1, blk, ch), lambda k: (0, k, 0)),
        out_shape=jax.ShapeDtypeStruct((1, rows, ch), F32),
        compiler_params=_cp(("arbitrary",)),
        name="hyena_filter_spectrum",
    )(a, mr)


def _mid_kernel(a_ref, mr_ref, mrt_ref, ks_ref, o_ref):
    kre, kim = ks_ref[0, :FFT_N2, :], ks_ref[0, FFT_N2:, :]
    for b in range(a_ref.shape[0]):
        x = jnp.dot(mr_ref[0], a_ref[b].astype(BF16), preferred_element_type=F32)
        xre, xim = x[:FFT_N2], x[FFT_N2:]
        y = jnp.concatenate([xre * kre - xim * kim, xre * kim + xim * kre], axis=0)
        o_ref[b] = jnp.dot(mrt_ref[0], y.astype(BF16), preferred_element_type=F32)


def _fft_mid(a, mr, mrt, kspec):
    bsz, rows, ch = a.shape
    blk = 2 * FFT_N2
    return pl.pallas_call(
        _mid_kernel,
        grid=(rows // blk,),
        in_specs=[pl.BlockSpec((bsz, blk, ch), lambda k: (0, k, 0)),
                  pl.BlockSpec((1, blk, blk), lambda k: (k, 0, 0)),
                  pl.BlockSpec((1, blk, blk), lambda k: (k, 0, 0)),
                  pl.BlockSpec((1, blk, ch), lambda k: (0, k, 0))],
        out_specs=pl.BlockSpec((bsz, blk, ch), lambda k: (0, k, 0)),
        out_shape=jax.ShapeDtypeStruct((bsz, rows, ch), F32),
        compiler_params=_cp(("arbitrary",)),
        name="hyena_fft_mid",
    )(a, mr, mrt, kspec)


def _ifft1_kernel(b_ref, g_ref, z_ref, x0_ref, skip_ref, o_ref, *, n1_out):
    g = g_ref[...]
    n_rows = g.shape[1]
    skip = skip_ref[...]

    def body(n2, carry):
        bs = b_ref[pl.ds(n2, n_rows, stride=FFT_N2), :].astype(BF16)
        y = jnp.dot(g, bs, preferred_element_type=F32)
        rows = pl.ds(n2, n1_out, stride=FFT_N2)
        zs = z_ref[rows, :]
        o_ref[rows, :] = x0_ref[rows, :] * (y + zs * skip)
        return carry
    lax.fori_loop(0, FFT_N2, body, 0, unroll=8)


def _ifft1(bmat, g, z, x0, skip, n1_out):
    bsz, rows, ch = bmat.shape
    n = n1_out * FFT_N2
    seq = pl.BlockSpec((None, n, LANES), lambda b, h: (b, 0, h))
    return pl.pallas_call(
        functools.partial(_ifft1_kernel, n1_out=n1_out),
        grid=(bsz, ch // LANES),
        in_specs=[pl.BlockSpec((None, rows, LANES), lambda b, h: (b, 0, h)),
                  pl.BlockSpec((n1_out, g.shape[1]), lambda b, h: (0, 0)),
                  seq, seq, pl.BlockSpec((1, LANES), lambda b, h: (0, h))],
        out_specs=seq,
        out_shape=jax.ShapeDtypeStruct((bsz, n, ch), F32),
        compiler_params=_cp(("arbitrary", "arbitrary")),
        name="hyena_ifft1",
    )(bmat, g[:n1_out], z, x0, skip.reshape(1, ch))


def _hyena_fftconv(z, x0, k2, skip):
    n = k2.shape[0]
    n_fft = 2 * n
    f1, g, mr, mrt = _dft_tables(n_fft)
    k_full = jnp.concatenate([k2[:, :HY_WIDTH], k2[::-1, HY_WIDTH:]], axis=0)[None]
    kspec = _fft_spectrum(_fft1(k_full, f1, n_fft // FFT_N2), mr)
    a = _fft1(z, f1, n // FFT_N2)
    bmat = _fft_mid(a, mr, mrt, kspec)
    return _ifft1(bmat, g, z, x0, skip, n // FFT_N2)


def _ctxconv_kernel(z_ref, x0_ref, kf_ref, fd_ref, gd_ref, skip_ref, o_ref, ks_ref):
    n = z_ref.shape[1]
    half = fd_ref.shape[0] // 2

    @pl.when(pl.program_id(0) == 0)
    def _():
        ks_ref[...] = jnp.dot(fd_ref[...], kf_ref[...].astype(BF16), preferred_element_type=F32)

    z = z_ref[0]
    x = jnp.dot(fd_ref[:, :n], z.astype(BF16), preferred_element_type=F32)
    xre, xim = x[:half], x[half:]
    kre, kim = ks_ref[:half, :], ks_ref[half:, :]
    y = jnp.concatenate([xre * kre - xim * kim, xre * kim + xim * kre], axis=0)
    conv = jnp.dot(gd_ref[...], y.astype(BF16), preferred_element_type=F32)
    o_ref[0] = x0_ref[0] * (conv + z * skip_ref[...])


def _hyena_ctxconv(z, x0, k2, skip, row0):
    n = k2.shape[0]
    bsz, _, ch = z.shape
    n_fft = 2 * n
    kk = jnp.arange(n_fft, dtype=I32)
    th = (2.0 * math.pi / n_fft) * ((kk[:, None] * kk[None, :]) % n_fft).astype(F32)
    fd = jnp.concatenate([jnp.cos(th), -jnp.sin(th)], axis=0).astype(BF16)
    gd = (jnp.concatenate([jnp.cos(th[:n]), -jnp.sin(th[:n])], axis=1) / n_fft).astype(BF16)
    k_full = jnp.concatenate([k2[:, :HY_WIDTH], k2[::-1, HY_WIDTH:]], axis=0)
    blk = row0 // n
    seq = pl.BlockSpec((1, n, ch), lambda b: (b, blk, 0))
    const = lambda s: pl.BlockSpec(s, lambda b: (0, 0))
    return pl.pallas_call(
        _ctxconv_kernel,
        grid=(bsz,),
        in_specs=[seq, seq, const((n_fft, ch)), const((2 * n_fft, n_fft)), const((n, 2 * n_fft)), const((1, ch))],
        out_specs=pl.BlockSpec((1, n, ch), lambda b: (b, 0, 0)),
        out_shape=jax.ShapeDtypeStruct((bsz, n, ch), F32),
        scratch_shapes=[pltpu.VMEM((2 * n_fft, ch), F32)],
        compiler_params=_cp(("arbitrary",)),
        name="hyena_ctxconv",
    )(z, x0, k_full, fd, gd, skip.reshape(1, ch))


def _rope_tables(n_lat, rows):
    nrow = n_lat // GRID_W
    row = jnp.repeat(jnp.arange(nrow, dtype=F32), GRID_W)
    col = jnp.tile(jnp.arange(GRID_W, dtype=F32), nrow)
    n_freq = HEAD_DIM // 4
    inv_freq = ROPE_THETA ** (-jnp.arange(n_freq, dtype=F32) / n_freq)
    ar = row[:, None] * inv_freq
    ac = col[:, None] * inv_freq
    cos64 = jnp.concatenate([jnp.cos(ar), jnp.cos(ar), jnp.cos(ac), jnp.cos(ac)], axis=-1)
    sin64 = jnp.concatenate([-jnp.sin(ar), jnp.sin(ar), -jnp.sin(ac), jnp.sin(ac)], axis=-1)
    pad = rows - n_lat
    cos_t = jnp.concatenate([jnp.tile(cos64, (1, LANES // HEAD_DIM)), jnp.ones((pad, LANES), F32)], axis=0)
    sin_t = jnp.concatenate([jnp.tile(sin64, (1, LANES // HEAD_DIM)), jnp.zeros((pad, LANES), F32)], axis=0)
    return cos_t, sin_t


def kernel(x, c, ctx, c_ctx, w_ada, b_ada, norm_mix, norm_ffn, w_in, b_in, hy_conv_w, hy_conv_b, hy_filt_w1, hy_filt_b1, hy_filt_w2, hy_filt_b2, hy_filt_w3, hy_filt_b3, hy_filt_freq, hy_filt_out, hy_skip, ga_q_norm, ga_k_norm, wa_sink, branch_norm, w_out, b_out, w_router, b_router, w_mlp1, b_mlp1, w_mlp2, b_mlp2, norm_final):
    bsz, n_lat, d = x.shape
    n_ctx = ctx.shape[1]
    depth = w_ada.shape[0]
    assert n_ctx == ROW_TILE and n_lat % KV_CHUNK == 0 and bsz < 16 and d == D_MODEL
    assert (n_lat + n_ctx) % (8 * GATE_STEPS) == 0
    nt = n_lat // ROW_TILE
    rows = n_lat + n_ctx
    cos_t, sin_t = _rope_tables(n_lat, rows)

    c_all = jnp.zeros((16, d), F32).at[:bsz].set(c).at[bsz].set(c_ctx)
    mod_all = _ada_mod(c_all, w_ada, b_ada)
    x_all = jnp.concatenate([x, ctx], axis=1)

    for l in range(depth):
        last = l == depth - 1
        ntile = nt if last else nt + 1
        m = mod_all[l].reshape(16, 6, d)
        mod = jnp.stack([m[:bsz], jnp.broadcast_to(m[bsz], (bsz, 6, d))], axis=1)

        hy, gq, gk, gv, wq, wk, wv = _inproj(x_all, mod, norm_mix[l], w_in[l], b_in[l], cos_t, sin_t,
                                             ga_q_norm[l], ga_k_norm[l], nt)
        filt = (hy_filt_w1[l], hy_filt_b1[l], hy_filt_w2[l], hy_filt_b2[l], hy_filt_w3[l], hy_filt_b3[l],
                hy_filt_freq[l], hy_filt_out[l])
        z, x0 = _hyena_gate(hy, hy_conv_w[l], hy_conv_b[l], n_lat)
        y_hy = _hyena_fftconv(z, x0, _hyena_filters_pl(n_lat, filt), hy_skip[l])
        if not last:
            yc_hy = _hyena_ctxconv(z, x0, _hyena_filters_pl(n_ctx, filt), hy_skip[l], n_lat)
            y_hy = jnp.concatenate([y_hy, yc_hy], axis=1)
        y_ga = _global_attention(gq, gk, gv, nt, ntile)
        y_wa = _window_attention(wq, wk, wv, wa_sink[l], nt, ntile)
        x_mid, h2, top_g, top_e, top_r, counts = _merge(x_all, y_hy, y_ga, y_wa, mod, branch_norm[l], w_out[l], b_out[l],
                                                        norm_ffn[l], w_router[l], b_router[l], nt, ntile)
        rows_e, block_expert, n_used, zero_blocks = _moe_plan(top_e, top_r, counts, MOE_TILE)
        x_buf = _dispatch(h2.reshape(-1, LANES), rows_e, zero_blocks, block_expert.shape[0], MOE_TILE)
        y_sorted = _moe_ffn(x_buf, block_expert, n_used, w_mlp1, b_mlp1[l], w_mlp2, b_mlp2[l], l, MOE_TILE)
        x_all = _combine(y_sorted, rows_e, x_mid, mod, top_g, norm_final, nt, ntile, last)
    return x_all
```

```python
import functools
import math

import jax
import jax.numpy as jnp
from jax import lax
from jax.experimental import pallas as pl
from jax.experimental.pallas import tpu as pltpu

F32 = jnp.float32
BF16 = jnp.bfloat16
I32 = jnp.int32

D_MODEL = 1024
HEAD_DIM = 64
GRID_W = 64
HY_WIDTH = 256
GA_HEADS = 6
GA_KV_HEADS = 2
WA_HEADS = 6
WA_KV_HEADS = 2
GROUP = GA_HEADS // GA_KV_HEADS
Q_WIDTH = GA_HEADS * HEAD_DIM
KV_WIDTH = GA_KV_HEADS * HEAD_DIM
MIX_WIDTH = HY_WIDTH + 2 * Q_WIDTH
IN_WIDTH = 3 * HY_WIDTH + 2 * (Q_WIDTH + 2 * KV_WIDTH)
WINDOW = 128
ROPE_THETA = 10000.0
ATTN_SCALE = HEAD_DIM ** -0.5
HY_SHORT = 3
HY_EMB = 33
HY_BANDS = (HY_EMB - 1) // 2
HY_FAST_DECAY = 0.3
HY_SLOW_DECAY = 1.5
HY_TARGET = 1e-2
N_EXPERTS = 32
TOP_K = 4
SWIGLU_ALPHA = 1.702
SWIGLU_LIMIT = 7.0
EPS = 1e-6
NEG_INF = -1e30

LANES = 128
VMEM_LIMIT_V7X = 56 * 1024 * 1024

ROW_TILE = 256
KV_CHUNK = 2048
MOE_TILE = 512
ROUTER_LANES = LANES


def _cp(sem, vmem=VMEM_LIMIT_V7X):
    return pltpu.CompilerParams(dimension_semantics=sem, vmem_limit_bytes=vmem)


def _rms(v, w):
    return v * lax.rsqrt(jnp.mean(v * v, axis=-1, keepdims=True) + EPS) * w


def _ada_kernel(c_ref, w_ref, b_ref, o_ref):
    c = c_ref[...]
    s = c / (1.0 + jnp.exp(-c))
    hi = s.astype(BF16)
    lo = (s - hi.astype(F32)).astype(BF16)
    w = w_ref[0]
    whi = w.astype(BF16)
    wlo = (w - whi.astype(F32)).astype(BF16)
    acc = jnp.dot(hi, whi, preferred_element_type=F32)
    acc += jnp.dot(lo, whi, preferred_element_type=F32)
    acc += jnp.dot(hi, wlo, preferred_element_type=F32)
    o_ref[0] = acc + b_ref[0]


def _ada_mod(c_all, w_ada, b_ada):
    n_layers, d, n6 = w_ada.shape
    tn = n6 // 4
    return pl.pallas_call(
        _ada_kernel,
        grid=(n_layers, n6 // tn),
        in_specs=[
            pl.BlockSpec((16, d), lambda l, j: (0, 0)),
            pl.BlockSpec((1, d, tn), lambda l, j: (l, 0, j)),
            pl.BlockSpec((1, 1, tn), lambda l, j: (l, 0, j)),
        ],
        out_specs=pl.BlockSpec((1, 16, tn), lambda l, j: (l, 0, j)),
        out_shape=jax.ShapeDtypeStruct((n_layers, 16, n6), F32),
        compiler_params=_cp(("arbitrary", "arbitrary")),
        name="ada_mod",
    )(c_all, w_ada, b_ada.reshape(n_layers, 1, n6))


def _inproj_kernel(xl_ref, xc_ref, mod_ref, nw_ref, w_ref, b_ref, cos_ref, sin_ref, qn_ref, kn_ref, g_ref,
                   hy_ref, gq_ref, gk_ref, gv_ref, wq_ref, wk_ref, wv_ref, *, nt):
    x = jnp.where(pl.program_id(1) < nt, xl_ref[0], xc_ref[0])
    tm = x.shape[0]
    shift = mod_ref[0, 0, 0:1, :]
    scale = mod_ref[0, 0, 1:2, :]
    h = _rms(x, nw_ref[...]) * (1.0 + scale) + shift
    p = jnp.dot(h.astype(BF16), w_ref[...], preferred_element_type=F32) + b_ref[...]
    hy_ref[0] = p[:, :3 * HY_WIDTH]

    cos = cos_ref[...]
    sin = sin_ref[...]
    lane = lax.broadcasted_iota(I32, (tm, LANES), 1)
    first = (lane % (HEAD_DIM // 2)) < (HEAD_DIM // 4)
    gmat = g_ref[...]

    def rope(v):
        sw = jnp.where(first, pltpu.roll(v, LANES - HEAD_DIM // 4, 1), pltpu.roll(v, HEAD_DIM // 4, 1))
        return v * cos + sw * sin

    def head_norm(v, w):
        t = v * v
        hi = t.astype(BF16)
        lo = (t - hi.astype(F32)).astype(BF16)
        ms = jnp.dot(hi, gmat, preferred_element_type=F32) + jnp.dot(lo, gmat, preferred_element_type=F32)
        return v * lax.rsqrt(ms + EPS) * w

    o = 3 * HY_WIDTH
    for u in range(Q_WIDTH // LANES):
        v = p[:, o + u * LANES:o + (u + 1) * LANES]
        v = rope(head_norm(v, qn_ref[...]))
        gq_ref[0, :, u * LANES:(u + 1) * LANES] = (v * ATTN_SCALE).astype(BF16)
    o += Q_WIDTH
    gk_ref[0] = rope(head_norm(p[:, o:o + KV_WIDTH], kn_ref[...])).astype(BF16)
    o += KV_WIDTH
    gv_ref[0] = p[:, o:o + KV_WIDTH].astype(BF16)
    o += KV_WIDTH
    for u in range(Q_WIDTH // LANES):
        v = rope(p[:, o + u * LANES:o + (u + 1) * LANES])
        wq_ref[0, :, u * LANES:(u + 1) * LANES] = (v * ATTN_SCALE).astype(BF16)
    o += Q_WIDTH
    wk_ref[0] = rope(p[:, o:o + KV_WIDTH]).astype(BF16)
    o += KV_WIDTH
    wv_ref[0] = p[:, o:o + KV_WIDTH].astype(BF16)


def _token_specs(nt, ctx_blk, d):
    return [pl.BlockSpec((1, ROW_TILE, d), lambda b, i: (b, jnp.minimum(i, nt - 1), 0)),
            pl.BlockSpec((1, ROW_TILE, d), lambda b, i: (b, ctx_blk, 0))]


def _inproj(x_lat, x_ctx, ctx_blk, mod, norm_w, w_in, b_in, cos_t, sin_t, q_norm, k_norm, nt):
    bsz, _, d = x_lat.shape
    tm = ROW_TILE
    ntile = nt + 1
    rows = ntile * tm
    gmat = jnp.kron(jnp.eye(LANES // HEAD_DIM, dtype=F32), jnp.full((HEAD_DIM, HEAD_DIM), 1.0 / HEAD_DIM, F32)).astype(BF16)
    qn = jnp.tile(q_norm, LANES // HEAD_DIM).reshape(1, LANES)
    kn = jnp.tile(k_norm, LANES // HEAD_DIM).reshape(1, LANES)
    tok = lambda w: pl.BlockSpec((1, tm, w), lambda b, i: (b, i, 0))
    const = lambda s: pl.BlockSpec(s, lambda b, i: (0,) * len(s))
    out_w = (3 * HY_WIDTH, Q_WIDTH, KV_WIDTH, KV_WIDTH, Q_WIDTH, KV_WIDTH, KV_WIDTH)
    out_dt = (F32, BF16, BF16, BF16, BF16, BF16, BF16)
    return pl.pallas_call(
        functools.partial(_inproj_kernel, nt=nt),
        grid=(bsz, ntile),
        in_specs=_token_specs(nt, ctx_blk, d) + [
            pl.BlockSpec((1, 1, 6, d), lambda b, i: (b, i // nt, 0, 0)),
            const((1, d)),
            const((d, IN_WIDTH)),
            const((1, IN_WIDTH)),
            pl.BlockSpec((tm, LANES), lambda b, i: (i, 0)),
            pl.BlockSpec((tm, LANES), lambda b, i: (i, 0)),
            const((1, LANES)),
            const((1, LANES)),
            const((LANES, LANES)),
        ],
        out_specs=[tok(w) for w in out_w],
        out_shape=[jax.ShapeDtypeStruct((bsz, rows, w), dt) for w, dt in zip(out_w, out_dt)],
        compiler_params=_cp(("arbitrary", "arbitrary")),
        name="inproj",
    )(x_lat, x_ctx, mod, norm_w.reshape(1, d), w_in.astype(BF16), b_in.reshape(1, IN_WIDTH), cos_t, sin_t, qn, kn, gmat)


def _stack_q(q_ref, qs_ref):
    qf = q_ref[0].astype(F32)
    tq = qf.shape[0]
    for j in range(GA_KV_HEADS):
        for g in range(GROUP):
            h = GROUP * j + g
            blk = qf[:, (h // 2) * LANES:(h // 2 + 1) * LANES]
            if h % 2 != j:
                blk = pltpu.roll(blk, HEAD_DIM, 1)
            qs_ref[j, g * tq:(g + 1) * tq, :] = blk.astype(BF16)


def _unstack_o(norm_fn, o_ref, tq):
    lane = lax.broadcasted_iota(I32, (tq, LANES), 1)
    heads = []
    for j in range(GA_KV_HEADS):
        for g in range(GROUP):
            h = GROUP * j + g
            o = norm_fn(j, g)
            if h % 2 != j:
                o = pltpu.roll(o, HEAD_DIM, 1)
            heads.append(o)
    for u in range(Q_WIDTH // LANES):
        o_ref[0, :, u * LANES:(u + 1) * LANES] = jnp.where(lane < HEAD_DIM, heads[2 * u], heads[2 * u + 1])


def _half_masks(rows):
    lane = lax.broadcasted_iota(I32, (rows, LANES), 1)
    return [lane < HEAD_DIM, lane >= HEAD_DIM]


def _dot_t(a, b):
    return lax.dot_general(a, b, (((1,), (1,)), ((), ())), preferred_element_type=F32)


def _ga_kernel(q_ref, k_ref, v_ref, o_ref, qs_ref, m_ref, acc_ref, *, nt, n_lat):
    i = pl.program_id(1)
    tq = q_ref.shape[1]
    _stack_q(q_ref, qs_ref)
    m_ref[...] = jnp.full(m_ref.shape, NEG_INF, F32)
    acc_ref[...] = jnp.zeros(acc_ref.shape, F32)

    def chunk(kc, vc):
        masks = _half_masks(kc.shape[0])
        for j in range(GA_KV_HEADS):
            km = jnp.where(masks[j], kc, jnp.zeros_like(kc))
            vm = jnp.where(masks[j], vc, jnp.ones_like(vc))
            s = _dot_t(qs_ref[j], km)
            m_old = m_ref[j]
            m_new = jnp.maximum(m_old, jnp.max(s, axis=-1, keepdims=True))
            p = jnp.exp((s - jnp.tile(m_new, (1, s.shape[1] // LANES))).astype(BF16))
            acc_ref[j] = jnp.exp(m_old - m_new) * acc_ref[j] + jnp.dot(p, vm, preferred_element_type=F32)
            m_ref[j] = m_new

    @pl.when(i < nt)
    def _():
        rows = k_ref.shape[1]
        for st in range(0, n_lat, KV_CHUNK):
            en = rows if st + KV_CHUNK >= n_lat else st + KV_CHUNK
            chunk(k_ref[0, st:en, :], v_ref[0, st:en, :])

    @pl.when(i >= nt)
    def _():
        chunk(k_ref[0, n_lat:, :], v_ref[0, n_lat:, :])

    def norm_fn(j, g):
        a = acc_ref[j, g * tq:(g + 1) * tq, :]
        return a / pltpu.roll(a, HEAD_DIM, 1)

    _unstack_o(norm_fn, o_ref, tq)


def _global_attention(gq, gk, gv, nt, ntile):
    bsz, rows, _ = gq.shape
    tq = ROW_TILE
    n_lat = nt * tq
    return pl.pallas_call(
        functools.partial(_ga_kernel, nt=nt, n_lat=n_lat),
        grid=(bsz, ntile),
        in_specs=[
            pl.BlockSpec((1, tq, Q_WIDTH), lambda b, i: (b, i, 0)),
            pl.BlockSpec((1, rows, KV_WIDTH), lambda b, i: (b, 0, 0)),
            pl.BlockSpec((1, rows, KV_WIDTH), lambda b, i: (b, 0, 0)),
        ],
        out_specs=pl.BlockSpec((1, tq, Q_WIDTH), lambda b, i: (b, i, 0)),
        out_shape=jax.ShapeDtypeStruct((bsz, ntile * tq, Q_WIDTH), F32),
        scratch_shapes=[
            pltpu.VMEM((GA_KV_HEADS, GROUP * tq, LANES), BF16),
            pltpu.VMEM((GA_KV_HEADS, GROUP * tq, LANES), F32),
            pltpu.VMEM((GA_KV_HEADS, GROUP * tq, LANES), F32),
        ],
        compiler_params=_cp(("arbitrary", "arbitrary")),
        name="global_attn",
    )(gq, gk, gv)


def _wa_kernel(q_ref, k_ref, v_ref, sink_ref, bias_ref, o_ref, qs_ref, acc_ref, l_ref, *, nt, n_lat):
    i = pl.program_id(1)
    tq = q_ref.shape[1]
    band = tq + 2 * WINDOW
    _stack_q(q_ref, qs_ref)
    st = pl.multiple_of(i * tq, tq)
    kb = k_ref[0, pl.ds(st, band), :]
    vb = v_ref[0, pl.ds(st, band), :]
    kc = k_ref[0, WINDOW + n_lat:WINDOW + n_lat + ROW_TILE, :]
    vc = v_ref[0, WINDOW + n_lat:WINDOW + n_lat + ROW_TILE, :]

    kpos = lax.broadcasted_iota(I32, (1, band), 1) + (i * tq - WINDOW)
    in_seq = (kpos >= 0) & (kpos < n_lat) & (i < nt)
    bias = bias_ref[...]

    mb = _half_masks(band)
    mc = _half_masks(ROW_TILE)
    for j in range(WA_KV_HEADS):
        q = qs_ref[j]
        s_loc = jnp.where(in_seq, _dot_t(q, jnp.where(mb[j], kb, jnp.zeros_like(kb))) + bias, NEG_INF)
        s_ctx = _dot_t(q, jnp.where(mc[j], kc, jnp.zeros_like(kc)))
        sink = jnp.concatenate(
            [jnp.broadcast_to(sink_ref[GROUP * j + g:GROUP * j + g + 1, :], (tq, LANES)) for g in range(GROUP)], axis=0)
        m = jnp.maximum(jnp.maximum(jnp.max(s_loc, axis=-1, keepdims=True), jnp.max(s_ctx, axis=-1, keepdims=True)), sink)
        p_loc = jnp.exp(s_loc - jnp.tile(m, (1, band // LANES))).astype(BF16)
        p_ctx = jnp.exp(s_ctx - jnp.tile(m, (1, ROW_TILE // LANES))).astype(BF16)
        acc_ref[j] = (jnp.dot(p_loc, jnp.where(mb[j], vb, jnp.ones_like(vb)), preferred_element_type=F32)
                      + jnp.dot(p_ctx, jnp.where(mc[j], vc, jnp.ones_like(vc)), preferred_element_type=F32))
        l_ref[j] = jnp.exp(sink - m)

    def norm_fn(j, g):
        a = acc_ref[j, g * tq:(g + 1) * tq, :]
        return a / (pltpu.roll(a, HEAD_DIM, 1) + l_ref[j, g * tq:(g + 1) * tq, :])

    _unstack_o(norm_fn, o_ref, tq)


def _window_attention(wq, wk, wv, sink, nt, ntile):
    bsz, rows, _ = wq.shape
    tq = ROW_TILE
    n_lat = nt * tq
    pad = ((0, 0), (WINDOW, WINDOW), (0, 0))
    kp = jnp.pad(wk, pad)
    vp = jnp.pad(wv, pad)
    sink_b = jnp.zeros((8, LANES), F32).at[:WA_HEADS].set(jnp.broadcast_to(sink[:, None], (WA_HEADS, LANES)))
    band = tq + 2 * WINDOW
    rel = jnp.arange(band, dtype=I32)[None, :] - WINDOW - (jnp.arange(GROUP * tq, dtype=I32) % tq)[:, None]
    band_bias = jnp.where(jnp.abs(rel) <= WINDOW, 0.0, NEG_INF).astype(F32)
    return pl.pallas_call(
        functools.partial(_wa_kernel, nt=nt, n_lat=n_lat),
        grid=(bsz, ntile),
        in_specs=[
            pl.BlockSpec((1, tq, Q_WIDTH), lambda b, i: (b, i, 0)),
            pl.BlockSpec((1, rows + 2 * WINDOW, KV_WIDTH), lambda b, i: (b, 0, 0)),
            pl.BlockSpec((1, rows + 2 * WINDOW, KV_WIDTH), lambda b, i: (b, 0, 0)),
            pl.BlockSpec((8, LANES), lambda b, i: (0, 0)),
            pl.BlockSpec((GROUP * tq, band), lambda b, i: (0, 0)),
        ],
        out_specs=pl.BlockSpec((1, tq, Q_WIDTH), lambda b, i: (b, i, 0)),
        out_shape=jax.ShapeDtypeStruct((bsz, ntile * tq, Q_WIDTH), F32),
        scratch_shapes=[
            pltpu.VMEM((WA_KV_HEADS, GROUP * tq, LANES), BF16),
            pltpu.VMEM((WA_KV_HEADS, GROUP * tq, LANES), F32),
            pltpu.VMEM((WA_KV_HEADS, GROUP * tq, LANES), F32),
        ],
        compiler_params=_cp(("arbitrary", "arbitrary")),
        name="window_attn",
    )(wq, kp, vp, sink_b, band_bias)


def _store_row_tiles(ref, v):
    rows = v.shape[0]
    for s in range(v.shape[1] // LANES):
        ref[pl.ds(s, rows, stride=8), :] = v[:, s * LANES:(s + 1) * LANES]


def _load_row_tiles(ref, row0, rows):
    return jnp.concatenate([ref[pl.ds(row0 * 8 + s, rows, stride=8), :] for s in range(8)], axis=1)


def _merge_kernel(xl_ref, xc_ref, hy_ref, ga_ref, wa_ref, mod_ref, bn_ref, wo_ref, bo_ref, nf_ref, wrh_ref, wrl_ref, br_ref,
                  tri_ref, xo_ref, h2_ref, tg_ref, te_ref, tr_ref, cnt_ref, base_ref, *, nt):
    @pl.when((pl.program_id(0) == 0) & (pl.program_id(1) == 0))
    def _():
        base_ref[...] = jnp.zeros(base_ref.shape, F32)

    bn = bn_ref[...]
    g0 = HY_WIDTH
    g1 = HY_WIDTH + Q_WIDTH
    y = jnp.concatenate([_rms(hy_ref[0], bn[:, :g0]), _rms(ga_ref[0], bn[:, g0:g1]), _rms(wa_ref[0], bn[:, g1:])], axis=-1)
    o = jnp.dot(y.astype(BF16), wo_ref[...], preferred_element_type=F32) + bo_ref[...]
    x1 = jnp.where(pl.program_id(1) < nt, xl_ref[0], xc_ref[0]) + mod_ref[0, 0, 2:3, :] * o
    xo_ref[0] = x1
    h2 = _rms(x1, nf_ref[...]) * (1.0 + mod_ref[0, 0, 4:5, :]) + mod_ref[0, 0, 3:4, :]
    _store_row_tiles(h2_ref, h2)

    hi = h2.astype(BF16)
    lo = (h2 - hi.astype(F32)).astype(BF16)
    logits = (jnp.dot(hi, wrh_ref[...], preferred_element_type=F32) + jnp.dot(lo, wrh_ref[...], preferred_element_type=F32)
              + jnp.dot(hi, wrl_ref[...], preferred_element_type=F32)) + br_ref[...]
    tm = logits.shape[0]
    lane = lax.broadcasted_iota(I32, (tm, ROUTER_LANES), 1)
    te = jnp.zeros((tm, ROUTER_LANES), F32)
    tr = jnp.zeros((tm, ROUTER_LANES), F32)
    tv = jnp.zeros((tm, ROUTER_LANES), F32)
    before = base_ref[...]
    tri = tri_ref[...]
    v0 = None
    for k in range(TOP_K):
        mx = jnp.max(logits, axis=-1, keepdims=True)
        idx = jnp.min(jnp.where(logits == mx, lane, ROUTER_LANES), axis=-1, keepdims=True)
        if k == 0:
            v0 = mx
        hit = lane == idx
        onehot = jnp.where(hit, 1.0, 0.0)
        prefix = jnp.dot(tri, onehot.astype(BF16), preferred_element_type=F32) + before
        rank = jnp.sum(onehot * prefix, axis=-1, keepdims=True)
        before = before + jnp.sum(onehot, axis=0, keepdims=True)
        te = jnp.where(lane == k, idx.astype(F32), te)
        tr = jnp.where(lane == k, rank, tr)
        tv = jnp.where(lane == k, jnp.exp(mx - v0), tv)
        logits = jnp.where(hit, NEG_INF, logits)
    base_ref[...] = before
    cnt_ref[...] = before
    tg_ref[0] = tv / jnp.sum(tv, axis=-1, keepdims=True)
    te_t = te.T
    tr_t = tr.T
    for k in range(TOP_K):
        te_ref[0, :, k * tm:(k + 1) * tm] = te_t[k:k + 1, :].astype(I32)
        tr_ref[0, :, k * tm:(k + 1) * tm] = tr_t[k:k + 1, :].astype(I32)


def _merge(x_lat, x_ctx, ctx_blk, y_hy, y_ga, y_wa, mod, branch_norm, w_out, b_out, norm_ffn, w_router, b_router, nt, ntile):
    bsz, _, d = x_lat.shape
    tm = ROW_TILE
    rows = ntile * tm
    wr = jnp.zeros((d, ROUTER_LANES), F32).at[:, :N_EXPERTS].set(w_router)
    wrh = wr.astype(BF16)
    wrl = (wr - wrh.astype(F32)).astype(BF16)
    br = jnp.full((1, ROUTER_LANES), NEG_INF, F32).at[0, :N_EXPERTS].set(b_router)
    tri = (jnp.arange(tm, dtype=I32)[:, None] > jnp.arange(tm, dtype=I32)[None, :]).astype(BF16)
    tok = lambda w: pl.BlockSpec((1, tm, w), lambda b, i: (b, i, 0))
    const = lambda s: pl.BlockSpec(s, lambda b, i: (0,) * len(s))
    flat = pl.BlockSpec((1, 1, TOP_K * tm), lambda b, i: (b * ntile + i, 0, 0))
    return pl.pallas_call(
        functools.partial(_merge_kernel, nt=nt),
        grid=(bsz, ntile),
        in_specs=_token_specs(nt, ctx_blk, d) + [
            tok(HY_WIDTH), tok(Q_WIDTH), tok(Q_WIDTH),
            pl.BlockSpec((1, 1, 6, d), lambda b, i: (b, i // nt, 0, 0)),
            const((1, MIX_WIDTH)), const((MIX_WIDTH, d)), const((1, d)), const((1, d)),
            const((d, ROUTER_LANES)), const((d, ROUTER_LANES)), const((1, ROUTER_LANES)), const((tm, tm)),
        ],
        out_specs=[tok(d), pl.BlockSpec((None, tm * 8, LANES), lambda b, i: (b, i, 0)), tok(ROUTER_LANES), flat, flat,
                   const((1, ROUTER_LANES))],
        out_shape=[
            jax.ShapeDtypeStruct((bsz, rows, d), F32),
            jax.ShapeDtypeStruct((bsz, rows * 8, LANES), F32),
            jax.ShapeDtypeStruct((bsz, rows, ROUTER_LANES), F32),
            jax.ShapeDtypeStruct((bsz * ntile, 1, TOP_K * tm), I32),
            jax.ShapeDtypeStruct((bsz * ntile, 1, TOP_K * tm), I32),
            jax.ShapeDtypeStruct((1, ROUTER_LANES), F32),
        ],
        scratch_shapes=[pltpu.VMEM((1, ROUTER_LANES), F32)],
        compiler_params=_cp(("arbitrary", "arbitrary")),
        name="merge_router",
    )(x_lat, x_ctx, y_hy, y_ga, y_wa, mod, branch_norm.reshape(1, MIX_WIDTH), w_out.astype(BF16), b_out.reshape(1, d),
      norm_ffn.reshape(1, d), wrh, wrl, br, tri)


def _moe_plan(te, tr, cnt, tm):
    steps, _, per = te.shape
    n_blocks = -(-(steps * per) // tm) + N_EXPERTS
    counts = cnt[0, :N_EXPERTS].astype(I32)
    nblk = (counts + tm - 1) // tm
    cend = jnp.cumsum(nblk)
    cstart = cend - nblk
    experts = jnp.arange(N_EXPERTS, dtype=I32)
    row = jnp.sum(jnp.where(te[..., None] == experts, cstart * tm, 0), axis=-1) + tr
    j = jnp.arange(n_blocks, dtype=I32)
    block_expert = jnp.minimum(jnp.sum((cend[None, :] <= j[:, None]).astype(I32), axis=1), N_EXPERTS - 1)
    past = cend[-1] + experts
    zero_blocks = jnp.concatenate([jnp.where(nblk > 0, cend - 1, -1), jnp.where(past < n_blocks, past, -1)])
    return row, block_expert, cend[-1].reshape(1), zero_blocks


def _dispatch_kernel(zb_ref, row_ref, h_ref, x_hbm, buf0, buf1, sem, *, block_rows):
    step = pl.program_id(0)
    tm = h_ref.shape[0] // 8
    per_block = block_rows // tm

    def wait(buf, s):
        for k in range(TOP_K):
            pltpu.make_async_copy(buf, x_hbm.at[pl.ds(0, tm * 8), :], sem.at[s, k]).wait()

    @pl.when(step == 0)
    def _():
        buf1[...] = jnp.zeros(buf1.shape, F32)
        def piece(e, q):
            dst = pl.multiple_of((zb_ref[e] * per_block + q) * (tm * 8), tm * 8)
            return pltpu.make_async_copy(buf1, x_hbm.at[pl.ds(dst, tm * 8), :], sem.at[1, 0])

        for phase in ("start", "wait"):
            for e in range(zb_ref.shape[0]):
                @pl.when(zb_ref[e] >= 0)
                def _():
                    for q in range(per_block):
                        getattr(piece(e, q), phase)()

    def run(cur, prev, sc, sp):
        cur[...] = h_ref[...]
        for k in range(TOP_K):
            for r in range(tm):
                dst = pl.multiple_of(row_ref[0, 0, k * tm + r], 8)
                pltpu.make_async_copy(cur.at[pl.ds(r * 8, 8), :], x_hbm.at[pl.ds(dst, 8), :], sem.at[sc, k]).start()

        @pl.when(step > 0)
        def _():
            wait(prev, sp)

        @pl.when(step == pl.num_programs(0) - 1)
        def _():
            wait(cur, sc)

    @pl.when(step % 2 == 0)
    def _():
        run(buf0, buf1, 0, 1)

    @pl.when(step % 2 == 1)
    def _():
        run(buf1, buf0, 1, 0)


def _dispatch(h_tiles, rows_e, zero_blocks, n_blocks, block_rows):
    steps = rows_e.shape[0]
    tm = ROW_TILE
    grid_spec = pltpu.PrefetchScalarGridSpec(
        num_scalar_prefetch=1,
        grid=(steps,),
        in_specs=[
            pl.BlockSpec((1, 1, TOP_K * tm), lambda i, zb: (i, 0, 0), memory_space=pltpu.SMEM),
            pl.BlockSpec((tm * 8, LANES), lambda i, zb: (i, 0)),
        ],
        out_specs=pl.BlockSpec(memory_space=pl.ANY),
        scratch_shapes=[pltpu.VMEM((tm * 8, LANES), F32), pltpu.VMEM((tm * 8, LANES), F32),
                        pltpu.SemaphoreType.DMA((2, TOP_K))],
    )
    return pl.pallas_call(
        functools.partial(_dispatch_kernel, block_rows=block_rows),
        grid_spec=grid_spec,
        out_shape=jax.ShapeDtypeStruct((n_blocks * block_rows * 8, LANES), F32),
        compiler_params=_cp(("arbitrary",)),
        name="moe_dispatch",
    )(zero_blocks, rows_e * 8, h_tiles)


def _ffn_kernel(be_ref, nu_ref, x_ref, w1_ref, b1_ref, w2_ref, b2_ref, y_ref, w1b, w2b):
    i = pl.program_id(0)
    tm = x_ref.shape[0] // 8
    f = w2_ref.shape[1]

    @pl.when(i < nu_ref[0])
    def _():
        @pl.when((i == 0) | (be_ref[i] != be_ref[jnp.maximum(i - 1, 0)]))
        def _():
            w1b[...] = w1_ref[0].astype(BF16)
            w2b[...] = w2_ref[0].astype(BF16)

        a = jnp.dot(_load_row_tiles(x_ref, 0, tm).astype(BF16), w1b[...], preferred_element_type=F32) + b1_ref[0]
        glu = jnp.minimum(a[:, :f], SWIGLU_LIMIT)
        lin = jnp.clip(a[:, f:], -SWIGLU_LIMIT, SWIGLU_LIMIT)
        act = glu / (1.0 + jnp.exp(-SWIGLU_ALPHA * glu)) * (lin + 1.0)
        _store_row_tiles(y_ref, jnp.dot(act.astype(BF16), w2b[...], preferred_element_type=F32) + b2_ref[0])

    @pl.when(i >= nu_ref[0])
    def _():
        y_ref[...] = jnp.zeros(y_ref.shape, F32)


def _moe_ffn(x_buf, block_expert, n_used, w1, b1, w2, b2, layer, tm):
    n_blocks = block_expert.shape[0]
    _, n_exp, d, f2 = w1.shape
    f = w2.shape[2]
    grid_spec = pltpu.PrefetchScalarGridSpec(
        num_scalar_prefetch=2,
        grid=(n_blocks,),
        in_specs=[
            pl.BlockSpec((tm * 8, LANES), lambda i, be, nu: (i, 0)),
            pl.BlockSpec((None, 1, d, f2), lambda i, be, nu: (layer, be[i], 0, 0)),
            pl.BlockSpec((1, 1, f2), lambda i, be, nu: (be[i], 0, 0)),
            pl.BlockSpec((None, 1, f, d), lambda i, be, nu: (layer, be[i], 0, 0)),
            pl.BlockSpec((1, 1, d), lambda i, be, nu: (be[i], 0, 0)),
        ],
        out_specs=pl.BlockSpec((tm * 8, LANES), lambda i, be, nu: (i, 0)),
        scratch_shapes=[pltpu.VMEM((d, f2), BF16), pltpu.VMEM((f, d), BF16)],
    )
    return pl.pallas_call(
        _ffn_kernel,
        grid_spec=grid_spec,
        out_shape=jax.ShapeDtypeStruct((n_blocks * tm * 8, LANES), F32),
        compiler_params=_cp(("arbitrary",)),
        name="moe_ffn",
    )(block_expert, n_used, x_buf, w1, b1.reshape(n_exp, 1, f2), w2, b2.reshape(n_exp, 1, d))


def _combine_kernel(slot_ref, slotn_ref, y_hbm, x_ref, mod_ref, gate_ref, nf_ref, o_ref, ybuf0, ybuf1, sem, *, final):
    b = pl.program_id(0)
    i = pl.program_id(1)
    step = b * pl.num_programs(1) + i
    n_steps = pl.num_programs(0) * pl.num_programs(1)
    tc = x_ref.shape[1]
    n_rows = TOP_K * tc

    def issue(idx_ref, buf, s):
        for j in range(n_rows):
            src = pl.multiple_of(idx_ref[0, 0, j], 8)
            pltpu.make_async_copy(y_hbm.at[pl.ds(src, 8), :], buf.at[pl.ds(j * 8, 8), :], sem.at[s]).start()

    def wait(buf, s):
        pltpu.make_async_copy(y_hbm.at[pl.ds(0, n_rows * 8), :], buf, sem.at[s]).wait()

    @pl.when(step == 0)
    def _():
        issue(slot_ref, ybuf0, 0)

    def run(cur, nxt, sc, sn):
        wait(cur, sc)
        issue(slotn_ref, nxt, sn)
        gates = gate_ref[0]
        acc = gates[:, 0:1] * _load_row_tiles(cur, 0, tc)
        for k in range(1, TOP_K):
            acc += gates[:, k:k + 1] * _load_row_tiles(cur, k * tc, tc)
        out = x_ref[0] + mod_ref[0, 0, 5:6, :] * acc
        if final:
            out = _rms(out, nf_ref[...])
        o_ref[0] = out

        @pl.when(step == n_steps - 1)
        def _():
            wait(nxt, sn)

    @pl.when(step % 2 == 0)
    def _():
        run(ybuf0, ybuf1, 0, 1)

    @pl.when(step % 2 == 1)
    def _():
        run(ybuf1, ybuf0, 1, 0)


def _combine(y_sorted, slot, x_mid, mod, gates, norm_final, nt, ntile, final):
    bsz, _, d = x_mid.shape
    tc = ROW_TILE
    n_steps = bsz * ntile
    slots = slot * 8
    tok = lambda w: pl.BlockSpec((1, tc, w), lambda b, i: (b, i, 0))
    return pl.pallas_call(
        functools.partial(_combine_kernel, final=final),
        grid=(bsz, ntile),
        in_specs=[
            pl.BlockSpec((1, 1, tc * TOP_K), lambda b, i: (b * ntile + i, 0, 0), memory_space=pltpu.SMEM),
            pl.BlockSpec((1, 1, tc * TOP_K), lambda b, i: (jnp.minimum(b * ntile + i + 1, n_steps - 1), 0, 0),
                         memory_space=pltpu.SMEM),
            pl.BlockSpec(memory_space=pl.ANY),
            tok(d),
            pl.BlockSpec((1, 1, 6, d), lambda b, i: (b, i // nt, 0, 0)),
            tok(ROUTER_LANES),
            pl.BlockSpec((1, d), lambda b, i: (0, 0)),
        ],
        out_specs=tok(d),
        out_shape=jax.ShapeDtypeStruct((bsz, ntile * tc, d), F32),
        scratch_shapes=[pltpu.VMEM((TOP_K * tc * 8, LANES), F32), pltpu.VMEM((TOP_K * tc * 8, LANES), F32),
                        pltpu.SemaphoreType.DMA((2,))],
        compiler_params=_cp(("arbitrary", "arbitrary")),
        name="moe_combine",
    )(slots, slots, y_sorted, x_mid, mod, gates, norm_final.reshape(1, d))


def _split_dot(a, w):
    ah = a.astype(BF16)
    al = (a - ah.astype(F32)).astype(BF16)
    wh = w.astype(BF16)
    wl = (w - wh.astype(F32)).astype(BF16)
    return (jnp.dot(ah, wh, preferred_element_type=F32) + jnp.dot(al, wh, preferred_element_type=F32)
            + jnp.dot(ah, wl, preferred_element_type=F32))


def _filt_kernel(z_ref, w1_ref, b1_ref, w2_ref, b2_ref, w3_ref, b3_ref, fr_ref, wo_ref, dec_ref, o_ref):
    fr = fr_ref[...]
    h = jnp.sin(fr * (_split_dot(z_ref[...], w1_ref[...]) + b1_ref[...]))
    h = jnp.sin(fr * (_split_dot(h, w2_ref[...]) + b2_ref[...]))
    h = jnp.sin(fr * (_split_dot(h, w3_ref[...]) + b3_ref[...]))
    o_ref[...] = _split_dot(h, wo_ref[...]) * dec_ref[...]


def _hyena_filters_pl(n, filt):
    w1, b1, w2, b2, w3, b3, freq, w_out = filt
    fw = w2.shape[0]
    t = jnp.linspace(0.0, 1.0, n, dtype=F32)[:, None]
    wpos = (2.0 * math.pi / n) * jnp.arange(n, dtype=F32)[:, None]
    bands = jnp.linspace(1e-4, HY_BANDS - 1, HY_BANDS, dtype=F32)
    z = jnp.concatenate([t, jnp.cos(wpos * bands), -jnp.sin(wpos * bands), jnp.zeros((n, fw - HY_EMB), F32)], axis=-1)
    w1p = jnp.concatenate([w1, jnp.zeros((fw - HY_EMB, fw), F32)], axis=0)
    max_decay = math.log(HY_TARGET) / HY_FAST_DECAY
    min_decay = math.log(HY_TARGET) / HY_SLOW_DECAY
    deltas = jnp.linspace(min_decay, max_decay, HY_WIDTH, dtype=F32)
    decay = jnp.tile(jnp.exp(-t * jnp.abs(deltas)), (1, 2))
    tm = min(n, 512)
    row = lambda w: pl.BlockSpec((tm, w), lambda i: (i, 0))
    const = lambda s: pl.BlockSpec(s, lambda i: (0, 0))
    return pl.pallas_call(
        _filt_kernel,
        grid=(n // tm,),
        in_specs=[row(fw), const((fw, fw)), const((1, fw)), const((fw, fw)), const((1, fw)), const((fw, fw)), const((1, fw)),
                  const((1, fw)), const((fw, 2 * HY_WIDTH)), row(2 * HY_WIDTH)],
        out_specs=row(2 * HY_WIDTH),
        out_shape=jax.ShapeDtypeStruct((n, 2 * HY_WIDTH), F32),
        compiler_params=_cp(("arbitrary",)),
        name="hyena_filters",
    )(z, w1p, b1.reshape(1, fw), w2, b2.reshape(1, fw), w3, b3.reshape(1, fw), freq.reshape(1, fw), w_out, decay)


GATE_STEPS = 4


def _gate_kernel(u_ref, prev_ref, next_ref, w_ref, b_ref, z_ref, x0_ref, *, n_lat):
    i = pl.program_id(1)
    u = u_ref[0]
    tm = u.shape[0]
    n_rows = pl.num_programs(1) * tm
    w = w_ref[...]
    row = lax.broadcasted_iota(I32, u.shape, 0)
    pos = row + i * tm
    up = jnp.where(row == 0, prev_ref[0, 7:8, :], pltpu.roll(u, 1, 0))
    up = jnp.where((pos == 0) | (pos == n_lat), 0.0, up)
    un = jnp.where(row == tm - 1, next_ref[0, 0:1, :], pltpu.roll(u, tm - 1, 0))
    un = jnp.where((pos == n_lat - 1) | (pos == n_rows - 1), 0.0, un)
    uc = up * w[0:1, :] + u * w[1:2, :] + un * w[2:3, :] + b_ref[...]
    x0_ref[0] = uc[:, :HY_WIDTH]
    z_ref[0] = uc[:, 2 * HY_WIDTH:] * uc[:, HY_WIDTH:2 * HY_WIDTH]


def _hyena_gate(hy, conv_w, conv_b, n_lat):
    bsz, rows, w3 = hy.shape
    tm = rows // GATE_STEPS
    sub = tm // 8
    last8 = rows // 8 - 1
    out = pl.BlockSpec((1, tm, HY_WIDTH), lambda b, i: (b, i, 0))
    return pl.pallas_call(
        functools.partial(_gate_kernel, n_lat=n_lat),
        grid=(bsz, GATE_STEPS),
        in_specs=[
            pl.BlockSpec((1, tm, w3), lambda b, i: (b, i, 0)),
            pl.BlockSpec((1, 8, w3), lambda b, i: (b, jnp.maximum(i * sub - 1, 0), 0)),
            pl.BlockSpec((1, 8, w3), lambda b, i: (b, jnp.minimum((i + 1) * sub, last8), 0)),
            pl.BlockSpec((HY_SHORT, w3), lambda b, i: (0, 0)),
            pl.BlockSpec((1, w3), lambda b, i: (0, 0)),
        ],
        out_specs=[out, out],
        out_shape=[jax.ShapeDtypeStruct((bsz, rows, HY_WIDTH), F32)] * 2,
        compiler_params=_cp(("arbitrary", "arbitrary")),
        name="hyena_gate",
    )(hy, hy, hy, conv_w, conv_b.reshape(1, w3))


FFT_N2 = LANES


def _dft_tables(n_fft):
    n1 = n_fft // FFT_N2
    nk = min(n1, -(-(n1 // 2 + 1) // 8) * 8)
    k1 = jnp.arange(nk, dtype=I32)
    nn = jnp.arange(n1, dtype=I32)
    a = (2.0 * math.pi / n1) * ((k1[:, None] * nn[None, :]) % n1).astype(F32)
    f1 = jnp.stack([jnp.cos(a), -jnp.sin(a)], axis=1).reshape(2 * nk, n1)
    w = jnp.where((k1 == 0) | (2 * k1 == n1), 1.0, jnp.where(2 * k1 < n1, 2.0, 0.0)) / n_fft
    g = jnp.stack([jnp.cos(a.T) * w, -jnp.sin(a.T) * w], axis=2).reshape(n1, 2 * nk)
    k2 = jnp.arange(FFT_N2, dtype=I32)
    m = (k2[None, None, :] * (k1[:, None, None] + n1 * k2[None, :, None])) % n_fft
    th = (2.0 * math.pi / n_fft) * m.astype(F32)
    mre, mim = jnp.cos(th), -jnp.sin(th)
    mr = jnp.concatenate([jnp.concatenate([mre, -mim], axis=2), jnp.concatenate([mim, mre], axis=2)], axis=1)
    return f1.astype(BF16), g.astype(BF16), mr.astype(BF16), jnp.swapaxes(mr, 1, 2).astype(BF16)


def _fft1_kernel(x_ref, f_ref, o_ref, *, n1_in):
    f = f_ref[...]
    n_rows = f.shape[0]

    def body(n2, carry):
        xs = x_ref[pl.ds(n2, n1_in, stride=FFT_N2), :].astype(BF16)
        o_ref[pl.ds(n2, n_rows, stride=FFT_N2), :] = jnp.dot(f, xs, preferred_element_type=F32)
        return carry
    lax.fori_loop(0, FFT_N2, body, 0, unroll=8)


def _fft1(x, f1, n1_in):
    bsz, _, ch = x.shape
    n_rows = f1.shape[0]
    return pl.pallas_call(
        functools.partial(_fft1_kernel, n1_in=n1_in),
        grid=(bsz, ch // LANES),
        in_specs=[pl.BlockSpec((None, n1_in * FFT_N2, LANES), lambda b, h: (b, 0, h)),
                  pl.BlockSpec((n_rows, n1_in), lambda b, h: (0, 0))],
        out_specs=pl.BlockSpec((None, n_rows * FFT_N2, LANES), lambda b, h: (b, 0, h)),
        out_shape=jax.ShapeDtypeStruct((bsz, n_rows * FFT_N2, ch), F32),
        compiler_params=_cp(("arbitrary", "arbitrary")),
        name="hyena_fft1",
    )(x, f1[:, :n1_in])


def _spec_kernel(a_ref, mr_ref, o_ref):
    o_ref[0] = jnp.dot(mr_ref[0], a_ref[0].astype(BF16), preferred_element_type=F32)


def _fft_spectrum(a, mr):
    _, rows, ch = a.shape
    blk = 2 * FFT_N2
    return pl.pallas_call(
        _spec_kernel,
        grid=(rows // blk,),
        in_specs=[pl.BlockSpec((1, blk, ch), lambda k: (0, k, 0)), pl.BlockSpec((1, blk, blk), lambda k: (k, 0, 0))],
        out_specs=pl.BlockSpec((1, blk, ch), lambda k: (0, k, 0)),
        out_shape=jax.ShapeDtypeStruct((1, rows, ch), F32),
        compiler_params=_cp(("arbitrary",)),
        name="hyena_filter_spectrum",
    )(a, mr)


def _mid_kernel(a_ref, mr_ref, mrt_ref, ks_ref, o_ref):
    kre, kim = ks_ref[0, :FFT_N2, :], ks_ref[0, FFT_N2:, :]
    for b in range(a_ref.shape[0]):
        x = jnp.dot(mr_ref[0], a_ref[b].astype(BF16), preferred_element_type=F32)
        xre, xim = x[:FFT_N2], x[FFT_N2:]
        y = jnp.concatenate([xre * kre - xim * kim, xre * kim + xim * kre], axis=0)
        o_ref[b] = jnp.dot(mrt_ref[0], y.astype(BF16), preferred_element_type=F32)


def _fft_mid(a, mr, mrt, kspec):
    bsz, rows, ch = a.shape
    blk = 2 * FFT_N2
    return pl.pallas_call(
        _mid_kernel,
        grid=(rows // blk,),
        in_specs=[pl.BlockSpec((bsz, blk, ch), lambda k: (0, k, 0)),
                  pl.BlockSpec((1, blk, blk), lambda k: (k, 0, 0)),
                  pl.BlockSpec((1, blk, blk), lambda k: (k, 0, 0)),
                  pl.BlockSpec((1, blk, ch), lambda k: (0, k, 0))],
        out_specs=pl.BlockSpec((bsz, blk, ch), lambda k: (0, k, 0)),
        out_shape=jax.ShapeDtypeStruct((bsz, rows, ch), F32),
        compiler_params=_cp(("arbitrary",)),
        name="hyena_fft_mid",
    )(a, mr, mrt, kspec)


def _ifft1_kernel(b_ref, g_ref, z_ref, x0_ref, skip_ref, o_ref, *, n1_out):
    g = g_ref[...]
    n_rows = g.shape[1]
    skip = skip_ref[...]

    def body(n2, carry):
        bs = b_ref[pl.ds(n2, n_rows, stride=FFT_N2), :].astype(BF16)
        y = jnp.dot(g, bs, preferred_element_type=F32)
        rows = pl.ds(n2, n1_out, stride=FFT_N2)
        zs = z_ref[rows, :]
        o_ref[rows, :] = x0_ref[rows, :] * (y + zs * skip)
        return carry
    lax.fori_loop(0, FFT_N2, body, 0, unroll=8)


def _ifft1(bmat, g, z, x0, skip, n1_out):
    bsz, rows, ch = bmat.shape
    n = n1_out * FFT_N2
    seq = pl.BlockSpec((None, n, LANES), lambda b, h: (b, 0, h))
    return pl.pallas_call(
        functools.partial(_ifft1_kernel, n1_out=n1_out),
        grid=(bsz, ch // LANES),
        in_specs=[pl.BlockSpec((None, rows, LANES), lambda b, h: (b, 0, h)),
                  pl.BlockSpec((n1_out, g.shape[1]), lambda b, h: (0, 0)),
                  seq, seq, pl.BlockSpec((1, LANES), lambda b, h: (0, h))],
        out_specs=seq,
        out_shape=jax.ShapeDtypeStruct((bsz, n, ch), F32),
        compiler_params=_cp(("arbitrary", "arbitrary")),
        name="hyena_ifft1",
    )(bmat, g[:n1_out], z, x0, skip.reshape(1, ch))


def _hyena_fftconv(z, x0, k2, skip):
    n = k2.shape[0]
    n_fft = 2 * n
    f1, g, mr, mrt = _dft_tables(n_fft)
    k_full = jnp.concatenate([k2[:, :HY_WIDTH], k2[::-1, HY_WIDTH:]], axis=0)[None]
    kspec = _fft_spectrum(_fft1(k_full, f1, n_fft // FFT_N2), mr)
    a = _fft1(z, f1, n // FFT_N2)
    bmat = _fft_mid(a, mr, mrt, kspec)
    return _ifft1(bmat, g, z, x0, skip, n // FFT_N2)


def _ctxconv_kernel(z_ref, x0_ref, kf_ref, fd_ref, gd_ref, skip_ref, o_ref, ks_ref):
    n = z_ref.shape[1]
    half = fd_ref.shape[0] // 2

    @pl.when(pl.program_id(0) == 0)
    def _():
        ks_ref[...] = jnp.dot(fd_ref[...], kf_ref[...].astype(BF16), preferred_element_type=F32)

    z = z_ref[0]
    x = jnp.dot(fd_ref[:, :n], z.astype(BF16), preferred_element_type=F32)
    xre, xim = x[:half], x[half:]
    kre, kim = ks_ref[:half, :], ks_ref[half:, :]
    y = jnp.concatenate([xre * kre - xim * kim, xre * kim + xim * kre], axis=0)
    conv = jnp.dot(gd_ref[...], y.astype(BF16), preferred_element_type=F32)
    o_ref[0] = x0_ref[0] * (conv + z * skip_ref[...])


def _hyena_ctxconv(z, x0, k2, skip, row0):
    n = k2.shape[0]
    bsz, _, ch = z.shape
    n_fft = 2 * n
    kk = jnp.arange(n_fft, dtype=I32)
    th = (2.0 * math.pi / n_fft) * ((kk[:, None] * kk[None, :]) % n_fft).astype(F32)
    fd = jnp.concatenate([jnp.cos(th), -jnp.sin(th)], axis=0).astype(BF16)
    gd = (jnp.concatenate([jnp.cos(th[:n]), -jnp.sin(th[:n])], axis=1) / n_fft).astype(BF16)
    k_full = jnp.concatenate([k2[:, :HY_WIDTH], k2[::-1, HY_WIDTH:]], axis=0)
    blk = row0 // n
    seq = pl.BlockSpec((1, n, ch), lambda b: (b, blk, 0))
    const = lambda s: pl.BlockSpec(s, lambda b: (0, 0))
    return pl.pallas_call(
        _ctxconv_kernel,
        grid=(bsz,),
        in_specs=[seq, seq, const((n_fft, ch)), const((2 * n_fft, n_fft)), const((n, 2 * n_fft)), const((1, ch))],
        out_specs=pl.BlockSpec((1, n, ch), lambda b: (b, 0, 0)),
        out_shape=jax.ShapeDtypeStruct((bsz, n, ch), F32),
        scratch_shapes=[pltpu.VMEM((2 * n_fft, ch), F32)],
        compiler_params=_cp(("arbitrary",)),
        name="hyena_ctxconv",
    )(z, x0, k_full, fd, gd, skip.reshape(1, ch))


def _rope_tables(n_lat, rows):
    nrow = n_lat // GRID_W
    row = jnp.repeat(jnp.arange(nrow, dtype=F32), GRID_W)
    col = jnp.tile(jnp.arange(GRID_W, dtype=F32), nrow)
    n_freq = HEAD_DIM // 4
    inv_freq = ROPE_THETA ** (-jnp.arange(n_freq, dtype=F32) / n_freq)
    ar = row[:, None] * inv_freq
    ac = col[:, None] * inv_freq
    cos64 = jnp.concatenate([jnp.cos(ar), jnp.cos(ar), jnp.cos(ac), jnp.cos(ac)], axis=-1)
    sin64 = jnp.concatenate([-jnp.sin(ar), jnp.sin(ar), -jnp.sin(ac), jnp.sin(ac)], axis=-1)
    pad = rows - n_lat
    cos_t = jnp.concatenate([jnp.tile(cos64, (1, LANES // HEAD_DIM)), jnp.ones((pad, LANES), F32)], axis=0)
    sin_t = jnp.concatenate([jnp.tile(sin64, (1, LANES // HEAD_DIM)), jnp.zeros((pad, LANES), F32)], axis=0)
    return cos_t, sin_t


def kernel(x, c, ctx, c_ctx, w_ada, b_ada, norm_mix, norm_ffn, w_in, b_in, hy_conv_w, hy_conv_b, hy_filt_w1, hy_filt_b1, hy_filt_w2, hy_filt_b2, hy_filt_w3, hy_filt_b3, hy_filt_freq, hy_filt_out, hy_skip, ga_q_norm, ga_k_norm, wa_sink, branch_norm, w_out, b_out, w_router, b_router, w_mlp1, b_mlp1, w_mlp2, b_mlp2, norm_final):
    bsz, n_lat, d = x.shape
    n_ctx = ctx.shape[1]
    depth = w_ada.shape[0]
    assert n_ctx == ROW_TILE and n_lat % KV_CHUNK == 0 and bsz < 16 and d == D_MODEL
    assert (n_lat + n_ctx) % (8 * GATE_STEPS) == 0
    nt = n_lat // ROW_TILE
    rows = n_lat + n_ctx
    cos_t, sin_t = _rope_tables(n_lat, rows)

    c_all = jnp.zeros((16, d), F32).at[:bsz].set(c).at[bsz].set(c_ctx)
    mod_all = _ada_mod(c_all, w_ada, b_ada)
    x_lat, x_ctx, ctx_blk = x, ctx, 0

    for l in range(depth):
        last = l == depth - 1
        ntile = nt if last else nt + 1
        m = mod_all[l].reshape(16, 6, d)
        mod = jnp.stack([m[:bsz], jnp.broadcast_to(m[bsz], (bsz, 6, d))], axis=1)

        hy, gq, gk, gv, wq, wk, wv = _inproj(x_lat, x_ctx, ctx_blk, mod, norm_mix[l], w_in[l], b_in[l], cos_t, sin_t,
                                             ga_q_norm[l], ga_k_norm[l], nt)
        filt = (hy_filt_w1[l], hy_filt_b1[l], hy_filt_w2[l], hy_filt_b2[l], hy_filt_w3[l], hy_filt_b3[l],
                hy_filt_freq[l], hy_filt_out[l])
        z, x0 = _hyena_gate(hy, hy_conv_w[l], hy_conv_b[l], n_lat)
        y_hy = _hyena_fftconv(z, x0, _hyena_filters_pl(n_lat, filt), hy_skip[l])
        if not last:
            yc_hy = _hyena_ctxconv(z, x0, _hyena_filters_pl(n_ctx, filt), hy_skip[l], n_lat)
            y_hy = jnp.concatenate([y_hy, yc_hy], axis=1)
        y_ga = _global_attention(gq, gk, gv, nt, ntile)
        y_wa = _window_attention(wq, wk, wv, wa_sink[l], nt, ntile)
        x_mid, h2, top_g, top_e, top_r, counts = _merge(x_lat, x_ctx, ctx_blk, y_hy, y_ga, y_wa, mod, branch_norm[l], w_out[l], b_out[l],
                                                        norm_ffn[l], w_router[l], b_router[l], nt, ntile)
        rows_e, block_expert, n_used, zero_blocks = _moe_plan(top_e, top_r, counts, MOE_TILE)
        x_buf = _dispatch(h2.reshape(-1, LANES), rows_e, zero_blocks, block_expert.shape[0], MOE_TILE)
        y_sorted = _moe_ffn(x_buf, block_expert, n_used, w_mlp1, b_mlp1[l], w_mlp2, b_mlp2[l], l, MOE_TILE)
        x_lat = x_ctx = _combine(y_sorted, rows_e, x_mid, mod, top_g, norm_final, nt, ntile, last)
        ctx_blk = nt
    return x_lat
```

```python
import functools
import math

import jax
import jax.numpy as jnp
from jax import lax
from jax.experimental import pallas as pl
from jax.experimental.pallas import tpu as pltpu

F32 = jnp.float32
BF16 = jnp.bfloat16
I32 = jnp.int32

D_MODEL = 1024
HEAD_DIM = 64
GRID_W = 64
HY_WIDTH = 256
GA_HEADS = 6
GA_KV_HEADS = 2
WA_HEADS = 6
WA_KV_HEADS = 2
GROUP = GA_HEADS // GA_KV_HEADS
Q_WIDTH = GA_HEADS * HEAD_DIM
KV_WIDTH = GA_KV_HEADS * HEAD_DIM
MIX_WIDTH = HY_WIDTH + 2 * Q_WIDTH
IN_WIDTH = 3 * HY_WIDTH + 2 * (Q_WIDTH + 2 * KV_WIDTH)
WINDOW = 128
ROPE_THETA = 10000.0
ATTN_SCALE = HEAD_DIM ** -0.5
HY_SHORT = 3
HY_EMB = 33
HY_BANDS = (HY_EMB - 1) // 2
HY_FAST_DECAY = 0.3
HY_SLOW_DECAY = 1.5
HY_TARGET = 1e-2
N_EXPERTS = 32
TOP_K = 4
SWIGLU_ALPHA = 1.702
SWIGLU_LIMIT = 7.0
EPS = 1e-6
NEG_INF = -1e30

LANES = 128
VMEM_LIMIT_V7X = 56 * 1024 * 1024

ROW_TILE = 256
KV_CHUNK = 2048
MOE_TILE = 512
ROUTER_LANES = LANES


def _cp(sem, vmem=VMEM_LIMIT_V7X):
    return pltpu.CompilerParams(dimension_semantics=sem, vmem_limit_bytes=vmem)


def _rms(v, w):
    return v * lax.rsqrt(jnp.mean(v * v, axis=-1, keepdims=True) + EPS) * w


def _ada_kernel(c_ref, w_ref, b_ref, o_ref):
    c = c_ref[...]
    s = c / (1.0 + jnp.exp(-c))
    hi = s.astype(BF16)
    lo = (s - hi.astype(F32)).astype(BF16)
    w = w_ref[0]
    whi = w.astype(BF16)
    wlo = (w - whi.astype(F32)).astype(BF16)
    acc = jnp.dot(hi, whi, preferred_element_type=F32)
    acc += jnp.dot(lo, whi, preferred_element_type=F32)
    acc += jnp.dot(hi, wlo, preferred_element_type=F32)
    o_ref[0] = acc + b_ref[0]


def _ada_mod(c_all, w_ada, b_ada):
    n_layers, d, n6 = w_ada.shape
    tn = n6 // 4
    return pl.pallas_call(
        _ada_kernel,
        grid=(n_layers, n6 // tn),
        in_specs=[
            pl.BlockSpec((16, d), lambda l, j: (0, 0)),
            pl.BlockSpec((1, d, tn), lambda l, j: (l, 0, j)),
            pl.BlockSpec((1, 1, tn), lambda l, j: (l, 0, j)),
        ],
        out_specs=pl.BlockSpec((1, 16, tn), lambda l, j: (l, 0, j)),
        out_shape=jax.ShapeDtypeStruct((n_layers, 16, n6), F32),
        compiler_params=_cp(("arbitrary", "arbitrary")),
        name="ada_mod",
    )(c_all, w_ada, b_ada.reshape(n_layers, 1, n6))


def _inproj_kernel(xl_ref, xc_ref, mod_ref, nw_ref, w_ref, b_ref, cos_ref, sin_ref, qn_ref, kn_ref, g_ref,
                   hy_ref, gq_ref, gk_ref, gv_ref, wq_ref, wk_ref, wv_ref, *, nt):
    x = jnp.where(pl.program_id(1) < nt, xl_ref[0], xc_ref[0])
    tm = x.shape[0]
    shift = mod_ref[0, 0, 0:1, :]
    scale = mod_ref[0, 0, 1:2, :]
    h = _rms(x, nw_ref[...]) * (1.0 + scale) + shift
    p = jnp.dot(h.astype(BF16), w_ref[...], preferred_element_type=F32) + b_ref[...]
    hy_ref[0] = p[:, :3 * HY_WIDTH]

    cos = cos_ref[...]
    sin = sin_ref[...]
    lane = lax.broadcasted_iota(I32, (tm, LANES), 1)
    first = (lane % (HEAD_DIM // 2)) < (HEAD_DIM // 4)
    gmat = g_ref[...]

    def rope(v):
        sw = jnp.where(first, pltpu.roll(v, LANES - HEAD_DIM // 4, 1), pltpu.roll(v, HEAD_DIM // 4, 1))
        return v * cos + sw * sin

    def head_norm(v, w):
        t = v * v
        hi = t.astype(BF16)
        lo = (t - hi.astype(F32)).astype(BF16)
        ms = jnp.dot(hi, gmat, preferred_element_type=F32) + jnp.dot(lo, gmat, preferred_element_type=F32)
        return v * lax.rsqrt(ms + EPS) * w

    o = 3 * HY_WIDTH
    for u in range(Q_WIDTH // LANES):
        v = p[:, o + u * LANES:o + (u + 1) * LANES]
        v = rope(head_norm(v, qn_ref[...]))
        gq_ref[0, :, u * LANES:(u + 1) * LANES] = (v * ATTN_SCALE).astype(BF16)
    o += Q_WIDTH
    gk_ref[0] = rope(head_norm(p[:, o:o + KV_WIDTH], kn_ref[...])).astype(BF16)
    o += KV_WIDTH
    gv_ref[0] = p[:, o:o + KV_WIDTH].astype(BF16)
    o += KV_WIDTH
    for u in range(Q_WIDTH // LANES):
        v = rope(p[:, o + u * LANES:o + (u + 1) * LANES])
        wq_ref[0, :, u * LANES:(u + 1) * LANES] = (v * ATTN_SCALE).astype(BF16)
    o += Q_WIDTH
    wk_ref[0] = rope(p[:, o:o + KV_WIDTH]).astype(BF16)
    o += KV_WIDTH
    wv_ref[0] = p[:, o:o + KV_WIDTH].astype(BF16)


def _token_specs(nt, ctx_blk, d):
    return [pl.BlockSpec((1, ROW_TILE, d), lambda b, i: (b, jnp.minimum(i, nt - 1), 0)),
            pl.BlockSpec((1, ROW_TILE, d), lambda b, i: (b, ctx_blk, 0))]


def _inproj(x_lat, x_ctx, ctx_blk, mod, norm_w, w_in, b_in, cos_t, sin_t, q_norm, k_norm, nt):
    bsz, _, d = x_lat.shape
    tm = ROW_TILE
    ntile = nt + 1
    rows = ntile * tm
    gmat = jnp.kron(jnp.eye(LANES // HEAD_DIM, dtype=F32), jnp.full((HEAD_DIM, HEAD_DIM), 1.0 / HEAD_DIM, F32)).astype(BF16)
    qn = jnp.tile(q_norm, LANES // HEAD_DIM).reshape(1, LANES)
    kn = jnp.tile(k_norm, LANES // HEAD_DIM).reshape(1, LANES)
    tok = lambda w: pl.BlockSpec((1, tm, w), lambda b, i: (b, i, 0))
    const = lambda s: pl.BlockSpec(s, lambda b, i: (0,) * len(s))
    out_w = (3 * HY_WIDTH, Q_WIDTH, KV_WIDTH, KV_WIDTH, Q_WIDTH, KV_WIDTH, KV_WIDTH)
    out_dt = (F32, BF16, BF16, BF16, BF16, BF16, BF16)
    return pl.pallas_call(
        functools.partial(_inproj_kernel, nt=nt),
        grid=(bsz, ntile),
        in_specs=_token_specs(nt, ctx_blk, d) + [
            pl.BlockSpec((1, 1, 6, d), lambda b, i: (b, i // nt, 0, 0)),
            const((1, d)),
            const((d, IN_WIDTH)),
            const((1, IN_WIDTH)),
            pl.BlockSpec((tm, LANES), lambda b, i: (i, 0)),
            pl.BlockSpec((tm, LANES), lambda b, i: (i, 0)),
            const((1, LANES)),
            const((1, LANES)),
            const((LANES, LANES)),
        ],
        out_specs=[tok(w) for w in out_w],
        out_shape=[jax.ShapeDtypeStruct((bsz, rows, w), dt) for w, dt in zip(out_w, out_dt)],
        compiler_params=_cp(("arbitrary", "arbitrary")),
        name="inproj",
    )(x_lat, x_ctx, mod, norm_w.reshape(1, d), w_in.astype(BF16), b_in.reshape(1, IN_WIDTH), cos_t, sin_t, qn, kn, gmat)


def _stack_q(q_ref, qs_ref):
    qf = q_ref[0].astype(F32)
    tq = qf.shape[0]
    for j in range(GA_KV_HEADS):
        for g in range(GROUP):
            h = GROUP * j + g
            blk = qf[:, (h // 2) * LANES:(h // 2 + 1) * LANES]
            if h % 2 != j:
                blk = pltpu.roll(blk, HEAD_DIM, 1)
            qs_ref[j, g * tq:(g + 1) * tq, :] = blk.astype(BF16)


def _unstack_o(norm_fn, o_ref, tq):
    lane = lax.broadcasted_iota(I32, (tq, LANES), 1)
    heads = []
    for j in range(GA_KV_HEADS):
        for g in range(GROUP):
            h = GROUP * j + g
            o = norm_fn(j, g)
            if h % 2 != j:
                o = pltpu.roll(o, HEAD_DIM, 1)
            heads.append(o)
    for u in range(Q_WIDTH // LANES):
        o_ref[0, :, u * LANES:(u + 1) * LANES] = jnp.where(lane < HEAD_DIM, heads[2 * u], heads[2 * u + 1])


def _half_masks(rows):
    lane = lax.broadcasted_iota(I32, (rows, LANES), 1)
    return [lane < HEAD_DIM, lane >= HEAD_DIM]


def _dot_t(a, b):
    return lax.dot_general(a, b, (((1,), (1,)), ((), ())), preferred_element_type=F32)


def _ga_kernel(q_ref, k_ref, v_ref, o_ref, qs_ref, m_ref, acc_ref, *, nt, n_lat):
    i = pl.program_id(1)
    tq = q_ref.shape[1]
    _stack_q(q_ref, qs_ref)
    m_ref[...] = jnp.full(m_ref.shape, NEG_INF, F32)
    acc_ref[...] = jnp.zeros(acc_ref.shape, F32)

    def chunk(kc, vc):
        masks = _half_masks(kc.shape[0])
        for j in range(GA_KV_HEADS):
            km = jnp.where(masks[j], kc, jnp.zeros_like(kc))
            vm = jnp.where(masks[j], vc, jnp.ones_like(vc))
            s = _dot_t(qs_ref[j], km)
            m_old = m_ref[j]
            m_new = jnp.maximum(m_old, jnp.max(s, axis=-1, keepdims=True))
            p = jnp.exp((s - jnp.tile(m_new, (1, s.shape[1] // LANES))).astype(BF16))
            acc_ref[j] = jnp.exp(m_old - m_new) * acc_ref[j] + jnp.dot(p, vm, preferred_element_type=F32)
            m_ref[j] = m_new

    @pl.when(i < nt)
    def _():
        rows = k_ref.shape[1]
        for st in range(0, n_lat, KV_CHUNK):
            en = rows if st + KV_CHUNK >= n_lat else st + KV_CHUNK
            chunk(k_ref[0, st:en, :], v_ref[0, st:en, :])

    @pl.when(i >= nt)
    def _():
        chunk(k_ref[0, n_lat:, :], v_ref[0, n_lat:, :])

    def norm_fn(j, g):
        a = acc_ref[j, g * tq:(g + 1) * tq, :]
        return a / pltpu.roll(a, HEAD_DIM, 1)

    _unstack_o(norm_fn, o_ref, tq)


def _global_attention(gq, gk, gv, nt, ntile):
    bsz, rows, _ = gq.shape
    tq = ROW_TILE
    n_lat = nt * tq
    return pl.pallas_call(
        functools.partial(_ga_kernel, nt=nt, n_lat=n_lat),
        grid=(bsz, ntile),
        in_specs=[
            pl.BlockSpec((1, tq, Q_WIDTH), lambda b, i: (b, i, 0)),
            pl.BlockSpec((1, rows, KV_WIDTH), lambda b, i: (b, 0, 0)),
            pl.BlockSpec((1, rows, KV_WIDTH), lambda b, i: (b, 0, 0)),
        ],
        out_specs=pl.BlockSpec((1, tq, Q_WIDTH), lambda b, i: (b, i, 0)),
        out_shape=jax.ShapeDtypeStruct((bsz, ntile * tq, Q_WIDTH), F32),
        scratch_shapes=[
            pltpu.VMEM((GA_KV_HEADS, GROUP * tq, LANES), BF16),
            pltpu.VMEM((GA_KV_HEADS, GROUP * tq, LANES), F32),
            pltpu.VMEM((GA_KV_HEADS, GROUP * tq, LANES), F32),
        ],
        compiler_params=_cp(("arbitrary", "arbitrary")),
        name="global_attn",
    )(gq, gk, gv)


def _wa_kernel(q_ref, k_ref, v_ref, sink_ref, bias_ref, o_ref, qs_ref, acc_ref, l_ref, *, nt, n_lat):
    i = pl.program_id(1)
    tq = q_ref.shape[1]
    band = tq + 2 * WINDOW
    _stack_q(q_ref, qs_ref)
    st = pl.multiple_of(i * tq, tq)
    kb = k_ref[0, pl.ds(st, band), :]
    vb = v_ref[0, pl.ds(st, band), :]
    kc = k_ref[0, WINDOW + n_lat:WINDOW + n_lat + ROW_TILE, :]
    vc = v_ref[0, WINDOW + n_lat:WINDOW + n_lat + ROW_TILE, :]

    kpos = lax.broadcasted_iota(I32, (1, band), 1) + (i * tq - WINDOW)
    in_seq = (kpos >= 0) & (kpos < n_lat) & (i < nt)
    bias = bias_ref[...]

    mb = _half_masks(band)
    mc = _half_masks(ROW_TILE)
    for j in range(WA_KV_HEADS):
        q = qs_ref[j]
        s_loc = jnp.where(in_seq, _dot_t(q, jnp.where(mb[j], kb, jnp.zeros_like(kb))) + bias, NEG_INF)
        s_ctx = _dot_t(q, jnp.where(mc[j], kc, jnp.zeros_like(kc)))
        sink = jnp.concatenate(
            [jnp.broadcast_to(sink_ref[GROUP * j + g:GROUP * j + g + 1, :], (tq, LANES)) for g in range(GROUP)], axis=0)
        m = jnp.maximum(jnp.maximum(jnp.max(s_loc, axis=-1, keepdims=True), jnp.max(s_ctx, axis=-1, keepdims=True)), sink)
        p_loc = jnp.exp(s_loc - jnp.tile(m, (1, band // LANES))).astype(BF16)
        p_ctx = jnp.exp(s_ctx - jnp.tile(m, (1, ROW_TILE // LANES))).astype(BF16)
        acc_ref[j] = (jnp.dot(p_loc, jnp.where(mb[j], vb, jnp.ones_like(vb)), preferred_element_type=F32)
                      + jnp.dot(p_ctx, jnp.where(mc[j], vc, jnp.ones_like(vc)), preferred_element_type=F32))
        l_ref[j] = jnp.exp(sink - m)

    def norm_fn(j, g):
        a = acc_ref[j, g * tq:(g + 1) * tq, :]
        return a / (pltpu.roll(a, HEAD_DIM, 1) + l_ref[j, g * tq:(g + 1) * tq, :])

    _unstack_o(norm_fn, o_ref, tq)


def _window_attention(wq, wk, wv, sink, nt, ntile):
    bsz, rows, _ = wq.shape
    tq = ROW_TILE
    n_lat = nt * tq
    pad = ((0, 0), (WINDOW, WINDOW), (0, 0))
    kp = jnp.pad(wk, pad)
    vp = jnp.pad(wv, pad)
    sink_b = jnp.zeros((8, LANES), F32).at[:WA_HEADS].set(jnp.broadcast_to(sink[:, None], (WA_HEADS, LANES)))
    band = tq + 2 * WINDOW
    rel = jnp.arange(band, dtype=I32)[None, :] - WINDOW - (jnp.arange(GROUP * tq, dtype=I32) % tq)[:, None]
    band_bias = jnp.where(jnp.abs(rel) <= WINDOW, 0.0, NEG_INF).astype(F32)
    return pl.pallas_call(
        functools.partial(_wa_kernel, nt=nt, n_lat=n_lat),
        grid=(bsz, ntile),
        in_specs=[
            pl.BlockSpec((1, tq, Q_WIDTH), lambda b, i: (b, i, 0)),
            pl.BlockSpec((1, rows + 2 * WINDOW, KV_WIDTH), lambda b, i: (b, 0, 0)),
            pl.BlockSpec((1, rows + 2 * WINDOW, KV_WIDTH), lambda b, i: (b, 0, 0)),
            pl.BlockSpec((8, LANES), lambda b, i: (0, 0)),
            pl.BlockSpec((GROUP * tq, band), lambda b, i: (0, 0)),
        ],
        out_specs=pl.BlockSpec((1, tq, Q_WIDTH), lambda b, i: (b, i, 0)),
        out_shape=jax.ShapeDtypeStruct((bsz, ntile * tq, Q_WIDTH), F32),
        scratch_shapes=[
            pltpu.VMEM((WA_KV_HEADS, GROUP * tq, LANES), BF16),
            pltpu.VMEM((WA_KV_HEADS, GROUP * tq, LANES), F32),
            pltpu.VMEM((WA_KV_HEADS, GROUP * tq, LANES), F32),
        ],
        compiler_params=_cp(("arbitrary", "arbitrary")),
        name="window_attn",
    )(wq, kp, vp, sink_b, band_bias)


def _store_row_tiles(ref, v):
    rows = v.shape[0]
    for s in range(v.shape[1] // LANES):
        ref[pl.ds(s, rows, stride=8), :] = v[:, s * LANES:(s + 1) * LANES]


def _load_row_tiles(ref, row0, rows):
    return jnp.concatenate([ref[pl.ds(row0 * 8 + s, rows, stride=8), :] for s in range(8)], axis=1)


def _merge_kernel(xl_ref, xc_ref, hy_ref, ga_ref, wa_ref, mod_ref, bn_ref, wo_ref, bo_ref, nf_ref, wrh_ref, wrl_ref, br_ref,
                  tri_ref, xo_ref, h2_ref, tg_ref, te_ref, tr_ref, cnt_ref, base_ref, *, nt):
    @pl.when((pl.program_id(0) == 0) & (pl.program_id(1) == 0))
    def _():
        base_ref[...] = jnp.zeros(base_ref.shape, F32)

    bn = bn_ref[...]
    g0 = HY_WIDTH
    g1 = HY_WIDTH + Q_WIDTH
    y = jnp.concatenate([_rms(hy_ref[0], bn[:, :g0]), _rms(ga_ref[0], bn[:, g0:g1]), _rms(wa_ref[0], bn[:, g1:])], axis=-1)
    o = jnp.dot(y.astype(BF16), wo_ref[...], preferred_element_type=F32) + bo_ref[...]
    x1 = jnp.where(pl.program_id(1) < nt, xl_ref[0], xc_ref[0]) + mod_ref[0, 0, 2:3, :] * o
    xo_ref[0] = x1
    h2 = _rms(x1, nf_ref[...]) * (1.0 + mod_ref[0, 0, 4:5, :]) + mod_ref[0, 0, 3:4, :]
    _store_row_tiles(h2_ref, h2)

    hi = h2.astype(BF16)
    lo = (h2 - hi.astype(F32)).astype(BF16)
    logits = (jnp.dot(hi, wrh_ref[...], preferred_element_type=F32) + jnp.dot(lo, wrh_ref[...], preferred_element_type=F32)
              + jnp.dot(hi, wrl_ref[...], preferred_element_type=F32)) + br_ref[...]
    tm = logits.shape[0]
    lane = lax.broadcasted_iota(I32, (tm, ROUTER_LANES), 1)
    te = jnp.zeros((tm, ROUTER_LANES), F32)
    tr = jnp.zeros((tm, ROUTER_LANES), F32)
    tv = jnp.zeros((tm, ROUTER_LANES), F32)
    before = base_ref[...]
    tri = tri_ref[...]
    v0 = None
    for k in range(TOP_K):
        mx = jnp.max(logits, axis=-1, keepdims=True)
        idx = jnp.min(jnp.where(logits == mx, lane, ROUTER_LANES), axis=-1, keepdims=True)
        if k == 0:
            v0 = mx
        hit = lane == idx
        onehot = jnp.where(hit, 1.0, 0.0)
        prefix = jnp.dot(tri, onehot.astype(BF16), preferred_element_type=F32) + before
        rank = jnp.sum(onehot * prefix, axis=-1, keepdims=True)
        before = before + jnp.sum(onehot, axis=0, keepdims=True)
        te = jnp.where(lane == k, idx.astype(F32), te)
        tr = jnp.where(lane == k, rank, tr)
        tv = jnp.where(lane == k, jnp.exp(mx - v0), tv)
        logits = jnp.where(hit, NEG_INF, logits)
    base_ref[...] = before
    cnt_ref[...] = before
    tg_ref[0] = tv / jnp.sum(tv, axis=-1, keepdims=True)
    te_t = te.T
    tr_t = tr.T
    for k in range(TOP_K):
        te_ref[0, :, k * tm:(k + 1) * tm] = te_t[k:k + 1, :].astype(I32)
        tr_ref[0, :, k * tm:(k + 1) * tm] = tr_t[k:k + 1, :].astype(I32)


def _merge(x_lat, x_ctx, ctx_blk, y_hy, y_ga, y_wa, mod, branch_norm, w_out, b_out, norm_ffn, w_router, b_router, nt, ntile):
    bsz, _, d = x_lat.shape
    tm = ROW_TILE
    rows = ntile * tm
    wr = jnp.zeros((d, ROUTER_LANES), F32).at[:, :N_EXPERTS].set(w_router)
    wrh = wr.astype(BF16)
    wrl = (wr - wrh.astype(F32)).astype(BF16)
    br = jnp.full((1, ROUTER_LANES), NEG_INF, F32).at[0, :N_EXPERTS].set(b_router)
    tri = (jnp.arange(tm, dtype=I32)[:, None] > jnp.arange(tm, dtype=I32)[None, :]).astype(BF16)
    tok = lambda w: pl.BlockSpec((1, tm, w), lambda b, i: (b, i, 0))
    const = lambda s: pl.BlockSpec(s, lambda b, i: (0,) * len(s))
    flat = pl.BlockSpec((1, 1, TOP_K * tm), lambda b, i: (b * ntile + i, 0, 0))
    return pl.pallas_call(
        functools.partial(_merge_kernel, nt=nt),
        grid=(bsz, ntile),
        in_specs=_token_specs(nt, ctx_blk, d) + [
            tok(HY_WIDTH), tok(Q_WIDTH), tok(Q_WIDTH),
            pl.BlockSpec((1, 1, 6, d), lambda b, i: (b, i // nt, 0, 0)),
            const((1, MIX_WIDTH)), const((MIX_WIDTH, d)), const((1, d)), const((1, d)),
            const((d, ROUTER_LANES)), const((d, ROUTER_LANES)), const((1, ROUTER_LANES)), const((tm, tm)),
        ],
        out_specs=[tok(d), pl.BlockSpec((None, tm * 8, LANES), lambda b, i: (b, i, 0)), tok(ROUTER_LANES), flat, flat,
                   const((1, ROUTER_LANES))],
        out_shape=[
            jax.ShapeDtypeStruct((bsz, rows, d), F32),
            jax.ShapeDtypeStruct((bsz, rows * 8, LANES), F32),
            jax.ShapeDtypeStruct((bsz, rows, ROUTER_LANES), F32),
            jax.ShapeDtypeStruct((bsz * ntile, 1, TOP_K * tm), I32),
            jax.ShapeDtypeStruct((bsz * ntile, 1, TOP_K * tm), I32),
            jax.ShapeDtypeStruct((1, ROUTER_LANES), F32),
        ],
        scratch_shapes=[pltpu.VMEM((1, ROUTER_LANES), F32)],
        compiler_params=_cp(("arbitrary", "arbitrary")),
        name="merge_router",
    )(x_lat, x_ctx, y_hy, y_ga, y_wa, mod, branch_norm.reshape(1, MIX_WIDTH), w_out.astype(BF16), b_out.reshape(1, d),
      norm_ffn.reshape(1, d), wrh, wrl, br, tri)


def _moe_plan(te, tr, cnt, tm):
    steps, _, per = te.shape
    n_blocks = -(-(steps * per) // tm) + N_EXPERTS
    counts = cnt[0, :N_EXPERTS].astype(I32)
    nblk = (counts + tm - 1) // tm
    cend = jnp.cumsum(nblk)
    cstart = cend - nblk
    experts = jnp.arange(N_EXPERTS, dtype=I32)
    row = jnp.sum(jnp.where(te[..., None] == experts, cstart * tm, 0), axis=-1) + tr
    j = jnp.arange(n_blocks, dtype=I32)
    block_expert = jnp.minimum(jnp.sum((cend[None, :] <= j[:, None]).astype(I32), axis=1), N_EXPERTS - 1)
    past = cend[-1] + experts
    zero_blocks = jnp.concatenate([jnp.where(nblk > 0, cend - 1, -1), jnp.where(past < n_blocks, past, -1)])
    return row, block_expert, cend[-1].reshape(1), zero_blocks


def _dispatch_kernel(zb_ref, row_ref, h_ref, x_hbm, buf0, buf1, sem, *, block_rows):
    step = pl.program_id(0)
    tm = h_ref.shape[0] // 8
    per_block = block_rows // tm

    def wait(buf, s):
        for k in range(TOP_K):
            pltpu.make_async_copy(buf, x_hbm.at[pl.ds(0, tm * 8), :], sem.at[s, k]).wait()

    @pl.when(step == 0)
    def _():
        buf1[...] = jnp.zeros(buf1.shape, F32)
        def piece(e, q):
            dst = pl.multiple_of((zb_ref[e] * per_block + q) * (tm * 8), tm * 8)
            return pltpu.make_async_copy(buf1, x_hbm.at[pl.ds(dst, tm * 8), :], sem.at[1, 0])

        for phase in ("start", "wait"):
            for e in range(zb_ref.shape[0]):
                @pl.when(zb_ref[e] >= 0)
                def _():
                    for q in range(per_block):
                        getattr(piece(e, q), phase)()

    def run(cur, prev, sc, sp):
        cur[...] = h_ref[...]
        for k in range(TOP_K):
            for r in range(tm):
                dst = pl.multiple_of(row_ref[0, 0, k * tm + r], 8)
                pltpu.make_async_copy(cur.at[pl.ds(r * 8, 8), :], x_hbm.at[pl.ds(dst, 8), :], sem.at[sc, k]).start()

        @pl.when(step > 0)
        def _():
            wait(prev, sp)

        @pl.when(step == pl.num_programs(0) - 1)
        def _():
            wait(cur, sc)

    @pl.when(step % 2 == 0)
    def _():
        run(buf0, buf1, 0, 1)

    @pl.when(step % 2 == 1)
    def _():
        run(buf1, buf0, 1, 0)


def _dispatch(h_tiles, rows_e, zero_blocks, n_blocks, block_rows):
    steps = rows_e.shape[0]
    tm = ROW_TILE
    grid_spec = pltpu.PrefetchScalarGridSpec(
        num_scalar_prefetch=1,
        grid=(steps,),
        in_specs=[
            pl.BlockSpec((1, 1, TOP_K * tm), lambda i, zb: (i, 0, 0), memory_space=pltpu.SMEM),
            pl.BlockSpec((tm * 8, LANES), lambda i, zb: (i, 0)),
        ],
        out_specs=pl.BlockSpec(memory_space=pl.ANY),
        scratch_shapes=[pltpu.VMEM((tm * 8, LANES), F32), pltpu.VMEM((tm * 8, LANES), F32),
                        pltpu.SemaphoreType.DMA((2, TOP_K))],
    )
    return pl.pallas_call(
        functools.partial(_dispatch_kernel, block_rows=block_rows),
        grid_spec=grid_spec,
        out_shape=jax.ShapeDtypeStruct((n_blocks * block_rows * 8, LANES), F32),
        compiler_params=_cp(("arbitrary",)),
        name="moe_dispatch",
    )(zero_blocks, rows_e * 8, h_tiles)


def _ffn_kernel(be_ref, nu_ref, x_ref, w1_ref, b1_ref, w2_ref, b2_ref, y_ref, w1b, w2b):
    i = pl.program_id(0)
    tm = x_ref.shape[0] // 8
    f = w2_ref.shape[1]

    @pl.when(i < nu_ref[0])
    def _():
        @pl.when((i == 0) | (be_ref[i] != be_ref[jnp.maximum(i - 1, 0)]))
        def _():
            w1b[...] = w1_ref[0].astype(BF16)
            w2b[...] = w2_ref[0].astype(BF16)

        a = jnp.dot(_load_row_tiles(x_ref, 0, tm).astype(BF16), w1b[...], preferred_element_type=F32) + b1_ref[0]
        glu = jnp.minimum(a[:, :f], SWIGLU_LIMIT)
        lin = jnp.clip(a[:, f:], -SWIGLU_LIMIT, SWIGLU_LIMIT)
        act = glu / (1.0 + jnp.exp(-SWIGLU_ALPHA * glu)) * (lin + 1.0)
        _store_row_tiles(y_ref, jnp.dot(act.astype(BF16), w2b[...], preferred_element_type=F32) + b2_ref[0])

    @pl.when(i >= nu_ref[0])
    def _():
        y_ref[...] = jnp.zeros(y_ref.shape, F32)


def _moe_ffn(x_buf, block_expert, n_used, w1, b1, w2, b2, layer, tm):
    n_blocks = block_expert.shape[0]
    _, n_exp, d, f2 = w1.shape
    f = w2.shape[2]
    grid_spec = pltpu.PrefetchScalarGridSpec(
        num_scalar_prefetch=2,
        grid=(n_blocks,),
        in_specs=[
            pl.BlockSpec((tm * 8, LANES), lambda i, be, nu: (i, 0)),
            pl.BlockSpec((None, 1, d, f2), lambda i, be, nu: (layer, be[i], 0, 0)),
            pl.BlockSpec((1, 1, f2), lambda i, be, nu: (be[i], 0, 0)),
            pl.BlockSpec((None, 1, f, d), lambda i, be, nu: (layer, be[i], 0, 0)),
            pl.BlockSpec((1, 1, d), lambda i, be, nu: (be[i], 0, 0)),
        ],
        out_specs=pl.BlockSpec((tm * 8, LANES), lambda i, be, nu: (i, 0)),
        scratch_shapes=[pltpu.VMEM((d, f2), BF16), pltpu.VMEM((f, d), BF16)],
    )
    return pl.pallas_call(
        _ffn_kernel,
        grid_spec=grid_spec,
        out_shape=jax.ShapeDtypeStruct((n_blocks * tm * 8, LANES), F32),
        compiler_params=_cp(("arbitrary",)),
        name="moe_ffn",
    )(block_expert, n_used, x_buf, w1, b1.reshape(n_exp, 1, f2), w2, b2.reshape(n_exp, 1, d))


def _combine_kernel(slot_ref, slotn_ref, y_hbm, x_ref, mod_ref, gate_ref, nf_ref, o_ref, ybuf0, ybuf1, sem, *, final):
    b = pl.program_id(0)
    i = pl.program_id(1)
    step = b * pl.num_programs(1) + i
    n_steps = pl.num_programs(0) * pl.num_programs(1)
    tc = x_ref.shape[1]
    n_rows = TOP_K * tc

    def issue(idx_ref, buf, s):
        for j in range(n_rows):
            src = pl.multiple_of(idx_ref[0, 0, j], 8)
            pltpu.make_async_copy(y_hbm.at[pl.ds(src, 8), :], buf.at[pl.ds(j * 8, 8), :], sem.at[s]).start()

    def wait(buf, s):
        pltpu.make_async_copy(y_hbm.at[pl.ds(0, n_rows * 8), :], buf, sem.at[s]).wait()

    @pl.when(step == 0)
    def _():
        issue(slot_ref, ybuf0, 0)

    def run(cur, nxt, sc, sn):
        issue(slotn_ref, nxt, sn)
        wait(cur, sc)
        gates = gate_ref[0]
        acc = gates[:, 0:1] * _load_row_tiles(cur, 0, tc)
        for k in range(1, TOP_K):
            acc += gates[:, k:k + 1] * _load_row_tiles(cur, k * tc, tc)
        out = x_ref[0] + mod_ref[0, 0, 5:6, :] * acc
        if final:
            out = _rms(out, nf_ref[...])
        o_ref[0] = out

        @pl.when(step == n_steps - 1)
        def _():
            wait(nxt, sn)

    @pl.when(step % 2 == 0)
    def _():
        run(ybuf0, ybuf1, 0, 1)

    @pl.when(step % 2 == 1)
    def _():
        run(ybuf1, ybuf0, 1, 0)


def _combine(y_sorted, slot, x_mid, mod, gates, norm_final, nt, ntile, final):
    bsz, _, d = x_mid.shape
    tc = ROW_TILE
    n_steps = bsz * ntile
    slots = slot * 8
    tok = lambda w: pl.BlockSpec((1, tc, w), lambda b, i: (b, i, 0))
    return pl.pallas_call(
        functools.partial(_combine_kernel, final=final),
        grid=(bsz, ntile),
        in_specs=[
            pl.BlockSpec((1, 1, tc * TOP_K), lambda b, i: (b * ntile + i, 0, 0), memory_space=pltpu.SMEM),
            pl.BlockSpec((1, 1, tc * TOP_K), lambda b, i: (jnp.minimum(b * ntile + i + 1, n_steps - 1), 0, 0),
                         memory_space=pltpu.SMEM),
            pl.BlockSpec(memory_space=pl.ANY),
            tok(d),
            pl.BlockSpec((1, 1, 6, d), lambda b, i: (b, i // nt, 0, 0)),
            tok(ROUTER_LANES),
            pl.BlockSpec((1, d), lambda b, i: (0, 0)),
        ],
        out_specs=tok(d),
        out_shape=jax.ShapeDtypeStruct((bsz, ntile * tc, d), F32),
        scratch_shapes=[pltpu.VMEM((TOP_K * tc * 8, LANES), F32), pltpu.VMEM((TOP_K * tc * 8, LANES), F32),
                        pltpu.SemaphoreType.DMA((2,))],
        compiler_params=_cp(("arbitrary", "arbitrary")),
        name="moe_combine",
    )(slots, slots, y_sorted, x_mid, mod, gates, norm_final.reshape(1, d))


def _split_dot(a, w):
    ah = a.astype(BF16)
    al = (a - ah.astype(F32)).astype(BF16)
    wh = w.astype(BF16)
    wl = (w - wh.astype(F32)).astype(BF16)
    return (jnp.dot(ah, wh, preferred_element_type=F32) + jnp.dot(al, wh, preferred_element_type=F32)
            + jnp.dot(ah, wl, preferred_element_type=F32))


def _filt_kernel(z_ref, w1_ref, b1_ref, w2_ref, b2_ref, w3_ref, b3_ref, fr_ref, wo_ref, dec_ref, o_ref):
    fr = fr_ref[...]
    h = jnp.sin(fr * (_split_dot(z_ref[...], w1_ref[...]) + b1_ref[...]))
    h = jnp.sin(fr * (_split_dot(h, w2_ref[...]) + b2_ref[...]))
    h = jnp.sin(fr * (_split_dot(h, w3_ref[...]) + b3_ref[...]))
    o_ref[...] = _split_dot(h, wo_ref[...]) * dec_ref[...]


def _hyena_filters_pl(n, filt):
    w1, b1, w2, b2, w3, b3, freq, w_out = filt
    fw = w2.shape[0]
    t = jnp.linspace(0.0, 1.0, n, dtype=F32)[:, None]
    wpos = (2.0 * math.pi / n) * jnp.arange(n, dtype=F32)[:, None]
    bands = jnp.linspace(1e-4, HY_BANDS - 1, HY_BANDS, dtype=F32)
    z = jnp.concatenate([t, jnp.cos(wpos * bands), -jnp.sin(wpos * bands), jnp.zeros((n, fw - HY_EMB), F32)], axis=-1)
    w1p = jnp.concatenate([w1, jnp.zeros((fw - HY_EMB, fw), F32)], axis=0)
    max_decay = math.log(HY_TARGET) / HY_FAST_DECAY
    min_decay = math.log(HY_TARGET) / HY_SLOW_DECAY
    deltas = jnp.linspace(min_decay, max_decay, HY_WIDTH, dtype=F32)
    decay = jnp.tile(jnp.exp(-t * jnp.abs(deltas)), (1, 2))
    tm = min(n, 512)
    row = lambda w: pl.BlockSpec((tm, w), lambda i: (i, 0))
    const = lambda s: pl.BlockSpec(s, lambda i: (0, 0))
    return pl.pallas_call(
        _filt_kernel,
        grid=(n // tm,),
        in_specs=[row(fw), const((fw, fw)), const((1, fw)), const((fw, fw)), const((1, fw)), const((fw, fw)), const((1, fw)),
                  const((1, fw)), const((fw, 2 * HY_WIDTH)), row(2 * HY_WIDTH)],
        out_specs=row(2 * HY_WIDTH),
        out_shape=jax.ShapeDtypeStruct((n, 2 * HY_WIDTH), F32),
        compiler_params=_cp(("arbitrary",)),
        name="hyena_filters",
    )(z, w1p, b1.reshape(1, fw), w2, b2.reshape(1, fw), w3, b3.reshape(1, fw), freq.reshape(1, fw), w_out, decay)


GATE_STEPS = 4


def _gate_kernel(u_ref, prev_ref, next_ref, w_ref, b_ref, z_ref, x0_ref, *, n_lat):
    i = pl.program_id(1)
    u = u_ref[0]
    tm = u.shape[0]
    n_rows = pl.num_programs(1) * tm
    w = w_ref[...]
    row = lax.broadcasted_iota(I32, u.shape, 0)
    pos = row + i * tm
    up = jnp.where(row == 0, prev_ref[0, 7:8, :], pltpu.roll(u, 1, 0))
    up = jnp.where((pos == 0) | (pos == n_lat), 0.0, up)
    un = jnp.where(row == tm - 1, next_ref[0, 0:1, :], pltpu.roll(u, tm - 1, 0))
    un = jnp.where((pos == n_lat - 1) | (pos == n_rows - 1), 0.0, un)
    uc = up * w[0:1, :] + u * w[1:2, :] + un * w[2:3, :] + b_ref[...]
    x0_ref[0] = uc[:, :HY_WIDTH]
    z_ref[0] = uc[:, 2 * HY_WIDTH:] * uc[:, HY_WIDTH:2 * HY_WIDTH]


def _hyena_gate(hy, conv_w, conv_b, n_lat):
    bsz, rows, w3 = hy.shape
    tm = rows // GATE_STEPS
    sub = tm // 8
    last8 = rows // 8 - 1
    out = pl.BlockSpec((1, tm, HY_WIDTH), lambda b, i: (b, i, 0))
    return pl.pallas_call(
        functools.partial(_gate_kernel, n_lat=n_lat),
        grid=(bsz, GATE_STEPS),
        in_specs=[
            pl.BlockSpec((1, tm, w3), lambda b, i: (b, i, 0)),
            pl.BlockSpec((1, 8, w3), lambda b, i: (b, jnp.maximum(i * sub - 1, 0), 0)),
            pl.BlockSpec((1, 8, w3), lambda b, i: (b, jnp.minimum((i + 1) * sub, last8), 0)),
            pl.BlockSpec((HY_SHORT, w3), lambda b, i: (0, 0)),
            pl.BlockSpec((1, w3), lambda b, i: (0, 0)),
        ],
        out_specs=[out, out],
        out_shape=[jax.ShapeDtypeStruct((bsz, rows, HY_WIDTH), F32)] * 2,
        compiler_params=_cp(("arbitrary", "arbitrary")),
        name="hyena_gate",
    )(hy, hy, hy, conv_w, conv_b.reshape(1, w3))


FFT_N2 = LANES


def _dft_tables(n_fft):
    n1 = n_fft // FFT_N2
    nk = min(n1, -(-(n1 // 2 + 1) // 8) * 8)
    k1 = jnp.arange(nk, dtype=I32)
    nn = jnp.arange(n1, dtype=I32)
    a = (2.0 * math.pi / n1) * ((k1[:, None] * nn[None, :]) % n1).astype(F32)
    f1 = jnp.stack([jnp.cos(a), -jnp.sin(a)], axis=1).reshape(2 * nk, n1)
    w = jnp.where((k1 == 0) | (2 * k1 == n1), 1.0, jnp.where(2 * k1 < n1, 2.0, 0.0)) / n_fft
    g = jnp.stack([jnp.cos(a.T) * w, -jnp.sin(a.T) * w], axis=2).reshape(n1, 2 * nk)
    k2 = jnp.arange(FFT_N2, dtype=I32)
    m = (k2[None, None, :] * (k1[:, None, None] + n1 * k2[None, :, None])) % n_fft
    th = (2.0 * math.pi / n_fft) * m.astype(F32)
    mre, mim = jnp.cos(th), -jnp.sin(th)
    mr = jnp.concatenate([jnp.concatenate([mre, -mim], axis=2), jnp.concatenate([mim, mre], axis=2)], axis=1)
    return f1.astype(BF16), g.astype(BF16), mr.astype(BF16), jnp.swapaxes(mr, 1, 2).astype(BF16)


def _fft1_kernel(x_ref, f_ref, o_ref, *, n1_in):
    f = f_ref[...]
    n_rows = f.shape[0]

    def body(n2, carry):
        xs = x_ref[pl.ds(n2, n1_in, stride=FFT_N2), :].astype(BF16)
        o_ref[pl.ds(n2, n_rows, stride=FFT_N2), :] = jnp.dot(f, xs, preferred_element_type=F32)
        return carry
    lax.fori_loop(0, FFT_N2, body, 0, unroll=8)


def _fft1(x, f1, n1_in):
    bsz, _, ch = x.shape
    n_rows = f1.shape[0]
    return pl.pallas_call(
        functools.partial(_fft1_kernel, n1_in=n1_in),
        grid=(bsz, ch // LANES),
        in_specs=[pl.BlockSpec((None, n1_in * FFT_N2, LANES), lambda b, h: (b, 0, h)),
                  pl.BlockSpec((n_rows, n1_in), lambda b, h: (0, 0))],
        out_specs=pl.BlockSpec((None, n_rows * FFT_N2, LANES), lambda b, h: (b, 0, h)),
        out_shape=jax.ShapeDtypeStruct((bsz, n_rows * FFT_N2, ch), F32),
        compiler_params=_cp(("arbitrary", "arbitrary")),
        name="hyena_fft1",
    )(x, f1[:, :n1_in])


def _spec_kernel(a_ref, mr_ref, o_ref):
    o_ref[0] = jnp.dot(mr_ref[0], a_ref[0].astype(BF16), preferred_element_type=F32)


def _fft_spectrum(a, mr):
    _, rows, ch = a.shape
    blk = 2 * FFT_N2
    return pl.pallas_call(
        _spec_kernel,
        grid=(rows // blk,),
        in_specs=[pl.BlockSpec((1, blk, ch), lambda k: (0, k, 0)), pl.BlockSpec((1, blk, blk), lambda k: (k, 0, 0))],
        out_specs=pl.BlockSpec((1, blk, ch), lambda k: (0, k, 0)),
        out_shape=jax.ShapeDtypeStruct((1, rows, ch), F32),
        compiler_params=_cp(("arbitrary",)),
        name="hyena_filter_spectrum",
    )(a, mr)


def _mid_kernel(a_ref, mr_ref, mrt_ref, ks_ref, o_ref):
    kre, kim = ks_ref[0, :FFT_N2, :], ks_ref[0, FFT_N2:, :]
    for b in range(a_ref.shape[0]):
        x = jnp.dot(mr_ref[0], a_ref[b].astype(BF16), preferred_element_type=F32)
        xre, xim = x[:FFT_N2], x[FFT_N2:]
        y = jnp.concatenate([xre * kre - xim * kim, xre * kim + xim * kre], axis=0)
        o_ref[b] = jnp.dot(mrt_ref[0], y.astype(BF16), preferred_element_type=F32)


def _fft_mid(a, mr, mrt, kspec):
    bsz, rows, ch = a.shape
    blk = 2 * FFT_N2
    return pl.pallas_call(
        _mid_kernel,
        grid=(rows // blk,),
        in_specs=[pl.BlockSpec((bsz, blk, ch), lambda k: (0, k, 0)),
                  pl.BlockSpec((1, blk, blk), lambda k: (k, 0, 0)),
                  pl.BlockSpec((1, blk, blk), lambda k: (k, 0, 0)),
                  pl.BlockSpec((1, blk, ch), lambda k: (0, k, 0))],
        out_specs=pl.BlockSpec((bsz, blk, ch), lambda k: (0, k, 0)),
        out_shape=jax.ShapeDtypeStruct((bsz, rows, ch), F32),
        compiler_params=_cp(("arbitrary",)),
        name="hyena_fft_mid",
    )(a, mr, mrt, kspec)


def _ifft1_kernel(b_ref, g_ref, z_ref, x0_ref, skip_ref, o_ref, *, n1_out):
    g = g_ref[...]
    n_rows = g.shape[1]
    skip = skip_ref[...]

    def body(n2, carry):
        bs = b_ref[pl.ds(n2, n_rows, stride=FFT_N2), :].astype(BF16)
        y = jnp.dot(g, bs, preferred_element_type=F32)
        rows = pl.ds(n2, n1_out, stride=FFT_N2)
        zs = z_ref[rows, :]
        o_ref[rows, :] = x0_ref[rows, :] * (y + zs * skip)
        return carry
    lax.fori_loop(0, FFT_N2, body, 0, unroll=8)


def _ifft1(bmat, g, z, x0, skip, n1_out):
    bsz, rows, ch = bmat.shape
    n = n1_out * FFT_N2
    seq = pl.BlockSpec((None, n, LANES), lambda b, h: (b, 0, h))
    return pl.pallas_call(
        functools.partial(_ifft1_kernel, n1_out=n1_out),
        grid=(bsz, ch // LANES),
        in_specs=[pl.BlockSpec((None, rows, LANES), lambda b, h: (b, 0, h)),
                  pl.BlockSpec((n1_out, g.shape[1]), lambda b, h: (0, 0)),
                  seq, seq, pl.BlockSpec((1, LANES), lambda b, h: (0, h))],
        out_specs=seq,
        out_shape=jax.ShapeDtypeStruct((bsz, n, ch), F32),
        compiler_params=_cp(("arbitrary", "arbitrary")),
        name="hyena_ifft1",
    )(bmat, g[:n1_out], z, x0, skip.reshape(1, ch))


def _hyena_fftconv(z, x0, k2, skip):
    n = k2.shape[0]
    n_fft = 2 * n
    f1, g, mr, mrt = _dft_tables(n_fft)
    k_full = jnp.concatenate([k2[:, :HY_WIDTH], k2[::-1, HY_WIDTH:]], axis=0)[None]
    kspec = _fft_spectrum(_fft1(k_full, f1, n_fft // FFT_N2), mr)
    a = _fft1(z, f1, n // FFT_N2)
    bmat = _fft_mid(a, mr, mrt, kspec)
    return _ifft1(bmat, g, z, x0, skip, n // FFT_N2)


def _ctxconv_kernel(z_ref, x0_ref, kf_ref, fd_ref, gd_ref, skip_ref, o_ref, ks_ref):
    n = z_ref.shape[1]
    half = fd_ref.shape[0] // 2

    @pl.when(pl.program_id(0) == 0)
    def _():
        ks_ref[...] = jnp.dot(fd_ref[...], kf_ref[...].astype(BF16), preferred_element_type=F32)

    z = z_ref[0]
    x = jnp.dot(fd_ref[:, :n], z.astype(BF16), preferred_element_type=F32)
    xre, xim = x[:half], x[half:]
    kre, kim = ks_ref[:half, :], ks_ref[half:, :]
    y = jnp.concatenate([xre * kre - xim * kim, xre * kim + xim * kre], axis=0)
    conv = jnp.dot(gd_ref[...], y.astype(BF16), preferred_element_type=F32)
    o_ref[0] = x0_ref[0] * (conv + z * skip_ref[...])


def _hyena_ctxconv(z, x0, k2, skip, row0):
    n = k2.shape[0]
    bsz, _, ch = z.shape
    n_fft = 2 * n
    kk = jnp.arange(n_fft, dtype=I32)
    th = (2.0 * math.pi / n_fft) * ((kk[:, None] * kk[None, :]) % n_fft).astype(F32)
    fd = jnp.concatenate([jnp.cos(th), -jnp.sin(th)], axis=0).astype(BF16)
    gd = (jnp.concatenate([jnp.cos(th[:n]), -jnp.sin(th[:n])], axis=1) / n_fft).astype(BF16)
    k_full = jnp.concatenate([k2[:, :HY_WIDTH], k2[::-1, HY_WIDTH:]], axis=0)
    blk = row0 // n
    seq = pl.BlockSpec((1, n, ch), lambda b: (b, blk, 0))
    const = lambda s: pl.BlockSpec(s, lambda b: (0, 0))
    return pl.pallas_call(
        _ctxconv_kernel,
        grid=(bsz,),
        in_specs=[seq, seq, const((n_fft, ch)), const((2 * n_fft, n_fft)), const((n, 2 * n_fft)), const((1, ch))],
        out_specs=pl.BlockSpec((1, n, ch), lambda b: (b, 0, 0)),
        out_shape=jax.ShapeDtypeStruct((bsz, n, ch), F32),
        scratch_shapes=[pltpu.VMEM((2 * n_fft, ch), F32)],
        compiler_params=_cp(("arbitrary",)),
        name="hyena_ctxconv",
    )(z, x0, k_full, fd, gd, skip.reshape(1, ch))


def _rope_tables(n_lat, rows):
    nrow = n_lat // GRID_W
    row = jnp.repeat(jnp.arange(nrow, dtype=F32), GRID_W)
    col = jnp.tile(jnp.arange(GRID_W, dtype=F32), nrow)
    n_freq = HEAD_DIM // 4
    inv_freq = ROPE_THETA ** (-jnp.arange(n_freq, dtype=F32) / n_freq)
    ar = row[:, None] * inv_freq
    ac = col[:, None] * inv_freq
    cos64 = jnp.concatenate([jnp.cos(ar), jnp.cos(ar), jnp.cos(ac), jnp.cos(ac)], axis=-1)
    sin64 = jnp.concatenate([-jnp.sin(ar), jnp.sin(ar), -jnp.sin(ac), jnp.sin(ac)], axis=-1)
    pad = rows - n_lat
    cos_t = jnp.concatenate([jnp.tile(cos64, (1, LANES // HEAD_DIM)), jnp.ones((pad, LANES), F32)], axis=0)
    sin_t = jnp.concatenate([jnp.tile(sin64, (1, LANES // HEAD_DIM)), jnp.zeros((pad, LANES), F32)], axis=0)
    return cos_t, sin_t


def kernel(x, c, ctx, c_ctx, w_ada, b_ada, norm_mix, norm_ffn, w_in, b_in, hy_conv_w, hy_conv_b, hy_filt_w1, hy_filt_b1, hy_filt_w2, hy_filt_b2, hy_filt_w3, hy_filt_b3, hy_filt_freq, hy_filt_out, hy_skip, ga_q_norm, ga_k_norm, wa_sink, branch_norm, w_out, b_out, w_router, b_router, w_mlp1, b_mlp1, w_mlp2, b_mlp2, norm_final):
    bsz, n_lat, d = x.shape
    n_ctx = ctx.shape[1]
    depth = w_ada.shape[0]
    assert n_ctx == ROW_TILE and n_lat % KV_CHUNK == 0 and bsz < 16 and d == D_MODEL
    assert (n_lat + n_ctx) % (8 * GATE_STEPS) == 0
    nt = n_lat // ROW_TILE
    rows = n_lat + n_ctx
    cos_t, sin_t = _rope_tables(n_lat, rows)

    c_all = jnp.zeros((16, d), F32).at[:bsz].set(c).at[bsz].set(c_ctx)
    mod_all = _ada_mod(c_all, w_ada, b_ada)
    x_lat, x_ctx, ctx_blk = x, ctx, 0

    for l in range(depth):
        last = l == depth - 1
        ntile = nt if last else nt + 1
        m = mod_all[l].reshape(16, 6, d)
        mod = jnp.stack([m[:bsz], jnp.broadcast_to(m[bsz], (bsz, 6, d))], axis=1)

        hy, gq, gk, gv, wq, wk, wv = _inproj(x_lat, x_ctx, ctx_blk, mod, norm_mix[l], w_in[l], b_in[l], cos_t, sin_t,
                                             ga_q_norm[l], ga_k_norm[l], nt)
        filt = (hy_filt_w1[l], hy_filt_b1[l], hy_filt_w2[l], hy_filt_b2[l], hy_filt_w3[l], hy_filt_b3[l],
                hy_filt_freq[l], hy_filt_out[l])
        z, x0 = _hyena_gate(hy, hy_conv_w[l], hy_conv_b[l], n_lat)
        y_hy = _hyena_fftconv(z, x0, _hyena_filters_pl(n_lat, filt), hy_skip[l])
        if not last:
            yc_hy = _hyena_ctxconv(z, x0, _hyena_filters_pl(n_ctx, filt), hy_skip[l], n_lat)
            y_hy = jnp.concatenate([y_hy, yc_hy], axis=1)
        y_ga = _global_attention(gq, gk, gv, nt, ntile)
        y_wa = _window_attention(wq, wk, wv, wa_sink[l], nt, ntile)
        x_mid, h2, top_g, top_e, top_r, counts = _merge(x_lat, x_ctx, ctx_blk, y_hy, y_ga, y_wa, mod, branch_norm[l], w_out[l], b_out[l],
                                                        norm_ffn[l], w_router[l], b_router[l], nt, ntile)
        rows_e, block_expert, n_used, zero_blocks = _moe_plan(top_e, top_r, counts, MOE_TILE)
        x_buf = _dispatch(h2.reshape(-1, LANES), rows_e, zero_blocks, block_expert.shape[0], MOE_TILE)
        y_sorted = _moe_ffn(x_buf, block_expert, n_used, w_mlp1, b_mlp1[l], w_mlp2, b_mlp2[l], l, MOE_TILE)
        x_lat = x_ctx = _combine(y_sorted, rows_e, x_mid, mod, top_g, norm_final, nt, ntile, last)
        ctx_blk = nt
    return x_lat
```

```python
import functools
import math

import jax
import jax.numpy as jnp
from jax import lax
from jax.experimental import pallas as pl
from jax.experimental.pallas import tpu as pltpu

F32 = jnp.float32
BF16 = jnp.bfloat16
I32 = jnp.int32

D_MODEL = 1024
HEAD_DIM = 64
GRID_W = 64
HY_WIDTH = 256
GA_HEADS = 6
GA_KV_HEADS = 2
WA_HEADS = 6
WA_KV_HEADS = 2
GROUP = GA_HEADS // GA_KV_HEADS
Q_WIDTH = GA_HEADS * HEAD_DIM
KV_WIDTH = GA_KV_HEADS * HEAD_DIM
MIX_WIDTH = HY_WIDTH + 2 * Q_WIDTH
IN_WIDTH = 3 * HY_WIDTH + 2 * (Q_WIDTH + 2 * KV_WIDTH)
WINDOW = 128
ROPE_THETA = 10000.0
ATTN_SCALE = HEAD_DIM ** -0.5
HY_SHORT = 3
HY_EMB = 33
HY_BANDS = (HY_EMB - 1) // 2
HY_FAST_DECAY = 0.3
HY_SLOW_DECAY = 1.5
HY_TARGET = 1e-2
N_EXPERTS = 32
TOP_K = 4
SWIGLU_ALPHA = 1.702
SWIGLU_LIMIT = 7.0
EPS = 1e-6
NEG_INF = -1e30

LANES = 128
VMEM_LIMIT_V7X = 56 * 1024 * 1024

ROW_TILE = 256
KV_CHUNK = 2048
MOE_TILE = 512
ROUTER_LANES = LANES


def _cp(sem, vmem=VMEM_LIMIT_V7X):
    return pltpu.CompilerParams(dimension_semantics=sem, vmem_limit_bytes=vmem)


def _rms(v, w):
    return v * lax.rsqrt(jnp.mean(v * v, axis=-1, keepdims=True) + EPS) * w


def _ada_kernel(c_ref, w_ref, b_ref, o_ref):
    c = c_ref[...]
    s = c / (1.0 + jnp.exp(-c))
    hi = s.astype(BF16)
    lo = (s - hi.astype(F32)).astype(BF16)
    w = w_ref[0]
    whi = w.astype(BF16)
    wlo = (w - whi.astype(F32)).astype(BF16)
    acc = jnp.dot(hi, whi, preferred_element_type=F32)
    acc += jnp.dot(lo, whi, preferred_element_type=F32)
    acc += jnp.dot(hi, wlo, preferred_element_type=F32)
    o_ref[0] = acc + b_ref[0]


def _ada_mod(c_all, w_ada, b_ada):
    n_layers, d, n6 = w_ada.shape
    tn = n6 // 4
    return pl.pallas_call(
        _ada_kernel,
        grid=(n_layers, n6 // tn),
        in_specs=[
            pl.BlockSpec((16, d), lambda l, j: (0, 0)),
            pl.BlockSpec((1, d, tn), lambda l, j: (l, 0, j)),
            pl.BlockSpec((1, 1, tn), lambda l, j: (l, 0, j)),
        ],
        out_specs=pl.BlockSpec((1, 16, tn), lambda l, j: (l, 0, j)),
        out_shape=jax.ShapeDtypeStruct((n_layers, 16, n6), F32),
        compiler_params=_cp(("arbitrary", "arbitrary")),
        name="ada_mod",
    )(c_all, w_ada, b_ada.reshape(n_layers, 1, n6))


def _inproj_kernel(xl_ref, xc_ref, mod_ref, nw_ref, w_ref, b_ref, cos_ref, sin_ref, qn_ref, kn_ref, g_ref,
                   hy_ref, gq_ref, gk_ref, gv_ref, wq_ref, wk_ref, wv_ref, *, nt):
    x = jnp.where(pl.program_id(1) < nt, xl_ref[0], xc_ref[0])
    tm = x.shape[0]
    shift = mod_ref[0, 0, 0:1, :]
    scale = mod_ref[0, 0, 1:2, :]
    h = _rms(x, nw_ref[...]) * (1.0 + scale) + shift
    p = jnp.dot(h.astype(BF16), w_ref[...], preferred_element_type=F32) + b_ref[...]
    hy_ref[0] = p[:, :3 * HY_WIDTH]

    cos = cos_ref[...]
    sin = sin_ref[...]
    lane = lax.broadcasted_iota(I32, (tm, LANES), 1)
    first = (lane % (HEAD_DIM // 2)) < (HEAD_DIM // 4)
    gmat = g_ref[...]

    def rope(v):
        sw = jnp.where(first, pltpu.roll(v, LANES - HEAD_DIM // 4, 1), pltpu.roll(v, HEAD_DIM // 4, 1))
        return v * cos + sw * sin

    def head_norm(v, w):
        t = v * v
        hi = t.astype(BF16)
        lo = (t - hi.astype(F32)).astype(BF16)
        ms = jnp.dot(hi, gmat, preferred_element_type=F32) + jnp.dot(lo, gmat, preferred_element_type=F32)
        return v * lax.rsqrt(ms + EPS) * w

    o = 3 * HY_WIDTH
    for u in range(Q_WIDTH // LANES):
        v = p[:, o + u * LANES:o + (u + 1) * LANES]
        v = rope(head_norm(v, qn_ref[...]))
        gq_ref[0, :, u * LANES:(u + 1) * LANES] = (v * ATTN_SCALE).astype(BF16)
    o += Q_WIDTH
    gk_ref[0] = rope(head_norm(p[:, o:o + KV_WIDTH], kn_ref[...])).astype(BF16)
    o += KV_WIDTH
    gv_ref[0] = p[:, o:o + KV_WIDTH].astype(BF16)
    o += KV_WIDTH
    for u in range(Q_WIDTH // LANES):
        v = rope(p[:, o + u * LANES:o + (u + 1) * LANES])
        wq_ref[0, :, u * LANES:(u + 1) * LANES] = (v * ATTN_SCALE).astype(BF16)
    o += Q_WIDTH
    wk_ref[0] = rope(p[:, o:o + KV_WIDTH]).astype(BF16)
    o += KV_WIDTH
    wv_ref[0] = p[:, o:o + KV_WIDTH].astype(BF16)


def _token_specs(nt, ctx_blk, d):
    return [pl.BlockSpec((1, ROW_TILE, d), lambda b, i: (b, jnp.minimum(i, nt - 1), 0)),
            pl.BlockSpec((1, ROW_TILE, d), lambda b, i: (b, ctx_blk, 0))]


def _inproj(x_lat, x_ctx, ctx_blk, mod, norm_w, w_in, b_in, cos_t, sin_t, q_norm, k_norm, nt):
    bsz, _, d = x_lat.shape
    tm = ROW_TILE
    ntile = nt + 1
    rows = ntile * tm
    gmat = jnp.kron(jnp.eye(LANES // HEAD_DIM, dtype=F32), jnp.full((HEAD_DIM, HEAD_DIM), 1.0 / HEAD_DIM, F32)).astype(BF16)
    qn = jnp.tile(q_norm, LANES // HEAD_DIM).reshape(1, LANES)
    kn = jnp.tile(k_norm, LANES // HEAD_DIM).reshape(1, LANES)
    tok = lambda w: pl.BlockSpec((1, tm, w), lambda b, i: (b, i, 0))
    const = lambda s: pl.BlockSpec(s, lambda b, i: (0,) * len(s))
    out_w = (3 * HY_WIDTH, Q_WIDTH, KV_WIDTH, KV_WIDTH, Q_WIDTH, KV_WIDTH, KV_WIDTH)
    out_dt = (F32, BF16, BF16, BF16, BF16, BF16, BF16)
    return pl.pallas_call(
        functools.partial(_inproj_kernel, nt=nt),
        grid=(bsz, ntile),
        in_specs=_token_specs(nt, ctx_blk, d) + [
            pl.BlockSpec((1, 1, 6, d), lambda b, i: (b, i // nt, 0, 0)),
            const((1, d)),
            const((d, IN_WIDTH)),
            const((1, IN_WIDTH)),
            pl.BlockSpec((tm, LANES), lambda b, i: (i, 0)),
            pl.BlockSpec((tm, LANES), lambda b, i: (i, 0)),
            const((1, LANES)),
            const((1, LANES)),
            const((LANES, LANES)),
        ],
        out_specs=[tok(w) for w in out_w],
        out_shape=[jax.ShapeDtypeStruct((bsz, rows, w), dt) for w, dt in zip(out_w, out_dt)],
        compiler_params=_cp(("arbitrary", "arbitrary")),
        name="inproj",
    )(x_lat, x_ctx, mod, norm_w.reshape(1, d), w_in.astype(BF16), b_in.reshape(1, IN_WIDTH), cos_t, sin_t, qn, kn, gmat)


def _stack_q(q_ref, qs_ref):
    qf = q_ref[0].astype(F32)
    tq = qf.shape[0]
    for j in range(GA_KV_HEADS):
        for g in range(GROUP):
            h = GROUP * j + g
            blk = qf[:, (h // 2) * LANES:(h // 2 + 1) * LANES]
            if h % 2 != j:
                blk = pltpu.roll(blk, HEAD_DIM, 1)
            qs_ref[j, g * tq:(g + 1) * tq, :] = blk.astype(BF16)


def _unstack_o(norm_fn, o_ref, tq):
    lane = lax.broadcasted_iota(I32, (tq, LANES), 1)
    heads = []
    for j in range(GA_KV_HEADS):
        for g in range(GROUP):
            h = GROUP * j + g
            o = norm_fn(j, g)
            if h % 2 != j:
                o = pltpu.roll(o, HEAD_DIM, 1)
            heads.append(o)
    for u in range(Q_WIDTH // LANES):
        o_ref[0, :, u * LANES:(u + 1) * LANES] = jnp.where(lane < HEAD_DIM, heads[2 * u], heads[2 * u + 1])


def _half_masks(rows):
    lane = lax.broadcasted_iota(I32, (rows, LANES), 1)
    return [lane < HEAD_DIM, lane >= HEAD_DIM]


def _dot_t(a, b):
    return lax.dot_general(a, b, (((1,), (1,)), ((), ())), preferred_element_type=F32)


def _ga_kernel(q_ref, k_ref, v_ref, o_ref, qs_ref, m_ref, acc_ref, *, nt, n_lat):
    i = pl.program_id(1)
    tq = q_ref.shape[1]
    _stack_q(q_ref, qs_ref)
    m_ref[...] = jnp.full(m_ref.shape, NEG_INF, F32)
    acc_ref[...] = jnp.zeros(acc_ref.shape, F32)

    def chunk(kc, vc):
        masks = _half_masks(kc.shape[0])
        heads = range(GA_KV_HEADS)
        scores = [_dot_t(qs_ref[j], jnp.where(masks[j], kc, jnp.zeros_like(kc))) for j in heads]
        probs = []
        for j in heads:
            m_old = m_ref[j]
            m_new = jnp.maximum(m_old, jnp.max(scores[j], axis=-1, keepdims=True))
            probs.append(jnp.exp((scores[j] - jnp.tile(m_new, (1, scores[j].shape[1] // LANES))).astype(BF16)))
            acc_ref[j] = jnp.exp(m_old - m_new) * acc_ref[j]
            m_ref[j] = m_new
        for j in heads:
            vm = jnp.where(masks[j], vc, jnp.ones_like(vc))
            acc_ref[j] += jnp.dot(probs[j], vm, preferred_element_type=F32)

    @pl.when(i < nt)
    def _():
        rows = k_ref.shape[1]
        for st in range(0, n_lat, KV_CHUNK):
            en = rows if st + KV_CHUNK >= n_lat else st + KV_CHUNK
            chunk(k_ref[0, st:en, :], v_ref[0, st:en, :])

    @pl.when(i >= nt)
    def _():
        chunk(k_ref[0, n_lat:, :], v_ref[0, n_lat:, :])

    def norm_fn(j, g):
        a = acc_ref[j, g * tq:(g + 1) * tq, :]
        return a / pltpu.roll(a, HEAD_DIM, 1)

    _unstack_o(norm_fn, o_ref, tq)


def _global_attention(gq, gk, gv, nt, ntile):
    bsz, rows, _ = gq.shape
    tq = ROW_TILE
    n_lat = nt * tq
    return pl.pallas_call(
        functools.partial(_ga_kernel, nt=nt, n_lat=n_lat),
        grid=(bsz, ntile),
        in_specs=[
            pl.BlockSpec((1, tq, Q_WIDTH), lambda b, i: (b, i, 0)),
            pl.BlockSpec((1, rows, KV_WIDTH), lambda b, i: (b, 0, 0)),
            pl.BlockSpec((1, rows, KV_WIDTH), lambda b, i: (b, 0, 0)),
        ],
        out_specs=pl.BlockSpec((1, tq, Q_WIDTH), lambda b, i: (b, i, 0)),
        out_shape=jax.ShapeDtypeStruct((bsz, ntile * tq, Q_WIDTH), F32),
        scratch_shapes=[
            pltpu.VMEM((GA_KV_HEADS, GROUP * tq, LANES), BF16),
            pltpu.VMEM((GA_KV_HEADS, GROUP * tq, LANES), F32),
            pltpu.VMEM((GA_KV_HEADS, GROUP * tq, LANES), F32),
        ],
        compiler_params=_cp(("arbitrary", "arbitrary")),
        name="global_attn",
    )(gq, gk, gv)


def _wa_kernel(q_ref, k_ref, v_ref, sink_ref, bias_ref, o_ref, qs_ref, acc_ref, l_ref, *, nt, n_lat):
    i = pl.program_id(1)
    tq = q_ref.shape[1]
    band = tq + 2 * WINDOW
    _stack_q(q_ref, qs_ref)
    st = pl.multiple_of(i * tq, tq)
    kb = k_ref[0, pl.ds(st, band), :]
    vb = v_ref[0, pl.ds(st, band), :]
    kc = k_ref[0, WINDOW + n_lat:WINDOW + n_lat + ROW_TILE, :]
    vc = v_ref[0, WINDOW + n_lat:WINDOW + n_lat + ROW_TILE, :]

    kpos = lax.broadcasted_iota(I32, (1, band), 1) + (i * tq - WINDOW)
    in_seq = (kpos >= 0) & (kpos < n_lat) & (i < nt)
    bias = bias_ref[...]

    mb = _half_masks(band)
    mc = _half_masks(ROW_TILE)
    heads = range(WA_KV_HEADS)
    s_loc = [jnp.where(in_seq, _dot_t(qs_ref[j], jnp.where(mb[j], kb, jnp.zeros_like(kb))) + bias, NEG_INF) for j in heads]
    s_ctx = [_dot_t(qs_ref[j], jnp.where(mc[j], kc, jnp.zeros_like(kc))) for j in heads]
    probs = []
    for j in heads:
        sink = jnp.concatenate(
            [jnp.broadcast_to(sink_ref[GROUP * j + g:GROUP * j + g + 1, :], (tq, LANES)) for g in range(GROUP)], axis=0)
        m = jnp.maximum(jnp.maximum(jnp.max(s_loc[j], axis=-1, keepdims=True), jnp.max(s_ctx[j], axis=-1, keepdims=True)),
                        sink)
        probs.append((jnp.exp(s_loc[j] - jnp.tile(m, (1, band // LANES))).astype(BF16),
                      jnp.exp(s_ctx[j] - jnp.tile(m, (1, ROW_TILE // LANES))).astype(BF16)))
        l_ref[j] = jnp.exp(sink - m)
    for j in heads:
        p_loc, p_ctx = probs[j]
        acc_ref[j] = (jnp.dot(p_loc, jnp.where(mb[j], vb, jnp.ones_like(vb)), preferred_element_type=F32)
                      + jnp.dot(p_ctx, jnp.where(mc[j], vc, jnp.ones_like(vc)), preferred_element_type=F32))

    def norm_fn(j, g):
        a = acc_ref[j, g * tq:(g + 1) * tq, :]
        return a / (pltpu.roll(a, HEAD_DIM, 1) + l_ref[j, g * tq:(g + 1) * tq, :])

    _unstack_o(norm_fn, o_ref, tq)


def _window_attention(wq, wk, wv, sink, nt, ntile):
    bsz, rows, _ = wq.shape
    tq = ROW_TILE
    n_lat = nt * tq
    pad = ((0, 0), (WINDOW, WINDOW), (0, 0))
    kp = jnp.pad(wk, pad)
    vp = jnp.pad(wv, pad)
    sink_b = jnp.zeros((8, LANES), F32).at[:WA_HEADS].set(jnp.broadcast_to(sink[:, None], (WA_HEADS, LANES)))
    band = tq + 2 * WINDOW
    rel = jnp.arange(band, dtype=I32)[None, :] - WINDOW - (jnp.arange(GROUP * tq, dtype=I32) % tq)[:, None]
    band_bias = jnp.where(jnp.abs(rel) <= WINDOW, 0.0, NEG_INF).astype(F32)
    return pl.pallas_call(
        functools.partial(_wa_kernel, nt=nt, n_lat=n_lat),
        grid=(bsz, ntile),
        in_specs=[
            pl.BlockSpec((1, tq, Q_WIDTH), lambda b, i: (b, i, 0)),
            pl.BlockSpec((1, rows + 2 * WINDOW, KV_WIDTH), lambda b, i: (b, 0, 0)),
            pl.BlockSpec((1, rows + 2 * WINDOW, KV_WIDTH), lambda b, i: (b, 0, 0)),
            pl.BlockSpec((8, LANES), lambda b, i: (0, 0)),
            pl.BlockSpec((GROUP * tq, band), lambda b, i: (0, 0)),
        ],
        out_specs=pl.BlockSpec((1, tq, Q_WIDTH), lambda b, i: (b, i, 0)),
        out_shape=jax.ShapeDtypeStruct((bsz, ntile * tq, Q_WIDTH), F32),
        scratch_shapes=[
            pltpu.VMEM((WA_KV_HEADS, GROUP * tq, LANES), BF16),
            pltpu.VMEM((WA_KV_HEADS, GROUP * tq, LANES), F32),
            pltpu.VMEM((WA_KV_HEADS, GROUP * tq, LANES), F32),
        ],
        compiler_params=_cp(("arbitrary", "arbitrary")),
        name="window_attn",
    )(wq, kp, vp, sink_b, band_bias)


def _store_row_tiles(ref, v):
    rows = v.shape[0]
    for s in range(v.shape[1] // LANES):
        ref[pl.ds(s, rows, stride=8), :] = v[:, s * LANES:(s + 1) * LANES]


def _load_row_tiles(ref, row0, rows):
    return jnp.concatenate([ref[pl.ds(row0 * 8 + s, rows, stride=8), :] for s in range(8)], axis=1)


def _merge_kernel(xl_ref, xc_ref, hy_ref, ga_ref, wa_ref, mod_ref, bn_ref, wo_ref, bo_ref, nf_ref, wrh_ref, wrl_ref, br_ref,
                  tri_ref, xo_ref, h2_ref, tg_ref, te_ref, tr_ref, cnt_ref, base_ref, *, nt):
    @pl.when((pl.program_id(0) == 0) & (pl.program_id(1) == 0))
    def _():
        base_ref[...] = jnp.zeros(base_ref.shape, F32)

    bn = bn_ref[...]
    g0 = HY_WIDTH
    g1 = HY_WIDTH + Q_WIDTH
    y = jnp.concatenate([_rms(hy_ref[0], bn[:, :g0]), _rms(ga_ref[0], bn[:, g0:g1]), _rms(wa_ref[0], bn[:, g1:])], axis=-1)
    o = jnp.dot(y.astype(BF16), wo_ref[...], preferred_element_type=F32) + bo_ref[...]
    x1 = jnp.where(pl.program_id(1) < nt, xl_ref[0], xc_ref[0]) + mod_ref[0, 0, 2:3, :] * o
    xo_ref[0] = x1
    h2 = _rms(x1, nf_ref[...]) * (1.0 + mod_ref[0, 0, 4:5, :]) + mod_ref[0, 0, 3:4, :]
    _store_row_tiles(h2_ref, h2)

    hi = h2.astype(BF16)
    lo = (h2 - hi.astype(F32)).astype(BF16)
    logits = (jnp.dot(hi, wrh_ref[...], preferred_element_type=F32) + jnp.dot(lo, wrh_ref[...], preferred_element_type=F32)
              + jnp.dot(hi, wrl_ref[...], preferred_element_type=F32)) + br_ref[...]
    tm = logits.shape[0]
    lane = lax.broadcasted_iota(I32, (tm, ROUTER_LANES), 1)
    te = jnp.zeros((tm, ROUTER_LANES), F32)
    tr = jnp.zeros((tm, ROUTER_LANES), F32)
    tv = jnp.zeros((tm, ROUTER_LANES), F32)
    before = base_ref[...]
    tri = tri_ref[...]
    v0 = None
    for k in range(TOP_K):
        mx = jnp.max(logits, axis=-1, keepdims=True)
        idx = jnp.min(jnp.where(logits == mx, lane, ROUTER_LANES), axis=-1, keepdims=True)
        if k == 0:
            v0 = mx
        hit = lane == idx
        onehot = jnp.where(hit, 1.0, 0.0)
        prefix = jnp.dot(tri, onehot.astype(BF16), preferred_element_type=F32) + before
        rank = jnp.sum(onehot * prefix, axis=-1, keepdims=True)
        before = before + jnp.sum(onehot, axis=0, keepdims=True)
        te = jnp.where(lane == k, idx.astype(F32), te)
        tr = jnp.where(lane == k, rank, tr)
        tv = jnp.where(lane == k, jnp.exp(mx - v0), tv)
        logits = jnp.where(hit, NEG_INF, logits)
    base_ref[...] = before
    cnt_ref[...] = before
    tg_ref[0] = tv / jnp.sum(tv, axis=-1, keepdims=True)
    te_t = te.T
    tr_t = tr.T
    for k in range(TOP_K):
        te_ref[0, :, k * tm:(k + 1) * tm] = te_t[k:k + 1, :].astype(I32)
        tr_ref[0, :, k * tm:(k + 1) * tm] = tr_t[k:k + 1, :].astype(I32)


def _merge(x_lat, x_ctx, ctx_blk, y_hy, y_ga, y_wa, mod, branch_norm, w_out, b_out, norm_ffn, w_router, b_router, nt, ntile):
    bsz, _, d = x_lat.shape
    tm = ROW_TILE
    rows = ntile * tm
    wr = jnp.zeros((d, ROUTER_LANES), F32).at[:, :N_EXPERTS].set(w_router)
    wrh = wr.astype(BF16)
    wrl = (wr - wrh.astype(F32)).astype(BF16)
    br = jnp.full((1, ROUTER_LANES), NEG_INF, F32).at[0, :N_EXPERTS].set(b_router)
    tri = (jnp.arange(tm, dtype=I32)[:, None] > jnp.arange(tm, dtype=I32)[None, :]).astype(BF16)
    tok = lambda w: pl.BlockSpec((1, tm, w), lambda b, i: (b, i, 0))
    const = lambda s: pl.BlockSpec(s, lambda b, i: (0,) * len(s))
    flat = pl.BlockSpec((1, 1, TOP_K * tm), lambda b, i: (b * ntile + i, 0, 0))
    return pl.pallas_call(
        functools.partial(_merge_kernel, nt=nt),
        grid=(bsz, ntile),
        in_specs=_token_specs(nt, ctx_blk, d) + [
            tok(HY_WIDTH), tok(Q_WIDTH), tok(Q_WIDTH),
            pl.BlockSpec((1, 1, 6, d), lambda b, i: (b, i // nt, 0, 0)),
            const((1, MIX_WIDTH)), const((MIX_WIDTH, d)), const((1, d)), const((1, d)),
            const((d, ROUTER_LANES)), const((d, ROUTER_LANES)), const((1, ROUTER_LANES)), const((tm, tm)),
        ],
        out_specs=[tok(d), pl.BlockSpec((None, tm * 8, LANES), lambda b, i: (b, i, 0)), tok(ROUTER_LANES), flat, flat,
                   const((1, ROUTER_LANES))],
        out_shape=[
            jax.ShapeDtypeStruct((bsz, rows, d), F32),
            jax.ShapeDtypeStruct((bsz, rows * 8, LANES), F32),
            jax.ShapeDtypeStruct((bsz, rows, ROUTER_LANES), F32),
            jax.ShapeDtypeStruct((bsz * ntile, 1, TOP_K * tm), I32),
            jax.ShapeDtypeStruct((bsz * ntile, 1, TOP_K * tm), I32),
            jax.ShapeDtypeStruct((1, ROUTER_LANES), F32),
        ],
        scratch_shapes=[pltpu.VMEM((1, ROUTER_LANES), F32)],
        compiler_params=_cp(("arbitrary", "arbitrary")),
        name="merge_router",
    )(x_lat, x_ctx, y_hy, y_ga, y_wa, mod, branch_norm.reshape(1, MIX_WIDTH), w_out.astype(BF16), b_out.reshape(1, d),
      norm_ffn.reshape(1, d), wrh, wrl, br, tri)


def _moe_plan(te, tr, cnt, tm):
    steps, _, per = te.shape
    n_blocks = -(-(steps * per) // tm) + N_EXPERTS
    counts = cnt[0, :N_EXPERTS].astype(I32)
    nblk = (counts + tm - 1) // tm
    cend = jnp.cumsum(nblk)
    cstart = cend - nblk
    experts = jnp.arange(N_EXPERTS, dtype=I32)
    row = jnp.sum(jnp.where(te[..., None] == experts, cstart * tm, 0), axis=-1) + tr
    j = jnp.arange(n_blocks, dtype=I32)
    block_expert = jnp.minimum(jnp.sum((cend[None, :] <= j[:, None]).astype(I32), axis=1), N_EXPERTS - 1)
    past = cend[-1] + experts
    zero_blocks = jnp.concatenate([jnp.where(nblk > 0, cend - 1, -1), jnp.where(past < n_blocks, past, -1)])
    return row, block_expert, cend[-1].reshape(1), zero_blocks


def _dispatch_kernel(zb_ref, row_ref, h_ref, x_hbm, buf0, buf1, sem, *, block_rows):
    step = pl.program_id(0)
    tm = h_ref.shape[0] // 8
    per_block = block_rows // tm

    def wait(buf, s):
        for k in range(TOP_K):
            pltpu.make_async_copy(buf, x_hbm.at[pl.ds(0, tm * 8), :], sem.at[s, k]).wait()

    @pl.when(step == 0)
    def _():
        buf1[...] = jnp.zeros(buf1.shape, F32)
        def piece(e, q):
            dst = pl.multiple_of((zb_ref[e] * per_block + q) * (tm * 8), tm * 8)
            return pltpu.make_async_copy(buf1, x_hbm.at[pl.ds(dst, tm * 8), :], sem.at[1, 0])

        for phase in ("start", "wait"):
            for e in range(zb_ref.shape[0]):
                @pl.when(zb_ref[e] >= 0)
                def _():
                    for q in range(per_block):
                        getattr(piece(e, q), phase)()

    def run(cur, prev, sc, sp):
        cur[...] = h_ref[...]
        for k in range(TOP_K):
            for r in range(tm):
                dst = pl.multiple_of(row_ref[0, 0, k * tm + r], 8)
                pltpu.make_async_copy(cur.at[pl.ds(r * 8, 8), :], x_hbm.at[pl.ds(dst, 8), :], sem.at[sc, k]).start()

        @pl.when(step > 0)
        def _():
            wait(prev, sp)

        @pl.when(step == pl.num_programs(0) - 1)
        def _():
            wait(cur, sc)

    @pl.when(step % 2 == 0)
    def _():
        run(buf0, buf1, 0, 1)

    @pl.when(step % 2 == 1)
    def _():
        run(buf1, buf0, 1, 0)


def _dispatch(h_tiles, rows_e, zero_blocks, n_blocks, block_rows):
    steps = rows_e.shape[0]
    tm = ROW_TILE
    grid_spec = pltpu.PrefetchScalarGridSpec(
        num_scalar_prefetch=1,
        grid=(steps,),
        in_specs=[
            pl.BlockSpec((1, 1, TOP_K * tm), lambda i, zb: (i, 0, 0), memory_space=pltpu.SMEM),
            pl.BlockSpec((tm * 8, LANES), lambda i, zb: (i, 0)),
        ],
        out_specs=pl.BlockSpec(memory_space=pl.ANY),
        scratch_shapes=[pltpu.VMEM((tm * 8, LANES), F32), pltpu.VMEM((tm * 8, LANES), F32),
                        pltpu.SemaphoreType.DMA((2, TOP_K))],
    )
    return pl.pallas_call(
        functools.partial(_dispatch_kernel, block_rows=block_rows),
        grid_spec=grid_spec,
        out_shape=jax.ShapeDtypeStruct((n_blocks * block_rows * 8, LANES), F32),
        compiler_params=_cp(("arbitrary",)),
        name="moe_dispatch",
    )(zero_blocks, rows_e * 8, h_tiles)


def _ffn_kernel(be_ref, nu_ref, x_ref, w1_ref, b1_ref, w2_ref, b2_ref, y_ref, w1b, w2b):
    i = pl.program_id(0)
    tm = x_ref.shape[0] // 8
    f = w2_ref.shape[1]

    @pl.when(i < nu_ref[0])
    def _():
        @pl.when((i == 0) | (be_ref[i] != be_ref[jnp.maximum(i - 1, 0)]))
        def _():
            w1b[...] = w1_ref[0].astype(BF16)
            w2b[...] = w2_ref[0].astype(BF16)

        a = jnp.dot(_load_row_tiles(x_ref, 0, tm).astype(BF16), w1b[...], preferred_element_type=F32) + b1_ref[0]
        glu = jnp.minimum(a[:, :f], SWIGLU_LIMIT)
        lin = jnp.clip(a[:, f:], -SWIGLU_LIMIT, SWIGLU_LIMIT)
        act = glu / (1.0 + jnp.exp(-SWIGLU_ALPHA * glu)) * (lin + 1.0)
        _store_row_tiles(y_ref, jnp.dot(act.astype(BF16), w2b[...], preferred_element_type=F32) + b2_ref[0])

    @pl.when(i >= nu_ref[0])
    def _():
        y_ref[...] = jnp.zeros(y_ref.shape, F32)


def _moe_ffn(x_buf, block_expert, n_used, w1, b1, w2, b2, layer, tm):
    n_blocks = block_expert.shape[0]
    _, n_exp, d, f2 = w1.shape
    f = w2.shape[2]
    grid_spec = pltpu.PrefetchScalarGridSpec(
        num_scalar_prefetch=2,
        grid=(n_blocks,),
        in_specs=[
            pl.BlockSpec((tm * 8, LANES), lambda i, be, nu: (i, 0)),
            pl.BlockSpec((None, 1, d, f2), lambda i, be, nu: (layer, be[i], 0, 0)),
            pl.BlockSpec((1, 1, f2), lambda i, be, nu: (be[i], 0, 0)),
            pl.BlockSpec((None, 1, f, d), lambda i, be, nu: (layer, be[i], 0, 0)),
            pl.BlockSpec((1, 1, d), lambda i, be, nu: (be[i], 0, 0)),
        ],
        out_specs=pl.BlockSpec((tm * 8, LANES), lambda i, be, nu: (i, 0)),
        scratch_shapes=[pltpu.VMEM((d, f2), BF16), pltpu.VMEM((f, d), BF16)],
    )
    return pl.pallas_call(
        _ffn_kernel,
        grid_spec=grid_spec,
        out_shape=jax.ShapeDtypeStruct((n_blocks * tm * 8, LANES), F32),
        compiler_params=_cp(("arbitrary",)),
        name="moe_ffn",
    )(block_expert, n_used, x_buf, w1, b1.reshape(n_exp, 1, f2), w2, b2.reshape(n_exp, 1, d))


def _combine_kernel(slot_ref, slotn_ref, y_hbm, x_ref, mod_ref, gate_ref, nf_ref, o_ref, ybuf0, ybuf1, sem, *, final):
    b = pl.program_id(0)
    i = pl.program_id(1)
    step = b * pl.num_programs(1) + i
    n_steps = pl.num_programs(0) * pl.num_programs(1)
    tc = x_ref.shape[1]
    n_rows = TOP_K * tc

    def issue(idx_ref, buf, s):
        for j in range(n_rows):
            src = pl.multiple_of(idx_ref[0, 0, j], 8)
            pltpu.make_async_copy(y_hbm.at[pl.ds(src, 8), :], buf.at[pl.ds(j * 8, 8), :], sem.at[s]).start()

    def wait(buf, s):
        pltpu.make_async_copy(y_hbm.at[pl.ds(0, n_rows * 8), :], buf, sem.at[s]).wait()

    @pl.when(step == 0)
    def _():
        issue(slot_ref, ybuf0, 0)

    def run(cur, nxt, sc, sn):
        issue(slotn_ref, nxt, sn)
        wait(cur, sc)
        gates = gate_ref[0]
        acc = gates[:, 0:1] * _load_row_tiles(cur, 0, tc)
        for k in range(1, TOP_K):
            acc += gates[:, k:k + 1] * _load_row_tiles(cur, k * tc, tc)
        out = x_ref[0] + mod_ref[0, 0, 5:6, :] * acc
        if final:
            out = _rms(out, nf_ref[...])
        o_ref[0] = out

        @pl.when(step == n_steps - 1)
        def _():
            wait(nxt, sn)

    @pl.when(step % 2 == 0)
    def _():
        run(ybuf0, ybuf1, 0, 1)

    @pl.when(step % 2 == 1)
    def _():
        run(ybuf1, ybuf0, 1, 0)


def _combine(y_sorted, slot, x_mid, mod, gates, norm_final, nt, ntile, final):
    bsz, _, d = x_mid.shape
    tc = ROW_TILE
    n_steps = bsz * ntile
    slots = slot * 8
    tok = lambda w: pl.BlockSpec((1, tc, w), lambda b, i: (b, i, 0))
    return pl.pallas_call(
        functools.partial(_combine_kernel, final=final),
        grid=(bsz, ntile),
        in_specs=[
            pl.BlockSpec((1, 1, tc * TOP_K), lambda b, i: (b * ntile + i, 0, 0), memory_space=pltpu.SMEM),
            pl.BlockSpec((1, 1, tc * TOP_K), lambda b, i: (jnp.minimum(b * ntile + i + 1, n_steps - 1), 0, 0),
                         memory_space=pltpu.SMEM),
            pl.BlockSpec(memory_space=pl.ANY),
            tok(d),
            pl.BlockSpec((1, 1, 6, d), lambda b, i: (b, i // nt, 0, 0)),
            tok(ROUTER_LANES),
            pl.BlockSpec((1, d), lambda b, i: (0, 0)),
        ],
        out_specs=tok(d),
        out_shape=jax.ShapeDtypeStruct((bsz, ntile * tc, d), F32),
        scratch_shapes=[pltpu.VMEM((TOP_K * tc * 8, LANES), F32), pltpu.VMEM((TOP_K * tc * 8, LANES), F32),
                        pltpu.SemaphoreType.DMA((2,))],
        compiler_params=_cp(("arbitrary", "arbitrary")),
        name="moe_combine",
    )(slots, slots, y_sorted, x_mid, mod, gates, norm_final.reshape(1, d))


def _split_dot(a, w):
    ah = a.astype(BF16)
    al = (a - ah.astype(F32)).astype(BF16)
    wh = w.astype(BF16)
    wl = (w - wh.astype(F32)).astype(BF16)
    return (jnp.dot(ah, wh, preferred_element_type=F32) + jnp.dot(al, wh, preferred_element_type=F32)
            + jnp.dot(ah, wl, preferred_element_type=F32))


def _filt_kernel(z_ref, w1_ref, b1_ref, w2_ref, b2_ref, w3_ref, b3_ref, fr_ref, wo_ref, dec_ref, o_ref):
    fr = fr_ref[...]
    h = jnp.sin(fr * (_split_dot(z_ref[...], w1_ref[...]) + b1_ref[...]))
    h = jnp.sin(fr * (_split_dot(h, w2_ref[...]) + b2_ref[...]))
    h = jnp.sin(fr * (_split_dot(h, w3_ref[...]) + b3_ref[...]))
    o_ref[...] = _split_dot(h, wo_ref[...]) * dec_ref[...]


def _hyena_filters_pl(n, filt):
    w1, b1, w2, b2, w3, b3, freq, w_out = filt
    fw = w2.shape[0]
    t = jnp.linspace(0.0, 1.0, n, dtype=F32)[:, None]
    wpos = (2.0 * math.pi / n) * jnp.arange(n, dtype=F32)[:, None]
    bands = jnp.linspace(1e-4, HY_BANDS - 1, HY_BANDS, dtype=F32)
    z = jnp.concatenate([t, jnp.cos(wpos * bands), -jnp.sin(wpos * bands), jnp.zeros((n, fw - HY_EMB), F32)], axis=-1)
    w1p = jnp.concatenate([w1, jnp.zeros((fw - HY_EMB, fw), F32)], axis=0)
    max_decay = math.log(HY_TARGET) / HY_FAST_DECAY
    min_decay = math.log(HY_TARGET) / HY_SLOW_DECAY
    deltas = jnp.linspace(min_decay, max_decay, HY_WIDTH, dtype=F32)
    decay = jnp.tile(jnp.exp(-t * jnp.abs(deltas)), (1, 2))
    tm = min(n, 512)
    row = lambda w: pl.BlockSpec((tm, w), lambda i: (i, 0))
    const = lambda s: pl.BlockSpec(s, lambda i: (0, 0))
    return pl.pallas_call(
        _filt_kernel,
        grid=(n // tm,),
        in_specs=[row(fw), const((fw, fw)), const((1, fw)), const((fw, fw)), const((1, fw)), const((fw, fw)), const((1, fw)),
                  const((1, fw)), const((fw, 2 * HY_WIDTH)), row(2 * HY_WIDTH)],
        out_specs=row(2 * HY_WIDTH),
        out_shape=jax.ShapeDtypeStruct((n, 2 * HY_WIDTH), F32),
        compiler_params=_cp(("arbitrary",)),
        name="hyena_filters",
    )(z, w1p, b1.reshape(1, fw), w2, b2.reshape(1, fw), w3, b3.reshape(1, fw), freq.reshape(1, fw), w_out, decay)


GATE_STEPS = 4


def _gate_kernel(u_ref, prev_ref, next_ref, w_ref, b_ref, z_ref, x0_ref, *, n_lat):
    i = pl.program_id(1)
    u = u_ref[0]
    tm = u.shape[0]
    n_rows = pl.num_programs(1) * tm
    w = w_ref[...]
    row = lax.broadcasted_iota(I32, u.shape, 0)
    pos = row + i * tm
    up = jnp.where(row == 0, prev_ref[0, 7:8, :], pltpu.roll(u, 1, 0))
    up = jnp.where((pos == 0) | (pos == n_lat), 0.0, up)
    un = jnp.where(row == tm - 1, next_ref[0, 0:1, :], pltpu.roll(u, tm - 1, 0))
    un = jnp.where((pos == n_lat - 1) | (pos == n_rows - 1), 0.0, un)
    uc = up * w[0:1, :] + u * w[1:2, :] + un * w[2:3, :] + b_ref[...]
    x0_ref[0] = uc[:, :HY_WIDTH]
    z_ref[0] = uc[:, 2 * HY_WIDTH:] * uc[:, HY_WIDTH:2 * HY_WIDTH]


def _hyena_gate(hy, conv_w, conv_b, n_lat):
    bsz, rows, w3 = hy.shape
    tm = rows // GATE_STEPS
    sub = tm // 8
    last8 = rows // 8 - 1
    out = pl.BlockSpec((1, tm, HY_WIDTH), lambda b, i: (b, i, 0))
    return pl.pallas_call(
        functools.partial(_gate_kernel, n_lat=n_lat),
        grid=(bsz, GATE_STEPS),
        in_specs=[
            pl.BlockSpec((1, tm, w3), lambda b, i: (b, i, 0)),
            pl.BlockSpec((1, 8, w3), lambda b, i: (b, jnp.maximum(i * sub - 1, 0), 0)),
            pl.BlockSpec((1, 8, w3), lambda b, i: (b, jnp.minimum((i + 1) * sub, last8), 0)),
            pl.BlockSpec((HY_SHORT, w3), lambda b, i: (0, 0)),
            pl.BlockSpec((1, w3), lambda b, i: (0, 0)),
        ],
        out_specs=[out, out],
        out_shape=[jax.ShapeDtypeStruct((bsz, rows, HY_WIDTH), F32)] * 2,
        compiler_params=_cp(("arbitrary", "arbitrary")),
        name="hyena_gate",
    )(hy, hy, hy, conv_w, conv_b.reshape(1, w3))


FFT_N2 = LANES


def _dft_tables(n_fft):
    n1 = n_fft // FFT_N2
    nk = min(n1, -(-(n1 // 2 + 1) // 8) * 8)
    k1 = jnp.arange(nk, dtype=I32)
    nn = jnp.arange(n1, dtype=I32)
    a = (2.0 * math.pi / n1) * ((k1[:, None] * nn[None, :]) % n1).astype(F32)
    f1 = jnp.stack([jnp.cos(a), -jnp.sin(a)], axis=1).reshape(2 * nk, n1)
    w = jnp.where((k1 == 0) | (2 * k1 == n1), 1.0, jnp.where(2 * k1 < n1, 2.0, 0.0)) / n_fft
    g = jnp.stack([jnp.cos(a.T) * w, -jnp.sin(a.T) * w], axis=2).reshape(n1, 2 * nk)
    k2 = jnp.arange(FFT_N2, dtype=I32)
    m = (k2[None, None, :] * (k1[:, None, None] + n1 * k2[None, :, None])) % n_fft
    th = (2.0 * math.pi / n_fft) * m.astype(F32)
    mre, mim = jnp.cos(th), -jnp.sin(th)
    mr = jnp.concatenate([jnp.concatenate([mre, -mim], axis=2), jnp.concatenate([mim, mre], axis=2)], axis=1)
    return f1.astype(BF16), g.astype(BF16), mr.astype(BF16), jnp.swapaxes(mr, 1, 2).astype(BF16)


def _fft1_kernel(x_ref, f_ref, o_ref, *, n1_in):
    f = f_ref[...]
    n_rows = f.shape[0]

    def body(n2, carry):
        xs = x_ref[pl.ds(n2, n1_in, stride=FFT_N2), :].astype(BF16)
        o_ref[pl.ds(n2, n_rows, stride=FFT_N2), :] = jnp.dot(f, xs, preferred_element_type=F32)
        return carry
    lax.fori_loop(0, FFT_N2, body, 0, unroll=8)


def _fft1(x, f1, n1_in):
    bsz, _, ch = x.shape
    n_rows = f1.shape[0]
    return pl.pallas_call(
        functools.partial(_fft1_kernel, n1_in=n1_in),
        grid=(bsz, ch // LANES),
        in_specs=[pl.BlockSpec((None, n1_in * FFT_N2, LANES), lambda b, h: (b, 0, h)),
                  pl.BlockSpec((n_rows, n1_in), lambda b, h: (0, 0))],
        out_specs=pl.BlockSpec((None, n_rows * FFT_N2, LANES), lambda b, h: (b, 0, h)),
        out_shape=jax.ShapeDtypeStruct((bsz, n_rows * FFT_N2, ch), F32),
        compiler_params=_cp(("arbitrary", "arbitrary")),
        name="hyena_fft1",
    )(x, f1[:, :n1_in])


def _spec_kernel(a_ref, mr_ref, o_ref):
    o_ref[0] = jnp.dot(mr_ref[0], a_ref[0].astype(BF16), preferred_element_type=F32)


def _fft_spectrum(a, mr):
    _, rows, ch = a.shape
    blk = 2 * FFT_N2
    return pl.pallas_call(
        _spec_kernel,
        grid=(rows // blk,),
        in_specs=[pl.BlockSpec((1, blk, ch), lambda k: (0, k, 0)), pl.BlockSpec((1, blk, blk), lambda k: (k, 0, 0))],
        out_specs=pl.BlockSpec((1, blk, ch), lambda k: (0, k, 0)),
        out_shape=jax.ShapeDtypeStruct((1, rows, ch), F32),
        compiler_params=_cp(("arbitrary",)),
        name="hyena_filter_spectrum",
    )(a, mr)


def _mid_kernel(a_ref, mr_ref, mrt_ref, ks_ref, o_ref):
    kre, kim = ks_ref[0, :FFT_N2, :], ks_ref[0, FFT_N2:, :]
    for b in range(a_ref.shape[0]):
        x = jnp.dot(mr_ref[0], a_ref[b].astype(BF16), preferred_element_type=F32)
        xre, xim = x[:FFT_N2], x[FFT_N2:]
        y = jnp.concatenate([xre * kre - xim * kim, xre * kim + xim * kre], axis=0)
        o_ref[b] = jnp.dot(mrt_ref[0], y.astype(BF16), preferred_element_type=F32)


def _fft_mid(a, mr, mrt, kspec):
    bsz, rows, ch = a.shape
    blk = 2 * FFT_N2
    return pl.pallas_call(
        _mid_kernel,
        grid=(rows // blk,),
        in_specs=[pl.BlockSpec((bsz, blk, ch), lambda k: (0, k, 0)),
                  pl.BlockSpec((1, blk, blk), lambda k: (k, 0, 0)),
                  pl.BlockSpec((1, blk, blk), lambda k: (k, 0, 0)),
                  pl.BlockSpec((1, blk, ch), lambda k: (0, k, 0))],
        out_specs=pl.BlockSpec((bsz, blk, ch), lambda k: (0, k, 0)),
        out_shape=jax.ShapeDtypeStruct((bsz, rows, ch), F32),
        compiler_params=_cp(("arbitrary",)),
        name="hyena_fft_mid",
    )(a, mr, mrt, kspec)


def _ifft1_kernel(b_ref, g_ref, z_ref, x0_ref, skip_ref, o_ref, *, n1_out):
    g = g_ref[...]
    n_rows = g.shape[1]
    skip = skip_ref[...]

    def body(n2, carry):
        bs = b_ref[pl.ds(n2, n_rows, stride=FFT_N2), :].astype(BF16)
        y = jnp.dot(g, bs, preferred_element_type=F32)
        rows = pl.ds(n2, n1_out, stride=FFT_N2)
        zs = z_ref[rows, :]
        o_ref[rows, :] = x0_ref[rows, :] * (y + zs * skip)
        return carry
    lax.fori_loop(0, FFT_N2, body, 0, unroll=8)


def _ifft1(bmat, g, z, x0, skip, n1_out):
    bsz, rows, ch = bmat.shape
    n = n1_out * FFT_N2
    seq = pl.BlockSpec((None, n, LANES), lambda b, h: (b, 0, h))
    return pl.pallas_call(
        functools.partial(_ifft1_kernel, n1_out=n1_out),
        grid=(bsz, ch // LANES),
        in_specs=[pl.BlockSpec((None, rows, LANES), lambda b, h: (b, 0, h)),
                  pl.BlockSpec((n1_out, g.shape[1]), lambda b, h: (0, 0)),
                  seq, seq, pl.BlockSpec((1, LANES), lambda b, h: (0, h))],
        out_specs=seq,
        out_shape=jax.ShapeDtypeStruct((bsz, n, ch), F32),
        compiler_params=_cp(("arbitrary", "arbitrary")),
        name="hyena_ifft1",
    )(bmat, g[:n1_out], z, x0, skip.reshape(1, ch))


def _hyena_fftconv(z, x0, k2, skip):
    n = k2.shape[0]
    n_fft = 2 * n
    f1, g, mr, mrt = _dft_tables(n_fft)
    k_full = jnp.concatenate([k2[:, :HY_WIDTH], k2[::-1, HY_WIDTH:]], axis=0)[None]
    kspec = _fft_spectrum(_fft1(k_full, f1, n_fft // FFT_N2), mr)
    a = _fft1(z, f1, n // FFT_N2)
    bmat = _fft_mid(a, mr, mrt, kspec)
    return _ifft1(bmat, g, z, x0, skip, n // FFT_N2)


def _ctxconv_kernel(z_ref, x0_ref, kf_ref, fd_ref, gd_ref, skip_ref, o_ref, ks_ref):
    n = z_ref.shape[1]
    half = fd_ref.shape[0] // 2

    @pl.when(pl.program_id(0) == 0)
    def _():
        ks_ref[...] = jnp.dot(fd_ref[...], kf_ref[...].astype(BF16), preferred_element_type=F32)

    z = z_ref[0]
    x = jnp.dot(fd_ref[:, :n], z.astype(BF16), preferred_element_type=F32)
    xre, xim = x[:half], x[half:]
    kre, kim = ks_ref[:half, :], ks_ref[half:, :]
    y = jnp.concatenate([xre * kre - xim * kim, xre * kim + xim * kre], axis=0)
    conv = jnp.dot(gd_ref[...], y.astype(BF16), preferred_element_type=F32)
    o_ref[0] = x0_ref[0] * (conv + z * skip_ref[...])


def _hyena_ctxconv(z, x0, k2, skip, row0):
    n = k2.shape[0]
    bsz, _, ch = z.shape
    n_fft = 2 * n
    kk = jnp.arange(n_fft, dtype=I32)
    th = (2.0 * math.pi / n_fft) * ((kk[:, None] * kk[None, :]) % n_fft).astype(F32)
    fd = jnp.concatenate([jnp.cos(th), -jnp.sin(th)], axis=0).astype(BF16)
    gd = (jnp.concatenate([jnp.cos(th[:n]), -jnp.sin(th[:n])], axis=1) / n_fft).astype(BF16)
    k_full = jnp.concatenate([k2[:, :HY_WIDTH], k2[::-1, HY_WIDTH:]], axis=0)
    blk = row0 // n
    seq = pl.BlockSpec((1, n, ch), lambda b: (b, blk, 0))
    const = lambda s: pl.BlockSpec(s, lambda b: (0, 0))
    return pl.pallas_call(
        _ctxconv_kernel,
        grid=(bsz,),
        in_specs=[seq, seq, const((n_fft, ch)), const((2 * n_fft, n_fft)), const((n, 2 * n_fft)), const((1, ch))],
        out_specs=pl.BlockSpec((1, n, ch), lambda b: (b, 0, 0)),
        out_shape=jax.ShapeDtypeStruct((bsz, n, ch), F32),
        scratch_shapes=[pltpu.VMEM((2 * n_fft, ch), F32)],
        compiler_params=_cp(("arbitrary",)),
        name="hyena_ctxconv",
    )(z, x0, k_full, fd, gd, skip.reshape(1, ch))


def _rope_tables(n_lat, rows):
    nrow = n_lat // GRID_W
    row = jnp.repeat(jnp.arange(nrow, dtype=F32), GRID_W)
    col = jnp.tile(jnp.arange(GRID_W, dtype=F32), nrow)
    n_freq = HEAD_DIM // 4
    inv_freq = ROPE_THETA ** (-jnp.arange(n_freq, dtype=F32) / n_freq)
    ar = row[:, None] * inv_freq
    ac = col[:, None] * inv_freq
    cos64 = jnp.concatenate([jnp.cos(ar), jnp.cos(ar), jnp.cos(ac), jnp.cos(ac)], axis=-1)
    sin64 = jnp.concatenate([-jnp.sin(ar), jnp.sin(ar), -jnp.sin(ac), jnp.sin(ac)], axis=-1)
    pad = rows - n_lat
    cos_t = jnp.concatenate([jnp.tile(cos64, (1, LANES // HEAD_DIM)), jnp.ones((pad, LANES), F32)], axis=0)
    sin_t = jnp.concatenate([jnp.tile(sin64, (1, LANES // HEAD_DIM)), jnp.zeros((pad, LANES), F32)], axis=0)
    return cos_t, sin_t


def kernel(x, c, ctx, c_ctx, w_ada, b_ada, norm_mix, norm_ffn, w_in, b_in, hy_conv_w, hy_conv_b, hy_filt_w1, hy_filt_b1, hy_filt_w2, hy_filt_b2, hy_filt_w3, hy_filt_b3, hy_filt_freq, hy_filt_out, hy_skip, ga_q_norm, ga_k_norm, wa_sink, branch_norm, w_out, b_out, w_router, b_router, w_mlp1, b_mlp1, w_mlp2, b_mlp2, norm_final):
    bsz, n_lat, d = x.shape
    n_ctx = ctx.shape[1]
    depth = w_ada.shape[0]
    assert n_ctx == ROW_TILE and n_lat % KV_CHUNK == 0 and bsz < 16 and d == D_MODEL
    assert (n_lat + n_ctx) % (8 * GATE_STEPS) == 0
    nt = n_lat // ROW_TILE
    rows = n_lat + n_ctx
    cos_t, sin_t = _rope_tables(n_lat, rows)

    c_all = jnp.zeros((16, d), F32).at[:bsz].set(c).at[bsz].set(c_ctx)
    mod_all = _ada_mod(c_all, w_ada, b_ada)
    x_lat, x_ctx, ctx_blk = x, ctx, 0

    for l in range(depth):
        last = l == depth - 1
        ntile = nt if last else nt + 1
        m = mod_all[l].reshape(16, 6, d)
        mod = jnp.stack([m[:bsz], jnp.broadcast_to(m[bsz], (bsz, 6, d))], axis=1)

        hy, gq, gk, gv, wq, wk, wv = _inproj(x_lat, x_ctx, ctx_blk, mod, norm_mix[l], w_in[l], b_in[l], cos_t, sin_t,
                                             ga_q_norm[l], ga_k_norm[l], nt)
        filt = (hy_filt_w1[l], hy_filt_b1[l], hy_filt_w2[l], hy_filt_b2[l], hy_filt_w3[l], hy_filt_b3[l],
                hy_filt_freq[l], hy_filt_out[l])
        z, x0 = _hyena_gate(hy, hy_conv_w[l], hy_conv_b[l], n_lat)
        y_hy = _hyena_fftconv(z, x0, _hyena_filters_pl(n_lat, filt), hy_skip[l])
        if not last:
            yc_hy = _hyena_ctxconv(z, x0, _hyena_filters_pl(n_ctx, filt), hy_skip[l], n_lat)
            y_hy = jnp.concatenate([y_hy, yc_hy], axis=1)
        y_ga = _global_attention(gq, gk, gv, nt, ntile)
        y_wa = _window_attention(wq, wk, wv, wa_sink[l], nt, ntile)
        x_mid, h2, top_g, top_e, top_r, counts = _merge(x_lat, x_ctx, ctx_blk, y_hy, y_ga, y_wa, mod, branch_norm[l], w_out[l], b_out[l],
                                                        norm_ffn[l], w_router[l], b_router[l], nt, ntile)
        rows_e, block_expert, n_used, zero_blocks = _moe_plan(top_e, top_r, counts, MOE_TILE)
        x_buf = _dispatch(h2.reshape(-1, LANES), rows_e, zero_blocks, block_expert.shape[0], MOE_TILE)
        y_sorted = _moe_ffn(x_buf, block_expert, n_used, w_mlp1, b_mlp1[l], w_mlp2, b_mlp2[l], l, MOE_TILE)
        x_lat = x_ctx = _combine(y_sorted, rows_e, x_mid, mod, top_g, norm_final, nt, ntile, last)
        ctx_blk = nt
    return x_lat
```

```python
import functools
import math

import jax
import jax.numpy as jnp
from jax import lax
from jax.experimental import pallas as pl
from jax.experimental.pallas import tpu as pltpu

F32 = jnp.float32
BF16 = jnp.bfloat16
I32 = jnp.int32

D_MODEL = 1024
HEAD_DIM = 64
GRID_W = 64
HY_WIDTH = 256
GA_HEADS = 6
GA_KV_HEADS = 2
WA_HEADS = 6
WA_KV_HEADS = 2
GROUP = GA_HEADS // GA_KV_HEADS
Q_WIDTH = GA_HEADS * HEAD_DIM
KV_WIDTH = GA_KV_HEADS * HEAD_DIM
MIX_WIDTH = HY_WIDTH + 2 * Q_WIDTH
IN_WIDTH = 3 * HY_WIDTH + 2 * (Q_WIDTH + 2 * KV_WIDTH)
WINDOW = 128
ROPE_THETA = 10000.0
ATTN_SCALE = HEAD_DIM ** -0.5
HY_SHORT = 3
HY_EMB = 33
HY_BANDS = (HY_EMB - 1) // 2
HY_FAST_DECAY = 0.3
HY_SLOW_DECAY = 1.5
HY_TARGET = 1e-2
N_EXPERTS = 32
TOP_K = 4
SWIGLU_ALPHA = 1.702
SWIGLU_LIMIT = 7.0
EPS = 1e-6
NEG_INF = -1e30

LANES = 128
VMEM_LIMIT_V7X = 56 * 1024 * 1024

ROW_TILE = 256
KV_CHUNK = 2048
MOE_TILE = 512
ROUTER_LANES = LANES


def _cp(sem, vmem=VMEM_LIMIT_V7X):
    return pltpu.CompilerParams(dimension_semantics=sem, vmem_limit_bytes=vmem)


def _rms(v, w):
    return v * lax.rsqrt(jnp.mean(v * v, axis=-1, keepdims=True) + EPS) * w


def _ada_kernel(c_ref, w_ref, b_ref, o_ref):
    c = c_ref[...]
    s = c / (1.0 + jnp.exp(-c))
    hi = s.astype(BF16)
    lo = (s - hi.astype(F32)).astype(BF16)
    w = w_ref[0]
    whi = w.astype(BF16)
    wlo = (w - whi.astype(F32)).astype(BF16)
    acc = jnp.dot(hi, whi, preferred_element_type=F32)
    acc += jnp.dot(lo, whi, preferred_element_type=F32)
    acc += jnp.dot(hi, wlo, preferred_element_type=F32)
    o_ref[0] = acc + b_ref[0]


def _ada_mod(c_all, w_ada, b_ada):
    n_layers, d, n6 = w_ada.shape
    tn = n6 // 4
    return pl.pallas_call(
        _ada_kernel,
        grid=(n_layers, n6 // tn),
        in_specs=[
            pl.BlockSpec((16, d), lambda l, j: (0, 0)),
            pl.BlockSpec((1, d, tn), lambda l, j: (l, 0, j)),
            pl.BlockSpec((1, 1, tn), lambda l, j: (l, 0, j)),
        ],
        out_specs=pl.BlockSpec((1, 16, tn), lambda l, j: (l, 0, j)),
        out_shape=jax.ShapeDtypeStruct((n_layers, 16, n6), F32),
        compiler_params=_cp(("arbitrary", "arbitrary")),
        name="ada_mod",
    )(c_all, w_ada, b_ada.reshape(n_layers, 1, n6))


def _inproj_kernel(xl_ref, xc_ref, mod_ref, nw_ref, w_ref, b_ref, cos_ref, sin_ref, qn_ref, kn_ref, g_ref,
                   hy_ref, gq_ref, gk_ref, gv_ref, wq_ref, wk_ref, wv_ref, *, nt):
    x = jnp.where(pl.program_id(1) < nt, xl_ref[0], xc_ref[0])
    tm = x.shape[0]
    shift = mod_ref[0, 0, 0:1, :]
    scale = mod_ref[0, 0, 1:2, :]
    h = _rms(x, nw_ref[...]) * (1.0 + scale) + shift
    p = jnp.dot(h.astype(BF16), w_ref[...], preferred_element_type=F32) + b_ref[...]
    hy_ref[0] = p[:, :3 * HY_WIDTH]

    cos = cos_ref[...]
    sin = sin_ref[...]
    lane = lax.broadcasted_iota(I32, (tm, LANES), 1)
    first = (lane % (HEAD_DIM // 2)) < (HEAD_DIM // 4)
    gmat = g_ref[...]

    def rope(v):
        sw = jnp.where(first, pltpu.roll(v, LANES - HEAD_DIM // 4, 1), pltpu.roll(v, HEAD_DIM // 4, 1))
        return v * cos + sw * sin

    def head_norm(v, w):
        t = v * v
        hi = t.astype(BF16)
        lo = (t - hi.astype(F32)).astype(BF16)
        ms = jnp.dot(hi, gmat, preferred_element_type=F32) + jnp.dot(lo, gmat, preferred_element_type=F32)
        return v * lax.rsqrt(ms + EPS) * w

    o = 3 * HY_WIDTH
    for u in range(Q_WIDTH // LANES):
        v = p[:, o + u * LANES:o + (u + 1) * LANES]
        v = rope(head_norm(v, qn_ref[...]))
        gq_ref[0, :, u * LANES:(u + 1) * LANES] = (v * ATTN_SCALE).astype(BF16)
    o += Q_WIDTH
    gk_ref[0] = rope(head_norm(p[:, o:o + KV_WIDTH], kn_ref[...])).astype(BF16)
    o += KV_WIDTH
    gv_ref[0] = p[:, o:o + KV_WIDTH].astype(BF16)
    o += KV_WIDTH
    for u in range(Q_WIDTH // LANES):
        v = rope(p[:, o + u * LANES:o + (u + 1) * LANES])
        wq_ref[0, :, u * LANES:(u + 1) * LANES] = (v * ATTN_SCALE).astype(BF16)
    o += Q_WIDTH
    wk_ref[0] = rope(p[:, o:o + KV_WIDTH]).astype(BF16)
    o += KV_WIDTH
    wv_ref[0] = p[:, o:o + KV_WIDTH].astype(BF16)


def _token_specs(nt, ctx_blk, d):
    return [pl.BlockSpec((1, ROW_TILE, d), lambda b, i: (b, jnp.minimum(i, nt - 1), 0)),
            pl.BlockSpec((1, ROW_TILE, d), lambda b, i: (b, ctx_blk, 0))]


def _inproj(x_lat, x_ctx, ctx_blk, mod, norm_w, w_in, b_in, cos_t, sin_t, q_norm, k_norm, nt):
    bsz, _, d = x_lat.shape
    tm = ROW_TILE
    ntile = nt + 1
    rows = ntile * tm
    gmat = jnp.kron(jnp.eye(LANES // HEAD_DIM, dtype=F32), jnp.full((HEAD_DIM, HEAD_DIM), 1.0 / HEAD_DIM, F32)).astype(BF16)
    qn = jnp.tile(q_norm, LANES // HEAD_DIM).reshape(1, LANES)
    kn = jnp.tile(k_norm, LANES // HEAD_DIM).reshape(1, LANES)
    tok = lambda w: pl.BlockSpec((1, tm, w), lambda b, i: (b, i, 0))
    const = lambda s: pl.BlockSpec(s, lambda b, i: (0,) * len(s))
    out_w = (3 * HY_WIDTH, Q_WIDTH, KV_WIDTH, KV_WIDTH, Q_WIDTH, KV_WIDTH, KV_WIDTH)
    out_dt = (F32, BF16, BF16, BF16, BF16, BF16, BF16)
    return pl.pallas_call(
        functools.partial(_inproj_kernel, nt=nt),
        grid=(bsz, ntile),
        in_specs=_token_specs(nt, ctx_blk, d) + [
            pl.BlockSpec((1, 1, 6, d), lambda b, i: (b, i // nt, 0, 0)),
            const((1, d)),
            const((d, IN_WIDTH)),
            const((1, IN_WIDTH)),
            pl.BlockSpec((tm, LANES), lambda b, i: (i, 0)),
            pl.BlockSpec((tm, LANES), lambda b, i: (i, 0)),
            const((1, LANES)),
            const((1, LANES)),
            const((LANES, LANES)),
        ],
        out_specs=[tok(w) for w in out_w],
        out_shape=[jax.ShapeDtypeStruct((bsz, rows, w), dt) for w, dt in zip(out_w, out_dt)],
        compiler_params=_cp(("arbitrary", "arbitrary")),
        name="inproj",
    )(x_lat, x_ctx, mod, norm_w.reshape(1, d), w_in.astype(BF16), b_in.reshape(1, IN_WIDTH), cos_t, sin_t, qn, kn, gmat)


def _stack_q(q_ref, qs_ref):
    qf = q_ref[0].astype(F32)
    tq = qf.shape[0]
    for j in range(GA_KV_HEADS):
        for g in range(GROUP):
            h = GROUP * j + g
            blk = qf[:, (h // 2) * LANES:(h // 2 + 1) * LANES]
            if h % 2 != j:
                blk = pltpu.roll(blk, HEAD_DIM, 1)
            qs_ref[j, g * tq:(g + 1) * tq, :] = blk.astype(BF16)


def _unstack_o(norm_fn, o_ref, tq):
    lane = lax.broadcasted_iota(I32, (tq, LANES), 1)
    heads = []
    for j in range(GA_KV_HEADS):
        for g in range(GROUP):
            h = GROUP * j + g
            o = norm_fn(j, g)
            if h % 2 != j:
                o = pltpu.roll(o, HEAD_DIM, 1)
            heads.append(o)
    for u in range(Q_WIDTH // LANES):
        o_ref[0, :, u * LANES:(u + 1) * LANES] = jnp.where(lane < HEAD_DIM, heads[2 * u], heads[2 * u + 1])


def _half_masks(rows):
    lane = lax.broadcasted_iota(I32, (rows, LANES), 1)
    return [lane < HEAD_DIM, lane >= HEAD_DIM]


def _dot_t(a, b):
    return lax.dot_general(a, b, (((1,), (1,)), ((), ())), preferred_element_type=F32)


def _ga_kernel(q_ref, k_ref, v_ref, o_ref, qs_ref, m_ref, acc_ref, *, nt, n_lat):
    i = pl.program_id(1)
    tq = q_ref.shape[1]
    _stack_q(q_ref, qs_ref)
    m_ref[...] = jnp.full(m_ref.shape, NEG_INF, F32)
    acc_ref[...] = jnp.zeros(acc_ref.shape, F32)

    def chunk(kc, vc):
        masks = _half_masks(kc.shape[0])
        heads = range(GA_KV_HEADS)
        scores = [_dot_t(qs_ref[j], jnp.where(masks[j], kc, jnp.zeros_like(kc))) for j in heads]
        probs = []
        for j in heads:
            m_old = m_ref[j]
            m_new = jnp.maximum(m_old, jnp.max(scores[j], axis=-1, keepdims=True))
            probs.append(jnp.exp((scores[j] - jnp.tile(m_new, (1, scores[j].shape[1] // LANES))).astype(BF16)))
            acc_ref[j] = jnp.exp(m_old - m_new) * acc_ref[j]
            m_ref[j] = m_new
        for j in heads:
            vm = jnp.where(masks[j], vc, jnp.ones_like(vc))
            acc_ref[j] += jnp.dot(probs[j], vm, preferred_element_type=F32)

    @pl.when(i < nt)
    def _():
        rows = k_ref.shape[1]
        for st in range(0, n_lat, KV_CHUNK):
            en = rows if st + KV_CHUNK >= n_lat else st + KV_CHUNK
            chunk(k_ref[0, st:en, :], v_ref[0, st:en, :])

    @pl.when(i >= nt)
    def _():
        chunk(k_ref[0, n_lat:, :], v_ref[0, n_lat:, :])

    def norm_fn(j, g):
        a = acc_ref[j, g * tq:(g + 1) * tq, :]
        return a / pltpu.roll(a, HEAD_DIM, 1)

    _unstack_o(norm_fn, o_ref, tq)


def _global_attention(gq, gk, gv, nt, ntile):
    bsz, rows, _ = gq.shape
    tq = ROW_TILE
    n_lat = nt * tq
    return pl.pallas_call(
        functools.partial(_ga_kernel, nt=nt, n_lat=n_lat),
        grid=(bsz, ntile),
        in_specs=[
            pl.BlockSpec((1, tq, Q_WIDTH), lambda b, i: (b, i, 0)),
            pl.BlockSpec((1, rows, KV_WIDTH), lambda b, i: (b, 0, 0)),
            pl.BlockSpec((1, rows, KV_WIDTH), lambda b, i: (b, 0, 0)),
        ],
        out_specs=pl.BlockSpec((1, tq, Q_WIDTH), lambda b, i: (b, i, 0)),
        out_shape=jax.ShapeDtypeStruct((bsz, ntile * tq, Q_WIDTH), F32),
        scratch_shapes=[
            pltpu.VMEM((GA_KV_HEADS, GROUP * tq, LANES), BF16),
            pltpu.VMEM((GA_KV_HEADS, GROUP * tq, LANES), F32),
            pltpu.VMEM((GA_KV_HEADS, GROUP * tq, LANES), F32),
        ],
        compiler_params=_cp(("arbitrary", "arbitrary")),
        name="global_attn",
    )(gq, gk, gv)


def _wa_kernel(q_ref, k_ref, v_ref, sink_ref, bias_ref, o_ref, qs_ref, acc_ref, l_ref, *, nt, n_lat):
    i = pl.program_id(1)
    tq = q_ref.shape[1]
    band = tq + 2 * WINDOW
    _stack_q(q_ref, qs_ref)
    st = pl.multiple_of(i * tq, tq)
    kb = k_ref[0, pl.ds(st, band), :]
    vb = v_ref[0, pl.ds(st, band), :]
    kc = k_ref[0, WINDOW + n_lat:WINDOW + n_lat + ROW_TILE, :]
    vc = v_ref[0, WINDOW + n_lat:WINDOW + n_lat + ROW_TILE, :]

    kpos = lax.broadcasted_iota(I32, (1, band), 1) + (i * tq - WINDOW)
    in_seq = (kpos >= 0) & (kpos < n_lat) & (i < nt)
    bias = bias_ref[...]

    mb = _half_masks(band)
    mc = _half_masks(ROW_TILE)
    heads = range(WA_KV_HEADS)
    s_loc = [jnp.where(in_seq, _dot_t(qs_ref[j], jnp.where(mb[j], kb, jnp.zeros_like(kb))) + bias, NEG_INF) for j in heads]
    s_ctx = [_dot_t(qs_ref[j], jnp.where(mc[j], kc, jnp.zeros_like(kc))) for j in heads]
    probs = []
    for j in heads:
        sink = jnp.concatenate(
            [jnp.broadcast_to(sink_ref[GROUP * j + g:GROUP * j + g + 1, :], (tq, LANES)) for g in range(GROUP)], axis=0)
        m = jnp.maximum(jnp.maximum(jnp.max(s_loc[j], axis=-1, keepdims=True), jnp.max(s_ctx[j], axis=-1, keepdims=True)),
                        sink)
        probs.append((jnp.exp(s_loc[j] - jnp.tile(m, (1, band // LANES))).astype(BF16),
                      jnp.exp(s_ctx[j] - jnp.tile(m, (1, ROW_TILE // LANES))).astype(BF16)))
        l_ref[j] = jnp.exp(sink - m)
    for j in heads:
        p_loc, p_ctx = probs[j]
        acc_ref[j] = (jnp.dot(p_loc, jnp.where(mb[j], vb, jnp.ones_like(vb)), preferred_element_type=F32)
                      + jnp.dot(p_ctx, jnp.where(mc[j], vc, jnp.ones_like(vc)), preferred_element_type=F32))

    def norm_fn(j, g):
        a = acc_ref[j, g * tq:(g + 1) * tq, :]
        return a / (pltpu.roll(a, HEAD_DIM, 1) + l_ref[j, g * tq:(g + 1) * tq, :])

    _unstack_o(norm_fn, o_ref, tq)


def _window_attention(wq, wk, wv, sink, nt, ntile):
    bsz, rows, _ = wq.shape
    tq = ROW_TILE
    n_lat = nt * tq
    pad = ((0, 0), (WINDOW, WINDOW), (0, 0))
    kp = jnp.pad(wk, pad)
    vp = jnp.pad(wv, pad)
    sink_b = jnp.zeros((8, LANES), F32).at[:WA_HEADS].set(jnp.broadcast_to(sink[:, None], (WA_HEADS, LANES)))
    band = tq + 2 * WINDOW
    rel = jnp.arange(band, dtype=I32)[None, :] - WINDOW - (jnp.arange(GROUP * tq, dtype=I32) % tq)[:, None]
    band_bias = jnp.where(jnp.abs(rel) <= WINDOW, 0.0, NEG_INF).astype(F32)
    return pl.pallas_call(
        functools.partial(_wa_kernel, nt=nt, n_lat=n_lat),
        grid=(bsz, ntile),
        in_specs=[
            pl.BlockSpec((1, tq, Q_WIDTH), lambda b, i: (b, i, 0)),
            pl.BlockSpec((1, rows + 2 * WINDOW, KV_WIDTH), lambda b, i: (b, 0, 0)),
            pl.BlockSpec((1, rows + 2 * WINDOW, KV_WIDTH), lambda b, i: (b, 0, 0)),
            pl.BlockSpec((8, LANES), lambda b, i: (0, 0)),
            pl.BlockSpec((GROUP * tq, band), lambda b, i: (0, 0)),
        ],
        out_specs=pl.BlockSpec((1, tq, Q_WIDTH), lambda b, i: (b, i, 0)),
        out_shape=jax.ShapeDtypeStruct((bsz, ntile * tq, Q_WIDTH), F32),
        scratch_shapes=[
            pltpu.VMEM((WA_KV_HEADS, GROUP * tq, LANES), BF16),
            pltpu.VMEM((WA_KV_HEADS, GROUP * tq, LANES), F32),
            pltpu.VMEM((WA_KV_HEADS, GROUP * tq, LANES), F32),
        ],
        compiler_params=_cp(("arbitrary", "arbitrary")),
        name="window_attn",
    )(wq, kp, vp, sink_b, band_bias)


def _store_row_tiles(ref, v):
    rows = v.shape[0]
    for s in range(v.shape[1] // LANES):
        ref[pl.ds(s, rows, stride=8), :] = v[:, s * LANES:(s + 1) * LANES]


def _load_row_tiles(ref, row0, rows):
    return jnp.concatenate([ref[pl.ds(row0 * 8 + s, rows, stride=8), :] for s in range(8)], axis=1)


def _merge_kernel(xl_ref, xc_ref, hy_ref, ga_ref, wa_ref, mod_ref, bn_ref, wo_ref, bo_ref, nf_ref, wrh_ref, wrl_ref, br_ref,
                  tri_ref, xo_ref, h2_ref, tg_ref, te_ref, tr_ref, cnt_ref, base_ref, *, nt):
    @pl.when((pl.program_id(0) == 0) & (pl.program_id(1) == 0))
    def _():
        base_ref[...] = jnp.zeros(base_ref.shape, F32)

    bn = bn_ref[...]
    g0 = HY_WIDTH
    g1 = HY_WIDTH + Q_WIDTH
    y = jnp.concatenate([_rms(hy_ref[0], bn[:, :g0]), _rms(ga_ref[0], bn[:, g0:g1]), _rms(wa_ref[0], bn[:, g1:])], axis=-1)
    o = jnp.dot(y.astype(BF16), wo_ref[...], preferred_element_type=F32) + bo_ref[...]
    x1 = jnp.where(pl.program_id(1) < nt, xl_ref[0], xc_ref[0]) + mod_ref[0, 0, 2:3, :] * o
    xo_ref[0] = x1
    h2 = _rms(x1, nf_ref[...]) * (1.0 + mod_ref[0, 0, 4:5, :]) + mod_ref[0, 0, 3:4, :]
    _store_row_tiles(h2_ref, h2)

    hi = h2.astype(BF16)
    lo = (h2 - hi.astype(F32)).astype(BF16)
    logits = (jnp.dot(hi, wrh_ref[...], preferred_element_type=F32) + jnp.dot(lo, wrh_ref[...], preferred_element_type=F32)
              + jnp.dot(hi, wrl_ref[...], preferred_element_type=F32)) + br_ref[...]
    tm = logits.shape[0]
    lane = lax.broadcasted_iota(I32, (tm, ROUTER_LANES), 1)
    te = jnp.zeros((tm, ROUTER_LANES), F32)
    tr = jnp.zeros((tm, ROUTER_LANES), F32)
    tv = jnp.zeros((tm, ROUTER_LANES), F32)
    before = base_ref[...]
    tri = tri_ref[...]
    v0 = None
    for k in range(TOP_K):
        mx = jnp.max(logits, axis=-1, keepdims=True)
        idx = jnp.min(jnp.where(logits == mx, lane, ROUTER_LANES), axis=-1, keepdims=True)
        if k == 0:
            v0 = mx
        hit = lane == idx
        onehot = jnp.where(hit, 1.0, 0.0)
        prefix = jnp.dot(tri, onehot.astype(BF16), preferred_element_type=F32) + before
        rank = jnp.sum(onehot * prefix, axis=-1, keepdims=True)
        before = before + jnp.sum(onehot, axis=0, keepdims=True)
        te = jnp.where(lane == k, idx.astype(F32), te)
        tr = jnp.where(lane == k, rank, tr)
        tv = jnp.where(lane == k, jnp.exp(mx - v0), tv)
        logits = jnp.where(hit, NEG_INF, logits)
    base_ref[...] = before
    cnt_ref[...] = before
    tg_ref[0] = tv / jnp.sum(tv, axis=-1, keepdims=True)
    te_t = te.T
    tr_t = tr.T
    for k in range(TOP_K):
        te_ref[0, :, k * tm:(k + 1) * tm] = te_t[k:k + 1, :].astype(I32)
        tr_ref[0, :, k * tm:(k + 1) * tm] = tr_t[k:k + 1, :].astype(I32)


def _merge(x_lat, x_ctx, ctx_blk, y_hy, y_ga, y_wa, mod, branch_norm, w_out, b_out, norm_ffn, w_router, b_router, nt, ntile):
    bsz, _, d = x_lat.shape
    tm = ROW_TILE
    rows = ntile * tm
    wr = jnp.zeros((d, ROUTER_LANES), F32).at[:, :N_EXPERTS].set(w_router)
    wrh = wr.astype(BF16)
    wrl = (wr - wrh.astype(F32)).astype(BF16)
    br = jnp.full((1, ROUTER_LANES), NEG_INF, F32).at[0, :N_EXPERTS].set(b_router)
    tri = (jnp.arange(tm, dtype=I32)[:, None] > jnp.arange(tm, dtype=I32)[None, :]).astype(BF16)
    tok = lambda w: pl.BlockSpec((1, tm, w), lambda b, i: (b, i, 0))
    const = lambda s: pl.BlockSpec(s, lambda b, i: (0,) * len(s))
    flat = pl.BlockSpec((1, 1, TOP_K * tm), lambda b, i: (b * ntile + i, 0, 0))
    return pl.pallas_call(
        functools.partial(_merge_kernel, nt=nt),
        grid=(bsz, ntile),
        in_specs=_token_specs(nt, ctx_blk, d) + [
            tok(HY_WIDTH), tok(Q_WIDTH), tok(Q_WIDTH),
            pl.BlockSpec((1, 1, 6, d), lambda b, i: (b, i // nt, 0, 0)),
            const((1, MIX_WIDTH)), const((MIX_WIDTH, d)), const((1, d)), const((1, d)),
            const((d, ROUTER_LANES)), const((d, ROUTER_LANES)), const((1, ROUTER_LANES)), const((tm, tm)),
        ],
        out_specs=[tok(d), pl.BlockSpec((None, tm * 8, LANES), lambda b, i: (b, i, 0)), tok(ROUTER_LANES), flat, flat,
                   const((1, ROUTER_LANES))],
        out_shape=[
            jax.ShapeDtypeStruct((bsz, rows, d), F32),
            jax.ShapeDtypeStruct((bsz, rows * 8, LANES), F32),
            jax.ShapeDtypeStruct((bsz, rows, ROUTER_LANES), F32),
            jax.ShapeDtypeStruct((bsz * ntile, 1, TOP_K * tm), I32),
            jax.ShapeDtypeStruct((bsz * ntile, 1, TOP_K * tm), I32),
            jax.ShapeDtypeStruct((1, ROUTER_LANES), F32),
        ],
        scratch_shapes=[pltpu.VMEM((1, ROUTER_LANES), F32)],
        compiler_params=_cp(("arbitrary", "arbitrary")),
        name="merge_router",
    )(x_lat, x_ctx, y_hy, y_ga, y_wa, mod, branch_norm.reshape(1, MIX_WIDTH), w_out.astype(BF16), b_out.reshape(1, d),
      norm_ffn.reshape(1, d), wrh, wrl, br, tri)


def _moe_plan(te, tr, cnt, tm):
    steps, _, per = te.shape
    n_blocks = -(-(steps * per) // tm) + N_EXPERTS
    counts = cnt[0, :N_EXPERTS].astype(I32)
    nblk = (counts + tm - 1) // tm
    cend = jnp.cumsum(nblk)
    cstart = cend - nblk
    experts = jnp.arange(N_EXPERTS, dtype=I32)
    row = jnp.sum(jnp.where(te[..., None] == experts, cstart * tm, 0), axis=-1) + tr
    j = jnp.arange(n_blocks, dtype=I32)
    block_expert = jnp.minimum(jnp.sum((cend[None, :] <= j[:, None]).astype(I32), axis=1), N_EXPERTS - 1)
    past = cend[-1] + experts
    zero_blocks = jnp.concatenate([jnp.where(nblk > 0, cend - 1, -1), jnp.where(past < n_blocks, past, -1)])
    return row, block_expert, cend[-1].reshape(1), zero_blocks


def _dispatch_kernel(zb_ref, row_ref, h_ref, x_hbm, buf0, buf1, sem, *, block_rows):
    step = pl.program_id(0)
    tm = h_ref.shape[0] // 8
    per_block = block_rows // tm

    def wait(buf, s):
        for k in range(TOP_K):
            pltpu.make_async_copy(buf, x_hbm.at[pl.ds(0, tm * 8), :], sem.at[s, k]).wait()

    @pl.when(step == 0)
    def _():
        buf1[...] = jnp.zeros(buf1.shape, F32)
        def piece(e, q):
            dst = pl.multiple_of((zb_ref[e] * per_block + q) * (tm * 8), tm * 8)
            return pltpu.make_async_copy(buf1, x_hbm.at[pl.ds(dst, tm * 8), :], sem.at[1, 0])

        for phase in ("start", "wait"):
            for e in range(zb_ref.shape[0]):
                @pl.when(zb_ref[e] >= 0)
                def _():
                    for q in range(per_block):
                        getattr(piece(e, q), phase)()

    def run(cur, prev, sc, sp):
        cur[...] = h_ref[...]
        for k in range(TOP_K):
            for r in range(tm):
                dst = pl.multiple_of(row_ref[0, 0, k * tm + r], 8)
                pltpu.make_async_copy(cur.at[pl.ds(r * 8, 8), :], x_hbm.at[pl.ds(dst, 8), :],
                                      sem.at[sc, k]).start(priority=r % 2)

        @pl.when(step > 0)
        def _():
            wait(prev, sp)

        @pl.when(step == pl.num_programs(0) - 1)
        def _():
            wait(cur, sc)

    @pl.when(step % 2 == 0)
    def _():
        run(buf0, buf1, 0, 1)

    @pl.when(step % 2 == 1)
    def _():
        run(buf1, buf0, 1, 0)


def _dispatch(h_tiles, rows_e, zero_blocks, n_blocks, block_rows):
    steps = rows_e.shape[0]
    tm = ROW_TILE
    grid_spec = pltpu.PrefetchScalarGridSpec(
        num_scalar_prefetch=1,
        grid=(steps,),
        in_specs=[
            pl.BlockSpec((1, 1, TOP_K * tm), lambda i, zb: (i, 0, 0), memory_space=pltpu.SMEM),
            pl.BlockSpec((tm * 8, LANES), lambda i, zb: (i, 0)),
        ],
        out_specs=pl.BlockSpec(memory_space=pl.ANY),
        scratch_shapes=[pltpu.VMEM((tm * 8, LANES), F32), pltpu.VMEM((tm * 8, LANES), F32),
                        pltpu.SemaphoreType.DMA((2, TOP_K))],
    )
    return pl.pallas_call(
        functools.partial(_dispatch_kernel, block_rows=block_rows),
        grid_spec=grid_spec,
        out_shape=jax.ShapeDtypeStruct((n_blocks * block_rows * 8, LANES), F32),
        compiler_params=_cp(("arbitrary",)),
        name="moe_dispatch",
    )(zero_blocks, rows_e * 8, h_tiles)


def _ffn_kernel(be_ref, nu_ref, x_ref, w1_ref, b1_ref, w2_ref, b2_ref, y_ref, w1b, w2b):
    i = pl.program_id(0)
    tm = x_ref.shape[0] // 8
    f = w2_ref.shape[1]

    @pl.when(i < nu_ref[0])
    def _():
        @pl.when((i == 0) | (be_ref[i] != be_ref[jnp.maximum(i - 1, 0)]))
        def _():
            w1b[...] = w1_ref[0].astype(BF16)
            w2b[...] = w2_ref[0].astype(BF16)

        a = jnp.dot(_load_row_tiles(x_ref, 0, tm).astype(BF16), w1b[...], preferred_element_type=F32) + b1_ref[0]
        glu = jnp.minimum(a[:, :f], SWIGLU_LIMIT)
        lin = jnp.clip(a[:, f:], -SWIGLU_LIMIT, SWIGLU_LIMIT)
        act = glu / (1.0 + jnp.exp(-SWIGLU_ALPHA * glu)) * (lin + 1.0)
        _store_row_tiles(y_ref, jnp.dot(act.astype(BF16), w2b[...], preferred_element_type=F32) + b2_ref[0])

    @pl.when(i >= nu_ref[0])
    def _():
        y_ref[...] = jnp.zeros(y_ref.shape, F32)


def _moe_ffn(x_buf, block_expert, n_used, w1, b1, w2, b2, layer, tm):
    n_blocks = block_expert.shape[0]
    _, n_exp, d, f2 = w1.shape
    f = w2.shape[2]
    grid_spec = pltpu.PrefetchScalarGridSpec(
        num_scalar_prefetch=2,
        grid=(n_blocks,),
        in_specs=[
            pl.BlockSpec((tm * 8, LANES), lambda i, be, nu: (i, 0)),
            pl.BlockSpec((None, 1, d, f2), lambda i, be, nu: (layer, be[i], 0, 0)),
            pl.BlockSpec((1, 1, f2), lambda i, be, nu: (be[i], 0, 0)),
            pl.BlockSpec((None, 1, f, d), lambda i, be, nu: (layer, be[i], 0, 0)),
            pl.BlockSpec((1, 1, d), lambda i, be, nu: (be[i], 0, 0)),
        ],
        out_specs=pl.BlockSpec((tm * 8, LANES), lambda i, be, nu: (i, 0)),
        scratch_shapes=[pltpu.VMEM((d, f2), BF16), pltpu.VMEM((f, d), BF16)],
    )
    return pl.pallas_call(
        _ffn_kernel,
        grid_spec=grid_spec,
        out_shape=jax.ShapeDtypeStruct((n_blocks * tm * 8, LANES), F32),
        compiler_params=_cp(("arbitrary",)),
        name="moe_ffn",
    )(block_expert, n_used, x_buf, w1, b1.reshape(n_exp, 1, f2), w2, b2.reshape(n_exp, 1, d))


def _combine_kernel(slot_ref, slotn_ref, y_hbm, x_ref, mod_ref, gate_ref, nf_ref, o_ref, ybuf0, ybuf1, sem, *, final):
    b = pl.program_id(0)
    i = pl.program_id(1)
    step = b * pl.num_programs(1) + i
    n_steps = pl.num_programs(0) * pl.num_programs(1)
    tc = x_ref.shape[1]
    n_rows = TOP_K * tc

    def issue(idx_ref, buf, s):
        for j in range(n_rows):
            src = pl.multiple_of(idx_ref[0, 0, j], 8)
            pltpu.make_async_copy(y_hbm.at[pl.ds(src, 8), :], buf.at[pl.ds(j * 8, 8), :], sem.at[s]).start(priority=j % 2)

    def wait(buf, s):
        pltpu.make_async_copy(y_hbm.at[pl.ds(0, n_rows * 8), :], buf, sem.at[s]).wait()

    @pl.when(step == 0)
    def _():
        issue(slot_ref, ybuf0, 0)

    def run(cur, nxt, sc, sn):
        issue(slotn_ref, nxt, sn)
        wait(cur, sc)
        gates = gate_ref[0]
        acc = gates[:, 0:1] * _load_row_tiles(cur, 0, tc)
        for k in range(1, TOP_K):
            acc += gates[:, k:k + 1] * _load_row_tiles(cur, k * tc, tc)
        out = x_ref[0] + mod_ref[0, 0, 5:6, :] * acc
        if final:
            out = _rms(out, nf_ref[...])
        o_ref[0] = out

        @pl.when(step == n_steps - 1)
        def _():
            wait(nxt, sn)

    @pl.when(step % 2 == 0)
    def _():
        run(ybuf0, ybuf1, 0, 1)

    @pl.when(step % 2 == 1)
    def _():
        run(ybuf1, ybuf0, 1, 0)


def _combine(y_sorted, slot, x_mid, mod, gates, norm_final, nt, ntile, final):
    bsz, _, d = x_mid.shape
    tc = ROW_TILE
    n_steps = bsz * ntile
    slots = slot * 8
    tok = lambda w: pl.BlockSpec((1, tc, w), lambda b, i: (b, i, 0))
    return pl.pallas_call(
        functools.partial(_combine_kernel, final=final),
        grid=(bsz, ntile),
        in_specs=[
            pl.BlockSpec((1, 1, tc * TOP_K), lambda b, i: (b * ntile + i, 0, 0), memory_space=pltpu.SMEM),
            pl.BlockSpec((1, 1, tc * TOP_K), lambda b, i: (jnp.minimum(b * ntile + i + 1, n_steps - 1), 0, 0),
                         memory_space=pltpu.SMEM),
            pl.BlockSpec(memory_space=pl.ANY),
            tok(d),
            pl.BlockSpec((1, 1, 6, d), lambda b, i: (b, i // nt, 0, 0)),
            tok(ROUTER_LANES),
            pl.BlockSpec((1, d), lambda b, i: (0, 0)),
        ],
        out_specs=tok(d),
        out_shape=jax.ShapeDtypeStruct((bsz, ntile * tc, d), F32),
        scratch_shapes=[pltpu.VMEM((TOP_K * tc * 8, LANES), F32), pltpu.VMEM((TOP_K * tc * 8, LANES), F32),
                        pltpu.SemaphoreType.DMA((2,))],
        compiler_params=_cp(("arbitrary", "arbitrary")),
        name="moe_combine",
    )(slots, slots, y_sorted, x_mid, mod, gates, norm_final.reshape(1, d))


def _split_dot(a, w):
    ah = a.astype(BF16)
    al = (a - ah.astype(F32)).astype(BF16)
    wh = w.astype(BF16)
    wl = (w - wh.astype(F32)).astype(BF16)
    return (jnp.dot(ah, wh, preferred_element_type=F32) + jnp.dot(al, wh, preferred_element_type=F32)
            + jnp.dot(ah, wl, preferred_element_type=F32))


def _filt_kernel(z_ref, w1_ref, b1_ref, w2_ref, b2_ref, w3_ref, b3_ref, fr_ref, wo_ref, dec_ref, o_ref):
    fr = fr_ref[...]
    h = jnp.sin(fr * (_split_dot(z_ref[...], w1_ref[...]) + b1_ref[...]))
    h = jnp.sin(fr * (_split_dot(h, w2_ref[...]) + b2_ref[...]))
    h = jnp.sin(fr * (_split_dot(h, w3_ref[...]) + b3_ref[...]))
    o_ref[...] = _split_dot(h, wo_ref[...]) * dec_ref[...]


def _hyena_filters_pl(n, filt):
    w1, b1, w2, b2, w3, b3, freq, w_out = filt
    fw = w2.shape[0]
    t = jnp.linspace(0.0, 1.0, n, dtype=F32)[:, None]
    wpos = (2.0 * math.pi / n) * jnp.arange(n, dtype=F32)[:, None]
    bands = jnp.linspace(1e-4, HY_BANDS - 1, HY_BANDS, dtype=F32)
    z = jnp.concatenate([t, jnp.cos(wpos * bands), -jnp.sin(wpos * bands), jnp.zeros((n, fw - HY_EMB), F32)], axis=-1)
    w1p = jnp.concatenate([w1, jnp.zeros((fw - HY_EMB, fw), F32)], axis=0)
    max_decay = math.log(HY_TARGET) / HY_FAST_DECAY
    min_decay = math.log(HY_TARGET) / HY_SLOW_DECAY
    deltas = jnp.linspace(min_decay, max_decay, HY_WIDTH, dtype=F32)
    decay = jnp.tile(jnp.exp(-t * jnp.abs(deltas)), (1, 2))
    tm = min(n, 512)
    row = lambda w: pl.BlockSpec((tm, w), lambda i: (i, 0))
    const = lambda s: pl.BlockSpec(s, lambda i: (0, 0))
    return pl.pallas_call(
        _filt_kernel,
        grid=(n // tm,),
        in_specs=[row(fw), const((fw, fw)), const((1, fw)), const((fw, fw)), const((1, fw)), const((fw, fw)), const((1, fw)),
                  const((1, fw)), const((fw, 2 * HY_WIDTH)), row(2 * HY_WIDTH)],
        out_specs=row(2 * HY_WIDTH),
        out_shape=jax.ShapeDtypeStruct((n, 2 * HY_WIDTH), F32),
        compiler_params=_cp(("arbitrary",)),
        name="hyena_filters",
    )(z, w1p, b1.reshape(1, fw), w2, b2.reshape(1, fw), w3, b3.reshape(1, fw), freq.reshape(1, fw), w_out, decay)


GATE_STEPS = 4


def _gate_kernel(u_ref, prev_ref, next_ref, w_ref, b_ref, z_ref, x0_ref, *, n_lat):
    i = pl.program_id(1)
    u = u_ref[0]
    tm = u.shape[0]
    n_rows = pl.num_programs(1) * tm
    w = w_ref[...]
    row = lax.broadcasted_iota(I32, u.shape, 0)
    pos = row + i * tm
    up = jnp.where(row == 0, prev_ref[0, 7:8, :], pltpu.roll(u, 1, 0))
    up = jnp.where((pos == 0) | (pos == n_lat), 0.0, up)
    un = jnp.where(row == tm - 1, next_ref[0, 0:1, :], pltpu.roll(u, tm - 1, 0))
    un = jnp.where((pos == n_lat - 1) | (pos == n_rows - 1), 0.0, un)
    uc = up * w[0:1, :] + u * w[1:2, :] + un * w[2:3, :] + b_ref[...]
    x0_ref[0] = uc[:, :HY_WIDTH]
    z_ref[0] = uc[:, 2 * HY_WIDTH:] * uc[:, HY_WIDTH:2 * HY_WIDTH]


def _hyena_gate(hy, conv_w, conv_b, n_lat):
    bsz, rows, w3 = hy.shape
    tm = rows // GATE_STEPS
    sub = tm // 8
    last8 = rows // 8 - 1
    out = pl.BlockSpec((1, tm, HY_WIDTH), lambda b, i: (b, i, 0))
    return pl.pallas_call(
        functools.partial(_gate_kernel, n_lat=n_lat),
        grid=(bsz, GATE_STEPS),
        in_specs=[
            pl.BlockSpec((1, tm, w3), lambda b, i: (b, i, 0)),
            pl.BlockSpec((1, 8, w3), lambda b, i: (b, jnp.maximum(i * sub - 1, 0), 0)),
            pl.BlockSpec((1, 8, w3), lambda b, i: (b, jnp.minimum((i + 1) * sub, last8), 0)),
            pl.BlockSpec((HY_SHORT, w3), lambda b, i: (0, 0)),
            pl.BlockSpec((1, w3), lambda b, i: (0, 0)),
        ],
        out_specs=[out, out],
        out_shape=[jax.ShapeDtypeStruct((bsz, rows, HY_WIDTH), F32)] * 2,
        compiler_params=_cp(("arbitrary", "arbitrary")),
        name="hyena_gate",
    )(hy, hy, hy, conv_w, conv_b.reshape(1, w3))


FFT_N2 = LANES


def _dft_tables(n_fft):
    n1 = n_fft // FFT_N2
    nk = min(n1, -(-(n1 // 2 + 1) // 8) * 8)
    k1 = jnp.arange(nk, dtype=I32)
    nn = jnp.arange(n1, dtype=I32)
    a = (2.0 * math.pi / n1) * ((k1[:, None] * nn[None, :]) % n1).astype(F32)
    f1 = jnp.stack([jnp.cos(a), -jnp.sin(a)], axis=1).reshape(2 * nk, n1)
    w = jnp.where((k1 == 0) | (2 * k1 == n1), 1.0, jnp.where(2 * k1 < n1, 2.0, 0.0)) / n_fft
    g = jnp.stack([jnp.cos(a.T) * w, -jnp.sin(a.T) * w], axis=2).reshape(n1, 2 * nk)
    k2 = jnp.arange(FFT_N2, dtype=I32)
    m = (k2[None, None, :] * (k1[:, None, None] + n1 * k2[None, :, None])) % n_fft
    th = (2.0 * math.pi / n_fft) * m.astype(F32)
    mre, mim = jnp.cos(th), -jnp.sin(th)
    mr = jnp.concatenate([jnp.concatenate([mre, -mim], axis=2), jnp.concatenate([mim, mre], axis=2)], axis=1)
    return f1.astype(BF16), g.astype(BF16), mr.astype(BF16), jnp.swapaxes(mr, 1, 2).astype(BF16)


def _fft1_kernel(x_ref, f_ref, o_ref, *, n1_in):
    f = f_ref[...]
    n_rows = f.shape[0]

    def body(n2, carry):
        xs = x_ref[pl.ds(n2, n1_in, stride=FFT_N2), :].astype(BF16)
        o_ref[pl.ds(n2, n_rows, stride=FFT_N2), :] = jnp.dot(f, xs, preferred_element_type=F32)
        return carry
    lax.fori_loop(0, FFT_N2, body, 0, unroll=8)


def _fft1(x, f1, n1_in):
    bsz, _, ch = x.shape
    n_rows = f1.shape[0]
    return pl.pallas_call(
        functools.partial(_fft1_kernel, n1_in=n1_in),
        grid=(bsz, ch // LANES),
        in_specs=[pl.BlockSpec((None, n1_in * FFT_N2, LANES), lambda b, h: (b, 0, h)),
                  pl.BlockSpec((n_rows, n1_in), lambda b, h: (0, 0))],
        out_specs=pl.BlockSpec((None, n_rows * FFT_N2, LANES), lambda b, h: (b, 0, h)),
        out_shape=jax.ShapeDtypeStruct((bsz, n_rows * FFT_N2, ch), F32),
        compiler_params=_cp(("arbitrary", "arbitrary")),
        name="hyena_fft1",
    )(x, f1[:, :n1_in])


def _spec_kernel(a_ref, mr_ref, o_ref):
    o_ref[0] = jnp.dot(mr_ref[0], a_ref[0].astype(BF16), preferred_element_type=F32)


def _fft_spectrum(a, mr):
    _, rows, ch = a.shape
    blk = 2 * FFT_N2
    return pl.pallas_call(
        _spec_kernel,
        grid=(rows // blk,),
        in_specs=[pl.BlockSpec((1, blk, ch), lambda k: (0, k, 0)), pl.BlockSpec((1, blk, blk), lambda k: (k, 0, 0))],
        out_specs=pl.BlockSpec((1, blk, ch), lambda k: (0, k, 0)),
        out_shape=jax.ShapeDtypeStruct((1, rows, ch), F32),
        compiler_params=_cp(("arbitrary",)),
        name="hyena_filter_spectrum",
    )(a, mr)


def _mid_kernel(a_ref, mr_ref, mrt_ref, ks_ref, o_ref):
    kre, kim = ks_ref[0, :FFT_N2, :], ks_ref[0, FFT_N2:, :]
    for b in range(a_ref.shape[0]):
        x = jnp.dot(mr_ref[0], a_ref[b].astype(BF16), preferred_element_type=F32)
        xre, xim = x[:FFT_N2], x[FFT_N2:]
        y = jnp.concatenate([xre * kre - xim * kim, xre * kim + xim * kre], axis=0)
        o_ref[b] = jnp.dot(mrt_ref[0], y.astype(BF16), preferred_element_type=F32)


def _fft_mid(a, mr, mrt, kspec):
    bsz, rows, ch = a.shape
    blk = 2 * FFT_N2
    return pl.pallas_call(
        _mid_kernel,
        grid=(rows // blk,),
        in_specs=[pl.BlockSpec((bsz, blk, ch), lambda k: (0, k, 0)),
                  pl.BlockSpec((1, blk, blk), lambda k: (k, 0, 0)),
                  pl.BlockSpec((1, blk, blk), lambda k: (k, 0, 0)),
                  pl.BlockSpec((1, blk, ch), lambda k: (0, k, 0))],
        out_specs=pl.BlockSpec((bsz, blk, ch), lambda k: (0, k, 0)),
        out_shape=jax.ShapeDtypeStruct((bsz, rows, ch), F32),
        compiler_params=_cp(("arbitrary",)),
        name="hyena_fft_mid",
    )(a, mr, mrt, kspec)


def _ifft1_kernel(b_ref, g_ref, z_ref, x0_ref, skip_ref, o_ref, *, n1_out):
    g = g_ref[...]
    n_rows = g.shape[1]
    skip = skip_ref[...]

    def body(n2, carry):
        bs = b_ref[pl.ds(n2, n_rows, stride=FFT_N2), :].astype(BF16)
        y = jnp.dot(g, bs, preferred_element_type=F32)
        rows = pl.ds(n2, n1_out, stride=FFT_N2)
        zs = z_ref[rows, :]
        o_ref[rows, :] = x0_ref[rows, :] * (y + zs * skip)
        return carry
    lax.fori_loop(0, FFT_N2, body, 0, unroll=8)


def _ifft1(bmat, g, z, x0, skip, n1_out):
    bsz, rows, ch = bmat.shape
    n = n1_out * FFT_N2
    seq = pl.BlockSpec((None, n, LANES), lambda b, h: (b, 0, h))
    return pl.pallas_call(
        functools.partial(_ifft1_kernel, n1_out=n1_out),
        grid=(bsz, ch // LANES),
        in_specs=[pl.BlockSpec((None, rows, LANES), lambda b, h: (b, 0, h)),
                  pl.BlockSpec((n1_out, g.shape[1]), lambda b, h: (0, 0)),
                  seq, seq, pl.BlockSpec((1, LANES), lambda b, h: (0, h))],
        out_specs=seq,
        out_shape=jax.ShapeDtypeStruct((bsz, n, ch), F32),
        compiler_params=_cp(("arbitrary", "arbitrary")),
        name="hyena_ifft1",
    )(bmat, g[:n1_out], z, x0, skip.reshape(1, ch))


def _hyena_fftconv(z, x0, k2, skip):
    n = k2.shape[0]
    n_fft = 2 * n
    f1, g, mr, mrt = _dft_tables(n_fft)
    k_full = jnp.concatenate([k2[:, :HY_WIDTH], k2[::-1, HY_WIDTH:]], axis=0)[None]
    kspec = _fft_spectrum(_fft1(k_full, f1, n_fft // FFT_N2), mr)
    a = _fft1(z, f1, n // FFT_N2)
    bmat = _fft_mid(a, mr, mrt, kspec)
    return _ifft1(bmat, g, z, x0, skip, n // FFT_N2)


def _ctxconv_kernel(z_ref, x0_ref, kf_ref, fd_ref, gd_ref, skip_ref, o_ref, ks_ref):
    n = z_ref.shape[1]
    half = fd_ref.shape[0] // 2

    @pl.when(pl.program_id(0) == 0)
    def _():
        ks_ref[...] = jnp.dot(fd_ref[...], kf_ref[...].astype(BF16), preferred_element_type=F32)

    z = z_ref[0]
    x = jnp.dot(fd_ref[:, :n], z.astype(BF16), preferred_element_type=F32)
    xre, xim = x[:half], x[half:]
    kre, kim = ks_ref[:half, :], ks_ref[half:, :]
    y = jnp.concatenate([xre * kre - xim * kim, xre * kim + xim * kre], axis=0)
    conv = jnp.dot(gd_ref[...], y.astype(BF16), preferred_element_type=F32)
    o_ref[0] = x0_ref[0] * (conv + z * skip_ref[...])


def _hyena_ctxconv(z, x0, k2, skip, row0):
    n = k2.shape[0]
    bsz, _, ch = z.shape
    n_fft = 2 * n
    kk = jnp.arange(n_fft, dtype=I32)
    th = (2.0 * math.pi / n_fft) * ((kk[:, None] * kk[None, :]) % n_fft).astype(F32)
    fd = jnp.concatenate([jnp.cos(th), -jnp.sin(th)], axis=0).astype(BF16)
    gd = (jnp.concatenate([jnp.cos(th[:n]), -jnp.sin(th[:n])], axis=1) / n_fft).astype(BF16)
    k_full = jnp.concatenate([k2[:, :HY_WIDTH], k2[::-1, HY_WIDTH:]], axis=0)
    blk = row0 // n
    seq = pl.BlockSpec((1, n, ch), lambda b: (b, blk, 0))
    const = lambda s: pl.BlockSpec(s, lambda b: (0, 0))
    return pl.pallas_call(
        _ctxconv_kernel,
        grid=(bsz,),
        in_specs=[seq, seq, const((n_fft, ch)), const((2 * n_fft, n_fft)), const((n, 2 * n_fft)), const((1, ch))],
        out_specs=pl.BlockSpec((1, n, ch), lambda b: (b, 0, 0)),
        out_shape=jax.ShapeDtypeStruct((bsz, n, ch), F32),
        scratch_shapes=[pltpu.VMEM((2 * n_fft, ch), F32)],
        compiler_params=_cp(("arbitrary",)),
        name="hyena_ctxconv",
    )(z, x0, k_full, fd, gd, skip.reshape(1, ch))


def _rope_tables(n_lat, rows):
    nrow = n_lat // GRID_W
    row = jnp.repeat(jnp.arange(nrow, dtype=F32), GRID_W)
    col = jnp.tile(jnp.arange(GRID_W, dtype=F32), nrow)
    n_freq = HEAD_DIM // 4
    inv_freq = ROPE_THETA ** (-jnp.arange(n_freq, dtype=F32) / n_freq)
    ar = row[:, None] * inv_freq
    ac = col[:, None] * inv_freq
    cos64 = jnp.concatenate([jnp.cos(ar), jnp.cos(ar), jnp.cos(ac), jnp.cos(ac)], axis=-1)
    sin64 = jnp.concatenate([-jnp.sin(ar), jnp.sin(ar), -jnp.sin(ac), jnp.sin(ac)], axis=-1)
    pad = rows - n_lat
    cos_t = jnp.concatenate([jnp.tile(cos64, (1, LANES // HEAD_DIM)), jnp.ones((pad, LANES), F32)], axis=0)
    sin_t = jnp.concatenate([jnp.tile(sin64, (1, LANES // HEAD_DIM)), jnp.zeros((pad, LANES), F32)], axis=0)
    return cos_t, sin_t


def kernel(x, c, ctx, c_ctx, w_ada, b_ada, norm_mix, norm_ffn, w_in, b_in, hy_conv_w, hy_conv_b, hy_filt_w1, hy_filt_b1, hy_filt_w2, hy_filt_b2, hy_filt_w3, hy_filt_b3, hy_filt_freq, hy_filt_out, hy_skip, ga_q_norm, ga_k_norm, wa_sink, branch_norm, w_out, b_out, w_router, b_router, w_mlp1, b_mlp1, w_mlp2, b_mlp2, norm_final):
    bsz, n_lat, d = x.shape
    n_ctx = ctx.shape[1]
    depth = w_ada.shape[0]
    assert n_ctx == ROW_TILE and n_lat % KV_CHUNK == 0 and bsz < 16 and d == D_MODEL
    assert (n_lat + n_ctx) % (8 * GATE_STEPS) == 0
    nt = n_lat // ROW_TILE
    rows = n_lat + n_ctx
    cos_t, sin_t = _rope_tables(n_lat, rows)

    c_all = jnp.zeros((16, d), F32).at[:bsz].set(c).at[bsz].set(c_ctx)
    mod_all = _ada_mod(c_all, w_ada, b_ada)
    x_lat, x_ctx, ctx_blk = x, ctx, 0

    for l in range(depth):
        last = l == depth - 1
        ntile = nt if last else nt + 1
        m = mod_all[l].reshape(16, 6, d)
        mod = jnp.stack([m[:bsz], jnp.broadcast_to(m[bsz], (bsz, 6, d))], axis=1)

        hy, gq, gk, gv, wq, wk, wv = _inproj(x_lat, x_ctx, ctx_blk, mod, norm_mix[l], w_in[l], b_in[l], cos_t, sin_t,
                                             ga_q_norm[l], ga_k_norm[l], nt)
        filt = (hy_filt_w1[l], hy_filt_b1[l], hy_filt_w2[l], hy_filt_b2[l], hy_filt_w3[l], hy_filt_b3[l],
                hy_filt_freq[l], hy_filt_out[l])
        z, x0 = _hyena_gate(hy, hy_conv_w[l], hy_conv_b[l], n_lat)
        y_hy = _hyena_fftconv(z, x0, _hyena_filters_pl(n_lat, filt), hy_skip[l])
        if not last:
            yc_hy = _hyena_ctxconv(z, x0, _hyena_filters_pl(n_ctx, filt), hy_skip[l], n_lat)
            y_hy = jnp.concatenate([y_hy, yc_hy], axis=1)
        y_ga = _global_attention(gq, gk, gv, nt, ntile)
        y_wa = _window_attention(wq, wk, wv, wa_sink[l], nt, ntile)
        x_mid, h2, top_g, top_e, top_r, counts = _merge(x_lat, x_ctx, ctx_blk, y_hy, y_ga, y_wa, mod, branch_norm[l], w_out[l], b_out[l],
                                                        norm_ffn[l], w_router[l], b_router[l], nt, ntile)
        rows_e, block_expert, n_used, zero_blocks = _moe_plan(top_e, top_r, counts, MOE_TILE)
        x_buf = _dispatch(h2.reshape(-1, LANES), rows_e, zero_blocks, block_expert.shape[0], MOE_TILE)
        y_sorted = _moe_ffn(x_buf, block_expert, n_used, w_mlp1, b_mlp1[l], w_mlp2, b_mlp2[l], l, MOE_TILE)
        x_lat = x_ctx = _combine(y_sorted, rows_e, x_mid, mod, top_g, norm_final, nt, ntile, last)
        ctx_blk = nt
    return x_lat
```
